```python
import jax, jax.numpy as jnp
from jax import lax
import numpy as np

D_MODEL = 1024
BATCH = 8
SEQ = 2048
DEPTH = 1

GRID_W = 64
CTX_LEN = 256
Q_BLOCK = 128
ROPE_THETA = 10000.0
EPS = 1e-6

A_HEADS = 8
A_KV_HEADS = 2
A_HEAD_DIM = 64
B_HEADS = 8
B_Q_RANK = 384
B_KV_RANK = 256
B_NOPE_DIM = 64
B_ROPE_DIM = 32
B_V_DIM = 64
FFN_HIDDEN = ((8 * D_MODEL // 3 + 255) // 256) * 256

A_SCALE = A_HEAD_DIM ** -0.5
B_SCALE = (B_NOPE_DIM + B_ROPE_DIM) ** -0.5
DEEPNORM_ALPHA = (2.0 * DEPTH) ** 0.25
DEEPNORM_BETA = (8.0 * DEPTH) ** -0.25

IN_SPLIT_SIZES = (
    A_HEADS * A_HEAD_DIM,
    A_KV_HEADS * A_HEAD_DIM,
    A_KV_HEADS * A_HEAD_DIM,
    B_Q_RANK,
    B_KV_RANK,
    B_ROPE_DIM,
    D_MODEL,
    D_MODEL,
)
W_IN_COLS = sum(IN_SPLIT_SIZES)
IN_SPLIT_IDX = tuple(sum(IN_SPLIT_SIZES[:i + 1]) for i in range(len(IN_SPLIT_SIZES) - 1))

kernel_name = "hybrid_gqa_mla_deepnorm_dit_layer"


def layer_norm(x):
    xf = x.astype(jnp.float32)
    mu = jnp.mean(xf, axis=-1, keepdims=True)
    var = jnp.mean(jnp.square(xf - mu), axis=-1, keepdims=True)
    return ((xf - mu) * lax.rsqrt(var + EPS)).astype(x.dtype)


def rms_norm(x, g):
    xf = x.astype(jnp.float32)
    y = xf * lax.rsqrt(jnp.mean(jnp.square(xf), axis=-1, keepdims=True) + EPS)
    return y.astype(x.dtype) * g


def modulate(x, shift, scale):
    return x * (1 + scale) + shift


def post_norm(x, y, g, b):
    return layer_norm(DEEPNORM_ALPHA * x + y) * g + b


def rope_1d(x, pos):
    half = x.shape[-1] // 2
    freqs = ROPE_THETA ** (-jnp.arange(half, dtype=jnp.float32) / half)
    ang = pos.astype(jnp.float32)[:, None] * freqs[None, :]
    cos = jnp.cos(ang)[None, :, None, :].astype(x.dtype)
    sin = jnp.sin(ang)[None, :, None, :].astype(x.dtype)
    x1, x2 = x[..., :half], x[..., half:]
    return jnp.concatenate([x1 * cos - x2 * sin, x1 * sin + x2 * cos], axis=-1)


def axial_rope(x, rows, cols):
    d = x.shape[-1] // 2
    return jnp.concatenate([rope_1d(x[..., :d], rows), rope_1d(x[..., d:], cols)], axis=-1)


def project_streams(h, w_in, q_norm_a, k_norm_a, cq_norm, ckv_norm, w_uq, w_ukv, rows, cols):
    b, t, _ = h.shape
    p = h @ w_in
    q_a, k_a, v_a, c_q, c_kv, k_r, g_a, g_b = jnp.split(p, IN_SPLIT_IDX, axis=-1)
    q_a = rms_norm(q_a.reshape(b, t, A_HEADS, A_HEAD_DIM), q_norm_a)
    k_a = rms_norm(k_a.reshape(b, t, A_KV_HEADS, A_HEAD_DIM), k_norm_a)
    v_a = v_a.reshape(b, t, A_KV_HEADS, A_HEAD_DIM)
    q_b = (rms_norm(c_q, cq_norm) @ w_uq).reshape(b, t, B_HEADS, B_NOPE_DIM + B_ROPE_DIM)
    q_nope, q_rope = q_b[..., :B_NOPE_DIM], q_b[..., B_NOPE_DIM:]
    kv = (rms_norm(c_kv, ckv_norm) @ w_ukv).reshape(b, t, B_HEADS, B_NOPE_DIM + B_V_DIM)
    k_nope, v_b = kv[..., :B_NOPE_DIM], kv[..., B_NOPE_DIM:]
    k_r = k_r[:, :, None, :]
    if rows is not None:
        q_a = axial_rope(q_a, rows, cols)
        k_a = axial_rope(k_a, rows, cols)
        q_rope = axial_rope(q_rope, rows, cols)
        k_r = axial_rope(k_r, rows, cols)
    q_b = jnp.concatenate([q_nope, q_rope], axis=-1)
    k_b = jnp.concatenate([k_nope, jnp.broadcast_to(k_r, (b, t, B_HEADS, B_ROPE_DIM))], axis=-1)
    return q_a, k_a, v_a, q_b, k_b, v_b, g_a, g_b


def block_attention(q, k, v, scale):
    b, s, h, dq = q.shape
    hkv, dv = k.shape[2], v.shape[-1]
    g = h // hkv
    nblk = s // Q_BLOCK
    qb = q.reshape(b, nblk, Q_BLOCK, hkv, g, dq).transpose(1, 0, 2, 3, 4, 5)

    def one_block(q_blk):
        sc = jnp.einsum('bqkgd,btkd->bkgqt', q_blk, k).astype(jnp.float32) * scale
        p = jax.nn.softmax(sc, axis=-1).astype(v.dtype)
        return jnp.einsum('bkgqt,btkd->bqkgd', p, v)

    out = lax.map(one_block, qb)
    return out.transpose(1, 0, 2, 3, 4, 5).reshape(b, s, h * dv)


def merge_branches(o_a, o_b, g_a, g_b, w_proj_a, w_proj_b, w_out):
    y = jax.nn.sigmoid(g_a) * (o_a @ w_proj_a) + jax.nn.sigmoid(g_b) * (o_b @ w_proj_b)
    return y @ w_out


def swiglu(h, w_up, w_down):
    a, u = jnp.split(h @ w_up, 2, axis=-1)
    return (jax.nn.silu(a) * u) @ w_down


def setup_inputs(seed: int = 0) -> dict:
    key = jax.random.key(seed)
    ks = jax.random.split(key, 24)
    L, D, F = DEPTH, D_MODEL, FFN_HIDDEN
    nrm = lambda k, shape, s: jax.random.normal(k, shape, jnp.float32) * s
    return {
        "x": nrm(ks[0], (BATCH, SEQ, D), 1.0),
        "c": nrm(ks[1], (BATCH, D), 1.0),
        "ctx": nrm(ks[2], (BATCH, CTX_LEN, D), 1.0),
        "c_ctx": nrm(ks[3], (D,), 1.0),
        "w_mod": nrm(ks[4], (L, D, 6 * D), D ** -0.5),
        "b_mod": nrm(ks[5], (L, 6 * D), 0.02),
        "w_in": nrm(ks[6], (L, D, W_IN_COLS), D ** -0.5),
        "q_norm_a": 1.0 + nrm(ks[7], (L, A_HEAD_DIM), 0.02),
        "k_norm_a": 1.0 + nrm(ks[8], (L, A_HEAD_DIM), 0.02),
        "cq_norm": 1.0 + nrm(ks[9], (L, B_Q_RANK), 0.02),
        "ckv_norm": 1.0 + nrm(ks[10], (L, B_KV_RANK), 0.02),
        "w_uq": nrm(ks[11], (L, B_Q_RANK, B_HEADS * (B_NOPE_DIM + B_ROPE_DIM)), B_Q_RANK ** -0.5),
        "w_ukv": nrm(ks[12], (L, B_KV_RANK, B_HEADS * (B_NOPE_DIM + B_V_DIM)), B_KV_RANK ** -0.5),
        "w_proj_a": nrm(ks[13], (L, A_HEADS * A_HEAD_DIM, D), (A_HEADS * A_HEAD_DIM) ** -0.5),
        "w_proj_b": nrm(ks[14], (L, B_HEADS * B_V_DIM, D), (B_HEADS * B_V_DIM) ** -0.5),
        "w_out": nrm(ks[15], (L, D, D), DEEPNORM_BETA * D ** -0.5),
        "ln1_g": 1.0 + nrm(ks[16], (L, D), 0.02),
        "ln1_b": nrm(ks[17], (L, D), 0.02),
        "w_up": nrm(ks[18], (L, D, 2 * F), D ** -0.5),
        "w_down": nrm(ks[19], (L, F, D), DEEPNORM_BETA * F ** -0.5),
        "ln2_g": 1.0 + nrm(ks[20], (L, D), 0.02),
        "ln2_b": nrm(ks[21], (L, D), 0.02),
    }


def reference(x, c, ctx, c_ctx, w_mod, b_mod, w_in, q_norm_a, k_norm_a, cq_norm, ckv_norm,
              w_uq, w_ukv, w_proj_a, w_proj_b, w_out, ln1_g, ln1_b, w_up, w_down, ln2_g, ln2_b):
    s = x.shape[1]
    n_rows = s // GRID_W
    rows = jnp.repeat(jnp.arange(n_rows, dtype=jnp.int32), GRID_W)
    cols = jnp.tile(jnp.arange(GRID_W, dtype=jnp.int32), n_rows)

    for l in range(DEPTH):
        mod = jax.nn.silu(c) @ w_mod[l] + b_mod[l]
        sh1, sc1, gt1, sh2, sc2, gt2 = [m[:, None, :] for m in jnp.split(mod, 6, axis=-1)]
        mod_c = jax.nn.silu(c_ctx) @ w_mod[l] + b_mod[l]
        csh1, csc1, cgt1, csh2, csc2, cgt2 = jnp.split(mod_c, 6, axis=-1)

        h = modulate(layer_norm(x), sh1, sc1)
        hc = modulate(layer_norm(ctx), csh1, csc1)
        qa, ka, va, qb, kb, vb, ga, gb = project_streams(
            h, w_in[l], q_norm_a[l], k_norm_a[l], cq_norm[l], ckv_norm[l], w_uq[l], w_ukv[l], rows, cols)
        qac, kac, vac, qbc, kbc, vbc, gac, gbc = project_streams(
            hc, w_in[l], q_norm_a[l], k_norm_a[l], cq_norm[l], ckv_norm[l], w_uq[l], w_ukv[l], None, None)

        oa = block_attention(qa, jnp.concatenate([kac, ka], axis=1),
                             jnp.concatenate([vac, va], axis=1), A_SCALE)
        ob = block_attention(qb, jnp.concatenate([kbc, kb], axis=1),
                             jnp.concatenate([vbc, vb], axis=1), B_SCALE)
        y = merge_branches(oa, ob, ga, gb, w_proj_a[l], w_proj_b[l], w_out[l])

        if l < DEPTH - 1:
            oac = block_attention(qac, kac, vac, A_SCALE)
            obc = block_attention(qbc, kbc, vbc, B_SCALE)
            yc = merge_branches(oac, obc, gac, gbc, w_proj_a[l], w_proj_b[l], w_out[l])
            ctx = post_norm(ctx, cgt1 * yc, ln1_g[l], ln1_b[l])
            hc2 = modulate(layer_norm(ctx), csh2, csc2)
            ctx = post_norm(ctx, cgt2 * swiglu(hc2, w_up[l], w_down[l]), ln2_g[l], ln2_b[l])

        x = post_norm(x, gt1 * y, ln1_g[l], ln1_b[l])

        h2 = modulate(layer_norm(x), sh2, sc2)
        x = post_norm(x, gt2 * swiglu(h2, w_up[l], w_down[l]), ln2_g[l], ln2_b[l])
    return x
```

```python
import functools
import math

import numpy as np
import jax
import jax.numpy as jnp
from jax import lax
from jax.experimental import pallas as pl
from jax.experimental.pallas import tpu as pltpu

D_MODEL = 1024
GRID_W = 64
CTX_LEN = 256
ROPE_THETA = 10000.0
EPS = 1e-6

A_HEADS = 8
A_KV_HEADS = 2
A_HEAD_DIM = 64
B_HEADS = 8
B_Q_RANK = 384
B_KV_RANK = 256
B_NOPE_DIM = 64
B_ROPE_DIM = 32
B_V_DIM = 64
FFN_HIDDEN = 2816
DEPTH = 1

A_SCALE = A_HEAD_DIM ** -0.5
B_SCALE = (B_NOPE_DIM + B_ROPE_DIM) ** -0.5
DEEPNORM_ALPHA = (2.0 * DEPTH) ** 0.25

QA_OFF = 0
KA_OFF = QA_OFF + A_HEADS * A_HEAD_DIM
VA_OFF = KA_OFF + A_KV_HEADS * A_HEAD_DIM
CQ_OFF = VA_OFF + A_KV_HEADS * A_HEAD_DIM
CKV_OFF = CQ_OFF + B_Q_RANK
KR_OFF = CKV_OFF + B_KV_RANK
GATE_OFF = KR_OFF + B_ROPE_DIM
QKV_COLS = GATE_OFF

N_HEADS = A_HEADS + B_HEADS
HEAD_PAD = 128
N_KSETS = 1 + B_HEADS
V_ROWS = A_KV_HEADS * A_HEAD_DIM + B_HEADS * B_V_DIM

PROJ_TILE = 256
ATTN_Q_TILE = 512
TOKEN_TILE = 256
MOD_COL_TILE = 1024
VMEM_LIMIT = 56 * 1024 * 1024

_NT = (((1,), (1,)), ((), ()))
_F32 = jnp.float32
_BF16 = jnp.bfloat16


def _layer_norm(x):
    mu = jnp.mean(x, axis=-1, keepdims=True)
    xc = x - mu
    var = jnp.mean(xc * xc, axis=-1, keepdims=True)
    return xc * lax.rsqrt(var + EPS)


def _sigmoid(x):
    return jax.nn.sigmoid(x)


def _mod_kernel(c_ref, w_ref, b_ref, o_ref):
    c = c_ref[...]
    a = (c * _sigmoid(c)).astype(_BF16)
    o_ref[...] = jnp.dot(a, w_ref[...].astype(_BF16), preferred_element_type=_F32) + b_ref[...]


def _modulation(cc, w_mod, b_mod):
    rows, d = cc.shape
    n = w_mod.shape[1]
    return pl.pallas_call(
        _mod_kernel,
        grid=(n // MOD_COL_TILE,),
        in_specs=[
            pl.BlockSpec((rows, d), lambda i: (0, 0)),
            pl.BlockSpec((d, MOD_COL_TILE), lambda i: (0, i)),
            pl.BlockSpec((1, MOD_COL_TILE), lambda i: (0, i)),
        ],
        out_specs=pl.BlockSpec((rows, MOD_COL_TILE), lambda i: (0, i)),
        out_shape=jax.ShapeDtypeStruct((rows, n), _F32),
        compiler_params=pltpu.CompilerParams(
            dimension_semantics=("arbitrary",), vmem_limit_bytes=VMEM_LIMIT),
        name="mod",
    )(cc, w_mod, b_mod)


def _rms_rows(x, g):
    ms = jnp.mean(x * x, axis=0, keepdims=True)
    return x * lax.rsqrt(ms + EPS) * g


def _axial_rope_rows(x, half, tab):
    r1, r2 = x[0:half], x[half:2 * half]
    c1, c2 = x[2 * half:3 * half], x[3 * half:4 * half]
    cr, sr, cc, sc = tab[0], tab[1], tab[2], tab[3]
    return jnp.concatenate(
        [r1 * cr - r2 * sr, r1 * sr + r2 * cr, c1 * cc - c2 * sc, c1 * sc + c2 * cc], axis=0)


def _proj_kernel(mod_ref, ctx_ref, x_ref, w1_ref, wuq_ref, wukv_ref,
                 qn_ref, kn_ref, cqn_ref, ckvn_ref, ta_ref, tb_ref,
                 q_out, k_out, v_out):
    b = pl.program_id(0)
    j = pl.program_id(1)
    is_ctx = j == 0
    xin = jnp.where(is_ctx, ctx_ref[0], x_ref[0])
    ctx_row = mod_ref.shape[0] // 2
    row = jnp.where(is_ctx, mod_ref[ctx_row:ctx_row + 1, 0:2 * D_MODEL],
                    mod_ref[pl.ds(b, 1), 0:2 * D_MODEL])
    shift, scale = row[:, 0:D_MODEL], row[:, D_MODEL:2 * D_MODEL]
    h = (_layer_norm(xin) * (1.0 + scale) + shift).astype(_BF16)
    pt = lax.dot_general(w1_ref[...], h, _NT, preferred_element_type=_F32)
    t = pt.shape[1]
    ta = ta_ref[...]
    tb = tb_ref[...]
    z64 = jnp.zeros((A_HEAD_DIM, t), _F32)
    z32 = jnp.zeros((HEAD_PAD - B_NOPE_DIM - B_ROPE_DIM, t), _F32)

    qn = qn_ref[...]
    group = A_HEADS // A_KV_HEADS
    for hd in range(A_HEADS):
        xh = pt[QA_OFF + hd * A_HEAD_DIM:QA_OFF + (hd + 1) * A_HEAD_DIM]
        xh = _axial_rope_rows(_rms_rows(xh, qn), A_HEAD_DIM // 4, ta)
        full = jnp.concatenate([xh, z64] if hd // group == 0 else [z64, xh], axis=0)
        q_out[0, hd] = full.astype(_BF16)

    kn = kn_ref[...]
    ka = [
        _axial_rope_rows(
            _rms_rows(pt[KA_OFF + g * A_HEAD_DIM:KA_OFF + (g + 1) * A_HEAD_DIM], kn),
            A_HEAD_DIM // 4, ta)
        for g in range(A_KV_HEADS)
    ]
    k_out[0, 0] = jnp.concatenate(ka, axis=0).T.astype(_BF16)
    v_out[0, 0:A_KV_HEADS * A_HEAD_DIM, :] = pt[VA_OFF:CQ_OFF].astype(_BF16)

    cq = _rms_rows(pt[CQ_OFF:CKV_OFF], cqn_ref[...]).astype(_BF16)
    qb = jnp.dot(wuq_ref[...], cq, preferred_element_type=_F32)
    qdim = B_NOPE_DIM + B_ROPE_DIM
    for hd in range(B_HEADS):
        blk = qb[hd * qdim:(hd + 1) * qdim]
        rp = _axial_rope_rows(blk[B_NOPE_DIM:qdim], B_ROPE_DIM // 4, tb)
        full = jnp.concatenate([blk[0:B_NOPE_DIM], rp, z32], axis=0)
        q_out[0, A_HEADS + hd] = full.astype(_BF16)

    ckv = _rms_rows(pt[CKV_OFF:KR_OFF], ckvn_ref[...]).astype(_BF16)
    kv = jnp.dot(wukv_ref[...], ckv, preferred_element_type=_F32)
    kr = _axial_rope_rows(pt[KR_OFF:GATE_OFF], B_ROPE_DIM // 4, tb)
    kvdim = B_NOPE_DIM + B_V_DIM
    v_base = A_KV_HEADS * A_HEAD_DIM
    for hd in range(B_HEADS):
        kn_h = kv[hd * kvdim:hd * kvdim + B_NOPE_DIM]
        v_h = kv[hd * kvdim + B_NOPE_DIM:(hd + 1) * kvdim]
        v_out[0, v_base + hd * B_V_DIM:v_base + (hd + 1) * B_V_DIM, :] = v_h.astype(_BF16)
        k_out[0, 1 + hd] = jnp.concatenate([kn_h, kr, z32], axis=0).T.astype(_BF16)


def _project(mod, ctx, x, w1t, wuqt, wukvt, qn, kn, cqn, ckvn, tab_a, tab_b):
    bsz, s, d = x.shape
    t = PROJ_TILE
    n_steps = (CTX_LEN + s) // t
    total = CTX_LEN + s
    lat = lambda j: jnp.maximum(j - 1, 0)
    full2 = lambda shape: pl.BlockSpec(shape, lambda b, j: (0, 0))
    return pl.pallas_call(
        _proj_kernel,
        grid=(bsz, n_steps),
        in_specs=[
            full2(mod.shape),
            pl.BlockSpec((1, CTX_LEN, d), lambda b, j: (b, 0, 0)),
            pl.BlockSpec((1, t, d), lambda b, j: (b, lat(j), 0)),
            full2(w1t.shape), full2(wuqt.shape), full2(wukvt.shape),
            full2(qn.shape), full2(kn.shape), full2(cqn.shape), full2(ckvn.shape),
            pl.BlockSpec((4, tab_a.shape[1], t), lambda b, j: (0, 0, j)),
            pl.BlockSpec((4, tab_b.shape[1], t), lambda b, j: (0, 0, j)),
        ],
        out_specs=[
            pl.BlockSpec((1, N_HEADS, HEAD_PAD, t), lambda b, j: (b, 0, 0, lat(j))),
            pl.BlockSpec((1, N_KSETS, t, HEAD_PAD), lambda b, j: (b, 0, j, 0)),
            pl.BlockSpec((1, V_ROWS, t), lambda b, j: (b, 0, j)),
        ],
        out_shape=[
            jax.ShapeDtypeStruct((bsz, N_HEADS, HEAD_PAD, s), _BF16),
            jax.ShapeDtypeStruct((bsz, N_KSETS, total, HEAD_PAD), _BF16),
            jax.ShapeDtypeStruct((bsz, V_ROWS, total), _BF16),
        ],
        compiler_params=pltpu.CompilerParams(
            dimension_semantics=("arbitrary", "arbitrary"), vmem_limit_bytes=VMEM_LIMIT),
        name="proj",
    )(mod, ctx, x, w1t, wuqt, wukvt, qn, kn, cqn, ckvn, tab_a, tab_b)


def _attn_kernel(q_ref, k_ref, v_ref, o_ref):
    hd = pl.program_id(1)
    scale = jnp.where(hd < A_HEADS, A_SCALE, B_SCALE).astype(_F32)
    s = jnp.dot(k_ref[0, 0], q_ref[0, 0], preferred_element_type=_F32)
    m = jnp.max(s, axis=0, keepdims=True)
    p = jnp.exp((s - m) * scale)
    l = jnp.sum(p, axis=0, keepdims=True)
    o = jnp.dot(v_ref[0], p.astype(_BF16), preferred_element_type=_F32)
    o_ref[0] = (o / l).astype(_BF16)


def _attention(q_all, k_all, v_all):
    bsz, _, _, s = q_all.shape
    total = k_all.shape[2]
    tq = ATTN_Q_TILE
    group = A_HEADS // A_KV_HEADS
    kset = lambda h: jnp.where(h < A_HEADS, 0, h - A_HEADS + 1)
    vblk = lambda h: jnp.where(h < A_HEADS, h // group, h - A_HEADS + A_KV_HEADS)
    return pl.pallas_call(
        _attn_kernel,
        grid=(bsz, N_HEADS, s // tq),
        in_specs=[
            pl.BlockSpec((1, 1, HEAD_PAD, tq), lambda b, h, i: (b, h, 0, i)),
            pl.BlockSpec((1, 1, total, HEAD_PAD), lambda b, h, i: (b, kset(h), 0, 0)),
            pl.BlockSpec((1, B_V_DIM, total), lambda b, h, i: (b, vblk(h), 0)),
        ],
        out_specs=pl.BlockSpec((1, B_V_DIM, tq), lambda b, h, i: (b, h, i)),
        out_shape=jax.ShapeDtypeStruct((bsz, N_HEADS * B_V_DIM, s), _BF16),
        compiler_params=pltpu.CompilerParams(
            dimension_semantics=("arbitrary", "arbitrary", "arbitrary"),
            vmem_limit_bytes=VMEM_LIMIT),
        name="attn",
    )(q_all, k_all, v_all)


def _merge_kernel(mod_ref, x_ref, ot_ref, wg_ref, wpa_ref, wpb_ref, wo_ref, g_ref, b_ref, out_ref):
    b = pl.program_id(0)
    x = x_ref[0]
    row = mod_ref[pl.ds(b, 1), 0:3 * D_MODEL]
    shift, scale, gate = row[:, 0:D_MODEL], row[:, D_MODEL:2 * D_MODEL], row[:, 2 * D_MODEL:3 * D_MODEL]
    h = (_layer_norm(x) * (1.0 + scale) + shift).astype(_BF16)
    gt = lax.dot_general(wg_ref[...], h, _NT, preferred_element_type=_F32)
    ot = ot_ref[0]
    na = A_HEADS * A_HEAD_DIM
    ya = jnp.dot(wpa_ref[...], ot[0:na], preferred_element_type=_F32)
    yb = jnp.dot(wpb_ref[...], ot[na:], preferred_element_type=_F32)
    y = _sigmoid(gt[0:D_MODEL]) * ya + _sigmoid(gt[D_MODEL:]) * yb
    zt = jnp.dot(wo_ref[...], y.astype(_BF16), preferred_element_type=_F32)
    r = DEEPNORM_ALPHA * x + gate * zt.T
    out_ref[0] = _layer_norm(r) * g_ref[...] + b_ref[...]


def _merge(mod, x, ot, wgt, wpat, wpbt, wot, g, bta):
    bsz, s, d = x.shape
    t = TOKEN_TILE
    full2 = lambda shape: pl.BlockSpec(shape, lambda b, i: (0, 0))
    return pl.pallas_call(
        _merge_kernel,
        grid=(bsz, s // t),
        in_specs=[
            full2(mod.shape),
            pl.BlockSpec((1, t, d), lambda b, i: (b, i, 0)),
            pl.BlockSpec((1, ot.shape[1], t), lambda b, i: (b, 0, i)),
            full2(wgt.shape), full2(wpat.shape), full2(wpbt.shape), full2(wot.shape),
            full2(g.shape), full2(bta.shape),
        ],
        out_specs=pl.BlockSpec((1, t, d), lambda b, i: (b, i, 0)),
        out_shape=jax.ShapeDtypeStruct((bsz, s, d), _F32),
        compiler_params=pltpu.CompilerParams(
            dimension_semantics=("arbitrary", "arbitrary"), vmem_limit_bytes=VMEM_LIMIT),
        name="merge",
    )(mod, x, ot, wgt, wpat, wpbt, wot, g, bta)


def _ffn_kernel(mod_ref, x_ref, wup_ref, wdn_ref, g_ref, b_ref, out_ref):
    b = pl.program_id(0)
    x = x_ref[0]
    row = mod_ref[pl.ds(b, 1), 3 * D_MODEL:6 * D_MODEL]
    shift, scale, gate = row[:, 0:D_MODEL], row[:, D_MODEL:2 * D_MODEL], row[:, 2 * D_MODEL:3 * D_MODEL]
    h = (_layer_norm(x) * (1.0 + scale) + shift).astype(_BF16)
    au = jnp.dot(h, wup_ref[...], preferred_element_type=_F32)
    a, u = au[:, 0:FFN_HIDDEN], au[:, FFN_HIDDEN:]
    f = (a * _sigmoid(a) * u).astype(_BF16)
    dn = jnp.dot(f, wdn_ref[...], preferred_element_type=_F32)
    r = DEEPNORM_ALPHA * x + gate * dn
    out_ref[0] = _layer_norm(r) * g_ref[...] + b_ref[...]


def _ffn(mod, x, wup, wdn, g, bta):
    bsz, s, d = x.shape
    t = TOKEN_TILE
    full2 = lambda shape: pl.BlockSpec(shape, lambda b, i: (0, 0))
    return pl.pallas_call(
        _ffn_kernel,
        grid=(bsz, s // t),
        in_specs=[
            full2(mod.shape),
            pl.BlockSpec((1, t, d), lambda b, i: (b, i, 0)),
            full2(wup.shape), full2(wdn.shape), full2(g.shape), full2(bta.shape),
        ],
        out_specs=pl.BlockSpec((1, t, d), lambda b, i: (b, i, 0)),
        out_shape=jax.ShapeDtypeStruct((bsz, s, d), _F32),
        compiler_params=pltpu.CompilerParams(
            dimension_semantics=("arbitrary", "arbitrary"), vmem_limit_bytes=VMEM_LIMIT),
        name="ffn",
    )(mod, x, wup, wdn, g, bta)


def _rope_tables(seq, half):
    pos = np.arange(seq)
    freqs = ROPE_THETA ** (-np.arange(half, dtype=np.float64) / half)
    tabs = []
    for p in (pos // GRID_W, pos % GRID_W):
        ang = p[None, :].astype(np.float64) * freqs[:, None]
        for fn, ident in ((np.cos, 1.0), (np.sin, 0.0)):
            tabs.append(np.concatenate([np.full((half, CTX_LEN), ident), fn(ang)], axis=1))
    return jnp.asarray(np.stack(tabs), dtype=_F32)


def kernel(x, c, ctx, c_ctx, w_mod, b_mod, w_in, q_norm_a, k_norm_a, cq_norm, ckv_norm,
           w_uq, w_ukv, w_proj_a, w_proj_b, w_out, ln1_g, ln1_b, w_up, w_down, ln2_g, ln2_b):
    bsz, s, d = x.shape
    assert w_mod.shape[0] == DEPTH == 1 and d == D_MODEL and ctx.shape[1] == CTX_LEN
    assert bsz == 8 and s % ATTN_Q_TILE == 0 and s % TOKEN_TILE == 0

    cc = jnp.concatenate([c, c_ctx[None, :], jnp.zeros((bsz - 1, d), _F32)], axis=0)
    mod = _modulation(cc, w_mod[0], b_mod[0][None, :])

    w_in0 = w_in[0]
    w1t = w_in0[:, :QKV_COLS].T.astype(_BF16)
    wgt = w_in0[:, QKV_COLS:].T.astype(_BF16)
    col = lambda v: v[0][:, None]
    tab_a = _rope_tables(s, A_HEAD_DIM // 4)
    tab_b = _rope_tables(s, B_ROPE_DIM // 4)

    q_all, k_all, v_all = _project(
        mod, ctx, x, w1t, w_uq[0].T.astype(_BF16), w_ukv[0].T.astype(_BF16),
        col(q_norm_a), col(k_norm_a), col(cq_norm), col(ckv_norm), tab_a, tab_b)
    ot = _attention(q_all, k_all, v_all)
    x1 = _merge(mod, x, ot, wgt, w_proj_a[0].T.astype(_BF16), w_proj_b[0].T.astype(_BF16),
                w_out[0].T.astype(_BF16), ln1_g, ln1_b)
    return _ffn(mod, x1, w_up[0].astype(_BF16), w_down[0].astype(_BF16), ln2_g, ln2_b)
```

```python
import functools
import math

import numpy as np
import jax
import jax.numpy as jnp
from jax import lax
from jax.experimental import pallas as pl
from jax.experimental.pallas import tpu as pltpu

D_MODEL = 1024
GRID_W = 64
CTX_LEN = 256
ROPE_THETA = 10000.0
EPS = 1e-6

A_HEADS = 8
A_KV_HEADS = 2
A_HEAD_DIM = 64
B_HEADS = 8
B_Q_RANK = 384
B_KV_RANK = 256
B_NOPE_DIM = 64
B_ROPE_DIM = 32
B_V_DIM = 64
FFN_HIDDEN = 2816
DEPTH = 1

A_SCALE = A_HEAD_DIM ** -0.5
B_SCALE = (B_NOPE_DIM + B_ROPE_DIM) ** -0.5
DEEPNORM_ALPHA = (2.0 * DEPTH) ** 0.25
LOG2_E = math.log2(math.e)

QA_OFF = 0
KA_OFF = QA_OFF + A_HEADS * A_HEAD_DIM
VA_OFF = KA_OFF + A_KV_HEADS * A_HEAD_DIM
CQ_OFF = VA_OFF + A_KV_HEADS * A_HEAD_DIM
CKV_OFF = CQ_OFF + B_Q_RANK
KR_OFF = CKV_OFF + B_KV_RANK
GATE_OFF = KR_OFF + B_ROPE_DIM
QKV_COLS = GATE_OFF

N_HEADS = A_HEADS + B_HEADS
HEAD_PAD = 128
N_KSETS = 1 + B_HEADS
V_ROWS = A_KV_HEADS * A_HEAD_DIM + B_HEADS * B_V_DIM

PROJ_TILE = 256
ATTN_Q_TILE = 512
ATTN_KEY_CHUNK = 256
TOKEN_TILE = 256
MOD_COL_TILE = 1024
VMEM_LIMIT = 56 * 1024 * 1024

_NT = (((1,), (1,)), ((), ()))
_F32 = jnp.float32
_BF16 = jnp.bfloat16


def _layer_norm(x):
    mu = jnp.mean(x, axis=-1, keepdims=True)
    xc = x - mu
    var = jnp.mean(xc * xc, axis=-1, keepdims=True)
    return xc * lax.rsqrt(var + EPS)


def _sigmoid(x):
    return jax.nn.sigmoid(x)


def _mod_kernel(c_ref, w_ref, b_ref, o_ref):
    c = c_ref[...]
    a = (c * _sigmoid(c)).astype(_BF16)
    o_ref[...] = jnp.dot(a, w_ref[...].astype(_BF16), preferred_element_type=_F32) + b_ref[...]


def _modulation(cc, w_mod, b_mod):
    rows, d = cc.shape
    n = w_mod.shape[1]
    return pl.pallas_call(
        _mod_kernel,
        grid=(n // MOD_COL_TILE,),
        in_specs=[
            pl.BlockSpec((rows, d), lambda i: (0, 0)),
            pl.BlockSpec((d, MOD_COL_TILE), lambda i: (0, i)),
            pl.BlockSpec((1, MOD_COL_TILE), lambda i: (0, i)),
        ],
        out_specs=pl.BlockSpec((rows, MOD_COL_TILE), lambda i: (0, i)),
        out_shape=jax.ShapeDtypeStruct((rows, n), _F32),
        compiler_params=pltpu.CompilerParams(
            dimension_semantics=("arbitrary",), vmem_limit_bytes=VMEM_LIMIT),
        name="mod",
    )(cc, w_mod, b_mod)


def _rms_rows(x, g):
    ms = jnp.mean(x * x, axis=0, keepdims=True)
    return x * lax.rsqrt(ms + EPS) * g


def _axial_rope_rows(x, half, tab):
    r1, r2 = x[0:half], x[half:2 * half]
    c1, c2 = x[2 * half:3 * half], x[3 * half:4 * half]
    cr, sr, cc, sc = tab[0], tab[1], tab[2], tab[3]
    return jnp.concatenate(
        [r1 * cr - r2 * sr, r1 * sr + r2 * cr, c1 * cc - c2 * sc, c1 * sc + c2 * cc], axis=0)


def _proj_kernel(mod_ref, ctx_ref, x_ref, w1_ref, wuq_ref, wukv_ref,
                 qn_ref, kn_ref, cqn_ref, ckvn_ref, ta_ref, tb_ref,
                 q_out, k_out, v_out):
    b = pl.program_id(0)
    j = pl.program_id(1)
    is_ctx = j == 0
    xin = jnp.where(is_ctx, ctx_ref[0], x_ref[0])
    ctx_row = mod_ref.shape[0] // 2
    row = jnp.where(is_ctx, mod_ref[ctx_row:ctx_row + 1, 0:2 * D_MODEL],
                    mod_ref[pl.ds(b, 1), 0:2 * D_MODEL])
    shift, scale = row[:, 0:D_MODEL], row[:, D_MODEL:2 * D_MODEL]
    h = (_layer_norm(xin) * (1.0 + scale) + shift).astype(_BF16)
    pt = lax.dot_general(w1_ref[...], h, _NT, preferred_element_type=_F32)
    t = pt.shape[1]
    ta = ta_ref[...]
    tb = tb_ref[...]
    z64 = jnp.zeros((A_HEAD_DIM, t), _F32)
    z32 = jnp.zeros((HEAD_PAD - B_NOPE_DIM - B_ROPE_DIM, t), _F32)

    qn = qn_ref[...]
    group = A_HEADS // A_KV_HEADS
    for hd in range(A_HEADS):
        xh = pt[QA_OFF + hd * A_HEAD_DIM:QA_OFF + (hd + 1) * A_HEAD_DIM]
        xh = _axial_rope_rows(_rms_rows(xh, qn), A_HEAD_DIM // 4, ta)
        full = jnp.concatenate([xh, z64] if hd // group == 0 else [z64, xh], axis=0)
        q_out[0, hd] = full.astype(_BF16)

    kn = kn_ref[...]
    ka = [
        _axial_rope_rows(
            _rms_rows(pt[KA_OFF + g * A_HEAD_DIM:KA_OFF + (g + 1) * A_HEAD_DIM], kn),
            A_HEAD_DIM // 4, ta)
        for g in range(A_KV_HEADS)
    ]
    k_out[0, 0] = jnp.concatenate(ka, axis=0).T.astype(_BF16)
    v_out[0, 0:A_KV_HEADS * A_HEAD_DIM, :] = pt[VA_OFF:CQ_OFF].astype(_BF16)

    cq = _rms_rows(pt[CQ_OFF:CKV_OFF], cqn_ref[...]).astype(_BF16)
    qb = jnp.dot(wuq_ref[...], cq, preferred_element_type=_F32)
    qdim = B_NOPE_DIM + B_ROPE_DIM
    for hd in range(B_HEADS):
        blk = qb[hd * qdim:(hd + 1) * qdim]
        rp = _axial_rope_rows(blk[B_NOPE_DIM:qdim], B_ROPE_DIM // 4, tb)
        full = jnp.concatenate([blk[0:B_NOPE_DIM], rp, z32], axis=0)
        q_out[0, A_HEADS + hd] = full.astype(_BF16)

    ckv = _rms_rows(pt[CKV_OFF:KR_OFF], ckvn_ref[...]).astype(_BF16)
    kv = jnp.dot(wukv_ref[...], ckv, preferred_element_type=_F32)
    kr = _axial_rope_rows(pt[KR_OFF:GATE_OFF], B_ROPE_DIM // 4, tb)
    kvdim = B_NOPE_DIM + B_V_DIM
    v_base = A_KV_HEADS * A_HEAD_DIM
    for hd in range(B_HEADS):
        kn_h = kv[hd * kvdim:hd * kvdim + B_NOPE_DIM]
        v_h = kv[hd * kvdim + B_NOPE_DIM:(hd + 1) * kvdim]
        v_out[0, v_base + hd * B_V_DIM:v_base + (hd + 1) * B_V_DIM, :] = v_h.astype(_BF16)
        k_out[0, 1 + hd] = jnp.concatenate([kn_h, kr, z32], axis=0).T.astype(_BF16)


def _project(mod, ctx, x, w1t, wuqt, wukvt, qn, kn, cqn, ckvn, tab_a, tab_b):
    bsz, s, d = x.shape
    t = PROJ_TILE
    n_steps = (CTX_LEN + s) // t
    total = CTX_LEN + s
    lat = lambda j: jnp.maximum(j - 1, 0)
    full2 = lambda shape: pl.BlockSpec(shape, lambda b, j: (0, 0))
    return pl.pallas_call(
        _proj_kernel,
        grid=(bsz, n_steps),
        in_specs=[
            full2(mod.shape),
            pl.BlockSpec((1, CTX_LEN, d), lambda b, j: (b, 0, 0)),
            pl.BlockSpec((1, t, d), lambda b, j: (b, lat(j), 0)),
            full2(w1t.shape), full2(wuqt.shape), full2(wukvt.shape),
            full2(qn.shape), full2(kn.shape), full2(cqn.shape), full2(ckvn.shape),
            pl.BlockSpec((4, tab_a.shape[1], t), lambda b, j: (0, 0, j)),
            pl.BlockSpec((4, tab_b.shape[1], t), lambda b, j: (0, 0, j)),
        ],
        out_specs=[
            pl.BlockSpec((1, N_HEADS, HEAD_PAD, t), lambda b, j: (b, 0, 0, lat(j))),
            pl.BlockSpec((1, N_KSETS, t, HEAD_PAD), lambda b, j: (b, 0, j, 0)),
            pl.BlockSpec((1, V_ROWS, t), lambda b, j: (b, 0, j)),
        ],
        out_shape=[
            jax.ShapeDtypeStruct((bsz, N_HEADS, HEAD_PAD, s), _BF16),
            jax.ShapeDtypeStruct((bsz, N_KSETS, total, HEAD_PAD), _BF16),
            jax.ShapeDtypeStruct((bsz, V_ROWS, total), _BF16),
        ],
        compiler_params=pltpu.CompilerParams(
            dimension_semantics=("arbitrary", "arbitrary"), vmem_limit_bytes=VMEM_LIMIT),
        name="proj",
    )(mod, ctx, x, w1t, wuqt, wukvt, qn, kn, cqn, ckvn, tab_a, tab_b)


def _attn_kernel(q_ref, k_ref, v_ref, o_ref, s0_ref, s1_ref):
    hd = pl.program_id(1)
    c2 = jnp.where(hd < A_HEADS, A_SCALE * LOG2_E, B_SCALE * LOG2_E).astype(_F32)
    tq, kc = ATTN_Q_TILE, ATTN_KEY_CHUNK
    n_tiles = q_ref.shape[3] // tq
    n_chunks = k_ref.shape[2] // kc
    bufs = (s0_ref, s1_ref)

    def score_chunk(t, c, m):
        rows = pl.ds(c * kc, kc)
        s = jnp.dot(k_ref[0, 0, rows, :], q_ref[0, 0, :, pl.ds(t * tq, tq)],
                    preferred_element_type=_F32)
        bufs[t % 2][rows, :] = s
        cm = jnp.max(s, axis=0, keepdims=True)
        return cm if m is None else jnp.maximum(m, cm)

    def value_chunk(t, c, m, l, acc):
        rows = pl.ds(c * kc, kc)
        p = jnp.exp2((bufs[t % 2][rows, :] - m) * c2)
        ps = jnp.sum(p, axis=0, keepdims=True)
        pv = jnp.dot(v_ref[0, :, rows], p.astype(_BF16), preferred_element_type=_F32)
        return (ps, pv) if l is None else (l + ps, acc + pv)

    m_cur = None
    for c in range(n_chunks):
        m_cur = score_chunk(0, c, m_cur)
    for t in range(n_tiles):
        m_next = l = acc = None
        for c in range(n_chunks):
            if t + 1 < n_tiles:
                m_next = score_chunk(t + 1, c, m_next)
            l, acc = value_chunk(t, c, m_cur, l, acc)
        o_ref[0, :, pl.ds(t * tq, tq)] = (acc / l).astype(_BF16)
        m_cur = m_next


def _attention(q_all, k_all, v_all):
    bsz, _, _, s = q_all.shape
    total = k_all.shape[2]
    group = A_HEADS // A_KV_HEADS
    kset = lambda h: jnp.where(h < A_HEADS, 0, h - A_HEADS + 1)
    vblk = lambda h: jnp.where(h < A_HEADS, h // group, h - A_HEADS + A_KV_HEADS)
    return pl.pallas_call(
        _attn_kernel,
        grid=(bsz, N_HEADS),
        in_specs=[
            pl.BlockSpec((1, 1, HEAD_PAD, s), lambda b, h: (b, h, 0, 0)),
            pl.BlockSpec((1, 1, total, HEAD_PAD), lambda b, h: (b, kset(h), 0, 0)),
            pl.BlockSpec((1, B_V_DIM, total), lambda b, h: (b, vblk(h), 0)),
        ],
        out_specs=pl.BlockSpec((1, B_V_DIM, s), lambda b, h: (b, h, 0)),
        out_shape=jax.ShapeDtypeStruct((bsz, N_HEADS * B_V_DIM, s), _BF16),
        scratch_shapes=[pltpu.VMEM((total, ATTN_Q_TILE), _F32),
                        pltpu.VMEM((total, ATTN_Q_TILE), _F32)],
        compiler_params=pltpu.CompilerParams(
            dimension_semantics=("arbitrary", "arbitrary"), vmem_limit_bytes=VMEM_LIMIT),
        name="attn",
    )(q_all, k_all, v_all)


def _merge_kernel(mod_ref, x_ref, ot_ref, wg_ref, wpa_ref, wpb_ref, wo_ref, g_ref, b_ref, out_ref):
    b = pl.program_id(0)
    x = x_ref[0]
    row = mod_ref[pl.ds(b, 1), 0:3 * D_MODEL]
    shift, scale, gate = row[:, 0:D_MODEL], row[:, D_MODEL:2 * D_MODEL], row[:, 2 * D_MODEL:3 * D_MODEL]
    h = (_layer_norm(x) * (1.0 + scale) + shift).astype(_BF16)
    gt = lax.dot_general(wg_ref[...], h, _NT, preferred_element_type=_F32)
    ot = ot_ref[0]
    na = A_HEADS * A_HEAD_DIM
    ya = jnp.dot(wpa_ref[...], ot[0:na], preferred_element_type=_F32)
    yb = jnp.dot(wpb_ref[...], ot[na:], preferred_element_type=_F32)
    y = _sigmoid(gt[0:D_MODEL]) * ya + _sigmoid(gt[D_MODEL:]) * yb
    zt = jnp.dot(wo_ref[...], y.astype(_BF16), preferred_element_type=_F32)
    r = DEEPNORM_ALPHA * x + gate * zt.T
    out_ref[0] = _layer_norm(r) * g_ref[...] + b_ref[...]


def _merge(mod, x, ot, wgt, wpat, wpbt, wot, g, bta):
    bsz, s, d = x.shape
    t = TOKEN_TILE
    full2 = lambda shape: pl.BlockSpec(shape, lambda b, i: (0, 0))
    return pl.pallas_call(
        _merge_kernel,
        grid=(bsz, s // t),
        in_specs=[
            full2(mod.shape),
            pl.BlockSpec((1, t, d), lambda b, i: (b, i, 0)),
            pl.BlockSpec((1, ot.shape[1], t), lambda b, i: (b, 0, i)),
            full2(wgt.shape), full2(wpat.shape), full2(wpbt.shape), full2(wot.shape),
            full2(g.shape), full2(bta.shape),
        ],
        out_specs=pl.BlockSpec((1, t, d), lambda b, i: (b, i, 0)),
        out_shape=jax.ShapeDtypeStruct((bsz, s, d), _F32),
        compiler_params=pltpu.CompilerParams(
            dimension_semantics=("arbitrary", "arbitrary"), vmem_limit_bytes=VMEM_LIMIT),
        name="merge",
    )(mod, x, ot, wgt, wpat, wpbt, wot, g, bta)


def _ffn_kernel(mod_ref, x_ref, wup_ref, wdn_ref, g_ref, b_ref, out_ref):
    b = pl.program_id(0)
    x = x_ref[0]
    row = mod_ref[pl.ds(b, 1), 3 * D_MODEL:6 * D_MODEL]
    shift, scale, gate = row[:, 0:D_MODEL], row[:, D_MODEL:2 * D_MODEL], row[:, 2 * D_MODEL:3 * D_MODEL]
    h = (_layer_norm(x) * (1.0 + scale) + shift).astype(_BF16)
    au = jnp.dot(h, wup_ref[...], preferred_element_type=_F32)
    a, u = au[:, 0:FFN_HIDDEN], au[:, FFN_HIDDEN:]
    f = (a * _sigmoid(a) * u).astype(_BF16)
    dn = jnp.dot(f, wdn_ref[...], preferred_element_type=_F32)
    r = DEEPNORM_ALPHA * x + gate * dn
    out_ref[0] = _layer_norm(r) * g_ref[...] + b_ref[...]


def _ffn(mod, x, wup, wdn, g, bta):
    bsz, s, d = x.shape
    t = TOKEN_TILE
    full2 = lambda shape: pl.BlockSpec(shape, lambda b, i: (0, 0))
    return pl.pallas_call(
        _ffn_kernel,
        grid=(bsz, s // t),
        in_specs=[
            full2(mod.shape),
            pl.BlockSpec((1, t, d), lambda b, i: (b, i, 0)),
            full2(wup.shape), full2(wdn.shape), full2(g.shape), full2(bta.shape),
        ],
        out_specs=pl.BlockSpec((1, t, d), lambda b, i: (b, i, 0)),
        out_shape=jax.ShapeDtypeStruct((bsz, s, d), _F32),
        compiler_params=pltpu.CompilerParams(
            dimension_semantics=("arbitrary", "arbitrary"), vmem_limit_bytes=VMEM_LIMIT),
        name="ffn",
    )(mod, x, wup, wdn, g, bta)


def _rope_tables(seq, half):
    pos = np.arange(seq)
    freqs = ROPE_THETA ** (-np.arange(half, dtype=np.float64) / half)
    tabs = []
    for p in (pos // GRID_W, pos % GRID_W):
        ang = p[None, :].astype(np.float64) * freqs[:, None]
        for fn, ident in ((np.cos, 1.0), (np.sin, 0.0)):
            tabs.append(np.concatenate([np.full((half, CTX_LEN), ident), fn(ang)], axis=1))
    return jnp.asarray(np.stack(tabs), dtype=_F32)


def kernel(x, c, ctx, c_ctx, w_mod, b_mod, w_in, q_norm_a, k_norm_a, cq_norm, ckv_norm,
           w_uq, w_ukv, w_proj_a, w_proj_b, w_out, ln1_g, ln1_b, w_up, w_down, ln2_g, ln2_b):
    bsz, s, d = x.shape
    assert w_mod.shape[0] == DEPTH == 1 and d == D_MODEL and ctx.shape[1] == CTX_LEN
    assert bsz == 8 and s % ATTN_Q_TILE == 0 and s % TOKEN_TILE == 0

    cc = jnp.concatenate([c, c_ctx[None, :], jnp.zeros((bsz - 1, d), _F32)], axis=0)
    mod = _modulation(cc, w_mod[0], b_mod[0][None, :])

    w_in0 = w_in[0]
    w1t = w_in0[:, :QKV_COLS].T.astype(_BF16)
    wgt = w_in0[:, QKV_COLS:].T.astype(_BF16)
    col = lambda v: v[0][:, None]
    tab_a = _rope_tables(s, A_HEAD_DIM // 4)
    tab_b = _rope_tables(s, B_ROPE_DIM // 4)

    q_all, k_all, v_all = _project(
        mod, ctx, x, w1t, w_uq[0].T.astype(_BF16), w_ukv[0].T.astype(_BF16),
        col(q_norm_a), col(k_norm_a), col(cq_norm), col(ckv_norm), tab_a, tab_b)
    ot = _attention(q_all, k_all, v_all)
    x1 = _merge(mod, x, ot, wgt, w_proj_a[0].T.astype(_BF16), w_proj_b[0].T.astype(_BF16),
                w_out[0].T.astype(_BF16), ln1_g, ln1_b)
    return _ffn(mod, x1, w_up[0].astype(_BF16), w_down[0].astype(_BF16), ln2_g, ln2_b)
```

```python
import functools
import math

import numpy as np
import jax
import jax.numpy as jnp
from jax import lax
from jax.experimental import pallas as pl
from jax.experimental.pallas import tpu as pltpu

D_MODEL = 1024
GRID_W = 64
CTX_LEN = 256
ROPE_THETA = 10000.0
EPS = 1e-6

A_HEADS = 8
A_KV_HEADS = 2
A_HEAD_DIM = 64
B_HEADS = 8
B_Q_RANK = 384
B_KV_RANK = 256
B_NOPE_DIM = 64
B_ROPE_DIM = 32
B_V_DIM = 64
FFN_HIDDEN = 2816
DEPTH = 1

A_SCALE = A_HEAD_DIM ** -0.5
B_SCALE = (B_NOPE_DIM + B_ROPE_DIM) ** -0.5
DEEPNORM_ALPHA = (2.0 * DEPTH) ** 0.25
LOG2_E = math.log2(math.e)

QA_OFF = 0
KA_OFF = QA_OFF + A_HEADS * A_HEAD_DIM
VA_OFF = KA_OFF + A_KV_HEADS * A_HEAD_DIM
CQ_OFF = VA_OFF + A_KV_HEADS * A_HEAD_DIM
CKV_OFF = CQ_OFF + B_Q_RANK
KR_OFF = CKV_OFF + B_KV_RANK
GATE_OFF = KR_OFF + B_ROPE_DIM
QKV_COLS = GATE_OFF

N_HEADS = A_HEADS + B_HEADS
HEAD_PAD = 128
N_KSETS = 1 + B_HEADS
V_ROWS = A_KV_HEADS * A_HEAD_DIM + B_HEADS * B_V_DIM

PROJ_TILE = 256
ATTN_Q_TILE = 512
ATTN_KEY_CHUNK = 256
TOKEN_TILE = 256
MOD_COL_TILE = 1024
VMEM_LIMIT = 56 * 1024 * 1024

_NT = (((1,), (1,)), ((), ()))
_F32 = jnp.float32
_BF16 = jnp.bfloat16


def _layer_norm(x):
    mu = jnp.mean(x, axis=-1, keepdims=True)
    xc = x - mu
    var = jnp.mean(xc * xc, axis=-1, keepdims=True)
    return xc * lax.rsqrt(var + EPS)


def _sigmoid(x):
    return jax.nn.sigmoid(x)


def _mod_kernel(c_ref, w_ref, b_ref, o_ref):
    c = c_ref[...]
    a = (c * _sigmoid(c)).astype(_BF16)
    o_ref[...] = jnp.dot(a, w_ref[...].astype(_BF16), preferred_element_type=_F32) + b_ref[...]


def _modulation(cc, w_mod, b_mod):
    rows, d = cc.shape
    n = w_mod.shape[1]
    return pl.pallas_call(
        _mod_kernel,
        grid=(n // MOD_COL_TILE,),
        in_specs=[
            pl.BlockSpec((rows, d), lambda i: (0, 0)),
            pl.BlockSpec((d, MOD_COL_TILE), lambda i: (0, i)),
            pl.BlockSpec((1, MOD_COL_TILE), lambda i: (0, i)),
        ],
        out_specs=pl.BlockSpec((rows, MOD_COL_TILE), lambda i: (0, i)),
        out_shape=jax.ShapeDtypeStruct((rows, n), _F32),
        compiler_params=pltpu.CompilerParams(
            dimension_semantics=("arbitrary",), vmem_limit_bytes=VMEM_LIMIT),
        name="mod",
    )(cc, w_mod, b_mod)


def _rms_rows(x, g):
    ms = jnp.mean(x * x, axis=0, keepdims=True)
    return x * lax.rsqrt(ms + EPS) * g


def _axial_rope_rows(x, half, tab):
    r1, r2 = x[0:half], x[half:2 * half]
    c1, c2 = x[2 * half:3 * half], x[3 * half:4 * half]
    cr, sr, cc, sc = tab[0], tab[1], tab[2], tab[3]
    return jnp.concatenate(
        [r1 * cr - r2 * sr, r1 * sr + r2 * cr, c1 * cc - c2 * sc, c1 * sc + c2 * cc], axis=0)


def _proj_kernel(mod_ref, ctx_ref, x_ref, w1_ref, wuq_ref, wukv_ref,
                 qn_ref, kn_ref, cqn_ref, ckvn_ref, ta_ref, tb_ref,
                 q_out, k_out, v_out):
    b = pl.program_id(0)
    j = pl.program_id(1)
    is_ctx = j == 0
    xin = jnp.where(is_ctx, ctx_ref[0], x_ref[0])
    ctx_row = mod_ref.shape[0] // 2
    row = jnp.where(is_ctx, mod_ref[ctx_row:ctx_row + 1, 0:2 * D_MODEL],
                    mod_ref[pl.ds(b, 1), 0:2 * D_MODEL])
    shift, scale = row[:, 0:D_MODEL], row[:, D_MODEL:2 * D_MODEL]
    h = (_layer_norm(xin) * (1.0 + scale) + shift).astype(_BF16)
    pt = lax.dot_general(w1_ref[...], h, _NT, preferred_element_type=_F32)
    t = pt.shape[1]
    ta = ta_ref[...]
    tb = tb_ref[...]
    z64 = jnp.zeros((A_HEAD_DIM, t), _F32)
    z32 = jnp.zeros((HEAD_PAD - B_NOPE_DIM - B_ROPE_DIM, t), _F32)

    qn = qn_ref[...]
    group = A_HEADS // A_KV_HEADS
    for hd in range(A_HEADS):
        xh = pt[QA_OFF + hd * A_HEAD_DIM:QA_OFF + (hd + 1) * A_HEAD_DIM]
        xh = _axial_rope_rows(_rms_rows(xh, qn), A_HEAD_DIM // 4, ta)
        full = jnp.concatenate([xh, z64] if hd // group == 0 else [z64, xh], axis=0)
        q_out[0, hd] = full.astype(_BF16)

    kn = kn_ref[...]
    ka = [
        _axial_rope_rows(
            _rms_rows(pt[KA_OFF + g * A_HEAD_DIM:KA_OFF + (g + 1) * A_HEAD_DIM], kn),
            A_HEAD_DIM // 4, ta)
        for g in range(A_KV_HEADS)
    ]
    k_out[0, 0] = jnp.concatenate(ka, axis=0).T.astype(_BF16)
    v_out[0, 0:A_KV_HEADS * A_HEAD_DIM, :] = pt[VA_OFF:CQ_OFF].astype(_BF16)

    cq = _rms_rows(pt[CQ_OFF:CKV_OFF], cqn_ref[...]).astype(_BF16)
    qb = jnp.dot(wuq_ref[...], cq, preferred_element_type=_F32)
    qdim = B_NOPE_DIM + B_ROPE_DIM
    for hd in range(B_HEADS):
        blk = qb[hd * qdim:(hd + 1) * qdim]
        rp = _axial_rope_rows(blk[B_NOPE_DIM:qdim], B_ROPE_DIM // 4, tb)
        full = jnp.concatenate([blk[0:B_NOPE_DIM], rp, z32], axis=0)
        q_out[0, A_HEADS + hd] = full.astype(_BF16)

    ckv = _rms_rows(pt[CKV_OFF:KR_OFF], ckvn_ref[...]).astype(_BF16)
    kv = jnp.dot(wukv_ref[...], ckv, preferred_element_type=_F32)
    kr = _axial_rope_rows(pt[KR_OFF:GATE_OFF], B_ROPE_DIM // 4, tb)
    kvdim = B_NOPE_DIM + B_V_DIM
    v_base = A_KV_HEADS * A_HEAD_DIM
    for hd in range(B_HEADS):
        kn_h = kv[hd * kvdim:hd * kvdim + B_NOPE_DIM]
        v_h = kv[hd * kvdim + B_NOPE_DIM:(hd + 1) * kvdim]
        v_out[0, v_base + hd * B_V_DIM:v_base + (hd + 1) * B_V_DIM, :] = v_h.astype(_BF16)
        k_out[0, 1 + hd] = jnp.concatenate([kn_h, kr, z32], axis=0).T.astype(_BF16)


def _project(mod, ctx, x, w1t, wuqt, wukvt, qn, kn, cqn, ckvn, tab_a, tab_b):
    bsz, s, d = x.shape
    t = PROJ_TILE
    n_steps = (CTX_LEN + s) // t
    total = CTX_LEN + s
    lat = lambda j: jnp.maximum(j - 1, 0)
    full2 = lambda shape: pl.BlockSpec(shape, lambda b, j: (0, 0))
    return pl.pallas_call(
        _proj_kernel,
        grid=(bsz, n_steps),
        in_specs=[
            full2(mod.shape),
            pl.BlockSpec((1, CTX_LEN, d), lambda b, j: (b, 0, 0)),
            pl.BlockSpec((1, t, d), lambda b, j: (b, lat(j), 0)),
            full2(w1t.shape), full2(wuqt.shape), full2(wukvt.shape),
            full2(qn.shape), full2(kn.shape), full2(cqn.shape), full2(ckvn.shape),
            pl.BlockSpec((4, tab_a.shape[1], t), lambda b, j: (0, 0, j)),
            pl.BlockSpec((4, tab_b.shape[1], t), lambda b, j: (0, 0, j)),
        ],
        out_specs=[
            pl.BlockSpec((1, N_HEADS, HEAD_PAD, t), lambda b, j: (b, 0, 0, lat(j))),
            pl.BlockSpec((1, N_KSETS, t, HEAD_PAD), lambda b, j: (b, 0, j, 0)),
            pl.BlockSpec((1, V_ROWS, t), lambda b, j: (b, 0, j)),
        ],
        out_shape=[
            jax.ShapeDtypeStruct((bsz, N_HEADS, HEAD_PAD, s), _BF16),
            jax.ShapeDtypeStruct((bsz, N_KSETS, total, HEAD_PAD), _BF16),
            jax.ShapeDtypeStruct((bsz, V_ROWS, total), _BF16),
        ],
        compiler_params=pltpu.CompilerParams(
            dimension_semantics=("arbitrary", "arbitrary"), vmem_limit_bytes=VMEM_LIMIT),
        name="proj",
    )(mod, ctx, x, w1t, wuqt, wukvt, qn, kn, cqn, ckvn, tab_a, tab_b)


def _attn_kernel(base_ref, q_ref, k_ref, v_ref, o_ref, s0_ref, s1_ref):
    hd = pl.program_id(1)
    c2 = jnp.where(hd < A_HEADS, A_SCALE * LOG2_E, B_SCALE * LOG2_E).astype(_F32)
    tq, kc = ATTN_Q_TILE, ATTN_KEY_CHUNK
    n_tiles = q_ref.shape[3] // tq
    n_chunks = k_ref.shape[2] // kc
    bufs = (s0_ref, s1_ref)
    base = base_ref[0]

    def buf_rows(c):
        return pl.ds(pl.multiple_of(base + c * kc, kc), kc)

    def score_chunk(t, c, m):
        rows = pl.ds(c * kc, kc)
        s = jnp.dot(k_ref[0, 0, rows, :], q_ref[0, 0, :, pl.ds(t * tq, tq)],
                    preferred_element_type=_F32)
        bufs[t % 2][buf_rows(c), :] = s
        cm = jnp.max(s, axis=0, keepdims=True)
        return cm if m is None else jnp.maximum(m, cm)

    def value_chunk(t, c, m, l, acc):
        rows = pl.ds(c * kc, kc)
        p = jnp.exp2((bufs[t % 2][buf_rows(c), :] - m) * c2)
        ps = jnp.sum(p, axis=0, keepdims=True)
        pv = jnp.dot(v_ref[0, :, rows], p.astype(_BF16), preferred_element_type=_F32)
        return (ps, pv) if l is None else (l + ps, acc + pv)

    m_cur = None
    for c in range(n_chunks):
        m_cur = score_chunk(0, c, m_cur)
    for t in range(n_tiles):
        m_next = l = acc = None
        for c in range(n_chunks):
            if t + 1 < n_tiles:
                m_next = score_chunk(t + 1, c, m_next)
            l, acc = value_chunk(t, c, m_cur, l, acc)
        o_ref[0, :, pl.ds(t * tq, tq)] = (acc / l).astype(_BF16)
        m_cur = m_next


def _attention(q_all, k_all, v_all):
    bsz, _, _, s = q_all.shape
    total = k_all.shape[2]
    group = A_HEADS // A_KV_HEADS
    kset = lambda h: jnp.where(h < A_HEADS, 0, h - A_HEADS + 1)
    vblk = lambda h: jnp.where(h < A_HEADS, h // group, h - A_HEADS + A_KV_HEADS)
    return pl.pallas_call(
        _attn_kernel,
        grid=(bsz, N_HEADS),
        in_specs=[
            pl.BlockSpec(memory_space=pltpu.SMEM),
            pl.BlockSpec((1, 1, HEAD_PAD, s), lambda b, h: (b, h, 0, 0)),
            pl.BlockSpec((1, 1, total, HEAD_PAD), lambda b, h: (b, kset(h), 0, 0)),
            pl.BlockSpec((1, B_V_DIM, total), lambda b, h: (b, vblk(h), 0)),
        ],
        out_specs=pl.BlockSpec((1, B_V_DIM, s), lambda b, h: (b, h, 0)),
        out_shape=jax.ShapeDtypeStruct((bsz, N_HEADS * B_V_DIM, s), _BF16),
        scratch_shapes=[pltpu.VMEM((total, ATTN_Q_TILE), _F32),
                        pltpu.VMEM((total, ATTN_Q_TILE), _F32)],
        compiler_params=pltpu.CompilerParams(
            dimension_semantics=("arbitrary", "arbitrary"), vmem_limit_bytes=VMEM_LIMIT),
        name="attn",
    )(jnp.zeros((1,), jnp.int32), q_all, k_all, v_all)


def _merge_tile(mrow, x, ot, wg_ref, wpa_ref, wpb_ref, wo_ref, g_ref, b_ref):
    shift, scale, gate = mrow[:, 0:D_MODEL], mrow[:, D_MODEL:2 * D_MODEL], mrow[:, 2 * D_MODEL:3 * D_MODEL]
    h = (_layer_norm(x) * (1.0 + scale) + shift).astype(_BF16)
    gt = lax.dot_general(wg_ref[...], h, _NT, preferred_element_type=_F32)
    na = A_HEADS * A_HEAD_DIM
    ya = jnp.dot(wpa_ref[...], ot[0:na], preferred_element_type=_F32)
    yb = jnp.dot(wpb_ref[...], ot[na:], preferred_element_type=_F32)
    y = _sigmoid(gt[0:D_MODEL]) * ya + _sigmoid(gt[D_MODEL:]) * yb
    zt = jnp.dot(wo_ref[...], y.astype(_BF16), preferred_element_type=_F32)
    r = DEEPNORM_ALPHA * x + gate * zt.T
    return _layer_norm(r) * g_ref[...] + b_ref[...]


def _ffn_tile(mrow, x, wup_ref, wdn_ref, g_ref, b_ref):
    shift, scale, gate = mrow[:, 0:D_MODEL], mrow[:, D_MODEL:2 * D_MODEL], mrow[:, 2 * D_MODEL:3 * D_MODEL]
    h = (_layer_norm(x) * (1.0 + scale) + shift).astype(_BF16)
    au = jnp.dot(h, wup_ref[...], preferred_element_type=_F32)
    a, u = au[:, 0:FFN_HIDDEN], au[:, FFN_HIDDEN:]
    f = (a * _sigmoid(a) * u).astype(_BF16)
    dn = jnp.dot(f, wdn_ref[...], preferred_element_type=_F32)
    r = DEEPNORM_ALPHA * x + gate * dn
    return _layer_norm(r) * g_ref[...] + b_ref[...]


def _post_kernel(mod_ref, x_ref, ot_ref, wg_ref, wpa_ref, wpb_ref, wo_ref, g1_ref, b1_ref,
                 wup_ref, wdn_ref, g2_ref, b2_ref, out_ref, x1_ref, *, tiles_per_batch):
    j = pl.program_id(0)
    n_tiles = pl.num_programs(0) - 1

    @pl.when(j == 0)
    def _():
        x1_ref[...] = jnp.zeros_like(x1_ref)

    b_prev = jnp.maximum(j - 1, 0) // tiles_per_batch
    b_cur = jnp.minimum(j, n_tiles - 1) // tiles_per_batch
    out_ref[0] = _ffn_tile(mod_ref[pl.ds(b_prev, 1), 3 * D_MODEL:6 * D_MODEL], x1_ref[...],
                           wup_ref, wdn_ref, g2_ref, b2_ref)
    x1_ref[...] = _merge_tile(mod_ref[pl.ds(b_cur, 1), 0:3 * D_MODEL], x_ref[0], ot_ref[0],
                              wg_ref, wpa_ref, wpb_ref, wo_ref, g1_ref, b1_ref)


def _post(mod, x, ot, wgt, wpat, wpbt, wot, g1, b1, wup, wdn, g2, b2):
    bsz, s, d = x.shape
    t = TOKEN_TILE
    tpb = s // t
    n_tiles = bsz * tpb
    const = lambda a: pl.BlockSpec(a.shape, lambda j: (0,) * a.ndim, pipeline_mode=pl.Buffered(1))
    cur = lambda j: jnp.minimum(j, n_tiles - 1)
    prev = lambda j: jnp.maximum(j - 1, 0)
    return pl.pallas_call(
        functools.partial(_post_kernel, tiles_per_batch=tpb),
        grid=(n_tiles + 1,),
        in_specs=[
            const(mod),
            pl.BlockSpec((1, t, d), lambda j: (cur(j) // tpb, cur(j) % tpb, 0)),
            pl.BlockSpec((1, ot.shape[1], t), lambda j: (cur(j) // tpb, 0, cur(j) % tpb)),
            const(wgt), const(wpat), const(wpbt), const(wot), const(g1), const(b1),
            const(wup), const(wdn), const(g2), const(b2),
        ],
        out_specs=pl.BlockSpec((1, t, d), lambda j: (prev(j) // tpb, prev(j) % tpb, 0)),
        out_shape=jax.ShapeDtypeStruct((bsz, s, d), _F32),
        scratch_shapes=[pltpu.VMEM((t, d), _F32)],
        compiler_params=pltpu.CompilerParams(
            dimension_semantics=("arbitrary",), vmem_limit_bytes=VMEM_LIMIT),
        name="post",
    )(mod, x, ot, wgt, wpat, wpbt, wot, g1, b1, wup, wdn, g2, b2)


def _rope_tables(seq, half):
    pos = np.arange(seq)
    freqs = ROPE_THETA ** (-np.arange(half, dtype=np.float64) / half)
    tabs = []
    for p in (pos // GRID_W, pos % GRID_W):
        ang = p[None, :].astype(np.float64) * freqs[:, None]
        for fn, ident in ((np.cos, 1.0), (np.sin, 0.0)):
            tabs.append(np.concatenate([np.full((half, CTX_LEN), ident), fn(ang)], axis=1))
    return jnp.asarray(np.stack(tabs), dtype=_F32)


def kernel(x, c, ctx, c_ctx, w_mod, b_mod, w_in, q_norm_a, k_norm_a, cq_norm, ckv_norm,
           w_uq, w_ukv, w_proj_a, w_proj_b, w_out, ln1_g, ln1_b, w_up, w_down, ln2_g, ln2_b):
    bsz, s, d = x.shape
    assert w_mod.shape[0] == DEPTH == 1 and d == D_MODEL and ctx.shape[1] == CTX_LEN
    assert bsz == 8 and s % ATTN_Q_TILE == 0 and s % TOKEN_TILE == 0

    cc = jnp.concatenate([c, c_ctx[None, :], jnp.zeros((bsz - 1, d), _F32)], axis=0)
    mod = _modulation(cc, w_mod[0], b_mod[0][None, :])

    w_in0 = w_in[0]
    w1t = w_in0[:, :QKV_COLS].T.astype(_BF16)
    wgt = w_in0[:, QKV_COLS:].T.astype(_BF16)
    col = lambda v: v[0][:, None]
    tab_a = _rope_tables(s, A_HEAD_DIM // 4)
    tab_b = _rope_tables(s, B_ROPE_DIM // 4)

    q_all, k_all, v_all = _project(
        mod, ctx, x, w1t, w_uq[0].T.astype(_BF16), w_ukv[0].T.astype(_BF16),
        col(q_norm_a), col(k_norm_a), col(cq_norm), col(ckv_norm), tab_a, tab_b)
    ot = _attention(q_all, k_all, v_all)
    return _post(mod, x, ot, wgt, w_proj_a[0].T.astype(_BF16), w_proj_b[0].T.astype(_BF16),
                 w_out[0].T.astype(_BF16), ln1_g, ln1_b,
                 w_up[0].astype(_BF16), w_down[0].astype(_BF16), ln2_g, ln2_b)
```

```python
import functools
import math

import numpy as np
import jax
import jax.numpy as jnp
from jax import lax
from jax.experimental import pallas as pl
from jax.experimental.pallas import tpu as pltpu

D_MODEL = 1024
GRID_W = 64
CTX_LEN = 256
ROPE_THETA = 10000.0
EPS = 1e-6

A_HEADS = 8
A_KV_HEADS = 2
A_HEAD_DIM = 64
B_HEADS = 8
B_Q_RANK = 384
B_KV_RANK = 256
B_NOPE_DIM = 64
B_ROPE_DIM = 32
B_V_DIM = 64
FFN_HIDDEN = 2816
DEPTH = 1

A_SCALE = A_HEAD_DIM ** -0.5
B_SCALE = (B_NOPE_DIM + B_ROPE_DIM) ** -0.5
DEEPNORM_ALPHA = (2.0 * DEPTH) ** 0.25
LOG2_E = math.log2(math.e)

QA_OFF = 0
KA_OFF = QA_OFF + A_HEADS * A_HEAD_DIM
VA_OFF = KA_OFF + A_KV_HEADS * A_HEAD_DIM
CQ_OFF = VA_OFF + A_KV_HEADS * A_HEAD_DIM
CKV_OFF = CQ_OFF + B_Q_RANK
KR_OFF = CKV_OFF + B_KV_RANK
GATE_OFF = KR_OFF + B_ROPE_DIM
QKV_COLS = GATE_OFF

N_HEADS = A_HEADS + B_HEADS
HEAD_PAD = 128
N_KSETS = 1 + B_HEADS
V_ROWS = A_KV_HEADS * A_HEAD_DIM + B_HEADS * B_V_DIM

PROJ_TILE = 256
ATTN_Q_TILE = 512
ATTN_KEY_CHUNK = 256
ATTN_ONES_ROWS = 16
TOKEN_TILE = 256
MOD_COL_TILE = 1024
VMEM_LIMIT = 56 * 1024 * 1024

_NT = (((1,), (1,)), ((), ()))
_F32 = jnp.float32
_BF16 = jnp.bfloat16


def _layer_norm(x):
    mu = jnp.mean(x, axis=-1, keepdims=True)
    xc = x - mu
    var = jnp.mean(xc * xc, axis=-1, keepdims=True)
    return xc * lax.rsqrt(var + EPS)


def _sigmoid(x):
    return jax.nn.sigmoid(x)


def _mod_kernel(c_ref, w_ref, b_ref, o_ref):
    c = c_ref[...]
    a = (c * _sigmoid(c)).astype(_BF16)
    o_ref[...] = jnp.dot(a, w_ref[...].astype(_BF16), preferred_element_type=_F32) + b_ref[...]


def _modulation(cc, w_mod, b_mod):
    rows, d = cc.shape
    n = w_mod.shape[1]
    return pl.pallas_call(
        _mod_kernel,
        grid=(n // MOD_COL_TILE,),
        in_specs=[
            pl.BlockSpec((rows, d), lambda i: (0, 0)),
            pl.BlockSpec((d, MOD_COL_TILE), lambda i: (0, i)),
            pl.BlockSpec((1, MOD_COL_TILE), lambda i: (0, i)),
        ],
        out_specs=pl.BlockSpec((rows, MOD_COL_TILE), lambda i: (0, i)),
        out_shape=jax.ShapeDtypeStruct((rows, n), _F32),
        compiler_params=pltpu.CompilerParams(
            dimension_semantics=("arbitrary",), vmem_limit_bytes=VMEM_LIMIT),
        name="mod",
    )(cc, w_mod, b_mod)


def _rms_rows(x, g):
    ms = jnp.mean(x * x, axis=0, keepdims=True)
    return x * lax.rsqrt(ms + EPS) * g


def _axial_rope_rows(x, half, tab):
    r1, r2 = x[0:half], x[half:2 * half]
    c1, c2 = x[2 * half:3 * half], x[3 * half:4 * half]
    cr, sr, cc, sc = tab[0], tab[1], tab[2], tab[3]
    return jnp.concatenate(
        [r1 * cr - r2 * sr, r1 * sr + r2 * cr, c1 * cc - c2 * sc, c1 * sc + c2 * cc], axis=0)


def _proj_kernel(mod_ref, ctx_ref, x_ref, w1_ref, wuq_ref, wukv_ref,
                 qn_ref, kn_ref, cqn_ref, ckvn_ref, ta_ref, tb_ref,
                 q_out, k_out, v_out):
    b = pl.program_id(0)
    j = pl.program_id(1)
    is_ctx = j == 0
    xin = jnp.where(is_ctx, ctx_ref[0], x_ref[0])
    ctx_row = mod_ref.shape[0] // 2
    row = jnp.where(is_ctx, mod_ref[ctx_row:ctx_row + 1, 0:2 * D_MODEL],
                    mod_ref[pl.ds(b, 1), 0:2 * D_MODEL])
    shift, scale = row[:, 0:D_MODEL], row[:, D_MODEL:2 * D_MODEL]
    h = (_layer_norm(xin) * (1.0 + scale) + shift).astype(_BF16)
    pt = lax.dot_general(w1_ref[...], h, _NT, preferred_element_type=_F32)
    t = pt.shape[1]
    ta = ta_ref[...]
    tb = tb_ref[...]
    z64 = jnp.zeros((A_HEAD_DIM, t), _F32)
    z32 = jnp.zeros((HEAD_PAD - B_NOPE_DIM - B_ROPE_DIM, t), _F32)

    qn = qn_ref[...] * (A_SCALE * LOG2_E)
    group = A_HEADS // A_KV_HEADS
    for hd in range(A_HEADS):
        xh = pt[QA_OFF + hd * A_HEAD_DIM:QA_OFF + (hd + 1) * A_HEAD_DIM]
        xh = _axial_rope_rows(_rms_rows(xh, qn), A_HEAD_DIM // 4, ta)
        full = jnp.concatenate([xh, z64] if hd // group == 0 else [z64, xh], axis=0)
        q_out[0, hd] = full.astype(_BF16)

    kn = kn_ref[...]
    ka = [
        _axial_rope_rows(
            _rms_rows(pt[KA_OFF + g * A_HEAD_DIM:KA_OFF + (g + 1) * A_HEAD_DIM], kn),
            A_HEAD_DIM // 4, ta)
        for g in range(A_KV_HEADS)
    ]
    k_out[0, 0] = jnp.concatenate(ka, axis=0).T.astype(_BF16)
    v_out[0, 0:A_KV_HEADS * A_HEAD_DIM, :] = pt[VA_OFF:CQ_OFF].astype(_BF16)

    cq = _rms_rows(pt[CQ_OFF:CKV_OFF], cqn_ref[...]).astype(_BF16)
    qb = jnp.dot(wuq_ref[...], cq, preferred_element_type=_F32) * (B_SCALE * LOG2_E)
    qdim = B_NOPE_DIM + B_ROPE_DIM
    for hd in range(B_HEADS):
        blk = qb[hd * qdim:(hd + 1) * qdim]
        rp = _axial_rope_rows(blk[B_NOPE_DIM:qdim], B_ROPE_DIM // 4, tb)
        full = jnp.concatenate([blk[0:B_NOPE_DIM], rp, z32], axis=0)
        q_out[0, A_HEADS + hd] = full.astype(_BF16)

    ckv = _rms_rows(pt[CKV_OFF:KR_OFF], ckvn_ref[...]).astype(_BF16)
    kv = jnp.dot(wukv_ref[...], ckv, preferred_element_type=_F32)
    kr = _axial_rope_rows(pt[KR_OFF:GATE_OFF], B_ROPE_DIM // 4, tb)
    kvdim = B_NOPE_DIM + B_V_DIM
    v_base = A_KV_HEADS * A_HEAD_DIM
    for hd in range(B_HEADS):
        kn_h = kv[hd * kvdim:hd * kvdim + B_NOPE_DIM]
        v_h = kv[hd * kvdim + B_NOPE_DIM:(hd + 1) * kvdim]
        v_out[0, v_base + hd * B_V_DIM:v_base + (hd + 1) * B_V_DIM, :] = v_h.astype(_BF16)
        k_out[0, 1 + hd] = jnp.concatenate([kn_h, kr, z32], axis=0).T.astype(_BF16)


def _project(mod, ctx, x, w1t, wuqt, wukvt, qn, kn, cqn, ckvn, tab_a, tab_b):
    bsz, s, d = x.shape
    t = PROJ_TILE
    n_steps = (CTX_LEN + s) // t
    total = CTX_LEN + s
    lat = lambda j: jnp.maximum(j - 1, 0)
    full2 = lambda shape: pl.BlockSpec(shape, lambda b, j: (0, 0))
    return pl.pallas_call(
        _proj_kernel,
        grid=(bsz, n_steps),
        in_specs=[
            full2(mod.shape),
            pl.BlockSpec((1, CTX_LEN, d), lambda b, j: (b, 0, 0)),
            pl.BlockSpec((1, t, d), lambda b, j: (b, lat(j), 0)),
            full2(w1t.shape), full2(wuqt.shape), full2(wukvt.shape),
            full2(qn.shape), full2(kn.shape), full2(cqn.shape), full2(ckvn.shape),
            pl.BlockSpec((4, tab_a.shape[1], t), lambda b, j: (0, 0, j)),
            pl.BlockSpec((4, tab_b.shape[1], t), lambda b, j: (0, 0, j)),
        ],
        out_specs=[
            pl.BlockSpec((1, N_HEADS, HEAD_PAD, t), lambda b, j: (b, 0, 0, lat(j))),
            pl.BlockSpec((1, N_KSETS, t, HEAD_PAD), lambda b, j: (b, 0, j, 0)),
            pl.BlockSpec((1, V_ROWS, t), lambda b, j: (b, 0, j)),
        ],
        out_shape=[
            jax.ShapeDtypeStruct((bsz, N_HEADS, HEAD_PAD, s), _BF16),
            jax.ShapeDtypeStruct((bsz, N_KSETS, total, HEAD_PAD), _BF16),
            jax.ShapeDtypeStruct((bsz, V_ROWS, total), _BF16),
        ],
        compiler_params=pltpu.CompilerParams(
            dimension_semantics=("arbitrary", "arbitrary"), vmem_limit_bytes=VMEM_LIMIT),
        name="proj",
    )(mod, ctx, x, w1t, wuqt, wukvt, qn, kn, cqn, ckvn, tab_a, tab_b)


def _attn_kernel(base_ref, q_ref, k_ref, v_ref, qn_ref, kn_ref, o_ref, s0_ref, s1_ref, m0_ref):
    tq, kc = ATTN_Q_TILE, ATTN_KEY_CHUNK
    n_tiles = q_ref.shape[3] // tq
    n_chunks = k_ref.shape[2] // kc
    assert n_tiles % 2 == 0
    bufs = (s0_ref, s1_ref)
    dv = v_ref.shape[1]
    base = base_ref[0]
    ones = jnp.ones((ATTN_ONES_ROWS, kc), _BF16)

    def buf_rows(c):
        return pl.ds(pl.multiple_of(base + c * kc, kc), kc)

    def score_chunk(keys_ref, q_tile, buf, c, m):
        s = jnp.dot(keys_ref[0, 0, pl.ds(c * kc, kc), :], q_tile,
                    preferred_element_type=_F32)
        buf[buf_rows(c), :] = s
        cm = jnp.max(s, axis=0, keepdims=True)
        return cm if m is None else jnp.maximum(m, cm)

    def value_chunk(buf, c, m, acc):
        p = jnp.exp2(buf[buf_rows(c), :] - m)
        vext = jnp.concatenate([v_ref[0, :, pl.ds(c * kc, kc)], ones], axis=0)
        pv = jnp.dot(vext, p.astype(_BF16), preferred_element_type=_F32)
        return pv if acc is None else acc + pv

    @pl.when((pl.program_id(0) == 0) & (pl.program_id(1) == 0))
    def _():
        m = None
        for c in range(n_chunks):
            m = score_chunk(k_ref, q_ref[0, 0, :, 0:tq], bufs[0], c, m)
        m0_ref[...] = m

    m_cur = m0_ref[...]
    for t in range(n_tiles):
        m_next = acc = None
        for c in range(n_chunks):
            if t + 1 < n_tiles:
                m_next = score_chunk(k_ref, q_ref[0, 0, :, pl.ds((t + 1) * tq, tq)],
                                     bufs[(t + 1) % 2], c, m_next)
            else:
                m_next = score_chunk(kn_ref, qn_ref[0, 0], bufs[0], c, m_next)
            acc = value_chunk(bufs[t % 2], c, m_cur, acc)
        o_ref[0, :, pl.ds(t * tq, tq)] = (acc[0:dv] / acc[dv:dv + 1]).astype(_BF16)
        m_cur = m_next
    m0_ref[...] = m_cur


def _attention(q_all, k_all, v_all):
    bsz, _, _, s = q_all.shape
    total = k_all.shape[2]
    group = A_HEADS // A_KV_HEADS
    kset = lambda h: jnp.where(h < A_HEADS, 0, h - A_HEADS + 1)
    vblk = lambda h: jnp.where(h < A_HEADS, h // group, h - A_HEADS + A_KV_HEADS)

    def nxt(b, h):
        g = jnp.minimum(b * N_HEADS + h + 1, bsz * N_HEADS - 1)
        return g // N_HEADS, g % N_HEADS

    return pl.pallas_call(
        _attn_kernel,
        grid=(bsz, N_HEADS),
        in_specs=[
            pl.BlockSpec(memory_space=pltpu.SMEM),
            pl.BlockSpec((1, 1, HEAD_PAD, s), lambda b, h: (b, h, 0, 0)),
            pl.BlockSpec((1, 1, total, HEAD_PAD), lambda b, h: (b, kset(h), 0, 0)),
            pl.BlockSpec((1, B_V_DIM, total), lambda b, h: (b, vblk(h), 0)),
            pl.BlockSpec((1, 1, HEAD_PAD, ATTN_Q_TILE), lambda b, h: (*nxt(b, h), 0, 0)),
            pl.BlockSpec((1, 1, total, HEAD_PAD),
                         lambda b, h: (nxt(b, h)[0], kset(nxt(b, h)[1]), 0, 0)),
        ],
        out_specs=pl.BlockSpec((1, B_V_DIM, s), lambda b, h: (b, h, 0)),
        out_shape=jax.ShapeDtypeStruct((bsz, N_HEADS * B_V_DIM, s), _BF16),
        scratch_shapes=[pltpu.VMEM((total, ATTN_Q_TILE), _F32),
                        pltpu.VMEM((total, ATTN_Q_TILE), _F32),
                        pltpu.VMEM((1, ATTN_Q_TILE), _F32)],
        compiler_params=pltpu.CompilerParams(
            dimension_semantics=("arbitrary", "arbitrary"), vmem_limit_bytes=VMEM_LIMIT),
        name="attn",
    )(jnp.zeros((1,), jnp.int32), q_all, k_all, v_all, q_all, k_all)


def _merge_tile(mrow, x, ot, wg_ref, wpa_ref, wpb_ref, wo_ref, g_ref, b_ref):
    shift, scale, gate = mrow[:, 0:D_MODEL], mrow[:, D_MODEL:2 * D_MODEL], mrow[:, 2 * D_MODEL:3 * D_MODEL]
    h = (_layer_norm(x) * (1.0 + scale) + shift).astype(_BF16)
    gt = lax.dot_general(wg_ref[...], h, _NT, preferred_element_type=_F32)
    na = A_HEADS * A_HEAD_DIM
    ya = jnp.dot(wpa_ref[...], ot[0:na], preferred_element_type=_F32)
    yb = jnp.dot(wpb_ref[...], ot[na:], preferred_element_type=_F32)
    y = _sigmoid(gt[0:D_MODEL]) * ya + _sigmoid(gt[D_MODEL:]) * yb
    zt = jnp.dot(wo_ref[...], y.astype(_BF16), preferred_element_type=_F32)
    r = DEEPNORM_ALPHA * x + gate * zt.T
    return _layer_norm(r) * g_ref[...] + b_ref[...]


def _ffn_tile(mrow, x, wup_ref, wdn_ref, g_ref, b_ref):
    shift, scale, gate = mrow[:, 0:D_MODEL], mrow[:, D_MODEL:2 * D_MODEL], mrow[:, 2 * D_MODEL:3 * D_MODEL]
    h = (_layer_norm(x) * (1.0 + scale) + shift).astype(_BF16)
    au = jnp.dot(h, wup_ref[...], preferred_element_type=_F32)
    a, u = au[:, 0:FFN_HIDDEN], au[:, FFN_HIDDEN:]
    f = (a * _sigmoid(a) * u).astype(_BF16)
    dn = jnp.dot(f, wdn_ref[...], preferred_element_type=_F32)
    r = DEEPNORM_ALPHA * x + gate * dn
    return _layer_norm(r) * g_ref[...] + b_ref[...]


def _post_kernel(mod_ref, x_ref, ot_ref, wg_ref, wpa_ref, wpb_ref, wo_ref, g1_ref, b1_ref,
                 wup_ref, wdn_ref, g2_ref, b2_ref, out_ref, x1_ref, *, tiles_per_batch):
    j = pl.program_id(0)
    n_tiles = pl.num_programs(0) - 1

    @pl.when(j == 0)
    def _():
        x1_ref[...] = jnp.zeros_like(x1_ref)

    b_prev = jnp.maximum(j - 1, 0) // tiles_per_batch
    b_cur = jnp.minimum(j, n_tiles - 1) // tiles_per_batch
    out_ref[0] = _ffn_tile(mod_ref[pl.ds(b_prev, 1), 3 * D_MODEL:6 * D_MODEL], x1_ref[...],
                           wup_ref, wdn_ref, g2_ref, b2_ref)
    x1_ref[...] = _merge_tile(mod_ref[pl.ds(b_cur, 1), 0:3 * D_MODEL], x_ref[0], ot_ref[0],
                              wg_ref, wpa_ref, wpb_ref, wo_ref, g1_ref, b1_ref)


def _post(mod, x, ot, wgt, wpat, wpbt, wot, g1, b1, wup, wdn, g2, b2):
    bsz, s, d = x.shape
    t = TOKEN_TILE
    tpb = s // t
    n_tiles = bsz * tpb
    const = lambda a: pl.BlockSpec(a.shape, lambda j: (0,) * a.ndim, pipeline_mode=pl.Buffered(1))
    cur = lambda j: jnp.minimum(j, n_tiles - 1)
    prev = lambda j: jnp.maximum(j - 1, 0)
    return pl.pallas_call(
        functools.partial(_post_kernel, tiles_per_batch=tpb),
        grid=(n_tiles + 1,),
        in_specs=[
            const(mod),
            pl.BlockSpec((1, t, d), lambda j: (cur(j) // tpb, cur(j) % tpb, 0)),
            pl.BlockSpec((1, ot.shape[1], t), lambda j: (cur(j) // tpb, 0, cur(j) % tpb)),
            const(wgt), const(wpat), const(wpbt), const(wot), const(g1), const(b1),
            const(wup), const(wdn), const(g2), const(b2),
        ],
        out_specs=pl.BlockSpec((1, t, d), lambda j: (prev(j) // tpb, prev(j) % tpb, 0)),
        out_shape=jax.ShapeDtypeStruct((bsz, s, d), _F32),
        scratch_shapes=[pltpu.VMEM((t, d), _F32)],
        compiler_params=pltpu.CompilerParams(
            dimension_semantics=("arbitrary",), vmem_limit_bytes=VMEM_LIMIT),
        name="post",
    )(mod, x, ot, wgt, wpat, wpbt, wot, g1, b1, wup, wdn, g2, b2)


def _rope_tables(seq, half):
    pos = np.arange(seq)
    freqs = ROPE_THETA ** (-np.arange(half, dtype=np.float64) / half)
    tabs = []
    for p in (pos // GRID_W, pos % GRID_W):
        ang = p[None, :].astype(np.float64) * freqs[:, None]
        for fn, ident in ((np.cos, 1.0), (np.sin, 0.0)):
            tabs.append(np.concatenate([np.full((half, CTX_LEN), ident), fn(ang)], axis=1))
    return jnp.asarray(np.stack(tabs), dtype=_F32)


def kernel(x, c, ctx, c_ctx, w_mod, b_mod, w_in, q_norm_a, k_norm_a, cq_norm, ckv_norm,
           w_uq, w_ukv, w_proj_a, w_proj_b, w_out, ln1_g, ln1_b, w_up, w_down, ln2_g, ln2_b):
    bsz, s, d = x.shape
    assert w_mod.shape[0] == DEPTH == 1 and d == D_MODEL and ctx.shape[1] == CTX_LEN
    assert bsz == 8 and s % ATTN_Q_TILE == 0 and s % TOKEN_TILE == 0

    cc = jnp.concatenate([c, c_ctx[None, :], jnp.zeros((bsz - 1, d), _F32)], axis=0)
    mod = _modulation(cc, w_mod[0], b_mod[0][None, :])

    w_in0 = w_in[0]
    w1t = w_in0[:, :QKV_COLS].T.astype(_BF16)
    wgt = w_in0[:, QKV_COLS:].T.astype(_BF16)
    col = lambda v: v[0][:, None]
    tab_a = _rope_tables(s, A_HEAD_DIM // 4)
    tab_b = _rope_tables(s, B_ROPE_DIM // 4)

    q_all, k_all, v_all = _project(
        mod, ctx, x, w1t, w_uq[0].T.astype(_BF16), w_ukv[0].T.astype(_BF16),
        col(q_norm_a), col(k_norm_a), col(cq_norm), col(ckv_norm), tab_a, tab_b)
    ot = _attention(q_all, k_all, v_all)
    return _post(mod, x, ot, wgt, w_proj_a[0].T.astype(_BF16), w_proj_b[0].T.astype(_BF16),
                 w_out[0].T.astype(_BF16), ln1_g, ln1_b,
                 w_up[0].astype(_BF16), w_down[0].astype(_BF16), ln2_g, ln2_b)
```

```python
import functools
import math

import numpy as np
import jax
import jax.numpy as jnp
from jax import lax
from jax.experimental import pallas as pl
from jax.experimental.pallas import tpu as pltpu

D_MODEL = 1024
GRID_W = 64
CTX_LEN = 256
ROPE_THETA = 10000.0
EPS = 1e-6

A_HEADS = 8
A_KV_HEADS = 2
A_HEAD_DIM = 64
B_HEADS = 8
B_Q_RANK = 384
B_KV_RANK = 256
B_NOPE_DIM = 64
B_ROPE_DIM = 32
B_V_DIM = 64
FFN_HIDDEN = 2816
DEPTH = 1

A_SCALE = A_HEAD_DIM ** -0.5
B_SCALE = (B_NOPE_DIM + B_ROPE_DIM) ** -0.5
DEEPNORM_ALPHA = (2.0 * DEPTH) ** 0.25
LOG2_E = math.log2(math.e)

QA_OFF = 0
KA_OFF = QA_OFF + A_HEADS * A_HEAD_DIM
VA_OFF = KA_OFF + A_KV_HEADS * A_HEAD_DIM
CQ_OFF = VA_OFF + A_KV_HEADS * A_HEAD_DIM
CKV_OFF = CQ_OFF + B_Q_RANK
KR_OFF = CKV_OFF + B_KV_RANK
GATE_OFF = KR_OFF + B_ROPE_DIM
QKV_COLS = GATE_OFF

N_HEADS = A_HEADS + B_HEADS
HEAD_PAD = 128
N_KSETS = 1 + B_HEADS
V_ROWS = A_KV_HEADS * A_HEAD_DIM + B_HEADS * B_V_DIM

PROJ_TILE = 256
ATTN_Q_TILE = 512
ATTN_KEY_CHUNK = 256
ATTN_ONES_ROWS = 16
TOKEN_TILE = 256
MOD_COL_TILE = 1024
MERGE_ROW_CHUNK = 256
FFN_COL_CHUNK = 256
POST_FFN_PIECES_PER_MERGE_PIECE = 2
VMEM_LIMIT = 56 * 1024 * 1024

_NT = (((1,), (1,)), ((), ()))
_F32 = jnp.float32
_BF16 = jnp.bfloat16


def _layer_norm(x):
    mu = jnp.mean(x, axis=-1, keepdims=True)
    xc = x - mu
    var = jnp.mean(xc * xc, axis=-1, keepdims=True)
    return xc * lax.rsqrt(var + EPS)


def _sigmoid(x):
    return jax.nn.sigmoid(x)


def _mod_kernel(c_ref, w_ref, b_ref, o_ref):
    c = c_ref[...]
    a = (c * _sigmoid(c)).astype(_BF16)
    o_ref[...] = jnp.dot(a, w_ref[...].astype(_BF16), preferred_element_type=_F32) + b_ref[...]


def _modulation(cc, w_mod, b_mod):
    rows, d = cc.shape
    n = w_mod.shape[1]
    return pl.pallas_call(
        _mod_kernel,
        grid=(n // MOD_COL_TILE,),
        in_specs=[
            pl.BlockSpec((rows, d), lambda i: (0, 0)),
            pl.BlockSpec((d, MOD_COL_TILE), lambda i: (0, i)),
            pl.BlockSpec((1, MOD_COL_TILE), lambda i: (0, i)),
        ],
        out_specs=pl.BlockSpec((rows, MOD_COL_TILE), lambda i: (0, i)),
        out_shape=jax.ShapeDtypeStruct((rows, n), _F32),
        compiler_params=pltpu.CompilerParams(
            dimension_semantics=("arbitrary",), vmem_limit_bytes=VMEM_LIMIT),
        name="mod",
    )(cc, w_mod, b_mod)


def _rms_rows(x, g):
    ms = jnp.mean(x * x, axis=0, keepdims=True)
    return x * lax.rsqrt(ms + EPS) * g


def _axial_rope_rows(x, half, tab):
    r1, r2 = x[0:half], x[half:2 * half]
    c1, c2 = x[2 * half:3 * half], x[3 * half:4 * half]
    cr, sr, cc, sc = tab[0], tab[1], tab[2], tab[3]
    return jnp.concatenate(
        [r1 * cr - r2 * sr, r1 * sr + r2 * cr, c1 * cc - c2 * sc, c1 * sc + c2 * cc], axis=0)


def _proj_kernel(mod_ref, ctx_ref, x_ref, w1_ref, wuq_ref, wukv_ref,
                 qn_ref, kn_ref, cqn_ref, ckvn_ref, ta_ref, tb_ref,
                 q_out, k_out, v_out):
    b = pl.program_id(0)
    j = pl.program_id(1)
    is_ctx = j == 0
    xin = jnp.where(is_ctx, ctx_ref[0], x_ref[0])
    ctx_row = mod_ref.shape[0] // 2
    row = jnp.where(is_ctx, mod_ref[ctx_row:ctx_row + 1, 0:2 * D_MODEL],
                    mod_ref[pl.ds(b, 1), 0:2 * D_MODEL])
    shift, scale = row[:, 0:D_MODEL], row[:, D_MODEL:2 * D_MODEL]
    h = (_layer_norm(xin) * (1.0 + scale) + shift).astype(_BF16)
    pt = lax.dot_general(w1_ref[...], h, _NT, preferred_element_type=_F32)
    t = pt.shape[1]
    ta = ta_ref[...]
    tb = tb_ref[...]
    z64 = jnp.zeros((A_HEAD_DIM, t), _F32)
    z32 = jnp.zeros((HEAD_PAD - B_NOPE_DIM - B_ROPE_DIM, t), _F32)

    qn = qn_ref[...] * (A_SCALE * LOG2_E)
    group = A_HEADS // A_KV_HEADS
    for hd in range(A_HEADS):
        xh = pt[QA_OFF + hd * A_HEAD_DIM:QA_OFF + (hd + 1) * A_HEAD_DIM]
        xh = _axial_rope_rows(_rms_rows(xh, qn), A_HEAD_DIM // 4, ta)
        full = jnp.concatenate([xh, z64] if hd // group == 0 else [z64, xh], axis=0)
        q_out[0, hd] = full.astype(_BF16)

    kn = kn_ref[...]
    ka = [
        _axial_rope_rows(
            _rms_rows(pt[KA_OFF + g * A_HEAD_DIM:KA_OFF + (g + 1) * A_HEAD_DIM], kn),
            A_HEAD_DIM // 4, ta)
        for g in range(A_KV_HEADS)
    ]
    k_out[0, 0] = jnp.concatenate(ka, axis=0).T.astype(_BF16)
    v_out[0, 0:A_KV_HEADS * A_HEAD_DIM, :] = pt[VA_OFF:CQ_OFF].astype(_BF16)

    cq = _rms_rows(pt[CQ_OFF:CKV_OFF], cqn_ref[...]).astype(_BF16)
    qb = jnp.dot(wuq_ref[...], cq, preferred_element_type=_F32) * (B_SCALE * LOG2_E)
    qdim = B_NOPE_DIM + B_ROPE_DIM
    for hd in range(B_HEADS):
        blk = qb[hd * qdim:(hd + 1) * qdim]
        rp = _axial_rope_rows(blk[B_NOPE_DIM:qdim], B_ROPE_DIM // 4, tb)
        full = jnp.concatenate([blk[0:B_NOPE_DIM], rp, z32], axis=0)
        q_out[0, A_HEADS + hd] = full.astype(_BF16)

    ckv = _rms_rows(pt[CKV_OFF:KR_OFF], ckvn_ref[...]).astype(_BF16)
    kv = jnp.dot(wukv_ref[...], ckv, preferred_element_type=_F32)
    kr = _axial_rope_rows(pt[KR_OFF:GATE_OFF], B_ROPE_DIM // 4, tb)
    kvdim = B_NOPE_DIM + B_V_DIM
    v_base = A_KV_HEADS * A_HEAD_DIM
    for hd in range(B_HEADS):
        kn_h = kv[hd * kvdim:hd * kvdim + B_NOPE_DIM]
        v_h = kv[hd * kvdim + B_NOPE_DIM:(hd + 1) * kvdim]
        v_out[0, v_base + hd * B_V_DIM:v_base + (hd + 1) * B_V_DIM, :] = v_h.astype(_BF16)
        k_out[0, 1 + hd] = jnp.concatenate([kn_h, kr, z32], axis=0).T.astype(_BF16)


def _project(mod, ctx, x, w1t, wuqt, wukvt, qn, kn, cqn, ckvn, tab_a, tab_b):
    bsz, s, d = x.shape
    t = PROJ_TILE
    n_steps = (CTX_LEN + s) // t
    total = CTX_LEN + s
    lat = lambda j: jnp.maximum(j - 1, 0)
    full2 = lambda shape: pl.BlockSpec(shape, lambda b, j: (0, 0))
    return pl.pallas_call(
        _proj_kernel,
        grid=(bsz, n_steps),
        in_specs=[
            full2(mod.shape),
            pl.BlockSpec((1, CTX_LEN, d), lambda b, j: (b, 0, 0)),
            pl.BlockSpec((1, t, d), lambda b, j: (b, lat(j), 0)),
            full2(w1t.shape), full2(wuqt.shape), full2(wukvt.shape),
            full2(qn.shape), full2(kn.shape), full2(cqn.shape), full2(ckvn.shape),
            pl.BlockSpec((4, tab_a.shape[1], t), lambda b, j: (0, 0, j)),
            pl.BlockSpec((4, tab_b.shape[1], t), lambda b, j: (0, 0, j)),
        ],
        out_specs=[
            pl.BlockSpec((1, N_HEADS, HEAD_PAD, t), lambda b, j: (b, 0, 0, lat(j))),
            pl.BlockSpec((1, N_KSETS, t, HEAD_PAD), lambda b, j: (b, 0, j, 0)),
            pl.BlockSpec((1, V_ROWS, t), lambda b, j: (b, 0, j)),
        ],
        out_shape=[
            jax.ShapeDtypeStruct((bsz, N_HEADS, HEAD_PAD, s), _BF16),
            jax.ShapeDtypeStruct((bsz, N_KSETS, total, HEAD_PAD), _BF16),
            jax.ShapeDtypeStruct((bsz, V_ROWS, total), _BF16),
        ],
        compiler_params=pltpu.CompilerParams(
            dimension_semantics=("arbitrary", "arbitrary"), vmem_limit_bytes=VMEM_LIMIT),
        name="proj",
    )(mod, ctx, x, w1t, wuqt, wukvt, qn, kn, cqn, ckvn, tab_a, tab_b)


def _attn_kernel(base_ref, q_ref, k_ref, v_ref, qn_ref, kn_ref, o_ref, s0_ref, s1_ref, m0_ref):
    tq, kc = ATTN_Q_TILE, ATTN_KEY_CHUNK
    n_tiles = q_ref.shape[3] // tq
    n_chunks = k_ref.shape[2] // kc
    assert n_tiles % 2 == 0
    bufs = (s0_ref, s1_ref)
    dv = v_ref.shape[1]
    base = base_ref[0]
    ones = jnp.ones((ATTN_ONES_ROWS, kc), _BF16)

    def buf_rows(c):
        return pl.ds(pl.multiple_of(base + c * kc, kc), kc)

    def score_chunk(keys_ref, q_tile, buf, c, m):
        s = jnp.dot(keys_ref[0, 0, pl.ds(c * kc, kc), :], q_tile,
                    preferred_element_type=_F32)
        buf[buf_rows(c), :] = s
        cm = jnp.max(s, axis=0, keepdims=True)
        return cm if m is None else jnp.maximum(m, cm)

    def value_chunk(buf, c, m, acc):
        p = jnp.exp2(buf[buf_rows(c), :] - m)
        vext = jnp.concatenate([v_ref[0, :, pl.ds(c * kc, kc)], ones], axis=0)
        pv = jnp.dot(vext, p.astype(_BF16), preferred_element_type=_F32)
        return pv if acc is None else acc + pv

    @pl.when((pl.program_id(0) == 0) & (pl.program_id(1) == 0))
    def _():
        m = None
        for c in range(n_chunks):
            m = score_chunk(k_ref, q_ref[0, 0, :, 0:tq], bufs[0], c, m)
        m0_ref[...] = m

    m_cur = m0_ref[...]
    for t in range(n_tiles):
        m_next = acc = None
        for c in range(n_chunks):
            if t + 1 < n_tiles:
                m_next = score_chunk(k_ref, q_ref[0, 0, :, pl.ds((t + 1) * tq, tq)],
                                     bufs[(t + 1) % 2], c, m_next)
            else:
                m_next = score_chunk(kn_ref, qn_ref[0, 0], bufs[0], c, m_next)
            acc = value_chunk(bufs[t % 2], c, m_cur, acc)
        o_ref[0, :, pl.ds(t * tq, tq)] = (acc[0:dv] / acc[dv:dv + 1]).astype(_BF16)
        m_cur = m_next
    m0_ref[...] = m_cur


def _attention(q_all, k_all, v_all):
    bsz, _, _, s = q_all.shape
    total = k_all.shape[2]
    group = A_HEADS // A_KV_HEADS
    kset = lambda h: jnp.where(h < A_HEADS, 0, h - A_HEADS + 1)
    vblk = lambda h: jnp.where(h < A_HEADS, h // group, h - A_HEADS + A_KV_HEADS)

    def nxt(b, h):
        g = jnp.minimum(b * N_HEADS + h + 1, bsz * N_HEADS - 1)
        return g // N_HEADS, g % N_HEADS

    return pl.pallas_call(
        _attn_kernel,
        grid=(bsz, N_HEADS),
        in_specs=[
            pl.BlockSpec(memory_space=pltpu.SMEM),
            pl.BlockSpec((1, 1, HEAD_PAD, s), lambda b, h: (b, h, 0, 0)),
            pl.BlockSpec((1, 1, total, HEAD_PAD), lambda b, h: (b, kset(h), 0, 0)),
            pl.BlockSpec((1, B_V_DIM, total), lambda b, h: (b, vblk(h), 0)),
            pl.BlockSpec((1, 1, HEAD_PAD, ATTN_Q_TILE), lambda b, h: (*nxt(b, h), 0, 0)),
            pl.BlockSpec((1, 1, total, HEAD_PAD),
                         lambda b, h: (nxt(b, h)[0], kset(nxt(b, h)[1]), 0, 0)),
        ],
        out_specs=pl.BlockSpec((1, B_V_DIM, s), lambda b, h: (b, h, 0)),
        out_shape=jax.ShapeDtypeStruct((bsz, N_HEADS * B_V_DIM, s), _BF16),
        scratch_shapes=[pltpu.VMEM((total, ATTN_Q_TILE), _F32),
                        pltpu.VMEM((total, ATTN_Q_TILE), _F32),
                        pltpu.VMEM((1, ATTN_Q_TILE), _F32)],
        compiler_params=pltpu.CompilerParams(
            dimension_semantics=("arbitrary", "arbitrary"), vmem_limit_bytes=VMEM_LIMIT),
        name="attn",
    )(jnp.zeros((1,), jnp.int32), q_all, k_all, v_all, q_all, k_all)


def _merge_pieces(mrow, x, ot, wg_ref, wpa_ref, wpb_ref, wo_ref, g_ref, b_ref, result):
    shift, scale, gate = mrow[:, 0:D_MODEL], mrow[:, D_MODEL:2 * D_MODEL], mrow[:, 2 * D_MODEL:3 * D_MODEL]
    h = (_layer_norm(x) * (1.0 + scale) + shift).astype(_BF16)
    yield
    na = A_HEADS * A_HEAD_DIM
    ys = []
    for r in range(D_MODEL // MERGE_ROW_CHUNK):
        ra = pl.ds(r * MERGE_ROW_CHUNK, MERGE_ROW_CHUNK)
        rb = pl.ds(D_MODEL + r * MERGE_ROW_CHUNK, MERGE_ROW_CHUNK)
        ga = lax.dot_general(wg_ref[ra, :], h, _NT, preferred_element_type=_F32)
        gb = lax.dot_general(wg_ref[rb, :], h, _NT, preferred_element_type=_F32)
        ya = jnp.dot(wpa_ref[ra, :], ot[0:na], preferred_element_type=_F32)
        yb = jnp.dot(wpb_ref[ra, :], ot[na:], preferred_element_type=_F32)
        ys.append((_sigmoid(ga) * ya + _sigmoid(gb) * yb).astype(_BF16))
        yield
    zt = jnp.dot(wo_ref[...], jnp.concatenate(ys, axis=0), preferred_element_type=_F32)
    r = DEEPNORM_ALPHA * x + gate * zt.T
    result.append(_layer_norm(r) * g_ref[...] + b_ref[...])
    yield


def _ffn_pieces(mrow, x, wup_ref, wdn_ref, g_ref, b_ref, result):
    shift, scale, gate = mrow[:, 0:D_MODEL], mrow[:, D_MODEL:2 * D_MODEL], mrow[:, 2 * D_MODEL:3 * D_MODEL]
    h = (_layer_norm(x) * (1.0 + scale) + shift).astype(_BF16)
    yield
    fs = []
    for c in range(FFN_HIDDEN // FFN_COL_CHUNK):
        a = jnp.dot(h, wup_ref[:, pl.ds(c * FFN_COL_CHUNK, FFN_COL_CHUNK)], preferred_element_type=_F32)
        u = jnp.dot(h, wup_ref[:, pl.ds(FFN_HIDDEN + c * FFN_COL_CHUNK, FFN_COL_CHUNK)],
                    preferred_element_type=_F32)
        fs.append((a * _sigmoid(a) * u).astype(_BF16))
        yield
    dn = jnp.dot(jnp.concatenate(fs, axis=1), wdn_ref[...], preferred_element_type=_F32)
    r = DEEPNORM_ALPHA * x + gate * dn
    result.append(_layer_norm(r) * g_ref[...] + b_ref[...])
    yield


def _interleave(major, minor, ratio):
    live = [major, minor]
    while live:
        for gen, n in ((major, ratio), (minor, 1)):
            for _ in range(n):
                if gen in live and next(gen, StopIteration) is StopIteration:
                    live.remove(gen)


def _post_kernel(mod_ref, x_ref, ot_ref, wg_ref, wpa_ref, wpb_ref, wo_ref, g1_ref, b1_ref,
                 wup_ref, wdn_ref, g2_ref, b2_ref, out_ref, x1_ref, *, tiles_per_batch):
    j = pl.program_id(0)
    n_tiles = pl.num_programs(0) - 1

    @pl.when(j == 0)
    def _():
        x1_ref[...] = jnp.zeros_like(x1_ref)

    b_prev = jnp.maximum(j - 1, 0) // tiles_per_batch
    b_cur = jnp.minimum(j, n_tiles - 1) // tiles_per_batch
    ffn_out, merge_out = [], []
    _interleave(
        _ffn_pieces(mod_ref[pl.ds(b_prev, 1), 3 * D_MODEL:6 * D_MODEL], x1_ref[...],
                    wup_ref, wdn_ref, g2_ref, b2_ref, ffn_out),
        _merge_pieces(mod_ref[pl.ds(b_cur, 1), 0:3 * D_MODEL], x_ref[0], ot_ref[0],
                      wg_ref, wpa_ref, wpb_ref, wo_ref, g1_ref, b1_ref, merge_out),
        ratio=POST_FFN_PIECES_PER_MERGE_PIECE)
    out_ref[0] = ffn_out[0]
    x1_ref[...] = merge_out[0]


def _post(mod, x, ot, wgt, wpat, wpbt, wot, g1, b1, wup, wdn, g2, b2):
    bsz, s, d = x.shape
    t = TOKEN_TILE
    tpb = s // t
    n_tiles = bsz * tpb
    const = lambda a: pl.BlockSpec(a.shape, lambda j: (0,) * a.ndim, pipeline_mode=pl.Buffered(1))
    cur = lambda j: jnp.minimum(j, n_tiles - 1)
    prev = lambda j: jnp.maximum(j - 1, 0)
    return pl.pallas_call(
        functools.partial(_post_kernel, tiles_per_batch=tpb),
        grid=(n_tiles + 1,),
        in_specs=[
            const(mod),
            pl.BlockSpec((1, t, d), lambda j: (cur(j) // tpb, cur(j) % tpb, 0)),
            pl.BlockSpec((1, ot.shape[1], t), lambda j: (cur(j) // tpb, 0, cur(j) % tpb)),
            const(wgt), const(wpat), const(wpbt), const(wot), const(g1), const(b1),
            const(wup), const(wdn), const(g2), const(b2),
        ],
        out_specs=pl.BlockSpec((1, t, d), lambda j: (prev(j) // tpb, prev(j) % tpb, 0)),
        out_shape=jax.ShapeDtypeStruct((bsz, s, d), _F32),
        scratch_shapes=[pltpu.VMEM((t, d), _F32)],
        compiler_params=pltpu.CompilerParams(
            dimension_semantics=("arbitrary",), vmem_limit_bytes=VMEM_LIMIT),
        name="post",
    )(mod, x, ot, wgt, wpat, wpbt, wot, g1, b1, wup, wdn, g2, b2)


def _rope_tables(seq, half):
    pos = np.arange(seq)
    freqs = ROPE_THETA ** (-np.arange(half, dtype=np.float64) / half)
    tabs = []
    for p in (pos // GRID_W, pos % GRID_W):
        ang = p[None, :].astype(np.float64) * freqs[:, None]
        for fn, ident in ((np.cos, 1.0), (np.sin, 0.0)):
            tabs.append(np.concatenate([np.full((half, CTX_LEN), ident), fn(ang)], axis=1))
    return jnp.asarray(np.stack(tabs), dtype=_F32)


def kernel(x, c, ctx, c_ctx, w_mod, b_mod, w_in, q_norm_a, k_norm_a, cq_norm, ckv_norm,
           w_uq, w_ukv, w_proj_a, w_proj_b, w_out, ln1_g, ln1_b, w_up, w_down, ln2_g, ln2_b):
    bsz, s, d = x.shape
    assert w_mod.shape[0] == DEPTH == 1 and d == D_MODEL and ctx.shape[1] == CTX_LEN
    assert bsz == 8 and s % ATTN_Q_TILE == 0 and s % TOKEN_TILE == 0

    cc = jnp.concatenate([c, c_ctx[None, :], jnp.zeros((bsz - 1, d), _F32)], axis=0)
    mod = _modulation(cc, w_mod[0], b_mod[0][None, :])

    w_in0 = w_in[0]
    w1t = w_in0[:, :QKV_COLS].T.astype(_BF16)
    wgt = w_in0[:, QKV_COLS:].T.astype(_BF16)
    col = lambda v: v[0][:, None]
    tab_a = _rope_tables(s, A_HEAD_DIM // 4)
    tab_b = _rope_tables(s, B_ROPE_DIM // 4)

    q_all, k_all, v_all = _project(
        mod, ctx, x, w1t, w_uq[0].T.astype(_BF16), w_ukv[0].T.astype(_BF16),
        col(q_norm_a), col(k_norm_a), col(cq_norm), col(ckv_norm), tab_a, tab_b)
    ot = _attention(q_all, k_all, v_all)
    return _post(mod, x, ot, wgt, w_proj_a[0].T.astype(_BF16), w_proj_b[0].T.astype(_BF16),
                 w_out[0].T.astype(_BF16), ln1_g, ln1_b,
                 w_up[0].astype(_BF16), w_down[0].astype(_BF16), ln2_g, ln2_b)
```

```python
import functools
import math

import numpy as np
import jax
import jax.numpy as jnp
from jax import lax
from jax.experimental import pallas as pl
from jax.experimental.pallas import tpu as pltpu

D_MODEL = 1024
GRID_W = 64
CTX_LEN = 256
ROPE_THETA = 10000.0
EPS = 1e-6

A_HEADS = 8
A_KV_HEADS = 2
A_HEAD_DIM = 64
B_HEADS = 8
B_Q_RANK = 384
B_KV_RANK = 256
B_NOPE_DIM = 64
B_ROPE_DIM = 32
B_V_DIM = 64
FFN_HIDDEN = 2816
DEPTH = 1

A_SCALE = A_HEAD_DIM ** -0.5
B_SCALE = (B_NOPE_DIM + B_ROPE_DIM) ** -0.5
DEEPNORM_ALPHA = (2.0 * DEPTH) ** 0.25
LOG2_E = math.log2(math.e)

QA_OFF = 0
KA_OFF = QA_OFF + A_HEADS * A_HEAD_DIM
VA_OFF = KA_OFF + A_KV_HEADS * A_HEAD_DIM
CQ_OFF = VA_OFF + A_KV_HEADS * A_HEAD_DIM
CKV_OFF = CQ_OFF + B_Q_RANK
KR_OFF = CKV_OFF + B_KV_RANK
GATE_OFF = KR_OFF + B_ROPE_DIM
QKV_COLS = GATE_OFF

N_HEADS = A_HEADS + B_HEADS
HEAD_PAD = 128
N_KSETS = 1 + B_HEADS
V_ROWS = A_KV_HEADS * A_HEAD_DIM + B_HEADS * B_V_DIM

PROJ_TILE = 256
ATTN_Q_TILE = 512
ATTN_KEY_CHUNK = 256
ATTN_ONES_ROWS = 16
TOKEN_TILE = 256
MOD_COL_TILE = 1024
PROJ_ROW_CHUNK = 288
PROJ_STREAM_LEAD = 3
MERGE_ROW_CHUNK = 256
FFN_COL_CHUNK = 256
POST_FFN_PIECES_PER_MERGE_PIECE = 2
VMEM_LIMIT = 56 * 1024 * 1024

_NT = (((1,), (1,)), ((), ()))
_F32 = jnp.float32
_BF16 = jnp.bfloat16


def _layer_norm(x):
    mu = jnp.mean(x, axis=-1, keepdims=True)
    xc = x - mu
    var = jnp.mean(xc * xc, axis=-1, keepdims=True)
    return xc * lax.rsqrt(var + EPS)


def _sigmoid(x):
    return jax.nn.sigmoid(x)


def _mod_kernel(c_ref, w_ref, b_ref, o_ref):
    c = c_ref[...]
    a = (c * _sigmoid(c)).astype(_BF16)
    o_ref[...] = jnp.dot(a, w_ref[...].astype(_BF16), preferred_element_type=_F32) + b_ref[...]


def _modulation(cc, w_mod, b_mod):
    rows, d = cc.shape
    n = w_mod.shape[1]
    return pl.pallas_call(
        _mod_kernel,
        grid=(n // MOD_COL_TILE,),
        in_specs=[
            pl.BlockSpec((rows, d), lambda i: (0, 0)),
            pl.BlockSpec((d, MOD_COL_TILE), lambda i: (0, i)),
            pl.BlockSpec((1, MOD_COL_TILE), lambda i: (0, i)),
        ],
        out_specs=pl.BlockSpec((rows, MOD_COL_TILE), lambda i: (0, i)),
        out_shape=jax.ShapeDtypeStruct((rows, n), _F32),
        compiler_params=pltpu.CompilerParams(
            dimension_semantics=("arbitrary",), vmem_limit_bytes=VMEM_LIMIT),
        name="mod",
    )(cc, w_mod, b_mod)


def _rms_rows(x, g):
    ms = jnp.mean(x * x, axis=0, keepdims=True)
    return x * lax.rsqrt(ms + EPS) * g


def _axial_rope_rows(x, half, tab):
    r1, r2 = x[0:half], x[half:2 * half]
    c1, c2 = x[2 * half:3 * half], x[3 * half:4 * half]
    cr, sr, cc, sc = tab[0], tab[1], tab[2], tab[3]
    return jnp.concatenate(
        [r1 * cr - r2 * sr, r1 * sr + r2 * cr, c1 * cc - c2 * sc, c1 * sc + c2 * cc], axis=0)


def _interleave(major, minor, ratio, lead=0):
    for _ in range(lead):
        next(major, None)
    live = [major, minor]
    while live:
        for gen, n in ((major, ratio), (minor, 1)):
            for _ in range(n):
                if gen in live and next(gen, StopIteration) is StopIteration:
                    live.remove(gen)


def _proj_pieces(mrow, xin, w1_ref, wuq_ref, wukv_ref, qn_ref, kn_ref, cqn_ref, ckvn_ref, ta, tb,
                 q_out, k_out, v_out):
    shift, scale = mrow[:, 0:D_MODEL], mrow[:, D_MODEL:2 * D_MODEL]
    h = (_layer_norm(xin) * (1.0 + scale) + shift).astype(_BF16)
    yield
    parts = []
    for r in range(QKV_COLS // PROJ_ROW_CHUNK):
        rows = pl.ds(r * PROJ_ROW_CHUNK, PROJ_ROW_CHUNK)
        parts.append(lax.dot_general(w1_ref[rows, :], h, _NT, preferred_element_type=_F32))
        yield
    pt = jnp.concatenate(parts, axis=0)
    t = pt.shape[1]
    z64 = jnp.zeros((A_HEAD_DIM, t), _F32)
    z32 = jnp.zeros((HEAD_PAD - B_NOPE_DIM - B_ROPE_DIM, t), _F32)

    qn = qn_ref[...] * (A_SCALE * LOG2_E)
    group = A_HEADS // A_KV_HEADS
    for hd in range(A_HEADS):
        xh = pt[QA_OFF + hd * A_HEAD_DIM:QA_OFF + (hd + 1) * A_HEAD_DIM]
        xh = _axial_rope_rows(_rms_rows(xh, qn), A_HEAD_DIM // 4, ta)
        full = jnp.concatenate([xh, z64] if hd // group == 0 else [z64, xh], axis=0)
        q_out[hd] = full.astype(_BF16)
        if hd % 4 == 3:
            yield

    kn = kn_ref[...]
    ka = [
        _axial_rope_rows(
            _rms_rows(pt[KA_OFF + g * A_HEAD_DIM:KA_OFF + (g + 1) * A_HEAD_DIM], kn),
            A_HEAD_DIM // 4, ta)
        for g in range(A_KV_HEADS)
    ]
    k_out[0] = jnp.concatenate(ka, axis=0).T.astype(_BF16)
    v_out[0:A_KV_HEADS * A_HEAD_DIM, :] = pt[VA_OFF:CQ_OFF].astype(_BF16)
    yield

    cq = _rms_rows(pt[CQ_OFF:CKV_OFF], cqn_ref[...]).astype(_BF16)
    qb = jnp.dot(wuq_ref[...], cq, preferred_element_type=_F32) * (B_SCALE * LOG2_E)
    yield
    qdim = B_NOPE_DIM + B_ROPE_DIM
    for hd in range(B_HEADS):
        blk = qb[hd * qdim:(hd + 1) * qdim]
        rp = _axial_rope_rows(blk[B_NOPE_DIM:qdim], B_ROPE_DIM // 4, tb)
        full = jnp.concatenate([blk[0:B_NOPE_DIM], rp, z32], axis=0)
        q_out[A_HEADS + hd] = full.astype(_BF16)
        if hd % 4 == 3:
            yield

    ckv = _rms_rows(pt[CKV_OFF:KR_OFF], ckvn_ref[...]).astype(_BF16)
    kv = jnp.dot(wukv_ref[...], ckv, preferred_element_type=_F32)
    kr = _axial_rope_rows(pt[KR_OFF:GATE_OFF], B_ROPE_DIM // 4, tb)
    yield
    kvdim = B_NOPE_DIM + B_V_DIM
    v_base = A_KV_HEADS * A_HEAD_DIM
    for hd in range(B_HEADS):
        kn_h = kv[hd * kvdim:hd * kvdim + B_NOPE_DIM]
        v_h = kv[hd * kvdim + B_NOPE_DIM:(hd + 1) * kvdim]
        v_out[v_base + hd * B_V_DIM:v_base + (hd + 1) * B_V_DIM, :] = v_h.astype(_BF16)
        k_out[1 + hd] = jnp.concatenate([kn_h, kr, z32], axis=0).T.astype(_BF16)
        if hd % 4 == 3:
            yield


def _proj_kernel(mod_ref, ctx_ref, x_ref, w1_ref, wuq_ref, wukv_ref,
                 qn_ref, kn_ref, cqn_ref, ckvn_ref, ta_ref, tb_ref,
                 q_out, k_out, v_out):
    b = pl.program_id(0)
    j = pl.program_id(1)
    half_batch = pl.num_programs(0)
    is_ctx = j == 0
    ctx_row = mod_ref.shape[0] // 2
    ta = ta_ref[...]
    tb = tb_ref[...]

    def stream(half):
        xin = jnp.where(is_ctx, ctx_ref[half, 0], x_ref[half, 0])
        mrow = jnp.where(is_ctx, mod_ref[ctx_row:ctx_row + 1, 0:2 * D_MODEL],
                         mod_ref[pl.ds(b + half * half_batch, 1), 0:2 * D_MODEL])
        return _proj_pieces(mrow, xin, w1_ref, wuq_ref, wukv_ref, qn_ref, kn_ref, cqn_ref, ckvn_ref,
                            ta, tb, q_out.at[half, 0], k_out.at[half, 0], v_out.at[half, 0])

    _interleave(stream(0), stream(1), ratio=1, lead=PROJ_STREAM_LEAD)


def _project(mod, ctx, x, w1t, wuqt, wukvt, qn, kn, cqn, ckvn, tab_a, tab_b):
    bsz, s, d = x.shape
    t = PROJ_TILE
    n_steps = (CTX_LEN + s) // t
    total = CTX_LEN + s
    hb = bsz // 2
    lat = lambda j: jnp.maximum(j - 1, 0)
    full2 = lambda shape: pl.BlockSpec(shape, lambda b, j: (0, 0))
    q_all, k_all, v_all = pl.pallas_call(
        _proj_kernel,
        grid=(hb, n_steps),
        in_specs=[
            full2(mod.shape),
            pl.BlockSpec((2, 1, CTX_LEN, d), lambda b, j: (0, b, 0, 0)),
            pl.BlockSpec((2, 1, t, d), lambda b, j: (0, b, lat(j), 0)),
            full2(w1t.shape), full2(wuqt.shape), full2(wukvt.shape),
            full2(qn.shape), full2(kn.shape), full2(cqn.shape), full2(ckvn.shape),
            pl.BlockSpec((4, tab_a.shape[1], t), lambda b, j: (0, 0, j)),
            pl.BlockSpec((4, tab_b.shape[1], t), lambda b, j: (0, 0, j)),
        ],
        out_specs=[
            pl.BlockSpec((2, 1, N_HEADS, HEAD_PAD, t), lambda b, j: (0, b, 0, 0, lat(j))),
            pl.BlockSpec((2, 1, N_KSETS, t, HEAD_PAD), lambda b, j: (0, b, 0, j, 0)),
            pl.BlockSpec((2, 1, V_ROWS, t), lambda b, j: (0, b, 0, j)),
        ],
        out_shape=[
            jax.ShapeDtypeStruct((2, hb, N_HEADS, HEAD_PAD, s), _BF16),
            jax.ShapeDtypeStruct((2, hb, N_KSETS, total, HEAD_PAD), _BF16),
            jax.ShapeDtypeStruct((2, hb, V_ROWS, total), _BF16),
        ],
        compiler_params=pltpu.CompilerParams(
            dimension_semantics=("arbitrary", "arbitrary"), vmem_limit_bytes=VMEM_LIMIT),
        name="proj",
    )(mod, ctx.reshape(2, hb, CTX_LEN, d), x.reshape(2, hb, s, d),
      w1t, wuqt, wukvt, qn, kn, cqn, ckvn, tab_a, tab_b)
    return (q_all.reshape(bsz, N_HEADS, HEAD_PAD, s), k_all.reshape(bsz, N_KSETS, total, HEAD_PAD),
            v_all.reshape(bsz, V_ROWS, total))


def _attn_kernel(base_ref, q_ref, k_ref, v_ref, qn_ref, kn_ref, o_ref, s0_ref, s1_ref, m0_ref):
    tq, kc = ATTN_Q_TILE, ATTN_KEY_CHUNK
    n_tiles = q_ref.shape[3] // tq
    n_chunks = k_ref.shape[2] // kc
    assert n_tiles % 2 == 0
    assert n_tiles * tq == q_ref.shape[3] and n_chunks * kc == k_ref.shape[2]
    bufs = (s0_ref, s1_ref)
    dv = v_ref.shape[1]
    base = base_ref[0]
    ones = jnp.ones((ATTN_ONES_ROWS, kc), _BF16)

    def buf_rows(c):
        return pl.ds(pl.multiple_of(base + c * kc, kc), kc)

    def score_chunk(keys_ref, q_tile, buf, c, m):
        s = jnp.dot(keys_ref[0, 0, pl.ds(c * kc, kc), :], q_tile,
                    preferred_element_type=_F32)
        buf[buf_rows(c), :] = s
        cm = jnp.max(s, axis=0, keepdims=True)
        return cm if m is None else jnp.maximum(m, cm)

    def value_chunk(buf, c, m, acc):
        p = jnp.exp2(buf[buf_rows(c), :] - m)
        vext = jnp.concatenate([v_ref[0, :, pl.ds(c * kc, kc)], ones], axis=0)
        pv = jnp.dot(vext, p.astype(_BF16), preferred_element_type=_F32)
        return pv if acc is None else acc + pv

    @pl.when((pl.program_id(0) == 0) & (pl.program_id(1) == 0))
    def _():
        m = None
        for c in range(n_chunks):
            m = score_chunk(k_ref, q_ref[0, 0, :, 0:tq], bufs[0], c, m)
        m0_ref[...] = m

    m_cur = m0_ref[...]
    for t in range(n_tiles):
        m_next = acc = None
        for c in range(n_chunks):
            if t + 1 < n_tiles:
                m_next = score_chunk(k_ref, q_ref[0, 0, :, pl.ds((t + 1) * tq, tq)],
                                     bufs[(t + 1) % 2], c, m_next)
            else:
                m_next = score_chunk(kn_ref, qn_ref[0, 0], bufs[0], c, m_next)
            acc = value_chunk(bufs[t % 2], c, m_cur, acc)
        o_ref[0, :, pl.ds(t * tq, tq)] = (acc[0:dv] / acc[dv:dv + 1]).astype(_BF16)
        m_cur = m_next
    m0_ref[...] = m_cur


def _attention(q_all, k_all, v_all):
    bsz, _, _, s = q_all.shape
    total = k_all.shape[2]
    group = A_HEADS // A_KV_HEADS
    kset = lambda h: jnp.where(h < A_HEADS, 0, h - A_HEADS + 1)
    vblk = lambda h: jnp.where(h < A_HEADS, h // group, h - A_HEADS + A_KV_HEADS)

    def nxt(b, h):
        g = jnp.minimum(b * N_HEADS + h + 1, bsz * N_HEADS - 1)
        return g // N_HEADS, g % N_HEADS

    return pl.pallas_call(
        _attn_kernel,
        grid=(bsz, N_HEADS),
        in_specs=[
            pl.BlockSpec(memory_space=pltpu.SMEM),
            pl.BlockSpec((1, 1, HEAD_PAD, s), lambda b, h: (b, h, 0, 0)),
            pl.BlockSpec((1, 1, total, HEAD_PAD), lambda b, h: (b, kset(h), 0, 0)),
            pl.BlockSpec((1, B_V_DIM, total), lambda b, h: (b, vblk(h), 0)),
            pl.BlockSpec((1, 1, HEAD_PAD, ATTN_Q_TILE), lambda b, h: (*nxt(b, h), 0, 0)),
            pl.BlockSpec((1, 1, total, HEAD_PAD),
                         lambda b, h: (nxt(b, h)[0], kset(nxt(b, h)[1]), 0, 0)),
        ],
        out_specs=pl.BlockSpec((1, B_V_DIM, s), lambda b, h: (b, h, 0)),
        out_shape=jax.ShapeDtypeStruct((bsz, N_HEADS * B_V_DIM, s), _BF16),
        scratch_shapes=[pltpu.VMEM((total, ATTN_Q_TILE), _F32),
                        pltpu.VMEM((total, ATTN_Q_TILE), _F32),
                        pltpu.VMEM((1, ATTN_Q_TILE), _F32)],
        compiler_params=pltpu.CompilerParams(
            dimension_semantics=("arbitrary", "arbitrary"), vmem_limit_bytes=VMEM_LIMIT),
        name="attn",
    )(jnp.zeros((1,), jnp.int32), q_all, k_all, v_all, q_all, k_all)


def _merge_pieces(mrow, x, ot, wg_ref, wpa_ref, wpb_ref, wo_ref, g_ref, b_ref, result):
    shift, scale, gate = mrow[:, 0:D_MODEL], mrow[:, D_MODEL:2 * D_MODEL], mrow[:, 2 * D_MODEL:3 * D_MODEL]
    h = (_layer_norm(x) * (1.0 + scale) + shift).astype(_BF16)
    yield
    na = A_HEADS * A_HEAD_DIM
    ys = []
    for r in range(D_MODEL // MERGE_ROW_CHUNK):
        ra = pl.ds(r * MERGE_ROW_CHUNK, MERGE_ROW_CHUNK)
        rb = pl.ds(D_MODEL + r * MERGE_ROW_CHUNK, MERGE_ROW_CHUNK)
        ga = lax.dot_general(wg_ref[ra, :], h, _NT, preferred_element_type=_F32)
        gb = lax.dot_general(wg_ref[rb, :], h, _NT, preferred_element_type=_F32)
        ya = jnp.dot(wpa_ref[ra, :], ot[0:na], preferred_element_type=_F32)
        yb = jnp.dot(wpb_ref[ra, :], ot[na:], preferred_element_type=_F32)
        ys.append((_sigmoid(ga) * ya + _sigmoid(gb) * yb).astype(_BF16))
        yield
    zt = jnp.dot(wo_ref[...], jnp.concatenate(ys, axis=0), preferred_element_type=_F32)
    r = DEEPNORM_ALPHA * x + gate * zt.T
    result.append(_layer_norm(r) * g_ref[...] + b_ref[...])
    yield


def _ffn_pieces(mrow, x, wup_ref, wdn_ref, g_ref, b_ref, result):
    shift, scale, gate = mrow[:, 0:D_MODEL], mrow[:, D_MODEL:2 * D_MODEL], mrow[:, 2 * D_MODEL:3 * D_MODEL]
    h = (_layer_norm(x) * (1.0 + scale) + shift).astype(_BF16)
    yield
    fs = []
    for c in range(FFN_HIDDEN // FFN_COL_CHUNK):
        a = jnp.dot(h, wup_ref[:, pl.ds(c * FFN_COL_CHUNK, FFN_COL_CHUNK)], preferred_element_type=_F32)
        u = jnp.dot(h, wup_ref[:, pl.ds(FFN_HIDDEN + c * FFN_COL_CHUNK, FFN_COL_CHUNK)],
                    preferred_element_type=_F32)
        fs.append((a * _sigmoid(a) * u).astype(_BF16))
        yield
    dn = jnp.dot(jnp.concatenate(fs, axis=1), wdn_ref[...], preferred_element_type=_F32)
    r = DEEPNORM_ALPHA * x + gate * dn
    result.append(_layer_norm(r) * g_ref[...] + b_ref[...])
    yield


def _post_kernel(mod_ref, x_ref, ot_ref, wg_ref, wpa_ref, wpb_ref, wo_ref, g1_ref, b1_ref,
                 wup_ref, wdn_ref, g2_ref, b2_ref, out_ref, x1_ref, *, tiles_per_batch):
    j = pl.program_id(0)
    n_tiles = pl.num_programs(0) - 1

    @pl.when(j == 0)
    def _():
        x1_ref[...] = jnp.zeros_like(x1_ref)

    b_prev = jnp.maximum(j - 1, 0) // tiles_per_batch
    b_cur = jnp.minimum(j, n_tiles - 1) // tiles_per_batch
    ffn_out, merge_out = [], []
    _interleave(
        _ffn_pieces(mod_ref[pl.ds(b_prev, 1), 3 * D_MODEL:6 * D_MODEL], x1_ref[...],
                    wup_ref, wdn_ref, g2_ref, b2_ref, ffn_out),
        _merge_pieces(mod_ref[pl.ds(b_cur, 1), 0:3 * D_MODEL], x_ref[0], ot_ref[0],
                      wg_ref, wpa_ref, wpb_ref, wo_ref, g1_ref, b1_ref, merge_out),
        ratio=POST_FFN_PIECES_PER_MERGE_PIECE)
    out_ref[0] = ffn_out[0]
    x1_ref[...] = merge_out[0]


def _post(mod, x, ot, wgt, wpat, wpbt, wot, g1, b1, wup, wdn, g2, b2):
    bsz, s, d = x.shape
    t = TOKEN_TILE
    tpb = s // t
    n_tiles = bsz * tpb
    const = lambda a: pl.BlockSpec(a.shape, lambda j: (0,) * a.ndim, pipeline_mode=pl.Buffered(1))
    cur = lambda j: jnp.minimum(j, n_tiles - 1)
    prev = lambda j: jnp.maximum(j - 1, 0)
    return pl.pallas_call(
        functools.partial(_post_kernel, tiles_per_batch=tpb),
        grid=(n_tiles + 1,),
        in_specs=[
            const(mod),
            pl.BlockSpec((1, t, d), lambda j: (cur(j) // tpb, cur(j) % tpb, 0)),
            pl.BlockSpec((1, ot.shape[1], t), lambda j: (cur(j) // tpb, 0, cur(j) % tpb)),
            const(wgt), const(wpat), const(wpbt), const(wot), const(g1), const(b1),
            const(wup), const(wdn), const(g2), const(b2),
        ],
        out_specs=pl.BlockSpec((1, t, d), lambda j: (prev(j) // tpb, prev(j) % tpb, 0)),
        out_shape=jax.ShapeDtypeStruct((bsz, s, d), _F32),
        scratch_shapes=[pltpu.VMEM((t, d), _F32)],
        compiler_params=pltpu.CompilerParams(
            dimension_semantics=("arbitrary",), vmem_limit_bytes=VMEM_LIMIT),
        name="post",
    )(mod, x, ot, wgt, wpat, wpbt, wot, g1, b1, wup, wdn, g2, b2)


def _rope_tables(seq, half):
    pos = np.arange(seq)
    freqs = ROPE_THETA ** (-np.arange(half, dtype=np.float64) / half)
    tabs = []
    for p in (pos // GRID_W, pos % GRID_W):
        ang = p[None, :].astype(np.float64) * freqs[:, None]
        for fn, ident in ((np.cos, 1.0), (np.sin, 0.0)):
            tabs.append(np.concatenate([np.full((half, CTX_LEN), ident), fn(ang)], axis=1))
    return jnp.asarray(np.stack(tabs), dtype=_F32)


def kernel(x, c, ctx, c_ctx, w_mod, b_mod, w_in, q_norm_a, k_norm_a, cq_norm, ckv_norm,
           w_uq, w_ukv, w_proj_a, w_proj_b, w_out, ln1_g, ln1_b, w_up, w_down, ln2_g, ln2_b):
    bsz, s, d = x.shape
    assert w_mod.shape[0] == DEPTH == 1 and d == D_MODEL and ctx.shape[1] == CTX_LEN
    assert bsz == 8 and s % ATTN_Q_TILE == 0 and s % TOKEN_TILE == 0

    cc = jnp.concatenate([c, c_ctx[None, :], jnp.zeros((bsz - 1, d), _F32)], axis=0)
    mod = _modulation(cc, w_mod[0], b_mod[0][None, :])

    w_in0 = w_in[0]
    w1t = w_in0[:, :QKV_COLS].T.astype(_BF16)
    wgt = w_in0[:, QKV_COLS:].T.astype(_BF16)
    col = lambda v: v[0][:, None]
    tab_a = _rope_tables(s, A_HEAD_DIM // 4)
    tab_b = _rope_tables(s, B_ROPE_DIM // 4)

    q_all, k_all, v_all = _project(
        mod, ctx, x, w1t, w_uq[0].T.astype(_BF16), w_ukv[0].T.astype(_BF16),
        col(q_norm_a), col(k_norm_a), col(cq_norm), col(ckv_norm), tab_a, tab_b)
    ot = _attention(q_all, k_all, v_all)
    return _post(mod, x, ot, wgt, w_proj_a[0].T.astype(_BF16), w_proj_b[0].T.astype(_BF16),
                 w_out[0].T.astype(_BF16), ln1_g, ln1_b,
                 w_up[0].astype(_BF16), w_down[0].astype(_BF16), ln2_g, ln2_b)
```

```python
import functools
import math

import numpy as np
import jax
import jax.numpy as jnp
from jax import lax
from jax.experimental import pallas as pl
from jax.experimental.pallas import tpu as pltpu

D_MODEL = 1024
GRID_W = 64
CTX_LEN = 256
ROPE_THETA = 10000.0
EPS = 1e-6

A_HEADS = 8
A_KV_HEADS = 2
A_HEAD_DIM = 64
B_HEADS = 8
B_Q_RANK = 384
B_KV_RANK = 256
B_NOPE_DIM = 64
B_ROPE_DIM = 32
B_V_DIM = 64
FFN_HIDDEN = 2816
DEPTH = 1

A_SCALE = A_HEAD_DIM ** -0.5
B_SCALE = (B_NOPE_DIM + B_ROPE_DIM) ** -0.5
DEEPNORM_ALPHA = (2.0 * DEPTH) ** 0.25
LOG2_E = math.log2(math.e)

QA_OFF = 0
KA_OFF = QA_OFF + A_HEADS * A_HEAD_DIM
VA_OFF = KA_OFF + A_KV_HEADS * A_HEAD_DIM
CQ_OFF = VA_OFF + A_KV_HEADS * A_HEAD_DIM
CKV_OFF = CQ_OFF + B_Q_RANK
KR_OFF = CKV_OFF + B_KV_RANK
GATE_OFF = KR_OFF + B_ROPE_DIM
QKV_COLS = GATE_OFF

N_HEADS = A_HEADS + B_HEADS
HEAD_PAD = 128
N_KSETS = 1 + B_HEADS
V_ROWS = A_KV_HEADS * A_HEAD_DIM + B_HEADS * B_V_DIM

PROJ_TILE = 256
ATTN_Q_TILE = 512
ATTN_KEY_CHUNK = 256
ATTN_ONES_ROWS = 16
TOKEN_TILE = 256
MOD_COL_TILE = 1024
PROJ_ROW_CHUNK = 288
PROJ_STREAM_LEAD = 3
MERGE_ROW_CHUNK = 256
FFN_COL_CHUNK = 256
POST_FFN_PIECES_PER_MERGE_PIECE = 2
POST_STREAM_LEAD = 10
VMEM_LIMIT = 56 * 1024 * 1024

_NT = (((1,), (1,)), ((), ()))
_F32 = jnp.float32
_BF16 = jnp.bfloat16


def _layer_norm(x):
    mu = jnp.mean(x, axis=-1, keepdims=True)
    xc = x - mu
    var = jnp.mean(xc * xc, axis=-1, keepdims=True)
    return xc * lax.rsqrt(var + EPS)


def _sigmoid(x):
    return jax.nn.sigmoid(x)


def _mod_kernel(c_ref, w_ref, b_ref, o_ref):
    c = c_ref[...]
    a = (c * _sigmoid(c)).astype(_BF16)
    o_ref[...] = jnp.dot(a, w_ref[...].astype(_BF16), preferred_element_type=_F32) + b_ref[...]


def _modulation(cc, w_mod, b_mod):
    rows, d = cc.shape
    n = w_mod.shape[1]
    return pl.pallas_call(
        _mod_kernel,
        grid=(n // MOD_COL_TILE,),
        in_specs=[
            pl.BlockSpec((rows, d), lambda i: (0, 0)),
            pl.BlockSpec((d, MOD_COL_TILE), lambda i: (0, i)),
            pl.BlockSpec((1, MOD_COL_TILE), lambda i: (0, i)),
        ],
        out_specs=pl.BlockSpec((rows, MOD_COL_TILE), lambda i: (0, i)),
        out_shape=jax.ShapeDtypeStruct((rows, n), _F32),
        compiler_params=pltpu.CompilerParams(
            dimension_semantics=("arbitrary",), vmem_limit_bytes=VMEM_LIMIT),
        name="mod",
    )(cc, w_mod, b_mod)


def _rms_rows(x, g):
    ms = jnp.mean(x * x, axis=0, keepdims=True)
    return x * lax.rsqrt(ms + EPS) * g


def _axial_rope_rows(x, half, tab):
    r1, r2 = x[0:half], x[half:2 * half]
    c1, c2 = x[2 * half:3 * half], x[3 * half:4 * half]
    cr, sr, cc, sc = tab[0], tab[1], tab[2], tab[3]
    return jnp.concatenate(
        [r1 * cr - r2 * sr, r1 * sr + r2 * cr, c1 * cc - c2 * sc, c1 * sc + c2 * cc], axis=0)


def _interleave(major, minor, ratio, lead=0):
    for _ in range(lead):
        next(major, None)
    live = [major, minor]
    while live:
        for gen, n in ((major, ratio), (minor, 1)):
            for _ in range(n):
                if gen in live and next(gen, StopIteration) is StopIteration:
                    live.remove(gen)


def _proj_pieces(mrow, xin, w1_ref, wuq_ref, wukv_ref, qn_ref, kn_ref, cqn_ref, ckvn_ref, ta, tb,
                 q_out, k_out, v_out):
    shift, scale = mrow[:, 0:D_MODEL], mrow[:, D_MODEL:2 * D_MODEL]
    h = (_layer_norm(xin) * (1.0 + scale) + shift).astype(_BF16)
    yield
    parts = []
    for r in range(QKV_COLS // PROJ_ROW_CHUNK):
        rows = pl.ds(r * PROJ_ROW_CHUNK, PROJ_ROW_CHUNK)
        parts.append(lax.dot_general(w1_ref[rows, :], h, _NT, preferred_element_type=_F32))
        yield
    pt = jnp.concatenate(parts, axis=0)
    t = pt.shape[1]
    z64 = jnp.zeros((A_HEAD_DIM, t), _F32)
    z32 = jnp.zeros((HEAD_PAD - B_NOPE_DIM - B_ROPE_DIM, t), _F32)

    qn = qn_ref[...] * (A_SCALE * LOG2_E)
    group = A_HEADS // A_KV_HEADS
    for hd in range(A_HEADS):
        xh = pt[QA_OFF + hd * A_HEAD_DIM:QA_OFF + (hd + 1) * A_HEAD_DIM]
        xh = _axial_rope_rows(_rms_rows(xh, qn), A_HEAD_DIM // 4, ta)
        full = jnp.concatenate([xh, z64] if hd // group == 0 else [z64, xh], axis=0)
        q_out[hd] = full.astype(_BF16)
        if hd % 4 == 3:
            yield

    kn = kn_ref[...]
    ka = [
        _axial_rope_rows(
            _rms_rows(pt[KA_OFF + g * A_HEAD_DIM:KA_OFF + (g + 1) * A_HEAD_DIM], kn),
            A_HEAD_DIM // 4, ta)
        for g in range(A_KV_HEADS)
    ]
    k_out[0] = jnp.concatenate(ka, axis=0).T.astype(_BF16)
    v_out[0:A_KV_HEADS * A_HEAD_DIM, :] = pt[VA_OFF:CQ_OFF].astype(_BF16)
    yield

    cq = _rms_rows(pt[CQ_OFF:CKV_OFF], cqn_ref[...]).astype(_BF16)
    qb = jnp.dot(wuq_ref[...], cq, preferred_element_type=_F32) * (B_SCALE * LOG2_E)
    yield
    qdim = B_NOPE_DIM + B_ROPE_DIM
    for hd in range(B_HEADS):
        blk = qb[hd * qdim:(hd + 1) * qdim]
        rp = _axial_rope_rows(blk[B_NOPE_DIM:qdim], B_ROPE_DIM // 4, tb)
        full = jnp.concatenate([blk[0:B_NOPE_DIM], rp, z32], axis=0)
        q_out[A_HEADS + hd] = full.astype(_BF16)
        if hd % 4 == 3:
            yield

    ckv = _rms_rows(pt[CKV_OFF:KR_OFF], ckvn_ref[...]).astype(_BF16)
    kv = jnp.dot(wukv_ref[...], ckv, preferred_element_type=_F32)
    kr = _axial_rope_rows(pt[KR_OFF:GATE_OFF], B_ROPE_DIM // 4, tb)
    yield
    kvdim = B_NOPE_DIM + B_V_DIM
    v_base = A_KV_HEADS * A_HEAD_DIM
    for hd in range(B_HEADS):
        kn_h = kv[hd * kvdim:hd * kvdim + B_NOPE_DIM]
        v_h = kv[hd * kvdim + B_NOPE_DIM:(hd + 1) * kvdim]
        v_out[v_base + hd * B_V_DIM:v_base + (hd + 1) * B_V_DIM, :] = v_h.astype(_BF16)
        k_out[1 + hd] = jnp.concatenate([kn_h, kr, z32], axis=0).T.astype(_BF16)
        if hd % 4 == 3:
            yield


def _proj_kernel(mod_ref, ctx_ref, x_ref, w1_ref, wuq_ref, wukv_ref,
                 qn_ref, kn_ref, cqn_ref, ckvn_ref, ta_ref, tb_ref,
                 q_out, k_out, v_out):
    b = pl.program_id(0)
    j = pl.program_id(1)
    half_batch = pl.num_programs(0)
    is_ctx = j == 0
    ctx_row = mod_ref.shape[0] // 2
    ta = ta_ref[...]
    tb = tb_ref[...]

    def stream(half):
        xin = jnp.where(is_ctx, ctx_ref[half, 0], x_ref[half, 0])
        mrow = jnp.where(is_ctx, mod_ref[ctx_row:ctx_row + 1, 0:2 * D_MODEL],
                         mod_ref[pl.ds(b + half * half_batch, 1), 0:2 * D_MODEL])
        return _proj_pieces(mrow, xin, w1_ref, wuq_ref, wukv_ref, qn_ref, kn_ref, cqn_ref, ckvn_ref,
                            ta, tb, q_out.at[half, 0], k_out.at[half, 0], v_out.at[half, 0])

    _interleave(stream(0), stream(1), ratio=1, lead=PROJ_STREAM_LEAD)


def _project(mod, ctx, x, w1t, wuqt, wukvt, qn, kn, cqn, ckvn, tab_a, tab_b):
    bsz, s, d = x.shape
    t = PROJ_TILE
    n_steps = (CTX_LEN + s) // t
    total = CTX_LEN + s
    hb = bsz // 2
    lat = lambda j: jnp.maximum(j - 1, 0)
    full2 = lambda shape: pl.BlockSpec(shape, lambda b, j: (0, 0))
    q_all, k_all, v_all = pl.pallas_call(
        _proj_kernel,
        grid=(hb, n_steps),
        in_specs=[
            full2(mod.shape),
            pl.BlockSpec((2, 1, CTX_LEN, d), lambda b, j: (0, b, 0, 0)),
            pl.BlockSpec((2, 1, t, d), lambda b, j: (0, b, lat(j), 0)),
            full2(w1t.shape), full2(wuqt.shape), full2(wukvt.shape),
            full2(qn.shape), full2(kn.shape), full2(cqn.shape), full2(ckvn.shape),
            pl.BlockSpec((4, tab_a.shape[1], t), lambda b, j: (0, 0, j)),
            pl.BlockSpec((4, tab_b.shape[1], t), lambda b, j: (0, 0, j)),
        ],
        out_specs=[
            pl.BlockSpec((2, 1, N_HEADS, HEAD_PAD, t), lambda b, j: (0, b, 0, 0, lat(j))),
            pl.BlockSpec((2, 1, N_KSETS, t, HEAD_PAD), lambda b, j: (0, b, 0, j, 0)),
            pl.BlockSpec((2, 1, V_ROWS, t), lambda b, j: (0, b, 0, j)),
        ],
        out_shape=[
            jax.ShapeDtypeStruct((2, hb, N_HEADS, HEAD_PAD, s), _BF16),
            jax.ShapeDtypeStruct((2, hb, N_KSETS, total, HEAD_PAD), _BF16),
            jax.ShapeDtypeStruct((2, hb, V_ROWS, total), _BF16),
        ],
        compiler_params=pltpu.CompilerParams(
            dimension_semantics=("arbitrary", "arbitrary"), vmem_limit_bytes=VMEM_LIMIT),
        name="proj",
    )(mod, ctx.reshape(2, hb, CTX_LEN, d), x.reshape(2, hb, s, d),
      w1t, wuqt, wukvt, qn, kn, cqn, ckvn, tab_a, tab_b)
    return (q_all.reshape(bsz, N_HEADS, HEAD_PAD, s), k_all.reshape(bsz, N_KSETS, total, HEAD_PAD),
            v_all.reshape(bsz, V_ROWS, total))


def _attn_kernel(base_ref, q_ref, k_ref, v_ref, qn_ref, kn_ref, o_ref, s0_ref, s1_ref, m0_ref):
    tq, kc = ATTN_Q_TILE, ATTN_KEY_CHUNK
    n_tiles = q_ref.shape[3] // tq
    n_chunks = k_ref.shape[2] // kc
    assert n_tiles % 2 == 0
    assert n_tiles * tq == q_ref.shape[3] and n_chunks * kc == k_ref.shape[2]
    bufs = (s0_ref, s1_ref)
    dv = v_ref.shape[1]
    base = base_ref[0]
    ones = jnp.ones((ATTN_ONES_ROWS, kc), _BF16)

    def buf_rows(c):
        return pl.ds(pl.multiple_of(base + c * kc, kc), kc)

    def score_chunk(keys_ref, q_tile, buf, c, m):
        s = jnp.dot(keys_ref[0, 0, pl.ds(c * kc, kc), :], q_tile,
                    preferred_element_type=_F32)
        buf[buf_rows(c), :] = s
        cm = jnp.max(s, axis=0, keepdims=True)
        return cm if m is None else jnp.maximum(m, cm)

    def value_chunk(buf, c, m, acc):
        p = jnp.exp2(buf[buf_rows(c), :] - m)
        vext = jnp.concatenate([v_ref[0, :, pl.ds(c * kc, kc)], ones], axis=0)
        pv = jnp.dot(vext, p.astype(_BF16), preferred_element_type=_F32)
        return pv if acc is None else acc + pv

    @pl.when((pl.program_id(0) == 0) & (pl.program_id(1) == 0))
    def _():
        m = None
        for c in range(n_chunks):
            m = score_chunk(k_ref, q_ref[0, 0, :, 0:tq], bufs[0], c, m)
        m0_ref[...] = m

    m_cur = m0_ref[...]
    for t in range(n_tiles):
        m_next = acc = None
        for c in range(n_chunks):
            if t + 1 < n_tiles:
                m_next = score_chunk(k_ref, q_ref[0, 0, :, pl.ds((t + 1) * tq, tq)],
                                     bufs[(t + 1) % 2], c, m_next)
            else:
                m_next = score_chunk(kn_ref, qn_ref[0, 0], bufs[0], c, m_next)
            acc = value_chunk(bufs[t % 2], c, m_cur, acc)
        o_ref[0, :, pl.ds(t * tq, tq)] = (acc[0:dv] / acc[dv:dv + 1]).astype(_BF16)
        m_cur = m_next
    m0_ref[...] = m_cur


def _attention(q_all, k_all, v_all):
    bsz, _, _, s = q_all.shape
    total = k_all.shape[2]
    group = A_HEADS // A_KV_HEADS
    kset = lambda h: jnp.where(h < A_HEADS, 0, h - A_HEADS + 1)
    vblk = lambda h: jnp.where(h < A_HEADS, h // group, h - A_HEADS + A_KV_HEADS)

    def nxt(b, h):
        g = jnp.minimum(b * N_HEADS + h + 1, bsz * N_HEADS - 1)
        return g // N_HEADS, g % N_HEADS

    return pl.pallas_call(
        _attn_kernel,
        grid=(bsz, N_HEADS),
        in_specs=[
            pl.BlockSpec(memory_space=pltpu.SMEM),
            pl.BlockSpec((1, 1, HEAD_PAD, s), lambda b, h: (b, h, 0, 0)),
            pl.BlockSpec((1, 1, total, HEAD_PAD), lambda b, h: (b, kset(h), 0, 0)),
            pl.BlockSpec((1, B_V_DIM, total), lambda b, h: (b, vblk(h), 0)),
            pl.BlockSpec((1, 1, HEAD_PAD, ATTN_Q_TILE), lambda b, h: (*nxt(b, h), 0, 0)),
            pl.BlockSpec((1, 1, total, HEAD_PAD),
                         lambda b, h: (nxt(b, h)[0], kset(nxt(b, h)[1]), 0, 0)),
        ],
        out_specs=pl.BlockSpec((1, B_V_DIM, s), lambda b, h: (b, h, 0)),
        out_shape=jax.ShapeDtypeStruct((bsz, N_HEADS * B_V_DIM, s), _BF16),
        scratch_shapes=[pltpu.VMEM((total, ATTN_Q_TILE), _F32),
                        pltpu.VMEM((total, ATTN_Q_TILE), _F32),
                        pltpu.VMEM((1, ATTN_Q_TILE), _F32)],
        compiler_params=pltpu.CompilerParams(
            dimension_semantics=("arbitrary", "arbitrary"), vmem_limit_bytes=VMEM_LIMIT),
        name="attn",
    )(jnp.zeros((1,), jnp.int32), q_all, k_all, v_all, q_all, k_all)


def _merge_pieces(mrow, x, ot, wg_ref, wpa_ref, wpb_ref, wo_ref, g_ref, b_ref, result):
    shift, scale, gate = mrow[:, 0:D_MODEL], mrow[:, D_MODEL:2 * D_MODEL], mrow[:, 2 * D_MODEL:3 * D_MODEL]
    h = (_layer_norm(x) * (1.0 + scale) + shift).astype(_BF16)
    yield
    na = A_HEADS * A_HEAD_DIM
    ys = []
    for r in range(D_MODEL // MERGE_ROW_CHUNK):
        ra = pl.ds(r * MERGE_ROW_CHUNK, MERGE_ROW_CHUNK)
        rb = pl.ds(D_MODEL + r * MERGE_ROW_CHUNK, MERGE_ROW_CHUNK)
        ga = lax.dot_general(wg_ref[ra, :], h, _NT, preferred_element_type=_F32)
        gb = lax.dot_general(wg_ref[rb, :], h, _NT, preferred_element_type=_F32)
        ya = jnp.dot(wpa_ref[ra, :], ot[0:na], preferred_element_type=_F32)
        yb = jnp.dot(wpb_ref[ra, :], ot[na:], preferred_element_type=_F32)
        ys.append((_sigmoid(ga) * ya + _sigmoid(gb) * yb).astype(_BF16))
        yield
    zt = jnp.dot(wo_ref[...], jnp.concatenate(ys, axis=0), preferred_element_type=_F32)
    r = DEEPNORM_ALPHA * x + gate * zt.T
    result.append(_layer_norm(r) * g_ref[...] + b_ref[...])
    yield


def _ffn_pieces(mrow, x, wup_ref, wdn_ref, g_ref, b_ref, result):
    shift, scale, gate = mrow[:, 0:D_MODEL], mrow[:, D_MODEL:2 * D_MODEL], mrow[:, 2 * D_MODEL:3 * D_MODEL]
    h = (_layer_norm(x) * (1.0 + scale) + shift).astype(_BF16)
    yield
    fs = []
    for c in range(FFN_HIDDEN // FFN_COL_CHUNK):
        a = jnp.dot(h, wup_ref[:, pl.ds(c * FFN_COL_CHUNK, FFN_COL_CHUNK)], preferred_element_type=_F32)
        u = jnp.dot(h, wup_ref[:, pl.ds(FFN_HIDDEN + c * FFN_COL_CHUNK, FFN_COL_CHUNK)],
                    preferred_element_type=_F32)
        fs.append((a * _sigmoid(a) * u).astype(_BF16))
        yield
    dn = jnp.dot(jnp.concatenate(fs, axis=1), wdn_ref[...], preferred_element_type=_F32)
    r = DEEPNORM_ALPHA * x + gate * dn
    result.append(_layer_norm(r) * g_ref[...] + b_ref[...])
    yield


def _weave(major, minor, ratio, lead=0):
    for _ in range(lead):
        if next(major, StopIteration) is not StopIteration:
            yield
    live = [major, minor]
    while live:
        for gen, n in ((major, ratio), (minor, 1)):
            for _ in range(n):
                if gen in live:
                    if next(gen, StopIteration) is StopIteration:
                        live.remove(gen)
                    else:
                        yield


def _post_kernel(mod_ref, x_ref, ot_ref, wg_ref, wpa_ref, wpb_ref, wo_ref, g1_ref, b1_ref,
                 wup_ref, wdn_ref, g2_ref, b2_ref, out_ref, x1_ref, *, tiles_per_batch):
    j = pl.program_id(0)
    n_tiles = pl.num_programs(0) - 1
    half_batch = n_tiles // tiles_per_batch

    @pl.when(j == 0)
    def _():
        x1_ref[...] = jnp.zeros_like(x1_ref)

    b_prev = jnp.maximum(j - 1, 0) // tiles_per_batch
    b_cur = jnp.minimum(j, n_tiles - 1) // tiles_per_batch
    ffn_out, merge_out = ([], []), ([], [])

    def stream(half):
        off = half * half_batch
        return _weave(
            _ffn_pieces(mod_ref[pl.ds(b_prev + off, 1), 3 * D_MODEL:6 * D_MODEL], x1_ref[half],
                        wup_ref, wdn_ref, g2_ref, b2_ref, ffn_out[half]),
            _merge_pieces(mod_ref[pl.ds(b_cur + off, 1), 0:3 * D_MODEL], x_ref[half, 0],
                          ot_ref[half, 0], wg_ref, wpa_ref, wpb_ref, wo_ref, g1_ref, b1_ref,
                          merge_out[half]),
            ratio=POST_FFN_PIECES_PER_MERGE_PIECE)

    _interleave(stream(0), stream(1), ratio=1, lead=POST_STREAM_LEAD)
    for half in range(2):
        out_ref[half, 0] = ffn_out[half][0]
        x1_ref[half] = merge_out[half][0]


def _post(mod, x, ot, wgt, wpat, wpbt, wot, g1, b1, wup, wdn, g2, b2):
    bsz, s, d = x.shape
    t = TOKEN_TILE
    tpb = s // t
    hb = bsz // 2
    n_tiles = hb * tpb
    const = lambda a: pl.BlockSpec(a.shape, lambda j: (0,) * a.ndim, pipeline_mode=pl.Buffered(1))
    cur = lambda j: jnp.minimum(j, n_tiles - 1)
    prev = lambda j: jnp.maximum(j - 1, 0)
    out = pl.pallas_call(
        functools.partial(_post_kernel, tiles_per_batch=tpb),
        grid=(n_tiles + 1,),
        in_specs=[
            const(mod),
            pl.BlockSpec((2, 1, t, d), lambda j: (0, cur(j) // tpb, cur(j) % tpb, 0)),
            pl.BlockSpec((2, 1, ot.shape[1], t), lambda j: (0, cur(j) // tpb, 0, cur(j) % tpb)),
            const(wgt), const(wpat), const(wpbt), const(wot), const(g1), const(b1),
            const(wup), const(wdn), const(g2), const(b2),
        ],
        out_specs=pl.BlockSpec((2, 1, t, d), lambda j: (0, prev(j) // tpb, prev(j) % tpb, 0)),
        out_shape=jax.ShapeDtypeStruct((2, hb, s, d), _F32),
        scratch_shapes=[pltpu.VMEM((2, t, d), _F32)],
        compiler_params=pltpu.CompilerParams(
            dimension_semantics=("arbitrary",), vmem_limit_bytes=VMEM_LIMIT),
        name="post",
    )(mod, x.reshape(2, hb, s, d), ot.reshape(2, hb, ot.shape[1], s),
      wgt, wpat, wpbt, wot, g1, b1, wup, wdn, g2, b2)
    return out.reshape(bsz, s, d)


def _rope_tables(seq, half):
    pos = np.arange(seq)
    freqs = ROPE_THETA ** (-np.arange(half, dtype=np.float64) / half)
    tabs = []
    for p in (pos // GRID_W, pos % GRID_W):
        ang = p[None, :].astype(np.float64) * freqs[:, None]
        for fn, ident in ((np.cos, 1.0), (np.sin, 0.0)):
            tabs.append(np.concatenate([np.full((half, CTX_LEN), ident), fn(ang)], axis=1))
    return jnp.asarray(np.stack(tabs), dtype=_F32)


def kernel(x, c, ctx, c_ctx, w_mod, b_mod, w_in, q_norm_a, k_norm_a, cq_norm, ckv_norm,
           w_uq, w_ukv, w_proj_a, w_proj_b, w_out, ln1_g, ln1_b, w_up, w_down, ln2_g, ln2_b):
    bsz, s, d = x.shape
    assert w_mod.shape[0] == DEPTH == 1 and d == D_MODEL and ctx.shape[1] == CTX_LEN
    assert bsz == 8 and s % ATTN_Q_TILE == 0 and s % TOKEN_TILE == 0

    cc = jnp.concatenate([c, c_ctx[None, :], jnp.zeros((bsz - 1, d), _F32)], axis=0)
    mod = _modulation(cc, w_mod[0], b_mod[0][None, :])

    w_in0 = w_in[0]
    w1t = w_in0[:, :QKV_COLS].T.astype(_BF16)
    wgt = w_in0[:, QKV_COLS:].T.astype(_BF16)
    col = lambda v: v[0][:, None]
    tab_a = _rope_tables(s, A_HEAD_DIM // 4)
    tab_b = _rope_tables(s, B_ROPE_DIM // 4)

    q_all, k_all, v_all = _project(
        mod, ctx, x, w1t, w_uq[0].T.astype(_BF16), w_ukv[0].T.astype(_BF16),
        col(q_norm_a), col(k_norm_a), col(cq_norm), col(ckv_norm), tab_a, tab_b)
    ot = _attention(q_all, k_all, v_all)
    return _post(mod, x, ot, wgt, w_proj_a[0].T.astype(_BF16), w_proj_b[0].T.astype(_BF16),
                 w_out[0].T.astype(_BF16), ln1_g, ln1_b,
                 w_up[0].astype(_BF16), w_down[0].astype(_BF16), ln2_g, ln2_b)
```

```python
import functools
import math

import numpy as np
import jax
import jax.numpy as jnp
from jax import lax
from jax.experimental import pallas as pl
from jax.experimental.pallas import tpu as pltpu

D_MODEL = 1024
GRID_W = 64
CTX_LEN = 256
ROPE_THETA = 10000.0
EPS = 1e-6

A_HEADS = 8
A_KV_HEADS = 2
A_HEAD_DIM = 64
B_HEADS = 8
B_Q_RANK = 384
B_KV_RANK = 256
B_NOPE_DIM = 64
B_ROPE_DIM = 32
B_V_DIM = 64
FFN_HIDDEN = 2816
DEPTH = 1

A_SCALE = A_HEAD_DIM ** -0.5
B_SCALE = (B_NOPE_DIM + B_ROPE_DIM) ** -0.5
DEEPNORM_ALPHA = (2.0 * DEPTH) ** 0.25
LOG2_E = math.log2(math.e)

QA_OFF = 0
KA_OFF = QA_OFF + A_HEADS * A_HEAD_DIM
VA_OFF = KA_OFF + A_KV_HEADS * A_HEAD_DIM
CQ_OFF = VA_OFF + A_KV_HEADS * A_HEAD_DIM
CKV_OFF = CQ_OFF + B_Q_RANK
KR_OFF = CKV_OFF + B_KV_RANK
GATE_OFF = KR_OFF + B_ROPE_DIM
QKV_COLS = GATE_OFF

N_HEADS = A_HEADS + B_HEADS
HEAD_PAD = 128
N_KSETS = 1 + B_HEADS
V_ROWS = A_KV_HEADS * A_HEAD_DIM + B_HEADS * B_V_DIM

PROJ_TILE = 256
ATTN_Q_TILE = 512
ATTN_KEY_CHUNK = 256
ATTN_STREAM_LEAD = 2
ATTN_ONES_ROWS = 16
TOKEN_TILE = 256
MOD_COL_TILE = 1024
PROJ_ROW_CHUNK = 288
PROJ_STREAM_LEAD = 3
MERGE_ROW_CHUNK = 256
FFN_COL_CHUNK = 256
POST_FFN_PIECES_PER_MERGE_PIECE = 2
POST_STREAM_LEAD = 10
VMEM_LIMIT = 56 * 1024 * 1024

_NT = (((1,), (1,)), ((), ()))
_F32 = jnp.float32
_BF16 = jnp.bfloat16


def _layer_norm(x):
    mu = jnp.mean(x, axis=-1, keepdims=True)
    xc = x - mu
    var = jnp.mean(xc * xc, axis=-1, keepdims=True)
    return xc * lax.rsqrt(var + EPS)


def _sigmoid(x):
    return jax.nn.sigmoid(x)


def _mod_kernel(c_ref, w_ref, b_ref, o_ref):
    c = c_ref[...]
    a = (c * _sigmoid(c)).astype(_BF16)
    o_ref[...] = jnp.dot(a, w_ref[...].astype(_BF16), preferred_element_type=_F32) + b_ref[...]


def _modulation(cc, w_mod, b_mod):
    rows, d = cc.shape
    n = w_mod.shape[1]
    return pl.pallas_call(
        _mod_kernel,
        grid=(n // MOD_COL_TILE,),
        in_specs=[
            pl.BlockSpec((rows, d), lambda i: (0, 0)),
            pl.BlockSpec((d, MOD_COL_TILE), lambda i: (0, i)),
            pl.BlockSpec((1, MOD_COL_TILE), lambda i: (0, i)),
        ],
        out_specs=pl.BlockSpec((rows, MOD_COL_TILE), lambda i: (0, i)),
        out_shape=jax.ShapeDtypeStruct((rows, n), _F32),
        compiler_params=pltpu.CompilerParams(
            dimension_semantics=("arbitrary",), vmem_limit_bytes=VMEM_LIMIT),
        name="mod",
    )(cc, w_mod, b_mod)


def _rms_rows(x, g):
    ms = jnp.mean(x * x, axis=0, keepdims=True)
    return x * lax.rsqrt(ms + EPS) * g


def _axial_rope_rows(x, half, tab):
    r1, r2 = x[0:half], x[half:2 * half]
    c1, c2 = x[2 * half:3 * half], x[3 * half:4 * half]
    cr, sr, cc, sc = tab[0], tab[1], tab[2], tab[3]
    return jnp.concatenate(
        [r1 * cr - r2 * sr, r1 * sr + r2 * cr, c1 * cc - c2 * sc, c1 * sc + c2 * cc], axis=0)


def _interleave(major, minor, ratio, lead=0):
    for _ in range(lead):
        next(major, None)
    live = [major, minor]
    while live:
        for gen, n in ((major, ratio), (minor, 1)):
            for _ in range(n):
                if gen in live and next(gen, StopIteration) is StopIteration:
                    live.remove(gen)


def _proj_pieces(mrow, xin, w1_ref, wuq_ref, wukv_ref, qn_ref, kn_ref, cqn_ref, ckvn_ref, ta, tb,
                 q_out, k_out, v_out):
    shift, scale = mrow[:, 0:D_MODEL], mrow[:, D_MODEL:2 * D_MODEL]
    h = (_layer_norm(xin) * (1.0 + scale) + shift).astype(_BF16)
    yield
    parts = []
    for r in range(QKV_COLS // PROJ_ROW_CHUNK):
        rows = pl.ds(r * PROJ_ROW_CHUNK, PROJ_ROW_CHUNK)
        parts.append(lax.dot_general(w1_ref[rows, :], h, _NT, preferred_element_type=_F32))
        yield
    pt = jnp.concatenate(parts, axis=0)
    t = pt.shape[1]
    z64 = jnp.zeros((A_HEAD_DIM, t), _F32)
    z32 = jnp.zeros((HEAD_PAD - B_NOPE_DIM - B_ROPE_DIM, t), _F32)

    qn = qn_ref[...] * (A_SCALE * LOG2_E)
    group = A_HEADS // A_KV_HEADS
    for hd in range(A_HEADS):
        xh = pt[QA_OFF + hd * A_HEAD_DIM:QA_OFF + (hd + 1) * A_HEAD_DIM]
        xh = _axial_rope_rows(_rms_rows(xh, qn), A_HEAD_DIM // 4, ta)
        full = jnp.concatenate([xh, z64] if hd // group == 0 else [z64, xh], axis=0)
        q_out[hd] = full.astype(_BF16)
        if hd % 4 == 3:
            yield

    kn = kn_ref[...]
    ka = [
        _axial_rope_rows(
            _rms_rows(pt[KA_OFF + g * A_HEAD_DIM:KA_OFF + (g + 1) * A_HEAD_DIM], kn),
            A_HEAD_DIM // 4, ta)
        for g in range(A_KV_HEADS)
    ]
    k_out[0] = jnp.concatenate(ka, axis=0).T.astype(_BF16)
    v_out[0:A_KV_HEADS * A_HEAD_DIM, :] = pt[VA_OFF:CQ_OFF].astype(_BF16)
    yield

    cq = _rms_rows(pt[CQ_OFF:CKV_OFF], cqn_ref[...]).astype(_BF16)
    qb = jnp.dot(wuq_ref[...], cq, preferred_element_type=_F32) * (B_SCALE * LOG2_E)
    yield
    qdim = B_NOPE_DIM + B_ROPE_DIM
    for hd in range(B_HEADS):
        blk = qb[hd * qdim:(hd + 1) * qdim]
        rp = _axial_rope_rows(blk[B_NOPE_DIM:qdim], B_ROPE_DIM // 4, tb)
        full = jnp.concatenate([blk[0:B_NOPE_DIM], rp, z32], axis=0)
        q_out[A_HEADS + hd] = full.astype(_BF16)
        if hd % 4 == 3:
            yield

    ckv = _rms_rows(pt[CKV_OFF:KR_OFF], ckvn_ref[...]).astype(_BF16)
    kv = jnp.dot(wukv_ref[...], ckv, preferred_element_type=_F32)
    kr = _axial_rope_rows(pt[KR_OFF:GATE_OFF], B_ROPE_DIM // 4, tb)
    yield
    kvdim = B_NOPE_DIM + B_V_DIM
    v_base = A_KV_HEADS * A_HEAD_DIM
    for hd in range(B_HEADS):
        kn_h = kv[hd * kvdim:hd * kvdim + B_NOPE_DIM]
        v_h = kv[hd * kvdim + B_NOPE_DIM:(hd + 1) * kvdim]
        v_out[v_base + hd * B_V_DIM:v_base + (hd + 1) * B_V_DIM, :] = v_h.astype(_BF16)
        k_out[1 + hd] = jnp.concatenate([kn_h, kr, z32], axis=0).T.astype(_BF16)
        if hd % 4 == 3:
            yield


def _proj_kernel(mod_ref, ctx_ref, x_ref, w1_ref, wuq_ref, wukv_ref,
                 qn_ref, kn_ref, cqn_ref, ckvn_ref, ta_ref, tb_ref,
                 q_out, k_out, v_out):
    b = pl.program_id(0)
    j = pl.program_id(1)
    half_batch = pl.num_programs(0)
    is_ctx = j == 0
    ctx_row = mod_ref.shape[0] // 2
    ta = ta_ref[...]
    tb = tb_ref[...]

    def stream(half):
        xin = jnp.where(is_ctx, ctx_ref[half, 0], x_ref[half, 0])
        mrow = jnp.where(is_ctx, mod_ref[ctx_row:ctx_row + 1, 0:2 * D_MODEL],
                         mod_ref[pl.ds(b + half * half_batch, 1), 0:2 * D_MODEL])
        return _proj_pieces(mrow, xin, w1_ref, wuq_ref, wukv_ref, qn_ref, kn_ref, cqn_ref, ckvn_ref,
                            ta, tb, q_out.at[half, 0], k_out.at[half, 0], v_out.at[half, 0])

    _interleave(stream(0), stream(1), ratio=1, lead=PROJ_STREAM_LEAD)


def _project(mod, ctx, x, w1t, wuqt, wukvt, qn, kn, cqn, ckvn, tab_a, tab_b):
    bsz, s, d = x.shape
    t = PROJ_TILE
    n_steps = (CTX_LEN + s) // t
    total = CTX_LEN + s
    hb = bsz // 2
    lat = lambda j: jnp.maximum(j - 1, 0)
    full2 = lambda shape: pl.BlockSpec(shape, lambda b, j: (0, 0))
    q_all, k_all, v_all = pl.pallas_call(
        _proj_kernel,
        grid=(hb, n_steps),
        in_specs=[
            full2(mod.shape),
            pl.BlockSpec((2, 1, CTX_LEN, d), lambda b, j: (0, b, 0, 0)),
            pl.BlockSpec((2, 1, t, d), lambda b, j: (0, b, lat(j), 0)),
            full2(w1t.shape), full2(wuqt.shape), full2(wukvt.shape),
            full2(qn.shape), full2(kn.shape), full2(cqn.shape), full2(ckvn.shape),
            pl.BlockSpec((4, tab_a.shape[1], t), lambda b, j: (0, 0, j)),
            pl.BlockSpec((4, tab_b.shape[1], t), lambda b, j: (0, 0, j)),
        ],
        out_specs=[
            pl.BlockSpec((2, 1, N_HEADS, HEAD_PAD, t), lambda b, j: (0, b, 0, 0, lat(j))),
            pl.BlockSpec((2, 1, N_KSETS, t, HEAD_PAD), lambda b, j: (0, b, 0, j, 0)),
            pl.BlockSpec((2, 1, V_ROWS, t), lambda b, j: (0, b, 0, j)),
        ],
        out_shape=[
            jax.ShapeDtypeStruct((2, hb, N_HEADS, HEAD_PAD, s), _BF16),
            jax.ShapeDtypeStruct((2, hb, N_KSETS, total, HEAD_PAD), _BF16),
            jax.ShapeDtypeStruct((2, hb, V_ROWS, total), _BF16),
        ],
        compiler_params=pltpu.CompilerParams(
            dimension_semantics=("arbitrary", "arbitrary"), vmem_limit_bytes=VMEM_LIMIT),
        name="proj",
    )(mod, ctx.reshape(2, hb, CTX_LEN, d), x.reshape(2, hb, s, d),
      w1t, wuqt, wukvt, qn, kn, cqn, ckvn, tab_a, tab_b)
    return (q_all.reshape(bsz, N_HEADS, HEAD_PAD, s), k_all.reshape(bsz, N_KSETS, total, HEAD_PAD),
            v_all.reshape(bsz, V_ROWS, total))


def _attn_stream(base, q_ref, k_ref, v_ref, qn_ref, kn_ref, o_ref, o_rows, bufs, m0_ref, slot):
    tq, kc = ATTN_Q_TILE, ATTN_KEY_CHUNK
    n_tiles = q_ref.shape[3] // tq
    n_chunks = k_ref.shape[2] // kc
    assert n_tiles % 2 == 0
    assert n_tiles * tq == q_ref.shape[3] and n_chunks * kc == k_ref.shape[2]
    dv = v_ref.shape[1]
    ones = jnp.ones((ATTN_ONES_ROWS, kc), _BF16)

    def buf_rows(c):
        return pl.ds(pl.multiple_of(base + c * kc, kc), kc)

    def score_chunk(keys_ref, q_tile, buf, c, m):
        s = jnp.dot(keys_ref[0, 0, pl.ds(c * kc, kc), :], q_tile,
                    preferred_element_type=_F32)
        buf[buf_rows(c), :] = s
        cm = jnp.max(s, axis=0, keepdims=True)
        return cm if m is None else jnp.maximum(m, cm)

    def value_chunk(buf, c, m, acc):
        p = jnp.exp2(buf[buf_rows(c), :] - m)
        vext = jnp.concatenate([v_ref[0, :, pl.ds(c * kc, kc)], ones], axis=0)
        pv = jnp.dot(vext, p.astype(_BF16), preferred_element_type=_F32)
        return pv if acc is None else acc + pv

    @pl.when((pl.program_id(0) == 0) & (pl.program_id(1) == 0))
    def _():
        m = None
        for c in range(n_chunks):
            m = score_chunk(k_ref, q_ref[0, 0, :, 0:tq], bufs[0], c, m)
        m0_ref[slot] = m
    yield

    m_cur = m0_ref[slot]
    for t in range(n_tiles):
        m_next = acc = None
        for c in range(n_chunks):
            if t + 1 < n_tiles:
                m_next = score_chunk(k_ref, q_ref[0, 0, :, pl.ds((t + 1) * tq, tq)],
                                     bufs[(t + 1) % 2], c, m_next)
            else:
                m_next = score_chunk(kn_ref, qn_ref[0, 0], bufs[0], c, m_next)
            acc = value_chunk(bufs[t % 2], c, m_cur, acc)
            yield
        o_ref[0, o_rows, pl.ds(t * tq, tq)] = (acc[0:dv] / acc[dv:dv + 1]).astype(_BF16)
        m_cur = m_next
    m0_ref[slot] = m_cur


def _attn_kernel(base_ref, qa_ref, ka_ref, va_ref, qna_ref, kna_ref,
                 qb_ref, kb_ref, vb_ref, qnb_ref, knb_ref, o_ref,
                 sa0_ref, sa1_ref, sb0_ref, sb1_ref, m0_ref):
    base = base_ref[0]
    dv = va_ref.shape[1]
    first = _attn_stream(base, qa_ref, ka_ref, va_ref, qna_ref, kna_ref, o_ref, pl.ds(0, dv),
                         (sa0_ref, sa1_ref), m0_ref, 0)
    second = _attn_stream(base, qb_ref, kb_ref, vb_ref, qnb_ref, knb_ref, o_ref, pl.ds(dv, dv),
                          (sb0_ref, sb1_ref), m0_ref, 1)
    next(first), next(second)
    _interleave(first, second, ratio=1, lead=ATTN_STREAM_LEAD)


def _attention(q_all, k_all, v_all):
    bsz, _, _, s = q_all.shape
    total = k_all.shape[2]
    group = A_HEADS // A_KV_HEADS
    n_pairs = N_HEADS // 2
    kset = lambda h: jnp.where(h < A_HEADS, 0, h - A_HEADS + 1)
    vblk = lambda h: jnp.where(h < A_HEADS, h // group, h - A_HEADS + A_KV_HEADS)

    def nxt(b, g):
        n = jnp.minimum(b * n_pairs + g + 1, bsz * n_pairs - 1)
        return n // n_pairs, n % n_pairs

    def stream_specs(which):
        head = lambda g: 2 * g + which
        return [
            pl.BlockSpec((1, 1, HEAD_PAD, s), lambda b, g: (b, head(g), 0, 0)),
            pl.BlockSpec((1, 1, total, HEAD_PAD), lambda b, g: (b, kset(head(g)), 0, 0)),
            pl.BlockSpec((1, B_V_DIM, total), lambda b, g: (b, vblk(head(g)), 0)),
            pl.BlockSpec((1, 1, HEAD_PAD, ATTN_Q_TILE),
                         lambda b, g: (nxt(b, g)[0], head(nxt(b, g)[1]), 0, 0)),
            pl.BlockSpec((1, 1, total, HEAD_PAD),
                         lambda b, g: (nxt(b, g)[0], kset(head(nxt(b, g)[1])), 0, 0)),
        ]

    stream_args = (q_all, k_all, v_all, q_all, k_all)
    return pl.pallas_call(
        _attn_kernel,
        grid=(bsz, n_pairs),
        in_specs=[pl.BlockSpec(memory_space=pltpu.SMEM)] + stream_specs(0) + stream_specs(1),
        out_specs=pl.BlockSpec((1, 2 * B_V_DIM, s), lambda b, g: (b, g, 0)),
        out_shape=jax.ShapeDtypeStruct((bsz, N_HEADS * B_V_DIM, s), _BF16),
        scratch_shapes=[pltpu.VMEM((total, ATTN_Q_TILE), _F32)] * 4
        + [pltpu.VMEM((2, 1, ATTN_Q_TILE), _F32)],
        compiler_params=pltpu.CompilerParams(
            dimension_semantics=("arbitrary", "arbitrary"), vmem_limit_bytes=VMEM_LIMIT),
        name="attn",
    )(jnp.zeros((1,), jnp.int32), *stream_args, *stream_args)


def _merge_pieces(mrow, x, ot, wg_ref, wpa_ref, wpb_ref, wo_ref, g_ref, b_ref, result):
    shift, scale, gate = mrow[:, 0:D_MODEL], mrow[:, D_MODEL:2 * D_MODEL], mrow[:, 2 * D_MODEL:3 * D_MODEL]
    h = (_layer_norm(x) * (1.0 + scale) + shift).astype(_BF16)
    yield
    na = A_HEADS * A_HEAD_DIM
    ys = []
    for r in range(D_MODEL // MERGE_ROW_CHUNK):
        ra = pl.ds(r * MERGE_ROW_CHUNK, MERGE_ROW_CHUNK)
        rb = pl.ds(D_MODEL + r * MERGE_ROW_CHUNK, MERGE_ROW_CHUNK)
        ga = lax.dot_general(wg_ref[ra, :], h, _NT, preferred_element_type=_F32)
        gb = lax.dot_general(wg_ref[rb, :], h, _NT, preferred_element_type=_F32)
        ya = jnp.dot(wpa_ref[ra, :], ot[0:na], preferred_element_type=_F32)
        yb = jnp.dot(wpb_ref[ra, :], ot[na:], preferred_element_type=_F32)
        ys.append((_sigmoid(ga) * ya + _sigmoid(gb) * yb).astype(_BF16))
        yield
    zt = jnp.dot(wo_ref[...], jnp.concatenate(ys, axis=0), preferred_element_type=_F32)
    r = DEEPNORM_ALPHA * x + gate * zt.T
    result.append(_layer_norm(r) * g_ref[...] + b_ref[...])
    yield


def _ffn_pieces(mrow, x, wup_ref, wdn_ref, g_ref, b_ref, result):
    shift, scale, gate = mrow[:, 0:D_MODEL], mrow[:, D_MODEL:2 * D_MODEL], mrow[:, 2 * D_MODEL:3 * D_MODEL]
    h = (_layer_norm(x) * (1.0 + scale) + shift).astype(_BF16)
    yield
    fs = []
    for c in range(FFN_HIDDEN // FFN_COL_CHUNK):
        a = jnp.dot(h, wup_ref[:, pl.ds(c * FFN_COL_CHUNK, FFN_COL_CHUNK)], preferred_element_type=_F32)
        u = jnp.dot(h, wup_ref[:, pl.ds(FFN_HIDDEN + c * FFN_COL_CHUNK, FFN_COL_CHUNK)],
                    preferred_element_type=_F32)
        fs.append((a * _sigmoid(a) * u).astype(_BF16))
        yield
    dn = jnp.dot(jnp.concatenate(fs, axis=1), wdn_ref[...], preferred_element_type=_F32)
    r = DEEPNORM_ALPHA * x + gate * dn
    result.append(_layer_norm(r) * g_ref[...] + b_ref[...])
    yield


def _weave(major, minor, ratio, lead=0):
    for _ in range(lead):
        if next(major, StopIteration) is not StopIteration:
            yield
    live = [major, minor]
    while live:
        for gen, n in ((major, ratio), (minor, 1)):
            for _ in range(n):
                if gen in live:
                    if next(gen, StopIteration) is StopIteration:
                        live.remove(gen)
                    else:
                        yield


def _post_kernel(mod_ref, x_ref, ot_ref, wg_ref, wpa_ref, wpb_ref, wo_ref, g1_ref, b1_ref,
                 wup_ref, wdn_ref, g2_ref, b2_ref, out_ref, x1_ref, *, tiles_per_batch):
    j = pl.program_id(0)
    n_tiles = pl.num_programs(0) - 1
    half_batch = n_tiles // tiles_per_batch

    @pl.when(j == 0)
    def _():
        x1_ref[...] = jnp.zeros_like(x1_ref)

    b_prev = jnp.maximum(j - 1, 0) // tiles_per_batch
    b_cur = jnp.minimum(j, n_tiles - 1) // tiles_per_batch
    ffn_out, merge_out = ([], []), ([], [])

    def stream(half):
        off = half * half_batch
        return _weave(
            _ffn_pieces(mod_ref[pl.ds(b_prev + off, 1), 3 * D_MODEL:6 * D_MODEL], x1_ref[half],
                        wup_ref, wdn_ref, g2_ref, b2_ref, ffn_out[half]),
            _merge_pieces(mod_ref[pl.ds(b_cur + off, 1), 0:3 * D_MODEL], x_ref[half, 0],
                          ot_ref[half, 0], wg_ref, wpa_ref, wpb_ref, wo_ref, g1_ref, b1_ref,
                          merge_out[half]),
            ratio=POST_FFN_PIECES_PER_MERGE_PIECE)

    _interleave(stream(0), stream(1), ratio=1, lead=POST_STREAM_LEAD)
    for half in range(2):
        out_ref[half, 0] = ffn_out[half][0]
        x1_ref[half] = merge_out[half][0]


def _post(mod, x, ot, wgt, wpat, wpbt, wot, g1, b1, wup, wdn, g2, b2):
    bsz, s, d = x.shape
    t = TOKEN_TILE
    tpb = s // t
    hb = bsz // 2
    n_tiles = hb * tpb
    const = lambda a: pl.BlockSpec(a.shape, lambda j: (0,) * a.ndim, pipeline_mode=pl.Buffered(1))
    cur = lambda j: jnp.minimum(j, n_tiles - 1)
    prev = lambda j: jnp.maximum(j - 1, 0)
    out = pl.pallas_call(
        functools.partial(_post_kernel, tiles_per_batch=tpb),
        grid=(n_tiles + 1,),
        in_specs=[
            const(mod),
            pl.BlockSpec((2, 1, t, d), lambda j: (0, cur(j) // tpb, cur(j) % tpb, 0)),
            pl.BlockSpec((2, 1, ot.shape[1], t), lambda j: (0, cur(j) // tpb, 0, cur(j) % tpb)),
            const(wgt), const(wpat), const(wpbt), const(wot), const(g1), const(b1),
            const(wup), const(wdn), const(g2), const(b2),
        ],
        out_specs=pl.BlockSpec((2, 1, t, d), lambda j: (0, prev(j) // tpb, prev(j) % tpb, 0)),
        out_shape=jax.ShapeDtypeStruct((2, hb, s, d), _F32),
        scratch_shapes=[pltpu.VMEM((2, t, d), _F32)],
        compiler_params=pltpu.CompilerParams(
            dimension_semantics=("arbitrary",), vmem_limit_bytes=VMEM_LIMIT),
        name="post",
    )(mod, x.reshape(2, hb, s, d), ot.reshape(2, hb, ot.shape[1], s),
      wgt, wpat, wpbt, wot, g1, b1, wup, wdn, g2, b2)
    return out.reshape(bsz, s, d)


def _rope_tables(seq, half):
    pos = np.arange(seq)
    freqs = ROPE_THETA ** (-np.arange(half, dtype=np.float64) / half)
    tabs = []
    for p in (pos // GRID_W, pos % GRID_W):
        ang = p[None, :].astype(np.float64) * freqs[:, None]
        for fn, ident in ((np.cos, 1.0), (np.sin, 0.0)):
            tabs.append(np.concatenate([np.full((half, CTX_LEN), ident), fn(ang)], axis=1))
    return jnp.asarray(np.stack(tabs), dtype=_F32)


def kernel(x, c, ctx, c_ctx, w_mod, b_mod, w_in, q_norm_a, k_norm_a, cq_norm, ckv_norm,
           w_uq, w_ukv, w_proj_a, w_proj_b, w_out, ln1_g, ln1_b, w_up, w_down, ln2_g, ln2_b):
    bsz, s, d = x.shape
    assert w_mod.shape[0] == DEPTH == 1 and d == D_MODEL and ctx.shape[1] == CTX_LEN
    assert bsz == 8 and s % ATTN_Q_TILE == 0 and s % TOKEN_TILE == 0

    cc = jnp.concatenate([c, c_ctx[None, :], jnp.zeros((bsz - 1, d), _F32)], axis=0)
    mod = _modulation(cc, w_mod[0], b_mod[0][None, :])

    w_in0 = w_in[0]
    w1t = w_in0[:, :QKV_COLS].T.astype(_BF16)
    wgt = w_in0[:, QKV_COLS:].T.astype(_BF16)
    col = lambda v: v[0][:, None]
    tab_a = _rope_tables(s, A_HEAD_DIM // 4)
    tab_b = _rope_tables(s, B_ROPE_DIM // 4)

    q_all, k_all, v_all = _project(
        mod, ctx, x, w1t, w_uq[0].T.astype(_BF16), w_ukv[0].T.astype(_BF16),
        col(q_norm_a), col(k_norm_a), col(cq_norm), col(ckv_norm), tab_a, tab_b)
    ot = _attention(q_all, k_all, v_all)
    return _post(mod, x, ot, wgt, w_proj_a[0].T.astype(_BF16), w_proj_b[0].T.astype(_BF16),
                 w_out[0].T.astype(_BF16), ln1_g, ln1_b,
                 w_up[0].astype(_BF16), w_down[0].astype(_BF16), ln2_g, ln2_b)
```

```python
import functools
import math

import numpy as np
import jax
import jax.numpy as jnp
from jax import lax
from jax.experimental import pallas as pl
from jax.experimental.pallas import tpu as pltpu

D_MODEL = 1024
GRID_W = 64
CTX_LEN = 256
ROPE_THETA = 10000.0
EPS = 1e-6

A_HEADS = 8
A_KV_HEADS = 2
A_HEAD_DIM = 64
B_HEADS = 8
B_Q_RANK = 384
B_KV_RANK = 256
B_NOPE_DIM = 64
B_ROPE_DIM = 32
B_V_DIM = 64
FFN_HIDDEN = 2816
DEPTH = 1

A_SCALE = A_HEAD_DIM ** -0.5
B_SCALE = (B_NOPE_DIM + B_ROPE_DIM) ** -0.5
DEEPNORM_ALPHA = (2.0 * DEPTH) ** 0.25
LOG2_E = math.log2(math.e)

QA_OFF = 0
KA_OFF = QA_OFF + A_HEADS * A_HEAD_DIM
VA_OFF = KA_OFF + A_KV_HEADS * A_HEAD_DIM
CQ_OFF = VA_OFF + A_KV_HEADS * A_HEAD_DIM
CKV_OFF = CQ_OFF + B_Q_RANK
KR_OFF = CKV_OFF + B_KV_RANK
GATE_OFF = KR_OFF + B_ROPE_DIM
QKV_COLS = GATE_OFF

N_HEADS = A_HEADS + B_HEADS
HEAD_PAD = 128
N_KSETS = 1 + B_HEADS
V_ROWS = A_KV_HEADS * A_HEAD_DIM + B_HEADS * B_V_DIM

PROJ_TILE = 256
ATTN_Q_TILE = 512
ATTN_KEY_CHUNK = 256
ATTN_BUF_PAD_LANES = 128
ATTN_STREAM_LEAD = 2
ATTN_ONES_ROWS = 16
TOKEN_TILE = 256
MOD_COL_TILE = 1024
PROJ_ROW_CHUNK = 288
PROJ_STREAM_LEAD = 3
MERGE_ROW_CHUNK = 256
FFN_COL_CHUNK = 256
POST_FFN_PIECES_PER_MERGE_PIECE = 2
POST_STREAM_LEAD = 10
VMEM_LIMIT = 56 * 1024 * 1024

_NT = (((1,), (1,)), ((), ()))
_F32 = jnp.float32
_BF16 = jnp.bfloat16


def _layer_norm(x):
    mu = jnp.mean(x, axis=-1, keepdims=True)
    xc = x - mu
    var = jnp.mean(xc * xc, axis=-1, keepdims=True)
    return xc * lax.rsqrt(var + EPS)


def _sigmoid(x):
    return jax.nn.sigmoid(x)


def _mod_kernel(c_ref, w_ref, b_ref, o_ref):
    c = c_ref[...]
    a = (c * _sigmoid(c)).astype(_BF16)
    o_ref[...] = jnp.dot(a, w_ref[...].astype(_BF16), preferred_element_type=_F32) + b_ref[...]


def _modulation(cc, w_mod, b_mod):
    rows, d = cc.shape
    n = w_mod.shape[1]
    return pl.pallas_call(
        _mod_kernel,
        grid=(n // MOD_COL_TILE,),
        in_specs=[
            pl.BlockSpec((rows, d), lambda i: (0, 0)),
            pl.BlockSpec((d, MOD_COL_TILE), lambda i: (0, i)),
            pl.BlockSpec((1, MOD_COL_TILE), lambda i: (0, i)),
        ],
        out_specs=pl.BlockSpec((rows, MOD_COL_TILE), lambda i: (0, i)),
        out_shape=jax.ShapeDtypeStruct((rows, n), _F32),
        compiler_params=pltpu.CompilerParams(
            dimension_semantics=("arbitrary",), vmem_limit_bytes=VMEM_LIMIT),
        name="mod",
    )(cc, w_mod, b_mod)


def _rms_rows(x, g):
    ms = jnp.mean(x * x, axis=0, keepdims=True)
    return x * lax.rsqrt(ms + EPS) * g


def _axial_rope_rows(x, half, tab):
    r1, r2 = x[0:half], x[half:2 * half]
    c1, c2 = x[2 * half:3 * half], x[3 * half:4 * half]
    cr, sr, cc, sc = tab[0], tab[1], tab[2], tab[3]
    return jnp.concatenate(
        [r1 * cr - r2 * sr, r1 * sr + r2 * cr, c1 * cc - c2 * sc, c1 * sc + c2 * cc], axis=0)


def _interleave(major, minor, ratio, lead=0):
    for _ in range(lead):
        next(major, None)
    live = [major, minor]
    while live:
        for gen, n in ((major, ratio), (minor, 1)):
            for _ in range(n):
                if gen in live and next(gen, StopIteration) is StopIteration:
                    live.remove(gen)


def _proj_pieces(mrow, xin, w1_ref, wuq_ref, wukv_ref, qn_ref, kn_ref, cqn_ref, ckvn_ref, ta, tb,
                 q_out, k_out, v_out):
    shift, scale = mrow[:, 0:D_MODEL], mrow[:, D_MODEL:2 * D_MODEL]
    h = (_layer_norm(xin) * (1.0 + scale) + shift).astype(_BF16)
    yield
    parts = []
    for r in range(QKV_COLS // PROJ_ROW_CHUNK):
        rows = pl.ds(r * PROJ_ROW_CHUNK, PROJ_ROW_CHUNK)
        parts.append(lax.dot_general(w1_ref[rows, :], h, _NT, preferred_element_type=_F32))
        yield
    pt = jnp.concatenate(parts, axis=0)
    t = pt.shape[1]
    z64 = jnp.zeros((A_HEAD_DIM, t), _F32)
    z32 = jnp.zeros((HEAD_PAD - B_NOPE_DIM - B_ROPE_DIM, t), _F32)

    qn = qn_ref[...] * (A_SCALE * LOG2_E)
    group = A_HEADS // A_KV_HEADS
    for hd in range(A_HEADS):
        xh = pt[QA_OFF + hd * A_HEAD_DIM:QA_OFF + (hd + 1) * A_HEAD_DIM]
        xh = _axial_rope_rows(_rms_rows(xh, qn), A_HEAD_DIM // 4, ta)
        full = jnp.concatenate([xh, z64] if hd // group == 0 else [z64, xh], axis=0)
        q_out[hd] = full.astype(_BF16)
        if hd % 4 == 3:
            yield

    kn = kn_ref[...]
    ka = [
        _axial_rope_rows(
            _rms_rows(pt[KA_OFF + g * A_HEAD_DIM:KA_OFF + (g + 1) * A_HEAD_DIM], kn),
            A_HEAD_DIM // 4, ta)
        for g in range(A_KV_HEADS)
    ]
    k_out[0] = jnp.concatenate(ka, axis=0).T.astype(_BF16)
    v_out[0:A_KV_HEADS * A_HEAD_DIM, :] = pt[VA_OFF:CQ_OFF].astype(_BF16)
    yield

    cq = _rms_rows(pt[CQ_OFF:CKV_OFF], cqn_ref[...]).astype(_BF16)
    qb = jnp.dot(wuq_ref[...], cq, preferred_element_type=_F32) * (B_SCALE * LOG2_E)
    yield
    qdim = B_NOPE_DIM + B_ROPE_DIM
    for hd in range(B_HEADS):
        blk = qb[hd * qdim:(hd + 1) * qdim]
        rp = _axial_rope_rows(blk[B_NOPE_DIM:qdim], B_ROPE_DIM // 4, tb)
        full = jnp.concatenate([blk[0:B_NOPE_DIM], rp, z32], axis=0)
        q_out[A_HEADS + hd] = full.astype(_BF16)
        if hd % 4 == 3:
            yield

    ckv = _rms_rows(pt[CKV_OFF:KR_OFF], ckvn_ref[...]).astype(_BF16)
    kv = jnp.dot(wukv_ref[...], ckv, preferred_element_type=_F32)
    kr = _axial_rope_rows(pt[KR_OFF:GATE_OFF], B_ROPE_DIM // 4, tb)
    yield
    kvdim = B_NOPE_DIM + B_V_DIM
    v_base = A_KV_HEADS * A_HEAD_DIM
    for hd in range(B_HEADS):
        kn_h = kv[hd * kvdim:hd * kvdim + B_NOPE_DIM]
        v_h = kv[hd * kvdim + B_NOPE_DIM:(hd + 1) * kvdim]
        v_out[v_base + hd * B_V_DIM:v_base + (hd + 1) * B_V_DIM, :] = v_h.astype(_BF16)
        k_out[1 + hd] = jnp.concatenate([kn_h, kr, z32], axis=0).T.astype(_BF16)
        if hd % 4 == 3:
            yield


def _proj_kernel(mod_ref, ctx_ref, x_ref, w1_ref, wuq_ref, wukv_ref,
                 qn_ref, kn_ref, cqn_ref, ckvn_ref, ta_ref, tb_ref,
                 q_out, k_out, v_out):
    b = pl.program_id(0)
    j = pl.program_id(1)
    half_batch = pl.num_programs(0)
    is_ctx = j == 0
    ctx_row = mod_ref.shape[0] // 2
    ta = ta_ref[...]
    tb = tb_ref[...]

    def stream(half):
        xin = jnp.where(is_ctx, ctx_ref[half, 0], x_ref[half, 0])
        mrow = jnp.where(is_ctx, mod_ref[ctx_row:ctx_row + 1, 0:2 * D_MODEL],
                         mod_ref[pl.ds(b + half * half_batch, 1), 0:2 * D_MODEL])
        return _proj_pieces(mrow, xin, w1_ref, wuq_ref, wukv_ref, qn_ref, kn_ref, cqn_ref, ckvn_ref,
                            ta, tb, q_out.at[half, 0], k_out.at[half, 0], v_out.at[half, 0])

    _interleave(stream(0), stream(1), ratio=1, lead=PROJ_STREAM_LEAD)


def _project(mod, ctx, x, w1t, wuqt, wukvt, qn, kn, cqn, ckvn, tab_a, tab_b):
    bsz, s, d = x.shape
    t = PROJ_TILE
    n_steps = (CTX_LEN + s) // t
    total = CTX_LEN + s
    hb = bsz // 2
    lat = lambda j: jnp.maximum(j - 1, 0)
    full2 = lambda shape: pl.BlockSpec(shape, lambda b, j: (0, 0))
    q_all, k_all, v_all = pl.pallas_call(
        _proj_kernel,
        grid=(hb, n_steps),
        in_specs=[
            full2(mod.shape),
            pl.BlockSpec((2, 1, CTX_LEN, d), lambda b, j: (0, b, 0, 0)),
            pl.BlockSpec((2, 1, t, d), lambda b, j: (0, b, lat(j), 0)),
            full2(w1t.shape), full2(wuqt.shape), full2(wukvt.shape),
            full2(qn.shape), full2(kn.shape), full2(cqn.shape), full2(ckvn.shape),
            pl.BlockSpec((4, tab_a.shape[1], t), lambda b, j: (0, 0, j)),
            pl.BlockSpec((4, tab_b.shape[1], t), lambda b, j: (0, 0, j)),
        ],
        out_specs=[
            pl.BlockSpec((2, 1, N_HEADS, HEAD_PAD, t), lambda b, j: (0, b, 0, 0, lat(j))),
            pl.BlockSpec((2, 1, N_KSETS, t, HEAD_PAD), lambda b, j: (0, b, 0, j, 0)),
            pl.BlockSpec((2, 1, V_ROWS, t), lambda b, j: (0, b, 0, j)),
        ],
        out_shape=[
            jax.ShapeDtypeStruct((2, hb, N_HEADS, HEAD_PAD, s), _BF16),
            jax.ShapeDtypeStruct((2, hb, N_KSETS, total, HEAD_PAD), _BF16),
            jax.ShapeDtypeStruct((2, hb, V_ROWS, total), _BF16),
        ],
        compiler_params=pltpu.CompilerParams(
            dimension_semantics=("arbitrary", "arbitrary"), vmem_limit_bytes=VMEM_LIMIT),
        name="proj",
    )(mod, ctx.reshape(2, hb, CTX_LEN, d), x.reshape(2, hb, s, d),
      w1t, wuqt, wukvt, qn, kn, cqn, ckvn, tab_a, tab_b)
    return (q_all.reshape(bsz, N_HEADS, HEAD_PAD, s), k_all.reshape(bsz, N_KSETS, total, HEAD_PAD),
            v_all.reshape(bsz, V_ROWS, total))


def _attn_stream(base, q_ref, k_ref, v_ref, qn_ref, kn_ref, o_ref, o_rows, bufs, m0_ref, slot):
    tq, kc = ATTN_Q_TILE, ATTN_KEY_CHUNK
    n_tiles = q_ref.shape[3] // tq
    n_chunks = k_ref.shape[2] // kc
    assert n_tiles % 2 == 0
    assert n_tiles * tq == q_ref.shape[3] and n_chunks * kc == k_ref.shape[2]
    dv = v_ref.shape[1]
    ones = jnp.ones((ATTN_ONES_ROWS, kc), _BF16)

    def buf_rows(c):
        return pl.ds(pl.multiple_of(base + c * kc, kc), kc)

    def score_chunk(keys_ref, q_tile, buf, c, m):
        s = jnp.dot(keys_ref[0, 0, pl.ds(c * kc, kc), :], q_tile,
                    preferred_element_type=_F32)
        buf[buf_rows(c), 0:tq] = s
        cm = jnp.max(s, axis=0, keepdims=True)
        return cm if m is None else jnp.maximum(m, cm)

    def value_chunk(buf, c, m, acc):
        p = jnp.exp2(buf[buf_rows(c), 0:tq] - m)
        vext = jnp.concatenate([v_ref[0, :, pl.ds(c * kc, kc)], ones], axis=0)
        pv = jnp.dot(vext, p.astype(_BF16), preferred_element_type=_F32)
        return pv if acc is None else acc + pv

    @pl.when((pl.program_id(0) == 0) & (pl.program_id(1) == 0))
    def _():
        m = None
        for c in range(n_chunks):
            m = score_chunk(k_ref, q_ref[0, 0, :, 0:tq], bufs[0], c, m)
        m0_ref[slot] = m
    yield

    m_cur = m0_ref[slot]
    for t in range(n_tiles):
        m_next = acc = None
        for c in range(n_chunks):
            if t + 1 < n_tiles:
                m_next = score_chunk(k_ref, q_ref[0, 0, :, pl.ds((t + 1) * tq, tq)],
                                     bufs[(t + 1) % 2], c, m_next)
            else:
                m_next = score_chunk(kn_ref, qn_ref[0, 0], bufs[0], c, m_next)
            acc = value_chunk(bufs[t % 2], c, m_cur, acc)
            yield
        o_ref[0, o_rows, pl.ds(t * tq, tq)] = (acc[0:dv] / acc[dv:dv + 1]).astype(_BF16)
        m_cur = m_next
    m0_ref[slot] = m_cur


def _attn_kernel(base_ref, qa_ref, ka_ref, va_ref, qna_ref, kna_ref,
                 qb_ref, kb_ref, vb_ref, qnb_ref, knb_ref, o_ref,
                 sa0_ref, sa1_ref, sb0_ref, sb1_ref, m0_ref):
    base = base_ref[0]
    dv = va_ref.shape[1]
    first = _attn_stream(base, qa_ref, ka_ref, va_ref, qna_ref, kna_ref, o_ref, pl.ds(0, dv),
                         (sa0_ref, sa1_ref), m0_ref, 0)
    second = _attn_stream(base, qb_ref, kb_ref, vb_ref, qnb_ref, knb_ref, o_ref, pl.ds(dv, dv),
                          (sb0_ref, sb1_ref), m0_ref, 1)
    next(first), next(second)
    _interleave(first, second, ratio=1, lead=ATTN_STREAM_LEAD)


def _attention(q_all, k_all, v_all):
    bsz, _, _, s = q_all.shape
    total = k_all.shape[2]
    group = A_HEADS // A_KV_HEADS
    n_pairs = N_HEADS // 2
    kset = lambda h: jnp.where(h < A_HEADS, 0, h - A_HEADS + 1)
    vblk = lambda h: jnp.where(h < A_HEADS, h // group, h - A_HEADS + A_KV_HEADS)

    def nxt(b, g):
        n = jnp.minimum(b * n_pairs + g + 1, bsz * n_pairs - 1)
        return n // n_pairs, n % n_pairs

    def stream_specs(which):
        head = lambda g: 2 * g + which
        return [
            pl.BlockSpec((1, 1, HEAD_PAD, s), lambda b, g: (b, head(g), 0, 0)),
            pl.BlockSpec((1, 1, total, HEAD_PAD), lambda b, g: (b, kset(head(g)), 0, 0)),
            pl.BlockSpec((1, B_V_DIM, total), lambda b, g: (b, vblk(head(g)), 0)),
            pl.BlockSpec((1, 1, HEAD_PAD, ATTN_Q_TILE),
                         lambda b, g: (nxt(b, g)[0], head(nxt(b, g)[1]), 0, 0)),
            pl.BlockSpec((1, 1, total, HEAD_PAD),
                         lambda b, g: (nxt(b, g)[0], kset(head(nxt(b, g)[1])), 0, 0)),
        ]

    stream_args = (q_all, k_all, v_all, q_all, k_all)
    return pl.pallas_call(
        _attn_kernel,
        grid=(bsz, n_pairs),
        in_specs=[pl.BlockSpec(memory_space=pltpu.SMEM)] + stream_specs(0) + stream_specs(1),
        out_specs=pl.BlockSpec((1, 2 * B_V_DIM, s), lambda b, g: (b, g, 0)),
        out_shape=jax.ShapeDtypeStruct((bsz, N_HEADS * B_V_DIM, s), _BF16),
        scratch_shapes=[pltpu.VMEM((total, ATTN_Q_TILE + ATTN_BUF_PAD_LANES), _F32)] * 4
        + [pltpu.VMEM((2, 1, ATTN_Q_TILE), _F32)],
        compiler_params=pltpu.CompilerParams(
            dimension_semantics=("arbitrary", "arbitrary"), vmem_limit_bytes=VMEM_LIMIT),
        name="attn",
    )(jnp.zeros((1,), jnp.int32), *stream_args, *stream_args)


def _merge_pieces(mrow, x, ot, wg_ref, wpa_ref, wpb_ref, wo_ref, g_ref, b_ref, result):
    shift, scale, gate = mrow[:, 0:D_MODEL], mrow[:, D_MODEL:2 * D_MODEL], mrow[:, 2 * D_MODEL:3 * D_MODEL]
    h = (_layer_norm(x) * (1.0 + scale) + shift).astype(_BF16)
    yield
    na = A_HEADS * A_HEAD_DIM
    ys = []
    for r in range(D_MODEL // MERGE_ROW_CHUNK):
        ra = pl.ds(r * MERGE_ROW_CHUNK, MERGE_ROW_CHUNK)
        rb = pl.ds(D_MODEL + r * MERGE_ROW_CHUNK, MERGE_ROW_CHUNK)
        ga = lax.dot_general(wg_ref[ra, :], h, _NT, preferred_element_type=_F32)
        gb = lax.dot_general(wg_ref[rb, :], h, _NT, preferred_element_type=_F32)
        ya = jnp.dot(wpa_ref[ra, :], ot[0:na], preferred_element_type=_F32)
        yb = jnp.dot(wpb_ref[ra, :], ot[na:], preferred_element_type=_F32)
        ys.append((_sigmoid(ga) * ya + _sigmoid(gb) * yb).astype(_BF16))
        yield
    zt = jnp.dot(wo_ref[...], jnp.concatenate(ys, axis=0), preferred_element_type=_F32)
    r = DEEPNORM_ALPHA * x + gate * zt.T
    result.append(_layer_norm(r) * g_ref[...] + b_ref[...])
    yield


def _ffn_pieces(mrow, x, wup_ref, wdn_ref, g_ref, b_ref, result):
    shift, scale, gate = mrow[:, 0:D_MODEL], mrow[:, D_MODEL:2 * D_MODEL], mrow[:, 2 * D_MODEL:3 * D_MODEL]
    h = (_layer_norm(x) * (1.0 + scale) + shift).astype(_BF16)
    yield
    fs = []
    for c in range(FFN_HIDDEN // FFN_COL_CHUNK):
        a = jnp.dot(h, wup_ref[:, pl.ds(c * FFN_COL_CHUNK, FFN_COL_CHUNK)], preferred_element_type=_F32)
        u = jnp.dot(h, wup_ref[:, pl.ds(FFN_HIDDEN + c * FFN_COL_CHUNK, FFN_COL_CHUNK)],
                    preferred_element_type=_F32)
        fs.append((a * _sigmoid(a) * u).astype(_BF16))
        yield
    dn = jnp.dot(jnp.concatenate(fs, axis=1), wdn_ref[...], preferred_element_type=_F32)
    r = DEEPNORM_ALPHA * x + gate * dn
    result.append(_layer_norm(r) * g_ref[...] + b_ref[...])
    yield


def _weave(major, minor, ratio, lead=0):
    for _ in range(lead):
        if next(major, StopIteration) is not StopIteration:
            yield
    live = [major, minor]
    while live:
        for gen, n in ((major, ratio), (minor, 1)):
            for _ in range(n):
                if gen in live:
                    if next(gen, StopIteration) is StopIteration:
                        live.remove(gen)
                    else:
                        yield


def _post_kernel(mod_ref, x_ref, ot_ref, wg_ref, wpa_ref, wpb_ref, wo_ref, g1_ref, b1_ref,
                 wup_ref, wdn_ref, g2_ref, b2_ref, out_ref, x1_ref, *, tiles_per_batch):
    j = pl.program_id(0)
    n_tiles = pl.num_programs(0) - 1
    half_batch = n_tiles // tiles_per_batch

    @pl.when(j == 0)
    def _():
        x1_ref[...] = jnp.zeros_like(x1_ref)

    b_prev = jnp.maximum(j - 1, 0) // tiles_per_batch
    b_cur = jnp.minimum(j, n_tiles - 1) // tiles_per_batch
    ffn_out, merge_out = ([], []), ([], [])

    def stream(half):
        off = half * half_batch
        return _weave(
            _ffn_pieces(mod_ref[pl.ds(b_prev + off, 1), 3 * D_MODEL:6 * D_MODEL], x1_ref[half],
                        wup_ref, wdn_ref, g2_ref, b2_ref, ffn_out[half]),
            _merge_pieces(mod_ref[pl.ds(b_cur + off, 1), 0:3 * D_MODEL], x_ref[half, 0],
                          ot_ref[half, 0], wg_ref, wpa_ref, wpb_ref, wo_ref, g1_ref, b1_ref,
                          merge_out[half]),
            ratio=POST_FFN_PIECES_PER_MERGE_PIECE)

    _interleave(stream(0), stream(1), ratio=1, lead=POST_STREAM_LEAD)
    for half in range(2):
        out_ref[half, 0] = ffn_out[half][0]
        x1_ref[half] = merge_out[half][0]


def _post(mod, x, ot, wgt, wpat, wpbt, wot, g1, b1, wup, wdn, g2, b2):
    bsz, s, d = x.shape
    t = TOKEN_TILE
    tpb = s // t
    hb = bsz // 2
    n_tiles = hb * tpb
    const = lambda a: pl.BlockSpec(a.shape, lambda j: (0,) * a.ndim, pipeline_mode=pl.Buffered(1))
    cur = lambda j: jnp.minimum(j, n_tiles - 1)
    prev = lambda j: jnp.maximum(j - 1, 0)
    out = pl.pallas_call(
        functools.partial(_post_kernel, tiles_per_batch=tpb),
        grid=(n_tiles + 1,),
        in_specs=[
            const(mod),
            pl.BlockSpec((2, 1, t, d), lambda j: (0, cur(j) // tpb, cur(j) % tpb, 0)),
            pl.BlockSpec((2, 1, ot.shape[1], t), lambda j: (0, cur(j) // tpb, 0, cur(j) % tpb)),
            const(wgt), const(wpat), const(wpbt), const(wot), const(g1), const(b1),
            const(wup), const(wdn), const(g2), const(b2),
        ],
        out_specs=pl.BlockSpec((2, 1, t, d), lambda j: (0, prev(j) // tpb, prev(j) % tpb, 0)),
        out_shape=jax.ShapeDtypeStruct((2, hb, s, d), _F32),
        scratch_shapes=[pltpu.VMEM((2, t, d), _F32)],
        compiler_params=pltpu.CompilerParams(
            dimension_semantics=("arbitrary",), vmem_limit_bytes=VMEM_LIMIT),
        name="post",
    )(mod, x.reshape(2, hb, s, d), ot.reshape(2, hb, ot.shape[1], s),
      wgt, wpat, wpbt, wot, g1, b1, wup, wdn, g2, b2)
    return out.reshape(bsz, s, d)


def _rope_tables(seq, half):
    pos = np.arange(seq)
    freqs = ROPE_THETA ** (-np.arange(half, dtype=np.float64) / half)
    tabs = []
    for p in (pos // GRID_W, pos % GRID_W):
        ang = p[None, :].astype(np.float64) * freqs[:, None]
        for fn, ident in ((np.cos, 1.0), (np.sin, 0.0)):
            tabs.append(np.concatenate([np.full((half, CTX_LEN), ident), fn(ang)], axis=1))
    return jnp.asarray(np.stack(tabs), dtype=_F32)


def kernel(x, c, ctx, c_ctx, w_mod, b_mod, w_in, q_norm_a, k_norm_a, cq_norm, ckv_norm,
           w_uq, w_ukv, w_proj_a, w_proj_b, w_out, ln1_g, ln1_b, w_up, w_down, ln2_g, ln2_b):
    bsz, s, d = x.shape
    assert w_mod.shape[0] == DEPTH == 1 and d == D_MODEL and ctx.shape[1] == CTX_LEN
    assert bsz == 8 and s % ATTN_Q_TILE == 0 and s % TOKEN_TILE == 0

    cc = jnp.concatenate([c, c_ctx[None, :], jnp.zeros((bsz - 1, d), _F32)], axis=0)
    mod = _modulation(cc, w_mod[0], b_mod[0][None, :])

    w_in0 = w_in[0]
    w1t = w_in0[:, :QKV_COLS].T.astype(_BF16)
    wgt = w_in0[:, QKV_COLS:].T.astype(_BF16)
    col = lambda v: v[0][:, None]
    tab_a = _rope_tables(s, A_HEAD_DIM // 4)
    tab_b = _rope_tables(s, B_ROPE_DIM // 4)

    q_all, k_all, v_all = _project(
        mod, ctx, x, w1t, w_uq[0].T.astype(_BF16), w_ukv[0].T.astype(_BF16),
        col(q_norm_a), col(k_norm_a), col(cq_norm), col(ckv_norm), tab_a, tab_b)
    ot = _attention(q_all, k_all, v_all)
    return _post(mod, x, ot, wgt, w_proj_a[0].T.astype(_BF16), w_proj_b[0].T.astype(_BF16),
                 w_out[0].T.astype(_BF16), ln1_g, ln1_b,
                 w_up[0].astype(_BF16), w_down[0].astype(_BF16), ln2_g, ln2_b)
```

```python
import functools
import math

import numpy as np
import jax
import jax.numpy as jnp
from jax import lax
from jax.experimental import pallas as pl
from jax.experimental.pallas import tpu as pltpu

D_MODEL = 1024
GRID_W = 64
CTX_LEN = 256
ROPE_THETA = 10000.0
EPS = 1e-6

A_HEADS = 8
A_KV_HEADS = 2
A_HEAD_DIM = 64
B_HEADS = 8
B_Q_RANK = 384
B_KV_RANK = 256
B_NOPE_DIM = 64
B_ROPE_DIM = 32
B_V_DIM = 64
FFN_HIDDEN = 2816
DEPTH = 1

A_SCALE = A_HEAD_DIM ** -0.5
B_SCALE = (B_NOPE_DIM + B_ROPE_DIM) ** -0.5
DEEPNORM_ALPHA = (2.0 * DEPTH) ** 0.25
LOG2_E = math.log2(math.e)

QA_OFF = 0
KA_OFF = QA_OFF + A_HEADS * A_HEAD_DIM
VA_OFF = KA_OFF + A_KV_HEADS * A_HEAD_DIM
CQ_OFF = VA_OFF + A_KV_HEADS * A_HEAD_DIM
CKV_OFF = CQ_OFF + B_Q_RANK
KR_OFF = CKV_OFF + B_KV_RANK
GATE_OFF = KR_OFF + B_ROPE_DIM
QKV_COLS = GATE_OFF

N_HEADS = A_HEADS + B_HEADS
HEAD_PAD = 128
N_KSETS = 1 + B_HEADS
V_ROWS = A_KV_HEADS * A_HEAD_DIM + B_HEADS * B_V_DIM

PROJ_TILE = 256
ATTN_Q_TILE = 512
ATTN_KEY_CHUNK = 256
ATTN_STREAM_LEAD = 2
ATTN_ONES_ROWS = 16
TOKEN_TILE = 256
MOD_COL_TILE = 1024
PROJ_ROW_CHUNK = 288
PROJ_STREAMS = 4
PROJ_STREAM_LEAD = 3
MERGE_ROW_CHUNK = 256
FFN_COL_CHUNK = 256
POST_FFN_PIECES_PER_MERGE_PIECE = 2
POST_STREAM_LEAD = 10
VMEM_LIMIT = 56 * 1024 * 1024

_NT = (((1,), (1,)), ((), ()))
_F32 = jnp.float32
_BF16 = jnp.bfloat16


def _layer_norm(x):
    mu = jnp.mean(x, axis=-1, keepdims=True)
    xc = x - mu
    var = jnp.mean(xc * xc, axis=-1, keepdims=True)
    return xc * lax.rsqrt(var + EPS)


def _sigmoid(x):
    return jax.nn.sigmoid(x)


def _mod_kernel(c_ref, w_ref, b_ref, o_ref):
    c = c_ref[...]
    a = (c * _sigmoid(c)).astype(_BF16)
    o_ref[...] = jnp.dot(a, w_ref[...].astype(_BF16), preferred_element_type=_F32) + b_ref[...]


def _modulation(cc, w_mod, b_mod):
    rows, d = cc.shape
    n = w_mod.shape[1]
    return pl.pallas_call(
        _mod_kernel,
        grid=(n // MOD_COL_TILE,),
        in_specs=[
            pl.BlockSpec((rows, d), lambda i: (0, 0)),
            pl.BlockSpec((d, MOD_COL_TILE), lambda i: (0, i)),
            pl.BlockSpec((1, MOD_COL_TILE), lambda i: (0, i)),
        ],
        out_specs=pl.BlockSpec((rows, MOD_COL_TILE), lambda i: (0, i)),
        out_shape=jax.ShapeDtypeStruct((rows, n), _F32),
        compiler_params=pltpu.CompilerParams(
            dimension_semantics=("arbitrary",), vmem_limit_bytes=VMEM_LIMIT),
        name="mod",
    )(cc, w_mod, b_mod)


def _rms_rows(x, g):
    ms = jnp.mean(x * x, axis=0, keepdims=True)
    return x * lax.rsqrt(ms + EPS) * g


def _axial_rope_rows(x, half, tab):
    r1, r2 = x[0:half], x[half:2 * half]
    c1, c2 = x[2 * half:3 * half], x[3 * half:4 * half]
    cr, sr, cc, sc = tab[0], tab[1], tab[2], tab[3]
    return jnp.concatenate(
        [r1 * cr - r2 * sr, r1 * sr + r2 * cr, c1 * cc - c2 * sc, c1 * sc + c2 * cc], axis=0)


def _stagger(streams, lead=0, rates=None):
    rates = rates or [1] * len(streams)
    live = dict(enumerate(streams))
    rnd = 0
    while live:
        for i in sorted(live):
            if rnd < i * lead:
                continue
            for _ in range(rates[i]):
                if next(live[i], StopIteration) is StopIteration:
                    del live[i]
                    break
                yield
        rnd += 1


def _trace(pieces):
    for _ in pieces:
        pass


def _proj_pieces(mrow, xin, w1_ref, wuq_ref, wukv_ref, qn_ref, kn_ref, cqn_ref, ckvn_ref, ta, tb,
                 q_out, k_out, v_out):
    shift, scale = mrow[:, 0:D_MODEL], mrow[:, D_MODEL:2 * D_MODEL]
    h = (_layer_norm(xin) * (1.0 + scale) + shift).astype(_BF16)
    yield
    parts = []
    for r in range(QKV_COLS // PROJ_ROW_CHUNK):
        rows = pl.ds(r * PROJ_ROW_CHUNK, PROJ_ROW_CHUNK)
        parts.append(lax.dot_general(w1_ref[rows, :], h, _NT, preferred_element_type=_F32))
        yield
    pt = jnp.concatenate(parts, axis=0)
    t = pt.shape[1]
    z64 = jnp.zeros((A_HEAD_DIM, t), _F32)
    z32 = jnp.zeros((HEAD_PAD - B_NOPE_DIM - B_ROPE_DIM, t), _F32)

    qn = qn_ref[...] * (A_SCALE * LOG2_E)
    group = A_HEADS // A_KV_HEADS
    for hd in range(A_HEADS):
        xh = pt[QA_OFF + hd * A_HEAD_DIM:QA_OFF + (hd + 1) * A_HEAD_DIM]
        xh = _axial_rope_rows(_rms_rows(xh, qn), A_HEAD_DIM // 4, ta)
        full = jnp.concatenate([xh, z64] if hd // group == 0 else [z64, xh], axis=0)
        q_out[hd] = full.astype(_BF16)
        if hd % 4 == 3:
            yield

    kn = kn_ref[...]
    ka = [
        _axial_rope_rows(
            _rms_rows(pt[KA_OFF + g * A_HEAD_DIM:KA_OFF + (g + 1) * A_HEAD_DIM], kn),
            A_HEAD_DIM // 4, ta)
        for g in range(A_KV_HEADS)
    ]
    k_out[0] = jnp.concatenate(ka, axis=0).T.astype(_BF16)
    v_out[0:A_KV_HEADS * A_HEAD_DIM, :] = pt[VA_OFF:CQ_OFF].astype(_BF16)
    yield

    cq = _rms_rows(pt[CQ_OFF:CKV_OFF], cqn_ref[...]).astype(_BF16)
    qb = jnp.dot(wuq_ref[...], cq, preferred_element_type=_F32) * (B_SCALE * LOG2_E)
    yield
    qdim = B_NOPE_DIM + B_ROPE_DIM
    for hd in range(B_HEADS):
        blk = qb[hd * qdim:(hd + 1) * qdim]
        rp = _axial_rope_rows(blk[B_NOPE_DIM:qdim], B_ROPE_DIM // 4, tb)
        full = jnp.concatenate([blk[0:B_NOPE_DIM], rp, z32], axis=0)
        q_out[A_HEADS + hd] = full.astype(_BF16)
        if hd % 4 == 3:
            yield

    ckv = _rms_rows(pt[CKV_OFF:KR_OFF], ckvn_ref[...]).astype(_BF16)
    kv = jnp.dot(wukv_ref[...], ckv, preferred_element_type=_F32)
    kr = _axial_rope_rows(pt[KR_OFF:GATE_OFF], B_ROPE_DIM // 4, tb)
    yield
    kvdim = B_NOPE_DIM + B_V_DIM
    v_base = A_KV_HEADS * A_HEAD_DIM
    for hd in range(B_HEADS):
        kn_h = kv[hd * kvdim:hd * kvdim + B_NOPE_DIM]
        v_h = kv[hd * kvdim + B_NOPE_DIM:(hd + 1) * kvdim]
        v_out[v_base + hd * B_V_DIM:v_base + (hd + 1) * B_V_DIM, :] = v_h.astype(_BF16)
        k_out[1 + hd] = jnp.concatenate([kn_h, kr, z32], axis=0).T.astype(_BF16)
        if hd % 4 == 3:
            yield


def _proj_kernel(mod_ref, ctx_ref, x_ref, w1_ref, wuq_ref, wukv_ref,
                 qn_ref, kn_ref, cqn_ref, ckvn_ref, ta_ref, tb_ref,
                 q_out, k_out, v_out):
    b = pl.program_id(0)
    j = pl.program_id(1)
    batch_stride = pl.num_programs(0)
    is_ctx = j == 0
    ctx_row = mod_ref.shape[0] // 2
    ta = ta_ref[...]
    tb = tb_ref[...]

    def stream(i):
        xin = jnp.where(is_ctx, ctx_ref[i, 0], x_ref[i, 0])
        mrow = jnp.where(is_ctx, mod_ref[ctx_row:ctx_row + 1, 0:2 * D_MODEL],
                         mod_ref[pl.ds(b + i * batch_stride, 1), 0:2 * D_MODEL])
        return _proj_pieces(mrow, xin, w1_ref, wuq_ref, wukv_ref, qn_ref, kn_ref, cqn_ref, ckvn_ref,
                            ta, tb, q_out.at[i, 0], k_out.at[i, 0], v_out.at[i, 0])

    _trace(_stagger([stream(i) for i in range(PROJ_STREAMS)], lead=PROJ_STREAM_LEAD))


def _project(mod, ctx, x, w1t, wuqt, wukvt, qn, kn, cqn, ckvn, tab_a, tab_b):
    bsz, s, d = x.shape
    t = PROJ_TILE
    n_steps = (CTX_LEN + s) // t
    total = CTX_LEN + s
    ns = PROJ_STREAMS
    hb = bsz // ns
    lat = lambda j: jnp.maximum(j - 1, 0)
    full2 = lambda shape: pl.BlockSpec(shape, lambda b, j: (0, 0))
    q_all, k_all, v_all = pl.pallas_call(
        _proj_kernel,
        grid=(hb, n_steps),
        in_specs=[
            full2(mod.shape),
            pl.BlockSpec((ns, 1, CTX_LEN, d), lambda b, j: (0, b, 0, 0)),
            pl.BlockSpec((ns, 1, t, d), lambda b, j: (0, b, lat(j), 0)),
            full2(w1t.shape), full2(wuqt.shape), full2(wukvt.shape),
            full2(qn.shape), full2(kn.shape), full2(cqn.shape), full2(ckvn.shape),
            pl.BlockSpec((4, tab_a.shape[1], t), lambda b, j: (0, 0, j)),
            pl.BlockSpec((4, tab_b.shape[1], t), lambda b, j: (0, 0, j)),
        ],
        out_specs=[
            pl.BlockSpec((ns, 1, N_HEADS, HEAD_PAD, t), lambda b, j: (0, b, 0, 0, lat(j))),
            pl.BlockSpec((ns, 1, N_KSETS, t, HEAD_PAD), lambda b, j: (0, b, 0, j, 0)),
            pl.BlockSpec((ns, 1, V_ROWS, t), lambda b, j: (0, b, 0, j)),
        ],
        out_shape=[
            jax.ShapeDtypeStruct((ns, hb, N_HEADS, HEAD_PAD, s), _BF16),
            jax.ShapeDtypeStruct((ns, hb, N_KSETS, total, HEAD_PAD), _BF16),
            jax.ShapeDtypeStruct((ns, hb, V_ROWS, total), _BF16),
        ],
        compiler_params=pltpu.CompilerParams(
            dimension_semantics=("arbitrary", "arbitrary"), vmem_limit_bytes=VMEM_LIMIT),
        name="proj",
    )(mod, ctx.reshape(ns, hb, CTX_LEN, d), x.reshape(ns, hb, s, d),
      w1t, wuqt, wukvt, qn, kn, cqn, ckvn, tab_a, tab_b)
    return (q_all.reshape(bsz, N_HEADS, HEAD_PAD, s), k_all.reshape(bsz, N_KSETS, total, HEAD_PAD),
            v_all.reshape(bsz, V_ROWS, total))


def _attn_stream(base, q_ref, k_ref, v_ref, qn_ref, kn_ref, o_ref, o_rows, bufs, m0_ref, slot):
    tq, kc = ATTN_Q_TILE, ATTN_KEY_CHUNK
    n_tiles = q_ref.shape[3] // tq
    n_chunks = k_ref.shape[2] // kc
    assert n_tiles % 2 == 0
    assert n_tiles * tq == q_ref.shape[3] and n_chunks * kc == k_ref.shape[2]
    dv = v_ref.shape[1]
    ones = jnp.ones((ATTN_ONES_ROWS, kc), _BF16)

    def buf_rows(c):
        return pl.ds(pl.multiple_of(base + c * kc, kc), kc)

    def score_chunk(keys_ref, q_tile, buf, c, m):
        s = jnp.dot(keys_ref[0, 0, pl.ds(c * kc, kc), :], q_tile,
                    preferred_element_type=_F32)
        buf[buf_rows(c), :] = s
        cm = jnp.max(s, axis=0, keepdims=True)
        return cm if m is None else jnp.maximum(m, cm)

    def value_chunk(buf, c, m, acc):
        p = jnp.exp2(buf[buf_rows(c), :] - m)
        vext = jnp.concatenate([v_ref[0, :, pl.ds(c * kc, kc)], ones], axis=0)
        pv = jnp.dot(vext, p.astype(_BF16), preferred_element_type=_F32)
        return pv if acc is None else acc + pv

    @pl.when((pl.program_id(0) == 0) & (pl.program_id(1) == 0))
    def _():
        m = None
        for c in range(n_chunks):
            m = score_chunk(k_ref, q_ref[0, 0, :, 0:tq], bufs[0], c, m)
        m0_ref[slot] = m
    yield

    m_cur = m0_ref[slot]
    for t in range(n_tiles):
        m_next = acc = None
        for c in range(n_chunks):
            if t + 1 < n_tiles:
                m_next = score_chunk(k_ref, q_ref[0, 0, :, pl.ds((t + 1) * tq, tq)],
                                     bufs[(t + 1) % 2], c, m_next)
            else:
                m_next = score_chunk(kn_ref, qn_ref[0, 0], bufs[0], c, m_next)
            acc = value_chunk(bufs[t % 2], c, m_cur, acc)
            yield
        o_ref[0, o_rows, pl.ds(t * tq, tq)] = (acc[0:dv] / acc[dv:dv + 1]).astype(_BF16)
        m_cur = m_next
    m0_ref[slot] = m_cur


def _attn_kernel(base_ref, qa_ref, ka_ref, va_ref, qna_ref, kna_ref,
                 qb_ref, kb_ref, vb_ref, qnb_ref, knb_ref, o_ref,
                 sa0_ref, sa1_ref, sb0_ref, sb1_ref, m0_ref):
    base = base_ref[0]
    dv = va_ref.shape[1]
    first = _attn_stream(base, qa_ref, ka_ref, va_ref, qna_ref, kna_ref, o_ref, pl.ds(0, dv),
                         (sa0_ref, sa1_ref), m0_ref, 0)
    second = _attn_stream(base, qb_ref, kb_ref, vb_ref, qnb_ref, knb_ref, o_ref, pl.ds(dv, dv),
                          (sb0_ref, sb1_ref), m0_ref, 1)
    next(first), next(second)
    _trace(_stagger([first, second], lead=ATTN_STREAM_LEAD))


def _attention(q_all, k_all, v_all):
    bsz, _, _, s = q_all.shape
    total = k_all.shape[2]
    group = A_HEADS // A_KV_HEADS
    n_pairs = N_HEADS // 2
    kset = lambda h: jnp.where(h < A_HEADS, 0, h - A_HEADS + 1)
    vblk = lambda h: jnp.where(h < A_HEADS, h // group, h - A_HEADS + A_KV_HEADS)

    def nxt(b, g):
        n = jnp.minimum(b * n_pairs + g + 1, bsz * n_pairs - 1)
        return n // n_pairs, n % n_pairs

    def stream_specs(which):
        head = lambda g: 2 * g + which
        return [
            pl.BlockSpec((1, 1, HEAD_PAD, s), lambda b, g: (b, head(g), 0, 0)),
            pl.BlockSpec((1, 1, total, HEAD_PAD), lambda b, g: (b, kset(head(g)), 0, 0)),
            pl.BlockSpec((1, B_V_DIM, total), lambda b, g: (b, vblk(head(g)), 0)),
            pl.BlockSpec((1, 1, HEAD_PAD, ATTN_Q_TILE),
                         lambda b, g: (nxt(b, g)[0], head(nxt(b, g)[1]), 0, 0)),
            pl.BlockSpec((1, 1, total, HEAD_PAD),
                         lambda b, g: (nxt(b, g)[0], kset(head(nxt(b, g)[1])), 0, 0)),
        ]

    stream_args = (q_all, k_all, v_all, q_all, k_all)
    return pl.pallas_call(
        _attn_kernel,
        grid=(bsz, n_pairs),
        in_specs=[pl.BlockSpec(memory_space=pltpu.SMEM)] + stream_specs(0) + stream_specs(1),
        out_specs=pl.BlockSpec((1, 2 * B_V_DIM, s), lambda b, g: (b, g, 0)),
        out_shape=jax.ShapeDtypeStruct((bsz, N_HEADS * B_V_DIM, s), _BF16),
        scratch_shapes=[pltpu.VMEM((total, ATTN_Q_TILE), _F32)] * 4
        + [pltpu.VMEM((2, 1, ATTN_Q_TILE), _F32)],
        compiler_params=pltpu.CompilerParams(
            dimension_semantics=("arbitrary", "arbitrary"), vmem_limit_bytes=VMEM_LIMIT),
        name="attn",
    )(jnp.zeros((1,), jnp.int32), *stream_args, *stream_args)


def _merge_pieces(mrow, x, ot, wg_ref, wpa_ref, wpb_ref, wo_ref, g_ref, b_ref, result):
    shift, scale, gate = mrow[:, 0:D_MODEL], mrow[:, D_MODEL:2 * D_MODEL], mrow[:, 2 * D_MODEL:3 * D_MODEL]
    h = (_layer_norm(x) * (1.0 + scale) + shift).astype(_BF16)
    yield
    na = A_HEADS * A_HEAD_DIM
    ys = []
    for r in range(D_MODEL // MERGE_ROW_CHUNK):
        ra = pl.ds(r * MERGE_ROW_CHUNK, MERGE_ROW_CHUNK)
        rb = pl.ds(D_MODEL + r * MERGE_ROW_CHUNK, MERGE_ROW_CHUNK)
        ga = lax.dot_general(wg_ref[ra, :], h, _NT, preferred_element_type=_F32)
        gb = lax.dot_general(wg_ref[rb, :], h, _NT, preferred_element_type=_F32)
        ya = jnp.dot(wpa_ref[ra, :], ot[0:na], preferred_element_type=_F32)
        yb = jnp.dot(wpb_ref[ra, :], ot[na:], preferred_element_type=_F32)
        ys.append((_sigmoid(ga) * ya + _sigmoid(gb) * yb).astype(_BF16))
        yield
    zt = jnp.dot(wo_ref[...], jnp.concatenate(ys, axis=0), preferred_element_type=_F32)
    r = DEEPNORM_ALPHA * x + gate * zt.T
    result.append(_layer_norm(r) * g_ref[...] + b_ref[...])
    yield


def _ffn_pieces(mrow, x, wup_ref, wdn_ref, g_ref, b_ref, result):
    shift, scale, gate = mrow[:, 0:D_MODEL], mrow[:, D_MODEL:2 * D_MODEL], mrow[:, 2 * D_MODEL:3 * D_MODEL]
    h = (_layer_norm(x) * (1.0 + scale) + shift).astype(_BF16)
    yield
    fs = []
    for c in range(FFN_HIDDEN // FFN_COL_CHUNK):
        a = jnp.dot(h, wup_ref[:, pl.ds(c * FFN_COL_CHUNK, FFN_COL_CHUNK)], preferred_element_type=_F32)
        u = jnp.dot(h, wup_ref[:, pl.ds(FFN_HIDDEN + c * FFN_COL_CHUNK, FFN_COL_CHUNK)],
                    preferred_element_type=_F32)
        fs.append((a * _sigmoid(a) * u).astype(_BF16))
        yield
    dn = jnp.dot(jnp.concatenate(fs, axis=1), wdn_ref[...], preferred_element_type=_F32)
    r = DEEPNORM_ALPHA * x + gate * dn
    result.append(_layer_norm(r) * g_ref[...] + b_ref[...])
    yield


def _post_kernel(mod_ref, x_ref, ot_ref, wg_ref, wpa_ref, wpb_ref, wo_ref, g1_ref, b1_ref,
                 wup_ref, wdn_ref, g2_ref, b2_ref, out_ref, x1_ref, *, tiles_per_batch):
    j = pl.program_id(0)
    n_tiles = pl.num_programs(0) - 1
    half_batch = n_tiles // tiles_per_batch

    @pl.when(j == 0)
    def _():
        x1_ref[...] = jnp.zeros_like(x1_ref)

    b_prev = jnp.maximum(j - 1, 0) // tiles_per_batch
    b_cur = jnp.minimum(j, n_tiles - 1) // tiles_per_batch
    ffn_out, merge_out = ([], []), ([], [])

    def stream(half):
        off = half * half_batch
        return _stagger(
            [_ffn_pieces(mod_ref[pl.ds(b_prev + off, 1), 3 * D_MODEL:6 * D_MODEL], x1_ref[half],
                         wup_ref, wdn_ref, g2_ref, b2_ref, ffn_out[half]),
             _merge_pieces(mod_ref[pl.ds(b_cur + off, 1), 0:3 * D_MODEL], x_ref[half, 0],
                           ot_ref[half, 0], wg_ref, wpa_ref, wpb_ref, wo_ref, g1_ref, b1_ref,
                           merge_out[half])],
            rates=[POST_FFN_PIECES_PER_MERGE_PIECE, 1])

    _trace(_stagger([stream(0), stream(1)], lead=POST_STREAM_LEAD))
    for half in range(2):
        out_ref[half, 0] = ffn_out[half][0]
        x1_ref[half] = merge_out[half][0]


def _post(mod, x, ot, wgt, wpat, wpbt, wot, g1, b1, wup, wdn, g2, b2):
    bsz, s, d = x.shape
    t = TOKEN_TILE
    tpb = s // t
    hb = bsz // 2
    n_tiles = hb * tpb
    const = lambda a: pl.BlockSpec(a.shape, lambda j: (0,) * a.ndim, pipeline_mode=pl.Buffered(1))
    cur = lambda j: jnp.minimum(j, n_tiles - 1)
    prev = lambda j: jnp.maximum(j - 1, 0)
    out = pl.pallas_call(
        functools.partial(_post_kernel, tiles_per_batch=tpb),
        grid=(n_tiles + 1,),
        in_specs=[
            const(mod),
            pl.BlockSpec((2, 1, t, d), lambda j: (0, cur(j) // tpb, cur(j) % tpb, 0)),
            pl.BlockSpec((2, 1, ot.shape[1], t), lambda j: (0, cur(j) // tpb, 0, cur(j) % tpb)),
            const(wgt), const(wpat), const(wpbt), const(wot), const(g1), const(b1),
            const(wup), const(wdn), const(g2), const(b2),
        ],
        out_specs=pl.BlockSpec((2, 1, t, d), lambda j: (0, prev(j) // tpb, prev(j) % tpb, 0)),
        out_shape=jax.ShapeDtypeStruct((2, hb, s, d), _F32),
        scratch_shapes=[pltpu.VMEM((2, t, d), _F32)],
        compiler_params=pltpu.CompilerParams(
            dimension_semantics=("arbitrary",), vmem_limit_bytes=VMEM_LIMIT),
        name="post",
    )(mod, x.reshape(2, hb, s, d), ot.reshape(2, hb, ot.shape[1], s),
      wgt, wpat, wpbt, wot, g1, b1, wup, wdn, g2, b2)
    return out.reshape(bsz, s, d)


def _rope_tables(seq, half):
    pos = np.arange(seq)
    freqs = ROPE_THETA ** (-np.arange(half, dtype=np.float64) / half)
    tabs = []
    for p in (pos // GRID_W, pos % GRID_W):
        ang = p[None, :].astype(np.float64) * freqs[:, None]
        for fn, ident in ((np.cos, 1.0), (np.sin, 0.0)):
            tabs.append(np.concatenate([np.full((half, CTX_LEN), ident), fn(ang)], axis=1))
    return jnp.asarray(np.stack(tabs), dtype=_F32)


def kernel(x, c, ctx, c_ctx, w_mod, b_mod, w_in, q_norm_a, k_norm_a, cq_norm, ckv_norm,
           w_uq, w_ukv, w_proj_a, w_proj_b, w_out, ln1_g, ln1_b, w_up, w_down, ln2_g, ln2_b):
    bsz, s, d = x.shape
    assert w_mod.shape[0] == DEPTH == 1 and d == D_MODEL and ctx.shape[1] == CTX_LEN
    assert bsz == 8 and s % ATTN_Q_TILE == 0 and s % TOKEN_TILE == 0

    cc = jnp.concatenate([c, c_ctx[None, :], jnp.zeros((bsz - 1, d), _F32)], axis=0)
    mod = _modulation(cc, w_mod[0], b_mod[0][None, :])

    w_in0 = w_in[0]
    w1t = w_in0[:, :QKV_COLS].T.astype(_BF16)
    wgt = w_in0[:, QKV_COLS:].T.astype(_BF16)
    col = lambda v: v[0][:, None]
    tab_a = _rope_tables(s, A_HEAD_DIM // 4)
    tab_b = _rope_tables(s, B_ROPE_DIM // 4)

    q_all, k_all, v_all = _project(
        mod, ctx, x, w1t, w_uq[0].T.astype(_BF16), w_ukv[0].T.astype(_BF16),
        col(q_norm_a), col(k_norm_a), col(cq_norm), col(ckv_norm), tab_a, tab_b)
    ot = _attention(q_all, k_all, v_all)
    return _post(mod, x, ot, wgt, w_proj_a[0].T.astype(_BF16), w_proj_b[0].T.astype(_BF16),
                 w_out[0].T.astype(_BF16), ln1_g, ln1_b,
                 w_up[0].astype(_BF16), w_down[0].astype(_BF16), ln2_g, ln2_b)
```

```python
import functools
import math

import numpy as np
import jax
import jax.numpy as jnp
from jax import lax
from jax.experimental import pallas as pl
from jax.experimental.pallas import tpu as pltpu

D_MODEL = 1024
GRID_W = 64
CTX_LEN = 256
ROPE_THETA = 10000.0
EPS = 1e-6

A_HEADS = 8
A_KV_HEADS = 2
A_HEAD_DIM = 64
B_HEADS = 8
B_Q_RANK = 384
B_KV_RANK = 256
B_NOPE_DIM = 64
B_ROPE_DIM = 32
B_V_DIM = 64
FFN_HIDDEN = 2816
DEPTH = 1

A_SCALE = A_HEAD_DIM ** -0.5
B_SCALE = (B_NOPE_DIM + B_ROPE_DIM) ** -0.5
DEEPNORM_ALPHA = (2.0 * DEPTH) ** 0.25
LOG2_E = math.log2(math.e)

QA_OFF = 0
KA_OFF = QA_OFF + A_HEADS * A_HEAD_DIM
VA_OFF = KA_OFF + A_KV_HEADS * A_HEAD_DIM
CQ_OFF = VA_OFF + A_KV_HEADS * A_HEAD_DIM
CKV_OFF = CQ_OFF + B_Q_RANK
KR_OFF = CKV_OFF + B_KV_RANK
GATE_OFF = KR_OFF + B_ROPE_DIM
QKV_COLS = GATE_OFF

N_HEADS = A_HEADS + B_HEADS
HEAD_PAD = 128
N_KSETS = 1 + B_HEADS
V_ROWS = A_KV_HEADS * A_HEAD_DIM + B_HEADS * B_V_DIM

MOD_COL_TILE = 1024
PROJ_TILE = 256
PROJ_ROW_CHUNK = 288
PROJ_STREAMS = 4
PROJ_STREAM_LEAD = 3
ATTN_Q_TILE = 512
ATTN_KEY_CHUNK = 256
ATTN_STREAM_LEAD = 2
ATTN_ONES_ROWS = 16
TOKEN_TILE = 256
MERGE_ROW_CHUNK = 256
FFN_COL_CHUNK = 256
POST_FFN_PIECES_PER_MERGE_PIECE = 2
POST_STREAM_LEAD = 10
VMEM_LIMIT = 56 * 1024 * 1024

_NT = (((1,), (1,)), ((), ()))
_F32 = jnp.float32
_BF16 = jnp.bfloat16


def _layer_norm(x):
    mu = jnp.mean(x, axis=-1, keepdims=True)
    xc = x - mu
    var = jnp.mean(xc * xc, axis=-1, keepdims=True)
    return xc * lax.rsqrt(var + EPS)


def _sigmoid(x):
    return jax.nn.sigmoid(x)


def _mod_kernel(c_ref, w_ref, b_ref, o_ref):
    c = c_ref[...]
    a = (c * _sigmoid(c)).astype(_BF16)
    o_ref[...] = jnp.dot(a, w_ref[...].astype(_BF16), preferred_element_type=_F32) + b_ref[...]


def _modulation(cc, w_mod, b_mod):
    rows, d = cc.shape
    n = w_mod.shape[1]
    return pl.pallas_call(
        _mod_kernel,
        grid=(n // MOD_COL_TILE,),
        in_specs=[
            pl.BlockSpec((rows, d), lambda i: (0, 0)),
            pl.BlockSpec((d, MOD_COL_TILE), lambda i: (0, i)),
            pl.BlockSpec((1, MOD_COL_TILE), lambda i: (0, i)),
        ],
        out_specs=pl.BlockSpec((rows, MOD_COL_TILE), lambda i: (0, i)),
        out_shape=jax.ShapeDtypeStruct((rows, n), _F32),
        compiler_params=pltpu.CompilerParams(
            dimension_semantics=("arbitrary",), vmem_limit_bytes=VMEM_LIMIT),
        name="mod",
    )(cc, w_mod, b_mod)


def _rms_rows(x, g):
    ms = jnp.mean(x * x, axis=0, keepdims=True)
    return x * lax.rsqrt(ms + EPS) * g


def _axial_rope_rows(x, half, tab):
    r1, r2 = x[0:half], x[half:2 * half]
    c1, c2 = x[2 * half:3 * half], x[3 * half:4 * half]
    cr, sr, cc, sc = tab[0], tab[1], tab[2], tab[3]
    return jnp.concatenate(
        [r1 * cr - r2 * sr, r1 * sr + r2 * cr, c1 * cc - c2 * sc, c1 * sc + c2 * cc], axis=0)


def _stagger(streams, lead=0, rates=None):
    rates = rates or [1] * len(streams)
    live = dict(enumerate(streams))
    rnd = 0
    while live:
        for i in sorted(live):
            if rnd < i * lead:
                continue
            for _ in range(rates[i]):
                if next(live[i], StopIteration) is StopIteration:
                    del live[i]
                    break
                yield
        rnd += 1


def _trace(pieces):
    for _ in pieces:
        pass


def _proj_pieces(mrow, xin, w1_ref, wuq_ref, wukv_ref, qn_ref, kn_ref, cqn_ref, ckvn_ref, ta, tb,
                 q_out, k_out, v_out):
    shift, scale = mrow[:, 0:D_MODEL], mrow[:, D_MODEL:2 * D_MODEL]
    h = (_layer_norm(xin) * (1.0 + scale) + shift).astype(_BF16)
    yield
    parts = []
    for r in range(QKV_COLS // PROJ_ROW_CHUNK):
        rows = pl.ds(r * PROJ_ROW_CHUNK, PROJ_ROW_CHUNK)
        parts.append(lax.dot_general(w1_ref[rows, :], h, _NT, preferred_element_type=_F32))
        yield
    pt = jnp.concatenate(parts, axis=0)
    t = pt.shape[1]
    z64 = jnp.zeros((A_HEAD_DIM, t), _F32)
    z32 = jnp.zeros((HEAD_PAD - B_NOPE_DIM - B_ROPE_DIM, t), _F32)

    qn = qn_ref[...] * (A_SCALE * LOG2_E)
    group = A_HEADS // A_KV_HEADS
    for hd in range(A_HEADS):
        xh = pt[QA_OFF + hd * A_HEAD_DIM:QA_OFF + (hd + 1) * A_HEAD_DIM]
        xh = _axial_rope_rows(_rms_rows(xh, qn), A_HEAD_DIM // 4, ta)
        full = jnp.concatenate([xh, z64] if hd // group == 0 else [z64, xh], axis=0)
        q_out[hd] = full.astype(_BF16)
        if hd % 4 == 3:
            yield

    kn = kn_ref[...]
    ka = [
        _axial_rope_rows(
            _rms_rows(pt[KA_OFF + g * A_HEAD_DIM:KA_OFF + (g + 1) * A_HEAD_DIM], kn),
            A_HEAD_DIM // 4, ta)
        for g in range(A_KV_HEADS)
    ]
    k_out[0] = jnp.concatenate(ka, axis=0).T.astype(_BF16)
    v_out[0:A_KV_HEADS * A_HEAD_DIM, :] = pt[VA_OFF:CQ_OFF].astype(_BF16)
    yield

    cq = _rms_rows(pt[CQ_OFF:CKV_OFF], cqn_ref[...]).astype(_BF16)
    qb = jnp.dot(wuq_ref[...], cq, preferred_element_type=_F32) * (B_SCALE * LOG2_E)
    yield
    qdim = B_NOPE_DIM + B_ROPE_DIM
    for hd in range(B_HEADS):
        blk = qb[hd * qdim:(hd + 1) * qdim]
        rp = _axial_rope_rows(blk[B_NOPE_DIM:qdim], B_ROPE_DIM // 4, tb)
        full = jnp.concatenate([blk[0:B_NOPE_DIM], rp, z32], axis=0)
        q_out[A_HEADS + hd] = full.astype(_BF16)
        if hd % 4 == 3:
            yield

    ckv = _rms_rows(pt[CKV_OFF:KR_OFF], ckvn_ref[...]).astype(_BF16)
    kv = jnp.dot(wukv_ref[...], ckv, preferred_element_type=_F32)
    kr = _axial_rope_rows(pt[KR_OFF:GATE_OFF], B_ROPE_DIM // 4, tb)
    yield
    kvdim = B_NOPE_DIM + B_V_DIM
    v_base = A_KV_HEADS * A_HEAD_DIM
    for hd in range(B_HEADS):
        kn_h = kv[hd * kvdim:hd * kvdim + B_NOPE_DIM]
        v_h = kv[hd * kvdim + B_NOPE_DIM:(hd + 1) * kvdim]
        v_out[v_base + hd * B_V_DIM:v_base + (hd + 1) * B_V_DIM, :] = v_h.astype(_BF16)
        k_out[1 + hd] = jnp.concatenate([kn_h, kr, z32], axis=0).T.astype(_BF16)
        if hd % 4 == 3:
            yield


def _proj_kernel(mod_ref, ctx_ref, x_ref, w1_ref, wuq_ref, wukv_ref,
                 qn_ref, kn_ref, cqn_ref, ckvn_ref, ta_ref, tb_ref,
                 q_out, k_out, v_out):
    b = pl.program_id(0)
    j = pl.program_id(1)
    batch_stride = pl.num_programs(0)
    is_ctx = j == 0
    ctx_row = mod_ref.shape[0] // 2
    ta = ta_ref[...]
    tb = tb_ref[...]

    def stream(i):
        xin = jnp.where(is_ctx, ctx_ref[i, 0], x_ref[i, 0])
        mrow = jnp.where(is_ctx, mod_ref[ctx_row:ctx_row + 1, 0:2 * D_MODEL],
                         mod_ref[pl.ds(b + i * batch_stride, 1), 0:2 * D_MODEL])
        return _proj_pieces(mrow, xin, w1_ref, wuq_ref, wukv_ref, qn_ref, kn_ref, cqn_ref, ckvn_ref,
                            ta, tb, q_out.at[i, 0], k_out.at[i, 0], v_out.at[i, 0])

    _trace(_stagger([stream(i) for i in range(PROJ_STREAMS)], lead=PROJ_STREAM_LEAD))


def _project(mod, ctx, x, w_in_t, wuqt, wukvt, qn, kn, cqn, ckvn, tab_a, tab_b):
    bsz, s, d = x.shape
    t = PROJ_TILE
    n_steps = (CTX_LEN + s) // t
    total = CTX_LEN + s
    ns = PROJ_STREAMS
    hb = bsz // ns
    lat = lambda j: jnp.maximum(j - 1, 0)
    full2 = lambda shape: pl.BlockSpec(shape, lambda b, j: (0, 0))
    q_all, k_all, v_all = pl.pallas_call(
        _proj_kernel,
        grid=(hb, n_steps),
        in_specs=[
            full2(mod.shape),
            pl.BlockSpec((ns, 1, CTX_LEN, d), lambda b, j: (0, b, 0, 0)),
            pl.BlockSpec((ns, 1, t, d), lambda b, j: (0, b, lat(j), 0)),
            full2((QKV_COLS, d)),
            full2(wuqt.shape), full2(wukvt.shape),
            full2(qn.shape), full2(kn.shape), full2(cqn.shape), full2(ckvn.shape),
            pl.BlockSpec((4, tab_a.shape[1], t), lambda b, j: (0, 0, j)),
            pl.BlockSpec((4, tab_b.shape[1], t), lambda b, j: (0, 0, j)),
        ],
        out_specs=[
            pl.BlockSpec((ns, 1, N_HEADS, HEAD_PAD, t), lambda b, j: (0, b, 0, 0, lat(j))),
            pl.BlockSpec((ns, 1, N_KSETS, t, HEAD_PAD), lambda b, j: (0, b, 0, j, 0)),
            pl.BlockSpec((ns, 1, V_ROWS, t), lambda b, j: (0, b, 0, j)),
        ],
        out_shape=[
            jax.ShapeDtypeStruct((ns, hb, N_HEADS, HEAD_PAD, s), _BF16),
            jax.ShapeDtypeStruct((ns, hb, N_KSETS, total, HEAD_PAD), _BF16),
            jax.ShapeDtypeStruct((ns, hb, V_ROWS, total), _BF16),
        ],
        compiler_params=pltpu.CompilerParams(
            dimension_semantics=("arbitrary", "arbitrary"), vmem_limit_bytes=VMEM_LIMIT),
        name="proj",
    )(mod, ctx.reshape(ns, hb, CTX_LEN, d), x.reshape(ns, hb, s, d),
      w_in_t, wuqt, wukvt, qn, kn, cqn, ckvn, tab_a, tab_b)
    return (q_all.reshape(bsz, N_HEADS, HEAD_PAD, s), k_all.reshape(bsz, N_KSETS, total, HEAD_PAD),
            v_all.reshape(bsz, V_ROWS, total))


def _attn_stream(base, q_ref, k_ref, v_ref, qn_ref, kn_ref, o_ref, o_rows, bufs, m0_ref, slot):
    tq, kc = ATTN_Q_TILE, ATTN_KEY_CHUNK
    n_tiles = q_ref.shape[3] // tq
    n_chunks = k_ref.shape[2] // kc
    assert n_tiles % 2 == 0
    assert n_tiles * tq == q_ref.shape[3] and n_chunks * kc == k_ref.shape[2]
    dv = v_ref.shape[1]
    ones = jnp.ones((ATTN_ONES_ROWS, kc), _BF16)

    def buf_rows(c):
        return pl.ds(pl.multiple_of(base + c * kc, kc), kc)

    def score_chunk(keys_ref, q_tile, buf, c, m):
        s = jnp.dot(keys_ref[0, 0, pl.ds(c * kc, kc), :], q_tile,
                    preferred_element_type=_F32)
        buf[buf_rows(c), :] = s
        cm = jnp.max(s, axis=0, keepdims=True)
        return cm if m is None else jnp.maximum(m, cm)

    def value_chunk(buf, c, m, acc):
        p = jnp.exp2(buf[buf_rows(c), :] - m)
        vext = jnp.concatenate([v_ref[0, :, pl.ds(c * kc, kc)], ones], axis=0)
        pv = jnp.dot(vext, p.astype(_BF16), preferred_element_type=_F32)
        return pv if acc is None else acc + pv

    @pl.when((pl.program_id(0) == 0) & (pl.program_id(1) == 0))
    def _():
        m = None
        for c in range(n_chunks):
            m = score_chunk(k_ref, q_ref[0, 0, :, 0:tq], bufs[0], c, m)
        m0_ref[slot] = m
    yield

    m_cur = m0_ref[slot]
    for t in range(n_tiles):
        m_next = acc = None
        for c in range(n_chunks):
            if t + 1 < n_tiles:
                m_next = score_chunk(k_ref, q_ref[0, 0, :, pl.ds((t + 1) * tq, tq)],
                                     bufs[(t + 1) % 2], c, m_next)
            else:
                m_next = score_chunk(kn_ref, qn_ref[0, 0], bufs[0], c, m_next)
            acc = value_chunk(bufs[t % 2], c, m_cur, acc)
            yield
        o_ref[0, o_rows, pl.ds(t * tq, tq)] = (acc[0:dv] / acc[dv:dv + 1]).astype(_BF16)
        m_cur = m_next
    m0_ref[slot] = m_cur


def _attn_kernel(base_ref, qa_ref, ka_ref, va_ref, qna_ref, kna_ref,
                 qb_ref, kb_ref, vb_ref, qnb_ref, knb_ref, o_ref,
                 sa0_ref, sa1_ref, sb0_ref, sb1_ref, m0_ref):
    base = base_ref[0]
    dv = va_ref.shape[1]
    first = _attn_stream(base, qa_ref, ka_ref, va_ref, qna_ref, kna_ref, o_ref, pl.ds(0, dv),
                         (sa0_ref, sa1_ref), m0_ref, 0)
    second = _attn_stream(base, qb_ref, kb_ref, vb_ref, qnb_ref, knb_ref, o_ref, pl.ds(dv, dv),
                          (sb0_ref, sb1_ref), m0_ref, 1)
    next(first), next(second)
    _trace(_stagger([first, second], lead=ATTN_STREAM_LEAD))


def _attention(q_all, k_all, v_all):
    bsz, _, _, s = q_all.shape
    total = k_all.shape[2]
    group = A_HEADS // A_KV_HEADS
    n_pairs = N_HEADS // 2
    kset = lambda h: jnp.where(h < A_HEADS, 0, h - A_HEADS + 1)
    vblk = lambda h: jnp.where(h < A_HEADS, h // group, h - A_HEADS + A_KV_HEADS)

    def nxt(b, g):
        n = jnp.minimum(b * n_pairs + g + 1, bsz * n_pairs - 1)
        return n // n_pairs, n % n_pairs

    def stream_specs(which):
        head = lambda g: 2 * g + which
        return [
            pl.BlockSpec((1, 1, HEAD_PAD, s), lambda b, g: (b, head(g), 0, 0)),
            pl.BlockSpec((1, 1, total, HEAD_PAD), lambda b, g: (b, kset(head(g)), 0, 0)),
            pl.BlockSpec((1, B_V_DIM, total), lambda b, g: (b, vblk(head(g)), 0)),
            pl.BlockSpec((1, 1, HEAD_PAD, ATTN_Q_TILE),
                         lambda b, g: (nxt(b, g)[0], head(nxt(b, g)[1]), 0, 0)),
            pl.BlockSpec((1, 1, total, HEAD_PAD),
                         lambda b, g: (nxt(b, g)[0], kset(head(nxt(b, g)[1])), 0, 0)),
        ]

    stream_args = (q_all, k_all, v_all, q_all, k_all)
    return pl.pallas_call(
        _attn_kernel,
        grid=(bsz, n_pairs),
        in_specs=[pl.BlockSpec(memory_space=pltpu.SMEM)] + stream_specs(0) + stream_specs(1),
        out_specs=pl.BlockSpec((1, 2 * B_V_DIM, s), lambda b, g: (b, g, 0)),
        out_shape=jax.ShapeDtypeStruct((bsz, N_HEADS * B_V_DIM, s), _BF16),
        scratch_shapes=[pltpu.VMEM((total, ATTN_Q_TILE), _F32)] * 4
        + [pltpu.VMEM((2, 1, ATTN_Q_TILE), _F32)],
        compiler_params=pltpu.CompilerParams(
            dimension_semantics=("arbitrary", "arbitrary"), vmem_limit_bytes=VMEM_LIMIT),
        name="attn",
    )(jnp.zeros((1,), jnp.int32), *stream_args, *stream_args)


def _merge_pieces(mrow, x, ot, win_ref, wpa_ref, wpb_ref, wo_ref, g_ref, b_ref, result):
    shift, scale, gate = mrow[:, 0:D_MODEL], mrow[:, D_MODEL:2 * D_MODEL], mrow[:, 2 * D_MODEL:3 * D_MODEL]
    h = (_layer_norm(x) * (1.0 + scale) + shift).astype(_BF16)
    yield
    na = A_HEADS * A_HEAD_DIM
    ys = []
    for r in range(D_MODEL // MERGE_ROW_CHUNK):
        ra = pl.ds(r * MERGE_ROW_CHUNK, MERGE_ROW_CHUNK)
        ga_rows = pl.ds(GATE_OFF + r * MERGE_ROW_CHUNK, MERGE_ROW_CHUNK)
        gb_rows = pl.ds(GATE_OFF + D_MODEL + r * MERGE_ROW_CHUNK, MERGE_ROW_CHUNK)
        ga = lax.dot_general(win_ref[ga_rows, :], h, _NT, preferred_element_type=_F32)
        gb = lax.dot_general(win_ref[gb_rows, :], h, _NT, preferred_element_type=_F32)
        ya = jnp.dot(wpa_ref[ra, :], ot[0:na], preferred_element_type=_F32)
        yb = jnp.dot(wpb_ref[ra, :], ot[na:], preferred_element_type=_F32)
        ys.append((_sigmoid(ga) * ya + _sigmoid(gb) * yb).astype(_BF16))
        yield
    zt = jnp.dot(wo_ref[...], jnp.concatenate(ys, axis=0), preferred_element_type=_F32)
    r = DEEPNORM_ALPHA * x + gate * zt.T
    result.append(_layer_norm(r) * g_ref[...] + b_ref[...])
    yield


def _ffn_pieces(mrow, x, wup_ref, wdn_ref, g_ref, b_ref, result):
    shift, scale, gate = mrow[:, 0:D_MODEL], mrow[:, D_MODEL:2 * D_MODEL], mrow[:, 2 * D_MODEL:3 * D_MODEL]
    h = (_layer_norm(x) * (1.0 + scale) + shift).astype(_BF16)
    yield
    fs = []
    for c in range(FFN_HIDDEN // FFN_COL_CHUNK):
        a = jnp.dot(h, wup_ref[:, pl.ds(c * FFN_COL_CHUNK, FFN_COL_CHUNK)], preferred_element_type=_F32)
        u = jnp.dot(h, wup_ref[:, pl.ds(FFN_HIDDEN + c * FFN_COL_CHUNK, FFN_COL_CHUNK)],
                    preferred_element_type=_F32)
        fs.append((a * _sigmoid(a) * u).astype(_BF16))
        yield
    dn = jnp.dot(jnp.concatenate(fs, axis=1), wdn_ref[...], preferred_element_type=_F32)
    r = DEEPNORM_ALPHA * x + gate * dn
    result.append(_layer_norm(r) * g_ref[...] + b_ref[...])
    yield


def _post_kernel(mod_ref, x_ref, ot_ref, win_ref, wpa_ref, wpb_ref, wo_ref, g1_ref, b1_ref,
                 wup_ref, wdn_ref, g2_ref, b2_ref, out_ref, x1_ref, *, tiles_per_batch):
    j = pl.program_id(0)
    n_tiles = pl.num_programs(0) - 1
    half_batch = n_tiles // tiles_per_batch

    @pl.when(j == 0)
    def _():
        x1_ref[...] = jnp.zeros_like(x1_ref)

    b_prev = jnp.maximum(j - 1, 0) // tiles_per_batch
    b_cur = jnp.minimum(j, n_tiles - 1) // tiles_per_batch
    ffn_out, merge_out = ([], []), ([], [])

    def stream(half):
        off = half * half_batch
        return _stagger(
            [_ffn_pieces(mod_ref[pl.ds(b_prev + off, 1), 3 * D_MODEL:6 * D_MODEL], x1_ref[half],
                         wup_ref, wdn_ref, g2_ref, b2_ref, ffn_out[half]),
             _merge_pieces(mod_ref[pl.ds(b_cur + off, 1), 0:3 * D_MODEL], x_ref[half, 0],
                           ot_ref[half, 0], win_ref, wpa_ref, wpb_ref, wo_ref, g1_ref, b1_ref,
                           merge_out[half])],
            rates=[POST_FFN_PIECES_PER_MERGE_PIECE, 1])

    _trace(_stagger([stream(0), stream(1)], lead=POST_STREAM_LEAD))
    for half in range(2):
        out_ref[half, 0] = ffn_out[half][0]
        x1_ref[half] = merge_out[half][0]


def _post(mod, x, ot, w_in_t, wpat, wpbt, wot, g1, b1, wup, wdn, g2, b2):
    bsz, s, d = x.shape
    t = TOKEN_TILE
    tpb = s // t
    hb = bsz // 2
    n_tiles = hb * tpb
    const = lambda a: pl.BlockSpec(a.shape, lambda j: (0,) * a.ndim, pipeline_mode=pl.Buffered(1))
    cur = lambda j: jnp.minimum(j, n_tiles - 1)
    prev = lambda j: jnp.maximum(j - 1, 0)
    out = pl.pallas_call(
        functools.partial(_post_kernel, tiles_per_batch=tpb),
        grid=(n_tiles + 1,),
        in_specs=[
            const(mod),
            pl.BlockSpec((2, 1, t, d), lambda j: (0, cur(j) // tpb, cur(j) % tpb, 0)),
            pl.BlockSpec((2, 1, ot.shape[1], t), lambda j: (0, cur(j) // tpb, 0, cur(j) % tpb)),
            const(w_in_t), const(wpat), const(wpbt), const(wot), const(g1), const(b1),
            const(wup), const(wdn), const(g2), const(b2),
        ],
        out_specs=pl.BlockSpec((2, 1, t, d), lambda j: (0, prev(j) // tpb, prev(j) % tpb, 0)),
        out_shape=jax.ShapeDtypeStruct((2, hb, s, d), _F32),
        scratch_shapes=[pltpu.VMEM((2, t, d), _F32)],
        compiler_params=pltpu.CompilerParams(
            dimension_semantics=("arbitrary",), vmem_limit_bytes=VMEM_LIMIT),
        name="post",
    )(mod, x.reshape(2, hb, s, d), ot.reshape(2, hb, ot.shape[1], s),
      w_in_t, wpat, wpbt, wot, g1, b1, wup, wdn, g2, b2)
    return out.reshape(bsz, s, d)


def _rope_tables(seq, half):
    pos = np.arange(seq)
    freqs = ROPE_THETA ** (-np.arange(half, dtype=np.float64) / half)
    tabs = []
    for p in (pos // GRID_W, pos % GRID_W):
        ang = p[None, :].astype(np.float64) * freqs[:, None]
        for fn, ident in ((np.cos, 1.0), (np.sin, 0.0)):
            tabs.append(np.concatenate([np.full((half, CTX_LEN), ident), fn(ang)], axis=1))
    return jnp.asarray(np.stack(tabs), dtype=_F32)


def kernel(x, c, ctx, c_ctx, w_mod, b_mod, w_in, q_norm_a, k_norm_a, cq_norm, ckv_norm,
           w_uq, w_ukv, w_proj_a, w_proj_b, w_out, ln1_g, ln1_b, w_up, w_down, ln2_g, ln2_b):
    bsz, s, d = x.shape
    assert w_mod.shape[0] == DEPTH == 1 and d == D_MODEL and ctx.shape[1] == CTX_LEN
    assert bsz == 8 and s % ATTN_Q_TILE == 0 and s % TOKEN_TILE == 0

    cc = jnp.concatenate([c, c_ctx[None, :], jnp.zeros((bsz - 1, d), _F32)], axis=0)
    mod = _modulation(cc, w_mod[0], b_mod[0][None, :])

    tbf = lambda w: w[0].astype(_BF16).T
    w_in_t = tbf(w_in)
    col = lambda v: v[0][:, None]
    tab_a = _rope_tables(s, A_HEAD_DIM // 4)
    tab_b = _rope_tables(s, B_ROPE_DIM // 4)

    q_all, k_all, v_all = _project(
        mod, ctx, x, w_in_t, tbf(w_uq), tbf(w_ukv),
        col(q_norm_a), col(k_norm_a), col(cq_norm), col(ckv_norm), tab_a, tab_b)
    ot = _attention(q_all, k_all, v_all)
    return _post(mod, x, ot, w_in_t, tbf(w_proj_a), tbf(w_proj_b), tbf(w_out), ln1_g, ln1_b,
                 w_up[0].astype(_BF16), w_down[0].astype(_BF16), ln2_g, ln2_b)
```

```python
import functools
import math

import numpy as np
import jax
import jax.numpy as jnp
from jax import lax
from jax.experimental import pallas as pl
from jax.experimental.pallas import tpu as pltpu

D_MODEL = 1024
GRID_W = 64
CTX_LEN = 256
ROPE_THETA = 10000.0
EPS = 1e-6

A_HEADS = 8
A_KV_HEADS = 2
A_HEAD_DIM = 64
B_HEADS = 8
B_Q_RANK = 384
B_KV_RANK = 256
B_NOPE_DIM = 64
B_ROPE_DIM = 32
B_V_DIM = 64
FFN_HIDDEN = 2816
DEPTH = 1

A_SCALE = A_HEAD_DIM ** -0.5
B_SCALE = (B_NOPE_DIM + B_ROPE_DIM) ** -0.5
DEEPNORM_ALPHA = (2.0 * DEPTH) ** 0.25
LOG2_E = math.log2(math.e)

QA_OFF = 0
KA_OFF = QA_OFF + A_HEADS * A_HEAD_DIM
VA_OFF = KA_OFF + A_KV_HEADS * A_HEAD_DIM
CQ_OFF = VA_OFF + A_KV_HEADS * A_HEAD_DIM
CKV_OFF = CQ_OFF + B_Q_RANK
KR_OFF = CKV_OFF + B_KV_RANK
GATE_OFF = KR_OFF + B_ROPE_DIM
QKV_COLS = GATE_OFF

N_HEADS = A_HEADS + B_HEADS
HEAD_PAD = 128
N_KSETS = 1 + B_HEADS
V_ROWS = A_KV_HEADS * A_HEAD_DIM + B_HEADS * B_V_DIM

MOD_COL_TILE = 1024
PROJ_TILE = 256
PROJ_ROW_CHUNK = 288
PROJ_STREAMS = 4
PROJ_STREAM_LEAD = 3
ATTN_Q_TILE = 512
ATTN_KEY_CHUNK = 256
ATTN_STREAM_LEAD = 9
ATTN_ONES_ROWS = 16
TOKEN_TILE = 256
MERGE_ROW_CHUNK = 256
FFN_COL_CHUNK = 256
POST_FFN_PIECES_PER_MERGE_PIECE = 2
POST_STREAM_LEAD = 10
VMEM_LIMIT = 56 * 1024 * 1024

_NT = (((1,), (1,)), ((), ()))
_F32 = jnp.float32
_BF16 = jnp.bfloat16


def _layer_norm(x):
    mu = jnp.mean(x, axis=-1, keepdims=True)
    xc = x - mu
    var = jnp.mean(xc * xc, axis=-1, keepdims=True)
    return xc * lax.rsqrt(var + EPS)


def _sigmoid(x):
    return jax.nn.sigmoid(x)


def _mod_kernel(c_ref, w_ref, b_ref, o_ref):
    c = c_ref[...]
    a = (c * _sigmoid(c)).astype(_BF16)
    o_ref[...] = jnp.dot(a, w_ref[...].astype(_BF16), preferred_element_type=_F32) + b_ref[...]


def _modulation(cc, w_mod, b_mod):
    rows, d = cc.shape
    n = w_mod.shape[1]
    return pl.pallas_call(
        _mod_kernel,
        grid=(n // MOD_COL_TILE,),
        in_specs=[
            pl.BlockSpec((rows, d), lambda i: (0, 0)),
            pl.BlockSpec((d, MOD_COL_TILE), lambda i: (0, i)),
            pl.BlockSpec((1, MOD_COL_TILE), lambda i: (0, i)),
        ],
        out_specs=pl.BlockSpec((rows, MOD_COL_TILE), lambda i: (0, i)),
        out_shape=jax.ShapeDtypeStruct((rows, n), _F32),
        compiler_params=pltpu.CompilerParams(
            dimension_semantics=("arbitrary",), vmem_limit_bytes=VMEM_LIMIT),
        name="mod",
    )(cc, w_mod, b_mod)


def _rms_rows(x, g):
    ms = jnp.mean(x * x, axis=0, keepdims=True)
    return x * lax.rsqrt(ms + EPS) * g


def _axial_rope_rows(x, half, tab):
    r1, r2 = x[0:half], x[half:2 * half]
    c1, c2 = x[2 * half:3 * half], x[3 * half:4 * half]
    cr, sr, cc, sc = tab[0], tab[1], tab[2], tab[3]
    return jnp.concatenate(
        [r1 * cr - r2 * sr, r1 * sr + r2 * cr, c1 * cc - c2 * sc, c1 * sc + c2 * cc], axis=0)


def _stagger(streams, lead=0, rates=None):
    rates = rates or [1] * len(streams)
    live = dict(enumerate(streams))
    rnd = 0
    while live:
        for i in sorted(live):
            if rnd < i * lead:
                continue
            for _ in range(rates[i]):
                if next(live[i], StopIteration) is StopIteration:
                    del live[i]
                    break
                yield
        rnd += 1


def _trace(pieces):
    for _ in pieces:
        pass


def _proj_pieces(mrow, xin, w1_ref, wuq_ref, wukv_ref, qn_ref, kn_ref, cqn_ref, ckvn_ref, ta, tb,
                 q_out, k_out, v_out):
    shift, scale = mrow[:, 0:D_MODEL], mrow[:, D_MODEL:2 * D_MODEL]
    h = (_layer_norm(xin) * (1.0 + scale) + shift).astype(_BF16)
    yield
    parts = []
    for r in range(QKV_COLS // PROJ_ROW_CHUNK):
        rows = pl.ds(r * PROJ_ROW_CHUNK, PROJ_ROW_CHUNK)
        parts.append(lax.dot_general(w1_ref[rows, :], h, _NT, preferred_element_type=_F32))
        yield
    pt = jnp.concatenate(parts, axis=0)
    t = pt.shape[1]
    z64 = jnp.zeros((A_HEAD_DIM, t), _F32)
    z32 = jnp.zeros((HEAD_PAD - B_NOPE_DIM - B_ROPE_DIM, t), _F32)

    qn = qn_ref[...] * (A_SCALE * LOG2_E)
    group = A_HEADS // A_KV_HEADS
    for hd in range(A_HEADS):
        xh = pt[QA_OFF + hd * A_HEAD_DIM:QA_OFF + (hd + 1) * A_HEAD_DIM]
        xh = _axial_rope_rows(_rms_rows(xh, qn), A_HEAD_DIM // 4, ta)
        full = jnp.concatenate([xh, z64] if hd // group == 0 else [z64, xh], axis=0)
        q_out[hd] = full.astype(_BF16)
        if hd % 4 == 3:
            yield

    kn = kn_ref[...]
    ka = [
        _axial_rope_rows(
            _rms_rows(pt[KA_OFF + g * A_HEAD_DIM:KA_OFF + (g + 1) * A_HEAD_DIM], kn),
            A_HEAD_DIM // 4, ta)
        for g in range(A_KV_HEADS)
    ]
    k_out[0] = jnp.concatenate(ka, axis=0).T.astype(_BF16)
    v_out[0:A_KV_HEADS * A_HEAD_DIM, :] = pt[VA_OFF:CQ_OFF].astype(_BF16)
    yield

    cq = _rms_rows(pt[CQ_OFF:CKV_OFF], cqn_ref[...]).astype(_BF16)
    qb = jnp.dot(wuq_ref[...], cq, preferred_element_type=_F32) * (B_SCALE * LOG2_E)
    yield
    qdim = B_NOPE_DIM + B_ROPE_DIM
    for hd in range(B_HEADS):
        blk = qb[hd * qdim:(hd + 1) * qdim]
        rp = _axial_rope_rows(blk[B_NOPE_DIM:qdim], B_ROPE_DIM // 4, tb)
        full = jnp.concatenate([blk[0:B_NOPE_DIM], rp, z32], axis=0)
        q_out[A_HEADS + hd] = full.astype(_BF16)
        if hd % 4 == 3:
            yield

    ckv = _rms_rows(pt[CKV_OFF:KR_OFF], ckvn_ref[...]).astype(_BF16)
    kv = jnp.dot(wukv_ref[...], ckv, preferred_element_type=_F32)
    kr = _axial_rope_rows(pt[KR_OFF:GATE_OFF], B_ROPE_DIM // 4, tb)
    yield
    kvdim = B_NOPE_DIM + B_V_DIM
    v_base = A_KV_HEADS * A_HEAD_DIM
    for hd in range(B_HEADS):
        kn_h = kv[hd * kvdim:hd * kvdim + B_NOPE_DIM]
        v_h = kv[hd * kvdim + B_NOPE_DIM:(hd + 1) * kvdim]
        v_out[v_base + hd * B_V_DIM:v_base + (hd + 1) * B_V_DIM, :] = v_h.astype(_BF16)
        k_out[1 + hd] = jnp.concatenate([kn_h, kr, z32], axis=0).T.astype(_BF16)
        if hd % 4 == 3:
            yield


def _proj_kernel(mod_ref, ctx_ref, x_ref, w1_ref, wuq_ref, wukv_ref,
                 qn_ref, kn_ref, cqn_ref, ckvn_ref, ta_ref, tb_ref,
                 q_out, k_out, v_out):
    b = pl.program_id(0)
    j = pl.program_id(1)
    batch_stride = pl.num_programs(0)
    is_ctx = j == 0
    ctx_row = mod_ref.shape[0] // 2
    ta = ta_ref[...]
    tb = tb_ref[...]

    def stream(i):
        xin = jnp.where(is_ctx, ctx_ref[i, 0], x_ref[i, 0])
        mrow = jnp.where(is_ctx, mod_ref[ctx_row:ctx_row + 1, 0:2 * D_MODEL],
                         mod_ref[pl.ds(b + i * batch_stride, 1), 0:2 * D_MODEL])
        return _proj_pieces(mrow, xin, w1_ref, wuq_ref, wukv_ref, qn_ref, kn_ref, cqn_ref, ckvn_ref,
                            ta, tb, q_out.at[i, 0], k_out.at[i, 0], v_out.at[i, 0])

    _trace(_stagger([stream(i) for i in range(PROJ_STREAMS)], lead=PROJ_STREAM_LEAD))


def _project(mod, ctx, x, w_in_t, wuqt, wukvt, qn, kn, cqn, ckvn, tab_a, tab_b):
    bsz, s, d = x.shape
    t = PROJ_TILE
    n_steps = (CTX_LEN + s) // t
    total = CTX_LEN + s
    ns = PROJ_STREAMS
    hb = bsz // ns
    lat = lambda j: jnp.maximum(j - 1, 0)
    full2 = lambda shape: pl.BlockSpec(shape, lambda b, j: (0, 0))
    q_all, k_all, v_all = pl.pallas_call(
        _proj_kernel,
        grid=(hb, n_steps),
        in_specs=[
            full2(mod.shape),
            pl.BlockSpec((ns, 1, CTX_LEN, d), lambda b, j: (0, b, 0, 0)),
            pl.BlockSpec((ns, 1, t, d), lambda b, j: (0, b, lat(j), 0)),
            full2((QKV_COLS, d)),
            full2(wuqt.shape), full2(wukvt.shape),
            full2(qn.shape), full2(kn.shape), full2(cqn.shape), full2(ckvn.shape),
            pl.BlockSpec((4, tab_a.shape[1], t), lambda b, j: (0, 0, j)),
            pl.BlockSpec((4, tab_b.shape[1], t), lambda b, j: (0, 0, j)),
        ],
        out_specs=[
            pl.BlockSpec((ns, 1, N_HEADS, HEAD_PAD, t), lambda b, j: (0, b, 0, 0, lat(j))),
            pl.BlockSpec((ns, 1, N_KSETS, t, HEAD_PAD), lambda b, j: (0, b, 0, j, 0)),
            pl.BlockSpec((ns, 1, V_ROWS, t), lambda b, j: (0, b, 0, j)),
        ],
        out_shape=[
            jax.ShapeDtypeStruct((ns, hb, N_HEADS, HEAD_PAD, s), _BF16),
            jax.ShapeDtypeStruct((ns, hb, N_KSETS, total, HEAD_PAD), _BF16),
            jax.ShapeDtypeStruct((ns, hb, V_ROWS, total), _BF16),
        ],
        compiler_params=pltpu.CompilerParams(
            dimension_semantics=("arbitrary", "arbitrary"), vmem_limit_bytes=VMEM_LIMIT),
        name="proj",
    )(mod, ctx.reshape(ns, hb, CTX_LEN, d), x.reshape(ns, hb, s, d),
      w_in_t, wuqt, wukvt, qn, kn, cqn, ckvn, tab_a, tab_b)
    return (q_all.reshape(bsz, N_HEADS, HEAD_PAD, s), k_all.reshape(bsz, N_KSETS, total, HEAD_PAD),
            v_all.reshape(bsz, V_ROWS, total))


def _attn_stream(base, q_ref, k_ref, v_ref, qn_ref, kn_ref, o_ref, o_rows, bufs, m0_ref, slot):
    tq, kc = ATTN_Q_TILE, ATTN_KEY_CHUNK
    n_tiles = q_ref.shape[3] // tq
    n_chunks = k_ref.shape[2] // kc
    assert n_tiles % 2 == 0
    assert n_tiles * tq == q_ref.shape[3] and n_chunks * kc == k_ref.shape[2]
    dv = v_ref.shape[1]
    ones = jnp.ones((ATTN_ONES_ROWS, kc), _BF16)

    def buf_rows(c):
        return pl.ds(pl.multiple_of(base + c * kc, kc), kc)

    def score_chunk(keys_ref, q_tile, buf, c, m):
        s = jnp.dot(keys_ref[0, 0, pl.ds(c * kc, kc), :], q_tile,
                    preferred_element_type=_F32)
        buf[buf_rows(c), :] = s
        cm = jnp.max(s, axis=0, keepdims=True)
        return cm if m is None else jnp.maximum(m, cm)

    def value_chunk(buf, c, m, acc):
        p = jnp.exp2(buf[buf_rows(c), :] - m)
        vext = jnp.concatenate([v_ref[0, :, pl.ds(c * kc, kc)], ones], axis=0)
        pv = jnp.dot(vext, p.astype(_BF16), preferred_element_type=_F32)
        return pv if acc is None else acc + pv

    @pl.when((pl.program_id(0) == 0) & (pl.program_id(1) == 0))
    def _():
        m = None
        for c in range(n_chunks):
            m = score_chunk(k_ref, q_ref[0, 0, :, 0:tq], bufs[0], c, m)
        m0_ref[slot] = m
    yield

    m_cur = m0_ref[slot]
    for t in range(n_tiles):
        m_next = acc = None
        for c in range(n_chunks):
            if t + 1 < n_tiles:
                m_next = score_chunk(k_ref, q_ref[0, 0, :, pl.ds((t + 1) * tq, tq)],
                                     bufs[(t + 1) % 2], c, m_next)
            else:
                m_next = score_chunk(kn_ref, qn_ref[0, 0], bufs[0], c, m_next)
            acc = value_chunk(bufs[t % 2], c, m_cur, acc)
            yield
        o_ref[0, o_rows, pl.ds(t * tq, tq)] = (acc[0:dv] / acc[dv:dv + 1]).astype(_BF16)
        m_cur = m_next
    m0_ref[slot] = m_cur


def _attn_kernel(base_ref, qa_ref, ka_ref, va_ref, qna_ref, kna_ref,
                 qb_ref, kb_ref, vb_ref, qnb_ref, knb_ref, o_ref,
                 sa0_ref, sa1_ref, sb0_ref, sb1_ref, m0_ref):
    base = base_ref[0]
    dv = va_ref.shape[1]
    first = _attn_stream(base, qa_ref, ka_ref, va_ref, qna_ref, kna_ref, o_ref, pl.ds(0, dv),
                         (sa0_ref, sa1_ref), m0_ref, 0)
    second = _attn_stream(base, qb_ref, kb_ref, vb_ref, qnb_ref, knb_ref, o_ref, pl.ds(dv, dv),
                          (sb0_ref, sb1_ref), m0_ref, 1)
    next(first), next(second)
    _trace(_stagger([first, second], lead=ATTN_STREAM_LEAD))


def _attention(q_all, k_all, v_all):
    bsz, _, _, s = q_all.shape
    total = k_all.shape[2]
    group = A_HEADS // A_KV_HEADS
    n_pairs = N_HEADS // 2
    kset = lambda h: jnp.where(h < A_HEADS, 0, h - A_HEADS + 1)
    vblk = lambda h: jnp.where(h < A_HEADS, h // group, h - A_HEADS + A_KV_HEADS)

    def nxt(b, g):
        n = jnp.minimum(b * n_pairs + g + 1, bsz * n_pairs - 1)
        return n // n_pairs, n % n_pairs

    def stream_specs(which):
        head = lambda g: 2 * g + which
        return [
            pl.BlockSpec((1, 1, HEAD_PAD, s), lambda b, g: (b, head(g), 0, 0)),
            pl.BlockSpec((1, 1, total, HEAD_PAD), lambda b, g: (b, kset(head(g)), 0, 0)),
            pl.BlockSpec((1, B_V_DIM, total), lambda b, g: (b, vblk(head(g)), 0)),
            pl.BlockSpec((1, 1, HEAD_PAD, ATTN_Q_TILE),
                         lambda b, g: (nxt(b, g)[0], head(nxt(b, g)[1]), 0, 0)),
            pl.BlockSpec((1, 1, total, HEAD_PAD),
                         lambda b, g: (nxt(b, g)[0], kset(head(nxt(b, g)[1])), 0, 0)),
        ]

    stream_args = (q_all, k_all, v_all, q_all, k_all)
    return pl.pallas_call(
        _attn_kernel,
        grid=(bsz, n_pairs),
        in_specs=[pl.BlockSpec(memory_space=pltpu.SMEM)] + stream_specs(0) + stream_specs(1),
        out_specs=pl.BlockSpec((1, 2 * B_V_DIM, s), lambda b, g: (b, g, 0)),
        out_shape=jax.ShapeDtypeStruct((bsz, N_HEADS * B_V_DIM, s), _BF16),
        scratch_shapes=[pltpu.VMEM((total, ATTN_Q_TILE), _F32)] * 4
        + [pltpu.VMEM((2, 1, ATTN_Q_TILE), _F32)],
        compiler_params=pltpu.CompilerParams(
            dimension_semantics=("arbitrary", "arbitrary"), vmem_limit_bytes=VMEM_LIMIT),
        name="attn",
    )(jnp.zeros((1,), jnp.int32), *stream_args, *stream_args)


def _merge_pieces(mrow, x, ot, win_ref, wpa_ref, wpb_ref, wo_ref, g_ref, b_ref, result):
    shift, scale, gate = mrow[:, 0:D_MODEL], mrow[:, D_MODEL:2 * D_MODEL], mrow[:, 2 * D_MODEL:3 * D_MODEL]
    h = (_layer_norm(x) * (1.0 + scale) + shift).astype(_BF16)
    yield
    na = A_HEADS * A_HEAD_DIM
    ys = []
    for r in range(D_MODEL // MERGE_ROW_CHUNK):
        ra = pl.ds(r * MERGE_ROW_CHUNK, MERGE_ROW_CHUNK)
        ga_rows = pl.ds(GATE_OFF + r * MERGE_ROW_CHUNK, MERGE_ROW_CHUNK)
        gb_rows = pl.ds(GATE_OFF + D_MODEL + r * MERGE_ROW_CHUNK, MERGE_ROW_CHUNK)
        ga = lax.dot_general(win_ref[ga_rows, :], h, _NT, preferred_element_type=_F32)
        gb = lax.dot_general(win_ref[gb_rows, :], h, _NT, preferred_element_type=_F32)
        ya = jnp.dot(wpa_ref[ra, :], ot[0:na], preferred_element_type=_F32)
        yb = jnp.dot(wpb_ref[ra, :], ot[na:], preferred_element_type=_F32)
        ys.append((_sigmoid(ga) * ya + _sigmoid(gb) * yb).astype(_BF16))
        yield
    zt = jnp.dot(wo_ref[...], jnp.concatenate(ys, axis=0), preferred_element_type=_F32)
    r = DEEPNORM_ALPHA * x + gate * zt.T
    result.append(_layer_norm(r) * g_ref[...] + b_ref[...])
    yield


def _ffn_pieces(mrow, x, wup_ref, wdn_ref, g_ref, b_ref, result):
    shift, scale, gate = mrow[:, 0:D_MODEL], mrow[:, D_MODEL:2 * D_MODEL], mrow[:, 2 * D_MODEL:3 * D_MODEL]
    h = (_layer_norm(x) * (1.0 + scale) + shift).astype(_BF16)
    yield
    fs = []
    for c in range(FFN_HIDDEN // FFN_COL_CHUNK):
        a = jnp.dot(h, wup_ref[:, pl.ds(c * FFN_COL_CHUNK, FFN_COL_CHUNK)], preferred_element_type=_F32)
        u = jnp.dot(h, wup_ref[:, pl.ds(FFN_HIDDEN + c * FFN_COL_CHUNK, FFN_COL_CHUNK)],
                    preferred_element_type=_F32)
        fs.append((a * _sigmoid(a) * u).astype(_BF16))
        yield
    dn = jnp.dot(jnp.concatenate(fs, axis=1), wdn_ref[...], preferred_element_type=_F32)
    r = DEEPNORM_ALPHA * x + gate * dn
    result.append(_layer_norm(r) * g_ref[...] + b_ref[...])
    yield


def _post_kernel(mod_ref, x_ref, ot_ref, win_ref, wpa_ref, wpb_ref, wo_ref, g1_ref, b1_ref,
                 wup_ref, wdn_ref, g2_ref, b2_ref, out_ref, x1_ref, *, tiles_per_batch):
    j = pl.program_id(0)
    n_tiles = pl.num_programs(0) - 1
    half_batch = n_tiles // tiles_per_batch

    @pl.when(j == 0)
    def _():
        x1_ref[...] = jnp.zeros_like(x1_ref)

    b_prev = jnp.maximum(j - 1, 0) // tiles_per_batch
    b_cur = jnp.minimum(j, n_tiles - 1) // tiles_per_batch
    ffn_out, merge_out = ([], []), ([], [])

    def stream(half):
        off = half * half_batch
        return _stagger(
            [_ffn_pieces(mod_ref[pl.ds(b_prev + off, 1), 3 * D_MODEL:6 * D_MODEL], x1_ref[half],
                         wup_ref, wdn_ref, g2_ref, b2_ref, ffn_out[half]),
             _merge_pieces(mod_ref[pl.ds(b_cur + off, 1), 0:3 * D_MODEL], x_ref[half, 0],
                           ot_ref[half, 0], win_ref, wpa_ref, wpb_ref, wo_ref, g1_ref, b1_ref,
                           merge_out[half])],
            rates=[POST_FFN_PIECES_PER_MERGE_PIECE, 1])

    _trace(_stagger([stream(0), stream(1)], lead=POST_STREAM_LEAD))
    for half in range(2):
        out_ref[half, 0] = ffn_out[half][0]
        x1_ref[half] = merge_out[half][0]


def _post(mod, x, ot, w_in_t, wpat, wpbt, wot, g1, b1, wup, wdn, g2, b2):
    bsz, s, d = x.shape
    t = TOKEN_TILE
    tpb = s // t
    hb = bsz // 2
    n_tiles = hb * tpb
    const = lambda a: pl.BlockSpec(a.shape, lambda j: (0,) * a.ndim, pipeline_mode=pl.Buffered(1))
    cur = lambda j: jnp.minimum(j, n_tiles - 1)
    prev = lambda j: jnp.maximum(j - 1, 0)
    out = pl.pallas_call(
        functools.partial(_post_kernel, tiles_per_batch=tpb),
        grid=(n_tiles + 1,),
        in_specs=[
            const(mod),
            pl.BlockSpec((2, 1, t, d), lambda j: (0, cur(j) // tpb, cur(j) % tpb, 0)),
            pl.BlockSpec((2, 1, ot.shape[1], t), lambda j: (0, cur(j) // tpb, 0, cur(j) % tpb)),
            const(w_in_t), const(wpat), const(wpbt), const(wot), const(g1), const(b1),
            const(wup), const(wdn), const(g2), const(b2),
        ],
        out_specs=pl.BlockSpec((2, 1, t, d), lambda j: (0, prev(j) // tpb, prev(j) % tpb, 0)),
        out_shape=jax.ShapeDtypeStruct((2, hb, s, d), _F32),
        scratch_shapes=[pltpu.VMEM((2, t, d), _F32)],
        compiler_params=pltpu.CompilerParams(
            dimension_semantics=("arbitrary",), vmem_limit_bytes=VMEM_LIMIT),
        name="post",
    )(mod, x.reshape(2, hb, s, d), ot.reshape(2, hb, ot.shape[1], s),
      w_in_t, wpat, wpbt, wot, g1, b1, wup, wdn, g2, b2)
    return out.reshape(bsz, s, d)


def _rope_tables(seq, half):
    pos = np.arange(seq)
    freqs = ROPE_THETA ** (-np.arange(half, dtype=np.float64) / half)
    tabs = []
    for p in (pos // GRID_W, pos % GRID_W):
        ang = p[None, :].astype(np.float64) * freqs[:, None]
        for fn, ident in ((np.cos, 1.0), (np.sin, 0.0)):
            tabs.append(np.concatenate([np.full((half, CTX_LEN), ident), fn(ang)], axis=1))
    return jnp.asarray(np.stack(tabs), dtype=_F32)


def kernel(x, c, ctx, c_ctx, w_mod, b_mod, w_in, q_norm_a, k_norm_a, cq_norm, ckv_norm,
           w_uq, w_ukv, w_proj_a, w_proj_b, w_out, ln1_g, ln1_b, w_up, w_down, ln2_g, ln2_b):
    bsz, s, d = x.shape
    assert w_mod.shape[0] == DEPTH == 1 and d == D_MODEL and ctx.shape[1] == CTX_LEN
    assert bsz == 8 and s % ATTN_Q_TILE == 0 and s % TOKEN_TILE == 0

    cc = jnp.concatenate([c, c_ctx[None, :], jnp.zeros((bsz - 1, d), _F32)], axis=0)
    mod = _modulation(cc, w_mod[0], b_mod[0][None, :])

    tbf = lambda w: w[0].astype(_BF16).T
    w_in_t = tbf(w_in)
    col = lambda v: v[0][:, None]
    tab_a = _rope_tables(s, A_HEAD_DIM // 4)
    tab_b = _rope_tables(s, B_ROPE_DIM // 4)

    q_all, k_all, v_all = _project(
        mod, ctx, x, w_in_t, tbf(w_uq), tbf(w_ukv),
        col(q_norm_a), col(k_norm_a), col(cq_norm), col(ckv_norm), tab_a, tab_b)
    ot = _attention(q_all, k_all, v_all)
    return _post(mod, x, ot, w_in_t, tbf(w_proj_a), tbf(w_proj_b), tbf(w_out), ln1_g, ln1_b,
                 w_up[0].astype(_BF16), w_down[0].astype(_BF16), ln2_g, ln2_b)
```

```python
import functools
import math

import numpy as np
import jax
import jax.numpy as jnp
from jax import lax
from jax.experimental import pallas as pl
from jax.experimental.pallas import tpu as pltpu

D_MODEL = 1024
GRID_W = 64
CTX_LEN = 256
ROPE_THETA = 10000.0
EPS = 1e-6

A_HEADS = 8
A_KV_HEADS = 2
A_HEAD_DIM = 64
B_HEADS = 8
B_Q_RANK = 384
B_KV_RANK = 256
B_NOPE_DIM = 64
B_ROPE_DIM = 32
B_V_DIM = 64
FFN_HIDDEN = 2816
DEPTH = 1

A_SCALE = A_HEAD_DIM ** -0.5
B_SCALE = (B_NOPE_DIM + B_ROPE_DIM) ** -0.5
DEEPNORM_ALPHA = (2.0 * DEPTH) ** 0.25
LOG2_E = math.log2(math.e)

QA_OFF = 0
KA_OFF = QA_OFF + A_HEADS * A_HEAD_DIM
VA_OFF = KA_OFF + A_KV_HEADS * A_HEAD_DIM
CQ_OFF = VA_OFF + A_KV_HEADS * A_HEAD_DIM
CKV_OFF = CQ_OFF + B_Q_RANK
KR_OFF = CKV_OFF + B_KV_RANK
GATE_OFF = KR_OFF + B_ROPE_DIM
QKV_COLS = GATE_OFF

N_HEADS = A_HEADS + B_HEADS
HEAD_PAD = 128
N_KSETS = 1 + B_HEADS
V_ROWS = A_KV_HEADS * A_HEAD_DIM + B_HEADS * B_V_DIM

MOD_COL_TILE = 1024
PROJ_TILE = 256
PROJ_ROW_CHUNK = 288
PROJ_STREAMS = 4
PROJ_STREAM_LEAD = 3
ATTN_Q_TILE = 512
ATTN_KEY_CHUNK = 256
ATTN_STREAM_LEAD = 18
ATTN_ONES_ROWS = 16
TOKEN_TILE = 256
MERGE_ROW_CHUNK = 256
FFN_COL_CHUNK = 256
POST_FFN_PIECES_PER_MERGE_PIECE = 2
POST_STREAM_LEAD = 10
VMEM_LIMIT = 56 * 1024 * 1024

_NT = (((1,), (1,)), ((), ()))
_F32 = jnp.float32
_BF16 = jnp.bfloat16


def _layer_norm(x):
    mu = jnp.mean(x, axis=-1, keepdims=True)
    xc = x - mu
    var = jnp.mean(xc * xc, axis=-1, keepdims=True)
    return xc * lax.rsqrt(var + EPS)


def _sigmoid(x):
    return jax.nn.sigmoid(x)


def _mod_kernel(c_ref, w_ref, b_ref, o_ref):
    c = c_ref[...]
    a = (c * _sigmoid(c)).astype(_BF16)
    o_ref[...] = jnp.dot(a, w_ref[...].astype(_BF16), preferred_element_type=_F32) + b_ref[...]


def _modulation(cc, w_mod, b_mod):
    rows, d = cc.shape
    n = w_mod.shape[1]
    return pl.pallas_call(
        _mod_kernel,
        grid=(n // MOD_COL_TILE,),
        in_specs=[
            pl.BlockSpec((rows, d), lambda i: (0, 0)),
            pl.BlockSpec((d, MOD_COL_TILE), lambda i: (0, i)),
            pl.BlockSpec((1, MOD_COL_TILE), lambda i: (0, i)),
        ],
        out_specs=pl.BlockSpec((rows, MOD_COL_TILE), lambda i: (0, i)),
        out_shape=jax.ShapeDtypeStruct((rows, n), _F32),
        compiler_params=pltpu.CompilerParams(
            dimension_semantics=("arbitrary",), vmem_limit_bytes=VMEM_LIMIT),
        name="mod",
    )(cc, w_mod, b_mod)


def _rms_rows(x, g):
    ms = jnp.mean(x * x, axis=0, keepdims=True)
    return x * lax.rsqrt(ms + EPS) * g


def _axial_rope_rows(x, half, tab):
    r1, r2 = x[0:half], x[half:2 * half]
    c1, c2 = x[2 * half:3 * half], x[3 * half:4 * half]
    cr, sr, cc, sc = tab[0], tab[1], tab[2], tab[3]
    return jnp.concatenate(
        [r1 * cr - r2 * sr, r1 * sr + r2 * cr, c1 * cc - c2 * sc, c1 * sc + c2 * cc], axis=0)


def _stagger(streams, lead=0, rates=None):
    rates = rates or [1] * len(streams)
    live = dict(enumerate(streams))
    rnd = 0
    while live:
        for i in sorted(live):
            if rnd < i * lead:
                continue
            for _ in range(rates[i]):
                if next(live[i], StopIteration) is StopIteration:
                    del live[i]
                    break
                yield
        rnd += 1


def _trace(pieces):
    for _ in pieces:
        pass


def _proj_pieces(mrow, xin, w1_ref, wuq_ref, wukv_ref, qn_ref, kn_ref, cqn_ref, ckvn_ref, ta, tb,
                 q_out, k_out, v_out):
    shift, scale = mrow[:, 0:D_MODEL], mrow[:, D_MODEL:2 * D_MODEL]
    h = (_layer_norm(xin) * (1.0 + scale) + shift).astype(_BF16)
    yield
    parts = []
    for r in range(QKV_COLS // PROJ_ROW_CHUNK):
        rows = pl.ds(r * PROJ_ROW_CHUNK, PROJ_ROW_CHUNK)
        parts.append(lax.dot_general(w1_ref[rows, :], h, _NT, preferred_element_type=_F32))
        yield
    pt = jnp.concatenate(parts, axis=0)
    t = pt.shape[1]
    z64 = jnp.zeros((A_HEAD_DIM, t), _F32)
    z32 = jnp.zeros((HEAD_PAD - B_NOPE_DIM - B_ROPE_DIM, t), _F32)

    qn = qn_ref[...] * (A_SCALE * LOG2_E)
    group = A_HEADS // A_KV_HEADS
    for hd in range(A_HEADS):
        xh = pt[QA_OFF + hd * A_HEAD_DIM:QA_OFF + (hd + 1) * A_HEAD_DIM]
        xh = _axial_rope_rows(_rms_rows(xh, qn), A_HEAD_DIM // 4, ta)
        full = jnp.concatenate([xh, z64] if hd // group == 0 else [z64, xh], axis=0)
        q_out[hd] = full.astype(_BF16)
        if hd % 4 == 3:
            yield

    kn = kn_ref[...]
    ka = [
        _axial_rope_rows(
            _rms_rows(pt[KA_OFF + g * A_HEAD_DIM:KA_OFF + (g + 1) * A_HEAD_DIM], kn),
            A_HEAD_DIM // 4, ta)
        for g in range(A_KV_HEADS)
    ]
    k_out[0] = jnp.concatenate(ka, axis=0).T.astype(_BF16)
    v_out[0:A_KV_HEADS * A_HEAD_DIM, :] = pt[VA_OFF:CQ_OFF].astype(_BF16)
    yield

    cq = _rms_rows(pt[CQ_OFF:CKV_OFF], cqn_ref[...]).astype(_BF16)
    qb = jnp.dot(wuq_ref[...], cq, preferred_element_type=_F32) * (B_SCALE * LOG2_E)
    yield
    qdim = B_NOPE_DIM + B_ROPE_DIM
    for hd in range(B_HEADS):
        blk = qb[hd * qdim:(hd + 1) * qdim]
        rp = _axial_rope_rows(blk[B_NOPE_DIM:qdim], B_ROPE_DIM // 4, tb)
        full = jnp.concatenate([blk[0:B_NOPE_DIM], rp, z32], axis=0)
        q_out[A_HEADS + hd] = full.astype(_BF16)
        if hd % 4 == 3:
            yield

    ckv = _rms_rows(pt[CKV_OFF:KR_OFF], ckvn_ref[...]).astype(_BF16)
    kv = jnp.dot(wukv_ref[...], ckv, preferred_element_type=_F32)
    kr = _axial_rope_rows(pt[KR_OFF:GATE_OFF], B_ROPE_DIM // 4, tb)
    yield
    kvdim = B_NOPE_DIM + B_V_DIM
    v_base = A_KV_HEADS * A_HEAD_DIM
    for hd in range(B_HEADS):
        kn_h = kv[hd * kvdim:hd * kvdim + B_NOPE_DIM]
        v_h = kv[hd * kvdim + B_NOPE_DIM:(hd + 1) * kvdim]
        v_out[v_base + hd * B_V_DIM:v_base + (hd + 1) * B_V_DIM, :] = v_h.astype(_BF16)
        k_out[1 + hd] = jnp.concatenate([kn_h, kr, z32], axis=0).T.astype(_BF16)
        if hd % 4 == 3:
            yield


def _proj_kernel(mod_ref, ctx_ref, x_ref, w1_ref, wuq_ref, wukv_ref,
                 qn_ref, kn_ref, cqn_ref, ckvn_ref, ta_ref, tb_ref,
                 q_out, k_out, v_out):
    b = pl.program_id(0)
    j = pl.program_id(1)
    batch_stride = pl.num_programs(0)
    is_ctx = j == 0
    ctx_row = mod_ref.shape[0] // 2
    ta = ta_ref[...]
    tb = tb_ref[...]

    def stream(i):
        xin = jnp.where(is_ctx, ctx_ref[i, 0], x_ref[i, 0])
        mrow = jnp.where(is_ctx, mod_ref[ctx_row:ctx_row + 1, 0:2 * D_MODEL],
                         mod_ref[pl.ds(b + i * batch_stride, 1), 0:2 * D_MODEL])
        return _proj_pieces(mrow, xin, w1_ref, wuq_ref, wukv_ref, qn_ref, kn_ref, cqn_ref, ckvn_ref,
                            ta, tb, q_out.at[i, 0], k_out.at[i, 0], v_out.at[i, 0])

    _trace(_stagger([stream(i) for i in range(PROJ_STREAMS)], lead=PROJ_STREAM_LEAD))


def _project(mod, ctx, x, w_in_t, wuqt, wukvt, qn, kn, cqn, ckvn, tab_a, tab_b):
    bsz, s, d = x.shape
    t = PROJ_TILE
    n_steps = (CTX_LEN + s) // t
    total = CTX_LEN + s
    ns = PROJ_STREAMS
    hb = bsz // ns
    lat = lambda j: jnp.maximum(j - 1, 0)
    full2 = lambda shape: pl.BlockSpec(shape, lambda b, j: (0, 0))
    q_all, k_all, v_all = pl.pallas_call(
        _proj_kernel,
        grid=(hb, n_steps),
        in_specs=[
            full2(mod.shape),
            pl.BlockSpec((ns, 1, CTX_LEN, d), lambda b, j: (0, b, 0, 0)),
            pl.BlockSpec((ns, 1, t, d), lambda b, j: (0, b, lat(j), 0)),
            full2((QKV_COLS, d)),
            full2(wuqt.shape), full2(wukvt.shape),
            full2(qn.shape), full2(kn.shape), full2(cqn.shape), full2(ckvn.shape),
            pl.BlockSpec((4, tab_a.shape[1], t), lambda b, j: (0, 0, j)),
            pl.BlockSpec((4, tab_b.shape[1], t), lambda b, j: (0, 0, j)),
        ],
        out_specs=[
            pl.BlockSpec((ns, 1, N_HEADS, HEAD_PAD, t), lambda b, j: (0, b, 0, 0, lat(j))),
            pl.BlockSpec((ns, 1, N_KSETS, t, HEAD_PAD), lambda b, j: (0, b, 0, j, 0)),
            pl.BlockSpec((ns, 1, V_ROWS, t), lambda b, j: (0, b, 0, j)),
        ],
        out_shape=[
            jax.ShapeDtypeStruct((ns, hb, N_HEADS, HEAD_PAD, s), _BF16),
            jax.ShapeDtypeStruct((ns, hb, N_KSETS, total, HEAD_PAD), _BF16),
            jax.ShapeDtypeStruct((ns, hb, V_ROWS, total), _BF16),
        ],
        compiler_params=pltpu.CompilerParams(
            dimension_semantics=("arbitrary", "arbitrary"), vmem_limit_bytes=VMEM_LIMIT),
        name="proj",
    )(mod, ctx.reshape(ns, hb, CTX_LEN, d), x.reshape(ns, hb, s, d),
      w_in_t, wuqt, wukvt, qn, kn, cqn, ckvn, tab_a, tab_b)
    return (q_all.reshape(bsz, N_HEADS, HEAD_PAD, s), k_all.reshape(bsz, N_KSETS, total, HEAD_PAD),
            v_all.reshape(bsz, V_ROWS, total))


def _attn_stream(base, q_ref, k_ref, v_ref, qn_ref, kn_ref, o_ref, o_rows, bufs, m0_ref, slot):
    tq, kc = ATTN_Q_TILE, ATTN_KEY_CHUNK
    n_tiles = q_ref.shape[3] // tq
    n_chunks = k_ref.shape[2] // kc
    assert n_tiles % 2 == 0
    assert n_tiles * tq == q_ref.shape[3] and n_chunks * kc == k_ref.shape[2]
    dv = v_ref.shape[1]
    ones = jnp.ones((ATTN_ONES_ROWS, kc), _BF16)

    def buf_rows(c):
        return pl.ds(pl.multiple_of(base + c * kc, kc), kc)

    def score_chunk(keys_ref, q_tile, buf, c, m):
        s = jnp.dot(keys_ref[0, 0, pl.ds(c * kc, kc), :], q_tile,
                    preferred_element_type=_F32)
        buf[buf_rows(c), :] = s
        cm = jnp.max(s, axis=0, keepdims=True)
        return cm if m is None else jnp.maximum(m, cm)

    def value_chunk(buf, c, m, acc):
        p = jnp.exp2(buf[buf_rows(c), :] - m)
        vext = jnp.concatenate([v_ref[0, :, pl.ds(c * kc, kc)], ones], axis=0)
        pv = jnp.dot(vext, p.astype(_BF16), preferred_element_type=_F32)
        return pv if acc is None else acc + pv

    @pl.when((pl.program_id(0) == 0) & (pl.program_id(1) == 0))
    def _():
        m = None
        for c in range(n_chunks):
            m = score_chunk(k_ref, q_ref[0, 0, :, 0:tq], bufs[0], c, m)
        m0_ref[slot] = m
    yield

    m_cur = m0_ref[slot]
    for t in range(n_tiles):
        m_next = acc = None
        for c in range(n_chunks):
            if t + 1 < n_tiles:
                m_next = score_chunk(k_ref, q_ref[0, 0, :, pl.ds((t + 1) * tq, tq)],
                                     bufs[(t + 1) % 2], c, m_next)
            else:
                m_next = score_chunk(kn_ref, qn_ref[0, 0], bufs[0], c, m_next)
            acc = value_chunk(bufs[t % 2], c, m_cur, acc)
            yield
        o_ref[0, o_rows, pl.ds(t * tq, tq)] = (acc[0:dv] / acc[dv:dv + 1]).astype(_BF16)
        m_cur = m_next
    m0_ref[slot] = m_cur


def _attn_kernel(base_ref, qa_ref, ka_ref, va_ref, qna_ref, kna_ref,
                 qb_ref, kb_ref, vb_ref, qnb_ref, knb_ref, o_ref,
                 sa0_ref, sa1_ref, sb0_ref, sb1_ref, m0_ref):
    base = base_ref[0]
    dv = va_ref.shape[1]
    first = _attn_stream(base, qa_ref, ka_ref, va_ref, qna_ref, kna_ref, o_ref, pl.ds(0, dv),
                         (sa0_ref, sa1_ref), m0_ref, 0)
    second = _attn_stream(base, qb_ref, kb_ref, vb_ref, qnb_ref, knb_ref, o_ref, pl.ds(dv, dv),
                          (sb0_ref, sb1_ref), m0_ref, 1)
    next(first), next(second)
    _trace(_stagger([first, second], lead=ATTN_STREAM_LEAD))


def _attention(q_all, k_all, v_all):
    bsz, _, _, s = q_all.shape
    total = k_all.shape[2]
    group = A_HEADS // A_KV_HEADS
    n_pairs = N_HEADS // 2
    kset = lambda h: jnp.where(h < A_HEADS, 0, h - A_HEADS + 1)
    vblk = lambda h: jnp.where(h < A_HEADS, h // group, h - A_HEADS + A_KV_HEADS)

    def nxt(b, g):
        n = jnp.minimum(b * n_pairs + g + 1, bsz * n_pairs - 1)
        return n // n_pairs, n % n_pairs

    def stream_specs(which):
        head = lambda g: 2 * g + which
        return [
            pl.BlockSpec((1, 1, HEAD_PAD, s), lambda b, g: (b, head(g), 0, 0)),
            pl.BlockSpec((1, 1, total, HEAD_PAD), lambda b, g: (b, kset(head(g)), 0, 0)),
            pl.BlockSpec((1, B_V_DIM, total), lambda b, g: (b, vblk(head(g)), 0)),
            pl.BlockSpec((1, 1, HEAD_PAD, ATTN_Q_TILE),
                         lambda b, g: (nxt(b, g)[0], head(nxt(b, g)[1]), 0, 0)),
            pl.BlockSpec((1, 1, total, HEAD_PAD),
                         lambda b, g: (nxt(b, g)[0], kset(head(nxt(b, g)[1])), 0, 0)),
        ]

    stream_args = (q_all, k_all, v_all, q_all, k_all)
    return pl.pallas_call(
        _attn_kernel,
        grid=(bsz, n_pairs),
        in_specs=[pl.BlockSpec(memory_space=pltpu.SMEM)] + stream_specs(0) + stream_specs(1),
        out_specs=pl.BlockSpec((1, 2 * B_V_DIM, s), lambda b, g: (b, g, 0)),
        out_shape=jax.ShapeDtypeStruct((bsz, N_HEADS * B_V_DIM, s), _BF16),
        scratch_shapes=[pltpu.VMEM((total, ATTN_Q_TILE), _F32)] * 4
        + [pltpu.VMEM((2, 1, ATTN_Q_TILE), _F32)],
        compiler_params=pltpu.CompilerParams(
            dimension_semantics=("arbitrary", "arbitrary"), vmem_limit_bytes=VMEM_LIMIT),
        name="attn",
    )(jnp.zeros((1,), jnp.int32), *stream_args, *stream_args)


def _merge_pieces(mrow, x, ot, win_ref, wpa_ref, wpb_ref, wo_ref, g_ref, b_ref, result):
    shift, scale, gate = mrow[:, 0:D_MODEL], mrow[:, D_MODEL:2 * D_MODEL], mrow[:, 2 * D_MODEL:3 * D_MODEL]
    h = (_layer_norm(x) * (1.0 + scale) + shift).astype(_BF16)
    yield
    na = A_HEADS * A_HEAD_DIM
    ys = []
    for r in range(D_MODEL // MERGE_ROW_CHUNK):
        ra = pl.ds(r * MERGE_ROW_CHUNK, MERGE_ROW_CHUNK)
        ga_rows = pl.ds(GATE_OFF + r * MERGE_ROW_CHUNK, MERGE_ROW_CHUNK)
        gb_rows = pl.ds(GATE_OFF + D_MODEL + r * MERGE_ROW_CHUNK, MERGE_ROW_CHUNK)
        ga = lax.dot_general(win_ref[ga_rows, :], h, _NT, preferred_element_type=_F32)
        gb = lax.dot_general(win_ref[gb_rows, :], h, _NT, preferred_element_type=_F32)
        ya = jnp.dot(wpa_ref[ra, :], ot[0:na], preferred_element_type=_F32)
        yb = jnp.dot(wpb_ref[ra, :], ot[na:], preferred_element_type=_F32)
        ys.append((_sigmoid(ga) * ya + _sigmoid(gb) * yb).astype(_BF16))
        yield
    zt = jnp.dot(wo_ref[...], jnp.concatenate(ys, axis=0), preferred_element_type=_F32)
    r = DEEPNORM_ALPHA * x + gate * zt.T
    result.append(_layer_norm(r) * g_ref[...] + b_ref[...])
    yield


def _ffn_pieces(mrow, x, wup_ref, wdn_ref, g_ref, b_ref, result):
    shift, scale, gate = mrow[:, 0:D_MODEL], mrow[:, D_MODEL:2 * D_MODEL], mrow[:, 2 * D_MODEL:3 * D_MODEL]
    h = (_layer_norm(x) * (1.0 + scale) + shift).astype(_BF16)
    yield
    fs = []
    for c in range(FFN_HIDDEN // FFN_COL_CHUNK):
        a = jnp.dot(h, wup_ref[:, pl.ds(c * FFN_COL_CHUNK, FFN_COL_CHUNK)], preferred_element_type=_F32)
        u = jnp.dot(h, wup_ref[:, pl.ds(FFN_HIDDEN + c * FFN_COL_CHUNK, FFN_COL_CHUNK)],
                    preferred_element_type=_F32)
        fs.append((a * _sigmoid(a) * u).astype(_BF16))
        yield
    dn = jnp.dot(jnp.concatenate(fs, axis=1), wdn_ref[...], preferred_element_type=_F32)
    r = DEEPNORM_ALPHA * x + gate * dn
    result.append(_layer_norm(r) * g_ref[...] + b_ref[...])
    yield


def _post_kernel(mod_ref, x_ref, ot_ref, win_ref, wpa_ref, wpb_ref, wo_ref, g1_ref, b1_ref,
                 wup_ref, wdn_ref, g2_ref, b2_ref, out_ref, x1_ref, *, tiles_per_batch):
    j = pl.program_id(0)
    n_tiles = pl.num_programs(0) - 1
    half_batch = n_tiles // tiles_per_batch

    @pl.when(j == 0)
    def _():
        x1_ref[...] = jnp.zeros_like(x1_ref)

    b_prev = jnp.maximum(j - 1, 0) // tiles_per_batch
    b_cur = jnp.minimum(j, n_tiles - 1) // tiles_per_batch
    ffn_out, merge_out = ([], []), ([], [])

    def stream(half):
        off = half * half_batch
        return _stagger(
            [_ffn_pieces(mod_ref[pl.ds(b_prev + off, 1), 3 * D_MODEL:6 * D_MODEL], x1_ref[half],
                         wup_ref, wdn_ref, g2_ref, b2_ref, ffn_out[half]),
             _merge_pieces(mod_ref[pl.ds(b_cur + off, 1), 0:3 * D_MODEL], x_ref[half, 0],
                           ot_ref[half, 0], win_ref, wpa_ref, wpb_ref, wo_ref, g1_ref, b1_ref,
                           merge_out[half])],
            rates=[POST_FFN_PIECES_PER_MERGE_PIECE, 1])

    _trace(_stagger([stream(0), stream(1)], lead=POST_STREAM_LEAD))
    for half in range(2):
        out_ref[half, 0] = ffn_out[half][0]
        x1_ref[half] = merge_out[half][0]


def _post(mod, x, ot, w_in_t, wpat, wpbt, wot, g1, b1, wup, wdn, g2, b2):
    bsz, s, d = x.shape
    t = TOKEN_TILE
    tpb = s // t
    hb = bsz // 2
    n_tiles = hb * tpb
    const = lambda a: pl.BlockSpec(a.shape, lambda j: (0,) * a.ndim, pipeline_mode=pl.Buffered(1))
    cur = lambda j: jnp.minimum(j, n_tiles - 1)
    prev = lambda j: jnp.maximum(j - 1, 0)
    out = pl.pallas_call(
        functools.partial(_post_kernel, tiles_per_batch=tpb),
        grid=(n_tiles + 1,),
        in_specs=[
            const(mod),
            pl.BlockSpec((2, 1, t, d), lambda j: (0, cur(j) // tpb, cur(j) % tpb, 0)),
            pl.BlockSpec((2, 1, ot.shape[1], t), lambda j: (0, cur(j) // tpb, 0, cur(j) % tpb)),
            const(w_in_t), const(wpat), const(wpbt), const(wot), const(g1), const(b1),
            const(wup), const(wdn), const(g2), const(b2),
        ],
        out_specs=pl.BlockSpec((2, 1, t, d), lambda j: (0, prev(j) // tpb, prev(j) % tpb, 0)),
        out_shape=jax.ShapeDtypeStruct((2, hb, s, d), _F32),
        scratch_shapes=[pltpu.VMEM((2, t, d), _F32)],
        compiler_params=pltpu.CompilerParams(
            dimension_semantics=("arbitrary",), vmem_limit_bytes=VMEM_LIMIT),
        name="post",
    )(mod, x.reshape(2, hb, s, d), ot.reshape(2, hb, ot.shape[1], s),
      w_in_t, wpat, wpbt, wot, g1, b1, wup, wdn, g2, b2)
    return out.reshape(bsz, s, d)


def _rope_tables(seq, half):
    pos = np.arange(seq)
    freqs = ROPE_THETA ** (-np.arange(half, dtype=np.float64) / half)
    tabs = []
    for p in (pos // GRID_W, pos % GRID_W):
        ang = p[None, :].astype(np.float64) * freqs[:, None]
        for fn, ident in ((np.cos, 1.0), (np.sin, 0.0)):
            tabs.append(np.concatenate([np.full((half, CTX_LEN), ident), fn(ang)], axis=1))
    return jnp.asarray(np.stack(tabs), dtype=_F32)


def kernel(x, c, ctx, c_ctx, w_mod, b_mod, w_in, q_norm_a, k_norm_a, cq_norm, ckv_norm,
           w_uq, w_ukv, w_proj_a, w_proj_b, w_out, ln1_g, ln1_b, w_up, w_down, ln2_g, ln2_b):
    bsz, s, d = x.shape
    assert w_mod.shape[0] == DEPTH == 1 and d == D_MODEL and ctx.shape[1] == CTX_LEN
    assert bsz == 8 and s % ATTN_Q_TILE == 0 and s % TOKEN_TILE == 0

    cc = jnp.concatenate([c, c_ctx[None, :], jnp.zeros((bsz - 1, d), _F32)], axis=0)
    mod = _modulation(cc, w_mod[0], b_mod[0][None, :])

    tbf = lambda w: w[0].astype(_BF16).T
    w_in_t = tbf(w_in)
    col = lambda v: v[0][:, None]
    tab_a = _rope_tables(s, A_HEAD_DIM // 4)
    tab_b = _rope_tables(s, B_ROPE_DIM // 4)

    q_all, k_all, v_all = _project(
        mod, ctx, x, w_in_t, tbf(w_uq), tbf(w_ukv),
        col(q_norm_a), col(k_norm_a), col(cq_norm), col(ckv_norm), tab_a, tab_b)
    ot = _attention(q_all, k_all, v_all)
    return _post(mod, x, ot, w_in_t, tbf(w_proj_a), tbf(w_proj_b), tbf(w_out), ln1_g, ln1_b,
                 w_up[0].astype(_BF16), w_down[0].astype(_BF16), ln2_g, ln2_b)
```

```python
import functools
import math

import numpy as np
import jax
import jax.numpy as jnp
from jax import lax
from jax.experimental import pallas as pl
from jax.experimental.pallas import tpu as pltpu

D_MODEL = 1024
GRID_W = 64
CTX_LEN = 256
ROPE_THETA = 10000.0
EPS = 1e-6

A_HEADS = 8
A_KV_HEADS = 2
A_HEAD_DIM = 64
B_HEADS = 8
B_Q_RANK = 384
B_KV_RANK = 256
B_NOPE_DIM = 64
B_ROPE_DIM = 32
B_V_DIM = 64
FFN_HIDDEN = 2816
DEPTH = 1

A_SCALE = A_HEAD_DIM ** -0.5
B_SCALE = (B_NOPE_DIM + B_ROPE_DIM) ** -0.5
DEEPNORM_ALPHA = (2.0 * DEPTH) ** 0.25
LOG2_E = math.log2(math.e)

QA_OFF = 0
KA_OFF = QA_OFF + A_HEADS * A_HEAD_DIM
VA_OFF = KA_OFF + A_KV_HEADS * A_HEAD_DIM
CQ_OFF = VA_OFF + A_KV_HEADS * A_HEAD_DIM
CKV_OFF = CQ_OFF + B_Q_RANK
KR_OFF = CKV_OFF + B_KV_RANK
GATE_OFF = KR_OFF + B_ROPE_DIM
QKV_COLS = GATE_OFF

N_HEADS = A_HEADS + B_HEADS
HEAD_PAD = 128
N_KSETS = 1 + B_HEADS
V_ROWS = A_KV_HEADS * A_HEAD_DIM + B_HEADS * B_V_DIM

MOD_COL_TILE = 1024
PROJ_TILE = 256
PROJ_ROW_CHUNK = 288
PROJ_STREAMS = 4
PROJ_STREAM_LEAD = 3
ATTN_Q_TILE = 512
ATTN_KEY_CHUNK = 256
ATTN_STREAM_LEAD = 12
ATTN_ONES_ROWS = 16
TOKEN_TILE = 256
MERGE_ROW_CHUNK = 256
FFN_COL_CHUNK = 256
POST_FFN_PIECES_PER_MERGE_PIECE = 2
POST_STREAM_LEAD = 10
VMEM_LIMIT = 56 * 1024 * 1024

_NT = (((1,), (1,)), ((), ()))
_F32 = jnp.float32
_BF16 = jnp.bfloat16


def _layer_norm(x):
    mu = jnp.mean(x, axis=-1, keepdims=True)
    xc = x - mu
    var = jnp.mean(xc * xc, axis=-1, keepdims=True)
    return xc * lax.rsqrt(var + EPS)


def _sigmoid(x):
    return jax.nn.sigmoid(x)


def _mod_kernel(c_ref, w_ref, b_ref, o_ref):
    c = c_ref[...]
    a = (c * _sigmoid(c)).astype(_BF16)
    o_ref[...] = jnp.dot(a, w_ref[...].astype(_BF16), preferred_element_type=_F32) + b_ref[...]


def _modulation(cc, w_mod, b_mod):
    rows, d = cc.shape
    n = w_mod.shape[1]
    return pl.pallas_call(
        _mod_kernel,
        grid=(n // MOD_COL_TILE,),
        in_specs=[
            pl.BlockSpec((rows, d), lambda i: (0, 0)),
            pl.BlockSpec((d, MOD_COL_TILE), lambda i: (0, i)),
            pl.BlockSpec((1, MOD_COL_TILE), lambda i: (0, i)),
        ],
        out_specs=pl.BlockSpec((rows, MOD_COL_TILE), lambda i: (0, i)),
        out_shape=jax.ShapeDtypeStruct((rows, n), _F32),
        compiler_params=pltpu.CompilerParams(
            dimension_semantics=("arbitrary",), vmem_limit_bytes=VMEM_LIMIT),
        name="mod",
    )(cc, w_mod, b_mod)


def _rms_rows(x, g):
    ms = jnp.mean(x * x, axis=0, keepdims=True)
    return x * lax.rsqrt(ms + EPS) * g


def _axial_rope_rows(x, half, tab):
    r1, r2 = x[0:half], x[half:2 * half]
    c1, c2 = x[2 * half:3 * half], x[3 * half:4 * half]
    cr, sr, cc, sc = tab[0], tab[1], tab[2], tab[3]
    return jnp.concatenate(
        [r1 * cr - r2 * sr, r1 * sr + r2 * cr, c1 * cc - c2 * sc, c1 * sc + c2 * cc], axis=0)


def _stagger(streams, lead=0, rates=None):
    rates = rates or [1] * len(streams)
    live = dict(enumerate(streams))
    rnd = 0
    while live:
        for i in sorted(live):
            if rnd < i * lead:
                continue
            for _ in range(rates[i]):
                if next(live[i], StopIteration) is StopIteration:
                    del live[i]
                    break
                yield
        rnd += 1


def _trace(pieces):
    for _ in pieces:
        pass


def _proj_pieces(mrow, xin, w1_ref, wuq_ref, wukv_ref, qn_ref, kn_ref, cqn_ref, ckvn_ref, ta, tb,
                 q_out, k_out, v_out):
    shift, scale = mrow[:, 0:D_MODEL], mrow[:, D_MODEL:2 * D_MODEL]
    h = (_layer_norm(xin) * (1.0 + scale) + shift).astype(_BF16)
    yield
    parts = []
    for r in range(QKV_COLS // PROJ_ROW_CHUNK):
        rows = pl.ds(r * PROJ_ROW_CHUNK, PROJ_ROW_CHUNK)
        parts.append(lax.dot_general(w1_ref[rows, :], h, _NT, preferred_element_type=_F32))
        yield
    pt = jnp.concatenate(parts, axis=0)
    t = pt.shape[1]
    z64 = jnp.zeros((A_HEAD_DIM, t), _F32)
    z32 = jnp.zeros((HEAD_PAD - B_NOPE_DIM - B_ROPE_DIM, t), _F32)

    qn = qn_ref[...] * (A_SCALE * LOG2_E)
    group = A_HEADS // A_KV_HEADS
    for hd in range(A_HEADS):
        xh = pt[QA_OFF + hd * A_HEAD_DIM:QA_OFF + (hd + 1) * A_HEAD_DIM]
        xh = _axial_rope_rows(_rms_rows(xh, qn), A_HEAD_DIM // 4, ta)
        full = jnp.concatenate([xh, z64] if hd // group == 0 else [z64, xh], axis=0)
        q_out[hd] = full.astype(_BF16)
        if hd % 4 == 3:
            yield

    kn = kn_ref[...]
    ka = [
        _axial_rope_rows(
            _rms_rows(pt[KA_OFF + g * A_HEAD_DIM:KA_OFF + (g + 1) * A_HEAD_DIM], kn),
            A_HEAD_DIM // 4, ta)
        for g in range(A_KV_HEADS)
    ]
    k_out[0] = jnp.concatenate(ka, axis=0).T.astype(_BF16)
    v_out[0:A_KV_HEADS * A_HEAD_DIM, :] = pt[VA_OFF:CQ_OFF].astype(_BF16)
    yield

    cq = _rms_rows(pt[CQ_OFF:CKV_OFF], cqn_ref[...]).astype(_BF16)
    qb = jnp.dot(wuq_ref[...], cq, preferred_element_type=_F32) * (B_SCALE * LOG2_E)
    yield
    qdim = B_NOPE_DIM + B_ROPE_DIM
    for hd in range(B_HEADS):
        blk = qb[hd * qdim:(hd + 1) * qdim]
        rp = _axial_rope_rows(blk[B_NOPE_DIM:qdim], B_ROPE_DIM // 4, tb)
        full = jnp.concatenate([blk[0:B_NOPE_DIM], rp, z32], axis=0)
        q_out[A_HEADS + hd] = full.astype(_BF16)
        if hd % 4 == 3:
            yield

    ckv = _rms_rows(pt[CKV_OFF:KR_OFF], ckvn_ref[...]).astype(_BF16)
    kv = jnp.dot(wukv_ref[...], ckv, preferred_element_type=_F32)
    kr = _axial_rope_rows(pt[KR_OFF:GATE_OFF], B_ROPE_DIM // 4, tb)
    yield
    kvdim = B_NOPE_DIM + B_V_DIM
    v_base = A_KV_HEADS * A_HEAD_DIM
    for hd in range(B_HEADS):
        kn_h = kv[hd * kvdim:hd * kvdim + B_NOPE_DIM]
        v_h = kv[hd * kvdim + B_NOPE_DIM:(hd + 1) * kvdim]
        v_out[v_base + hd * B_V_DIM:v_base + (hd + 1) * B_V_DIM, :] = v_h.astype(_BF16)
        k_out[1 + hd] = jnp.concatenate([kn_h, kr, z32], axis=0).T.astype(_BF16)
        if hd % 4 == 3:
            yield


def _proj_kernel(mod_ref, ctx_ref, x_ref, w1_ref, wuq_ref, wukv_ref,
                 qn_ref, kn_ref, cqn_ref, ckvn_ref, ta_ref, tb_ref,
                 q_out, k_out, v_out):
    b = pl.program_id(0)
    j = pl.program_id(1)
    batch_stride = pl.num_programs(0)
    is_ctx = j == 0
    ctx_row = mod_ref.shape[0] // 2
    ta = ta_ref[...]
    tb = tb_ref[...]

    def stream(i):
        xin = jnp.where(is_ctx, ctx_ref[i, 0], x_ref[i, 0])
        mrow = jnp.where(is_ctx, mod_ref[ctx_row:ctx_row + 1, 0:2 * D_MODEL],
                         mod_ref[pl.ds(b + i * batch_stride, 1), 0:2 * D_MODEL])
        return _proj_pieces(mrow, xin, w1_ref, wuq_ref, wukv_ref, qn_ref, kn_ref, cqn_ref, ckvn_ref,
                            ta, tb, q_out.at[i, 0], k_out.at[i, 0], v_out.at[i, 0])

    _trace(_stagger([stream(i) for i in range(PROJ_STREAMS)], lead=PROJ_STREAM_LEAD))


def _project(mod, ctx, x, w_in_t, wuqt, wukvt, qn, kn, cqn, ckvn, tab_a, tab_b):
    bsz, s, d = x.shape
    t = PROJ_TILE
    n_steps = (CTX_LEN + s) // t
    total = CTX_LEN + s
    ns = PROJ_STREAMS
    hb = bsz // ns
    lat = lambda j: jnp.maximum(j - 1, 0)
    full2 = lambda shape: pl.BlockSpec(shape, lambda b, j: (0, 0))
    q_all, k_all, v_all = pl.pallas_call(
        _proj_kernel,
        grid=(hb, n_steps),
        in_specs=[
            full2(mod.shape),
            pl.BlockSpec((ns, 1, CTX_LEN, d), lambda b, j: (0, b, 0, 0)),
            pl.BlockSpec((ns, 1, t, d), lambda b, j: (0, b, lat(j), 0)),
            full2((QKV_COLS, d)),
            full2(wuqt.shape), full2(wukvt.shape),
            full2(qn.shape), full2(kn.shape), full2(cqn.shape), full2(ckvn.shape),
            pl.BlockSpec((4, tab_a.shape[1], t), lambda b, j: (0, 0, j)),
            pl.BlockSpec((4, tab_b.shape[1], t), lambda b, j: (0, 0, j)),
        ],
        out_specs=[
            pl.BlockSpec((ns, 1, N_HEADS, HEAD_PAD, t), lambda b, j: (0, b, 0, 0, lat(j))),
            pl.BlockSpec((ns, 1, N_KSETS, t, HEAD_PAD), lambda b, j: (0, b, 0, j, 0)),
            pl.BlockSpec((ns, 1, V_ROWS, t), lambda b, j: (0, b, 0, j)),
        ],
        out_shape=[
            jax.ShapeDtypeStruct((ns, hb, N_HEADS, HEAD_PAD, s), _BF16),
            jax.ShapeDtypeStruct((ns, hb, N_KSETS, total, HEAD_PAD), _BF16),
            jax.ShapeDtypeStruct((ns, hb, V_ROWS, total), _BF16),
        ],
        compiler_params=pltpu.CompilerParams(
            dimension_semantics=("arbitrary", "arbitrary"), vmem_limit_bytes=VMEM_LIMIT),
        name="proj",
    )(mod, ctx.reshape(ns, hb, CTX_LEN, d), x.reshape(ns, hb, s, d),
      w_in_t, wuqt, wukvt, qn, kn, cqn, ckvn, tab_a, tab_b)
    return (q_all.reshape(bsz, N_HEADS, HEAD_PAD, s), k_all.reshape(bsz, N_KSETS, total, HEAD_PAD),
            v_all.reshape(bsz, V_ROWS, total))


def _attn_stream(base, q_ref, k_ref, v_ref, qn_ref, kn_ref, o_ref, o_rows, bufs, m0_ref, slot):
    tq, kc = ATTN_Q_TILE, ATTN_KEY_CHUNK
    n_tiles = q_ref.shape[3] // tq
    n_chunks = k_ref.shape[2] // kc
    assert n_tiles % 2 == 0
    assert n_tiles * tq == q_ref.shape[3] and n_chunks * kc == k_ref.shape[2]
    dv = v_ref.shape[1]
    ones = jnp.ones((ATTN_ONES_ROWS, kc), _BF16)

    def buf_rows(c):
        return pl.ds(pl.multiple_of(base + c * kc, kc), kc)

    def score_chunk(keys_ref, q_tile, buf, c, m):
        s = jnp.dot(keys_ref[0, 0, pl.ds(c * kc, kc), :], q_tile,
                    preferred_element_type=_F32)
        buf[buf_rows(c), :] = s
        cm = jnp.max(s, axis=0, keepdims=True)
        return cm if m is None else jnp.maximum(m, cm)

    def value_chunk(buf, c, m, acc):
        p = jnp.exp2(buf[buf_rows(c), :] - m)
        vext = jnp.concatenate([v_ref[0, :, pl.ds(c * kc, kc)], ones], axis=0)
        pv = jnp.dot(vext, p.astype(_BF16), preferred_element_type=_F32)
        return pv if acc is None else acc + pv

    @pl.when((pl.program_id(0) == 0) & (pl.program_id(1) == 0))
    def _():
        m = None
        for c in range(n_chunks):
            m = score_chunk(k_ref, q_ref[0, 0, :, 0:tq], bufs[0], c, m)
        m0_ref[slot] = m
    yield

    m_cur = m0_ref[slot]
    for t in range(n_tiles):
        m_next = acc = None
        for c in range(n_chunks):
            if t + 1 < n_tiles:
                m_next = score_chunk(k_ref, q_ref[0, 0, :, pl.ds((t + 1) * tq, tq)],
                                     bufs[(t + 1) % 2], c, m_next)
            else:
                m_next = score_chunk(kn_ref, qn_ref[0, 0], bufs[0], c, m_next)
            acc = value_chunk(bufs[t % 2], c, m_cur, acc)
            yield
        o_ref[0, o_rows, pl.ds(t * tq, tq)] = (acc[0:dv] / acc[dv:dv + 1]).astype(_BF16)
        m_cur = m_next
    m0_ref[slot] = m_cur


def _attn_kernel(base_ref, qa_ref, ka_ref, va_ref, qna_ref, kna_ref,
                 qb_ref, kb_ref, vb_ref, qnb_ref, knb_ref, o_ref,
                 sa0_ref, sa1_ref, sb0_ref, sb1_ref, m0_ref):
    base = base_ref[0]
    dv = va_ref.shape[1]
    first = _attn_stream(base, qa_ref, ka_ref, va_ref, qna_ref, kna_ref, o_ref, pl.ds(0, dv),
                         (sa0_ref, sa1_ref), m0_ref, 0)
    second = _attn_stream(base, qb_ref, kb_ref, vb_ref, qnb_ref, knb_ref, o_ref, pl.ds(dv, dv),
                          (sb0_ref, sb1_ref), m0_ref, 1)
    next(first), next(second)
    _trace(_stagger([first, second], lead=ATTN_STREAM_LEAD))


def _attention(q_all, k_all, v_all):
    bsz, _, _, s = q_all.shape
    total = k_all.shape[2]
    group = A_HEADS // A_KV_HEADS
    n_pairs = N_HEADS // 2
    kset = lambda h: jnp.where(h < A_HEADS, 0, h - A_HEADS + 1)
    vblk = lambda h: jnp.where(h < A_HEADS, h // group, h - A_HEADS + A_KV_HEADS)

    def nxt(b, g):
        n = jnp.minimum(b * n_pairs + g + 1, bsz * n_pairs - 1)
        return n // n_pairs, n % n_pairs

    def stream_specs(which):
        head = lambda g: 2 * g + which
        return [
            pl.BlockSpec((1, 1, HEAD_PAD, s), lambda b, g: (b, head(g), 0, 0)),
            pl.BlockSpec((1, 1, total, HEAD_PAD), lambda b, g: (b, kset(head(g)), 0, 0)),
            pl.BlockSpec((1, B_V_DIM, total), lambda b, g: (b, vblk(head(g)), 0)),
            pl.BlockSpec((1, 1, HEAD_PAD, ATTN_Q_TILE),
                         lambda b, g: (nxt(b, g)[0], head(nxt(b, g)[1]), 0, 0)),
            pl.BlockSpec((1, 1, total, HEAD_PAD),
                         lambda b, g: (nxt(b, g)[0], kset(head(nxt(b, g)[1])), 0, 0)),
        ]

    stream_args = (q_all, k_all, v_all, q_all, k_all)
    return pl.pallas_call(
        _attn_kernel,
        grid=(bsz, n_pairs),
        in_specs=[pl.BlockSpec(memory_space=pltpu.SMEM)] + stream_specs(0) + stream_specs(1),
        out_specs=pl.BlockSpec((1, 2 * B_V_DIM, s), lambda b, g: (b, g, 0)),
        out_shape=jax.ShapeDtypeStruct((bsz, N_HEADS * B_V_DIM, s), _BF16),
        scratch_shapes=[pltpu.VMEM((total, ATTN_Q_TILE), _F32)] * 4
        + [pltpu.VMEM((2, 1, ATTN_Q_TILE), _F32)],
        compiler_params=pltpu.CompilerParams(
            dimension_semantics=("arbitrary", "arbitrary"), vmem_limit_bytes=VMEM_LIMIT),
        name="attn",
    )(jnp.zeros((1,), jnp.int32), *stream_args, *stream_args)


def _merge_pieces(mrow, x, ot, win_ref, wpa_ref, wpb_ref, wo_ref, g_ref, b_ref, result):
    shift, scale, gate = mrow[:, 0:D_MODEL], mrow[:, D_MODEL:2 * D_MODEL], mrow[:, 2 * D_MODEL:3 * D_MODEL]
    h = (_layer_norm(x) * (1.0 + scale) + shift).astype(_BF16)
    yield
    na = A_HEADS * A_HEAD_DIM
    ys = []
    for r in range(D_MODEL // MERGE_ROW_CHUNK):
        ra = pl.ds(r * MERGE_ROW_CHUNK, MERGE_ROW_CHUNK)
        ga_rows = pl.ds(GATE_OFF + r * MERGE_ROW_CHUNK, MERGE_ROW_CHUNK)
        gb_rows = pl.ds(GATE_OFF + D_MODEL + r * MERGE_ROW_CHUNK, MERGE_ROW_CHUNK)
        ga = lax.dot_general(win_ref[ga_rows, :], h, _NT, preferred_element_type=_F32)
        gb = lax.dot_general(win_ref[gb_rows, :], h, _NT, preferred_element_type=_F32)
        ya = jnp.dot(wpa_ref[ra, :], ot[0:na], preferred_element_type=_F32)
        yb = jnp.dot(wpb_ref[ra, :], ot[na:], preferred_element_type=_F32)
        ys.append((_sigmoid(ga) * ya + _sigmoid(gb) * yb).astype(_BF16))
        yield
    zt = jnp.dot(wo_ref[...], jnp.concatenate(ys, axis=0), preferred_element_type=_F32)
    r = DEEPNORM_ALPHA * x + gate * zt.T
    result.append(_layer_norm(r) * g_ref[...] + b_ref[...])
    yield


def _ffn_pieces(mrow, x, wup_ref, wdn_ref, g_ref, b_ref, result):
    shift, scale, gate = mrow[:, 0:D_MODEL], mrow[:, D_MODEL:2 * D_MODEL], mrow[:, 2 * D_MODEL:3 * D_MODEL]
    h = (_layer_norm(x) * (1.0 + scale) + shift).astype(_BF16)
    yield
    fs = []
    for c in range(FFN_HIDDEN // FFN_COL_CHUNK):
        a = jnp.dot(h, wup_ref[:, pl.ds(c * FFN_COL_CHUNK, FFN_COL_CHUNK)], preferred_element_type=_F32)
        u = jnp.dot(h, wup_ref[:, pl.ds(FFN_HIDDEN + c * FFN_COL_CHUNK, FFN_COL_CHUNK)],
                    preferred_element_type=_F32)
        fs.append((a * _sigmoid(a) * u).astype(_BF16))
        yield
    dn = jnp.dot(jnp.concatenate(fs, axis=1), wdn_ref[...], preferred_element_type=_F32)
    r = DEEPNORM_ALPHA * x + gate * dn
    result.append(_layer_norm(r) * g_ref[...] + b_ref[...])
    yield


def _post_kernel(mod_ref, x_ref, ot_ref, win_ref, wpa_ref, wpb_ref, wo_ref, g1_ref, b1_ref,
                 wup_ref, wdn_ref, g2_ref, b2_ref, out_ref, x1_ref, *, tiles_per_batch):
    j = pl.program_id(0)
    n_tiles = pl.num_programs(0) - 1
    half_batch = n_tiles // tiles_per_batch

    @pl.when(j == 0)
    def _():
        x1_ref[...] = jnp.zeros_like(x1_ref)

    b_prev = jnp.maximum(j - 1, 0) // tiles_per_batch
    b_cur = jnp.minimum(j, n_tiles - 1) // tiles_per_batch
    ffn_out, merge_out = ([], []), ([], [])

    def stream(half):
        off = half * half_batch
        return _stagger(
            [_ffn_pieces(mod_ref[pl.ds(b_prev + off, 1), 3 * D_MODEL:6 * D_MODEL], x1_ref[half],
                         wup_ref, wdn_ref, g2_ref, b2_ref, ffn_out[half]),
             _merge_pieces(mod_ref[pl.ds(b_cur + off, 1), 0:3 * D_MODEL], x_ref[half, 0],
                           ot_ref[half, 0], win_ref, wpa_ref, wpb_ref, wo_ref, g1_ref, b1_ref,
                           merge_out[half])],
            rates=[POST_FFN_PIECES_PER_MERGE_PIECE, 1])

    _trace(_stagger([stream(0), stream(1)], lead=POST_STREAM_LEAD))
    for half in range(2):
        out_ref[half, 0] = ffn_out[half][0]
        x1_ref[half] = merge_out[half][0]


def _post(mod, x, ot, w_in_t, wpat, wpbt, wot, g1, b1, wup, wdn, g2, b2):
    bsz, s, d = x.shape
    t = TOKEN_TILE
    tpb = s // t
    hb = bsz // 2
    n_tiles = hb * tpb
    const = lambda a: pl.BlockSpec(a.shape, lambda j: (0,) * a.ndim, pipeline_mode=pl.Buffered(1))
    cur = lambda j: jnp.minimum(j, n_tiles - 1)
    prev = lambda j: jnp.maximum(j - 1, 0)
    out = pl.pallas_call(
        functools.partial(_post_kernel, tiles_per_batch=tpb),
        grid=(n_tiles + 1,),
        in_specs=[
            const(mod),
            pl.BlockSpec((2, 1, t, d), lambda j: (0, cur(j) // tpb, cur(j) % tpb, 0)),
            pl.BlockSpec((2, 1, ot.shape[1], t), lambda j: (0, cur(j) // tpb, 0, cur(j) % tpb)),
            const(w_in_t), const(wpat), const(wpbt), const(wot), const(g1), const(b1),
            const(wup), const(wdn), const(g2), const(b2),
        ],
        out_specs=pl.BlockSpec((2, 1, t, d), lambda j: (0, prev(j) // tpb, prev(j) % tpb, 0)),
        out_shape=jax.ShapeDtypeStruct((2, hb, s, d), _F32),
        scratch_shapes=[pltpu.VMEM((2, t, d), _F32)],
        compiler_params=pltpu.CompilerParams(
            dimension_semantics=("arbitrary",), vmem_limit_bytes=VMEM_LIMIT),
        name="post",
    )(mod, x.reshape(2, hb, s, d), ot.reshape(2, hb, ot.shape[1], s),
      w_in_t, wpat, wpbt, wot, g1, b1, wup, wdn, g2, b2)
    return out.reshape(bsz, s, d)


def _rope_tables(seq, half):
    pos = np.arange(seq)
    freqs = ROPE_THETA ** (-np.arange(half, dtype=np.float64) / half)
    tabs = []
    for p in (pos // GRID_W, pos % GRID_W):
        ang = p[None, :].astype(np.float64) * freqs[:, None]
        for fn, ident in ((np.cos, 1.0), (np.sin, 0.0)):
            tabs.append(np.concatenate([np.full((half, CTX_LEN), ident), fn(ang)], axis=1))
    return jnp.asarray(np.stack(tabs), dtype=_F32)


def kernel(x, c, ctx, c_ctx, w_mod, b_mod, w_in, q_norm_a, k_norm_a, cq_norm, ckv_norm,
           w_uq, w_ukv, w_proj_a, w_proj_b, w_out, ln1_g, ln1_b, w_up, w_down, ln2_g, ln2_b):
    bsz, s, d = x.shape
    assert w_mod.shape[0] == DEPTH == 1 and d == D_MODEL and ctx.shape[1] == CTX_LEN
    assert bsz == 8 and s % ATTN_Q_TILE == 0 and s % TOKEN_TILE == 0

    cc = jnp.concatenate([c, c_ctx[None, :], jnp.zeros((bsz - 1, d), _F32)], axis=0)
    mod = _modulation(cc, w_mod[0], b_mod[0][None, :])

    tbf = lambda w: w[0].astype(_BF16).T
    w_in_t = tbf(w_in)
    col = lambda v: v[0][:, None]
    tab_a = _rope_tables(s, A_HEAD_DIM // 4)
    tab_b = _rope_tables(s, B_ROPE_DIM // 4)

    q_all, k_all, v_all = _project(
        mod, ctx, x, w_in_t, tbf(w_uq), tbf(w_ukv),
        col(q_norm_a), col(k_norm_a), col(cq_norm), col(ckv_norm), tab_a, tab_b)
    ot = _attention(q_all, k_all, v_all)
    return _post(mod, x, ot, w_in_t, tbf(w_proj_a), tbf(w_proj_b), tbf(w_out), ln1_g, ln1_b,
                 w_up[0].astype(_BF16), w_down[0].astype(_BF16), ln2_g, ln2_b)
```

```python
import functools
import math

import numpy as np
import jax
import jax.numpy as jnp
from jax import lax
from jax.experimental import pallas as pl
from jax.experimental.pallas import tpu as pltpu

D_MODEL = 1024
GRID_W = 64
CTX_LEN = 256
ROPE_THETA = 10000.0
EPS = 1e-6

A_HEADS = 8
A_KV_HEADS = 2
A_HEAD_DIM = 64
B_HEADS = 8
B_Q_RANK = 384
B_KV_RANK = 256
B_NOPE_DIM = 64
B_ROPE_DIM = 32
B_V_DIM = 64
FFN_HIDDEN = 2816
DEPTH = 1

A_SCALE = A_HEAD_DIM ** -0.5
B_SCALE = (B_NOPE_DIM + B_ROPE_DIM) ** -0.5
DEEPNORM_ALPHA = (2.0 * DEPTH) ** 0.25
LOG2_E = math.log2(math.e)

QA_OFF = 0
KA_OFF = QA_OFF + A_HEADS * A_HEAD_DIM
VA_OFF = KA_OFF + A_KV_HEADS * A_HEAD_DIM
CQ_OFF = VA_OFF + A_KV_HEADS * A_HEAD_DIM
CKV_OFF = CQ_OFF + B_Q_RANK
KR_OFF = CKV_OFF + B_KV_RANK
GATE_OFF = KR_OFF + B_ROPE_DIM
QKV_COLS = GATE_OFF

N_HEADS = A_HEADS + B_HEADS
HEAD_PAD = 128
N_KSETS = 1 + B_HEADS
V_ROWS = A_KV_HEADS * A_HEAD_DIM + B_HEADS * B_V_DIM

MOD_COL_TILE = 1024
PROJ_TILE = 256
PROJ_ROW_CHUNK = 288
PROJ_STREAMS = 4
PROJ_STREAM_LEAD = 5
ATTN_Q_TILE = 512
ATTN_KEY_CHUNK = 256
ATTN_STREAM_LEAD = 9
ATTN_ONES_ROWS = 16
TOKEN_TILE = 256
MERGE_ROW_CHUNK = 256
FFN_COL_CHUNK = 256
POST_FFN_PIECES_PER_MERGE_PIECE = 2
POST_STREAM_LEAD = 10
VMEM_LIMIT = 56 * 1024 * 1024

_NT = (((1,), (1,)), ((), ()))
_F32 = jnp.float32
_BF16 = jnp.bfloat16


def _layer_norm(x):
    mu = jnp.mean(x, axis=-1, keepdims=True)
    xc = x - mu
    var = jnp.mean(xc * xc, axis=-1, keepdims=True)
    return xc * lax.rsqrt(var + EPS)


def _sigmoid(x):
    return jax.nn.sigmoid(x)


def _mod_kernel(c_ref, w_ref, b_ref, o_ref):
    c = c_ref[...]
    a = (c * _sigmoid(c)).astype(_BF16)
    o_ref[...] = jnp.dot(a, w_ref[...].astype(_BF16), preferred_element_type=_F32) + b_ref[...]


def _modulation(cc, w_mod, b_mod):
    rows, d = cc.shape
    n = w_mod.shape[1]
    return pl.pallas_call(
        _mod_kernel,
        grid=(n // MOD_COL_TILE,),
        in_specs=[
            pl.BlockSpec((rows, d), lambda i: (0, 0)),
            pl.BlockSpec((d, MOD_COL_TILE), lambda i: (0, i)),
            pl.BlockSpec((1, MOD_COL_TILE), lambda i: (0, i)),
        ],
        out_specs=pl.BlockSpec((rows, MOD_COL_TILE), lambda i: (0, i)),
        out_shape=jax.ShapeDtypeStruct((rows, n), _F32),
        compiler_params=pltpu.CompilerParams(
            dimension_semantics=("arbitrary",), vmem_limit_bytes=VMEM_LIMIT),
        name="mod",
    )(cc, w_mod, b_mod)


def _rms_rows(x, g):
    ms = jnp.mean(x * x, axis=0, keepdims=True)
    return x * lax.rsqrt(ms + EPS) * g


def _axial_rope_rows(x, half, tab):
    r1, r2 = x[0:half], x[half:2 * half]
    c1, c2 = x[2 * half:3 * half], x[3 * half:4 * half]
    cr, sr, cc, sc = tab[0], tab[1], tab[2], tab[3]
    return jnp.concatenate(
        [r1 * cr - r2 * sr, r1 * sr + r2 * cr, c1 * cc - c2 * sc, c1 * sc + c2 * cc], axis=0)


def _stagger(streams, lead=0, rates=None):
    rates = rates or [1] * len(streams)
    live = dict(enumerate(streams))
    rnd = 0
    while live:
        for i in sorted(live):
            if rnd < i * lead:
                continue
            for _ in range(rates[i]):
                if next(live[i], StopIteration) is StopIteration:
                    del live[i]
                    break
                yield
        rnd += 1


def _trace(pieces):
    for _ in pieces:
        pass


def _proj_pieces(mrow, xin, w1_ref, wuq_ref, wukv_ref, qn_ref, kn_ref, cqn_ref, ckvn_ref, ta, tb,
                 q_out, k_out, v_out):
    shift, scale = mrow[:, 0:D_MODEL], mrow[:, D_MODEL:2 * D_MODEL]
    h = (_layer_norm(xin) * (1.0 + scale) + shift).astype(_BF16)
    yield
    parts = []
    for r in range(QKV_COLS // PROJ_ROW_CHUNK):
        rows = pl.ds(r * PROJ_ROW_CHUNK, PROJ_ROW_CHUNK)
        parts.append(lax.dot_general(w1_ref[rows, :], h, _NT, preferred_element_type=_F32))
        yield
    pt = jnp.concatenate(parts, axis=0)
    t = pt.shape[1]
    z64 = jnp.zeros((A_HEAD_DIM, t), _F32)
    z32 = jnp.zeros((HEAD_PAD - B_NOPE_DIM - B_ROPE_DIM, t), _F32)

    qn = qn_ref[...] * (A_SCALE * LOG2_E)
    group = A_HEADS // A_KV_HEADS
    for hd in range(A_HEADS):
        xh = pt[QA_OFF + hd * A_HEAD_DIM:QA_OFF + (hd + 1) * A_HEAD_DIM]
        xh = _axial_rope_rows(_rms_rows(xh, qn), A_HEAD_DIM // 4, ta)
        full = jnp.concatenate([xh, z64] if hd // group == 0 else [z64, xh], axis=0)
        q_out[hd] = full.astype(_BF16)
        if hd % 4 == 3:
            yield

    kn = kn_ref[...]
    ka = [
        _axial_rope_rows(
            _rms_rows(pt[KA_OFF + g * A_HEAD_DIM:KA_OFF + (g + 1) * A_HEAD_DIM], kn),
            A_HEAD_DIM // 4, ta)
        for g in range(A_KV_HEADS)
    ]
    k_out[0] = jnp.concatenate(ka, axis=0).T.astype(_BF16)
    v_out[0:A_KV_HEADS * A_HEAD_DIM, :] = pt[VA_OFF:CQ_OFF].astype(_BF16)
    yield

    cq = _rms_rows(pt[CQ_OFF:CKV_OFF], cqn_ref[...]).astype(_BF16)
    qb = jnp.dot(wuq_ref[...], cq, preferred_element_type=_F32) * (B_SCALE * LOG2_E)
    yield
    qdim = B_NOPE_DIM + B_ROPE_DIM
    for hd in range(B_HEADS):
        blk = qb[hd * qdim:(hd + 1) * qdim]
        rp = _axial_rope_rows(blk[B_NOPE_DIM:qdim], B_ROPE_DIM // 4, tb)
        full = jnp.concatenate([blk[0:B_NOPE_DIM], rp, z32], axis=0)
        q_out[A_HEADS + hd] = full.astype(_BF16)
        if hd % 4 == 3:
            yield

    ckv = _rms_rows(pt[CKV_OFF:KR_OFF], ckvn_ref[...]).astype(_BF16)
    kv = jnp.dot(wukv_ref[...], ckv, preferred_element_type=_F32)
    kr = _axial_rope_rows(pt[KR_OFF:GATE_OFF], B_ROPE_DIM // 4, tb)
    yield
    kvdim = B_NOPE_DIM + B_V_DIM
    v_base = A_KV_HEADS * A_HEAD_DIM
    for hd in range(B_HEADS):
        kn_h = kv[hd * kvdim:hd * kvdim + B_NOPE_DIM]
        v_h = kv[hd * kvdim + B_NOPE_DIM:(hd + 1) * kvdim]
        v_out[v_base + hd * B_V_DIM:v_base + (hd + 1) * B_V_DIM, :] = v_h.astype(_BF16)
        k_out[1 + hd] = jnp.concatenate([kn_h, kr, z32], axis=0).T.astype(_BF16)
        if hd % 4 == 3:
            yield


def _proj_kernel(mod_ref, ctx_ref, x_ref, w1_ref, wuq_ref, wukv_ref,
                 qn_ref, kn_ref, cqn_ref, ckvn_ref, ta_ref, tb_ref,
                 q_out, k_out, v_out):
    b = pl.program_id(0)
    j = pl.program_id(1)
    batch_stride = pl.num_programs(0)
    is_ctx = j == 0
    ctx_row = mod_ref.shape[0] // 2
    ta = ta_ref[...]
    tb = tb_ref[...]

    def stream(i):
        xin = jnp.where(is_ctx, ctx_ref[i, 0], x_ref[i, 0])
        mrow = jnp.where(is_ctx, mod_ref[ctx_row:ctx_row + 1, 0:2 * D_MODEL],
                         mod_ref[pl.ds(b + i * batch_stride, 1), 0:2 * D_MODEL])
        return _proj_pieces(mrow, xin, w1_ref, wuq_ref, wukv_ref, qn_ref, kn_ref, cqn_ref, ckvn_ref,
                            ta, tb, q_out.at[i, 0], k_out.at[i, 0], v_out.at[i, 0])

    _trace(_stagger([stream(i) for i in range(PROJ_STREAMS)], lead=PROJ_STREAM_LEAD))


def _project(mod, ctx, x, w_in_t, wuqt, wukvt, qn, kn, cqn, ckvn, tab_a, tab_b):
    bsz, s, d = x.shape
    t = PROJ_TILE
    n_steps = (CTX_LEN + s) // t
    total = CTX_LEN + s
    ns = PROJ_STREAMS
    hb = bsz // ns
    lat = lambda j: jnp.maximum(j - 1, 0)
    full2 = lambda shape: pl.BlockSpec(shape, lambda b, j: (0, 0))
    q_all, k_all, v_all = pl.pallas_call(
        _proj_kernel,
        grid=(hb, n_steps),
        in_specs=[
            full2(mod.shape),
            pl.BlockSpec((ns, 1, CTX_LEN, d), lambda b, j: (0, b, 0, 0)),
            pl.BlockSpec((ns, 1, t, d), lambda b, j: (0, b, lat(j), 0)),
            full2((QKV_COLS, d)),
            full2(wuqt.shape), full2(wukvt.shape),
            full2(qn.shape), full2(kn.shape), full2(cqn.shape), full2(ckvn.shape),
            pl.BlockSpec((4, tab_a.shape[1], t), lambda b, j: (0, 0, j)),
            pl.BlockSpec((4, tab_b.shape[1], t), lambda b, j: (0, 0, j)),
        ],
        out_specs=[
            pl.BlockSpec((ns, 1, N_HEADS, HEAD_PAD, t), lambda b, j: (0, b, 0, 0, lat(j))),
            pl.BlockSpec((ns, 1, N_KSETS, t, HEAD_PAD), lambda b, j: (0, b, 0, j, 0)),
            pl.BlockSpec((ns, 1, V_ROWS, t), lambda b, j: (0, b, 0, j)),
        ],
        out_shape=[
            jax.ShapeDtypeStruct((ns, hb, N_HEADS, HEAD_PAD, s), _BF16),
            jax.ShapeDtypeStruct((ns, hb, N_KSETS, total, HEAD_PAD), _BF16),
            jax.ShapeDtypeStruct((ns, hb, V_ROWS, total), _BF16),
        ],
        compiler_params=pltpu.CompilerParams(
            dimension_semantics=("arbitrary", "arbitrary"), vmem_limit_bytes=VMEM_LIMIT),
        name="proj",
    )(mod, ctx.reshape(ns, hb, CTX_LEN, d), x.reshape(ns, hb, s, d),
      w_in_t, wuqt, wukvt, qn, kn, cqn, ckvn, tab_a, tab_b)
    return (q_all.reshape(bsz, N_HEADS, HEAD_PAD, s), k_all.reshape(bsz, N_KSETS, total, HEAD_PAD),
            v_all.reshape(bsz, V_ROWS, total))


def _attn_stream(base, q_ref, k_ref, v_ref, qn_ref, kn_ref, o_ref, o_rows, bufs, m0_ref, slot):
    tq, kc = ATTN_Q_TILE, ATTN_KEY_CHUNK
    n_tiles = q_ref.shape[3] // tq
    n_chunks = k_ref.shape[2] // kc
    assert n_tiles % 2 == 0
    assert n_tiles * tq == q_ref.shape[3] and n_chunks * kc == k_ref.shape[2]
    dv = v_ref.shape[1]
    ones = jnp.ones((ATTN_ONES_ROWS, kc), _BF16)

    def buf_rows(c):
        return pl.ds(pl.multiple_of(base + c * kc, kc), kc)

    def score_chunk(keys_ref, q_tile, buf, c, m):
        s = jnp.dot(keys_ref[0, 0, pl.ds(c * kc, kc), :], q_tile,
                    preferred_element_type=_F32)
        buf[buf_rows(c), :] = s
        cm = jnp.max(s, axis=0, keepdims=True)
        return cm if m is None else jnp.maximum(m, cm)

    def value_chunk(buf, c, m, acc):
        p = jnp.exp2(buf[buf_rows(c), :] - m)
        vext = jnp.concatenate([v_ref[0, :, pl.ds(c * kc, kc)], ones], axis=0)
        pv = jnp.dot(vext, p.astype(_BF16), preferred_element_type=_F32)
        return pv if acc is None else acc + pv

    @pl.when((pl.program_id(0) == 0) & (pl.program_id(1) == 0))
    def _():
        m = None
        for c in range(n_chunks):
            m = score_chunk(k_ref, q_ref[0, 0, :, 0:tq], bufs[0], c, m)
        m0_ref[slot] = m
    yield

    m_cur = m0_ref[slot]
    for t in range(n_tiles):
        m_next = acc = None
        for c in range(n_chunks):
            if t + 1 < n_tiles:
                m_next = score_chunk(k_ref, q_ref[0, 0, :, pl.ds((t + 1) * tq, tq)],
                                     bufs[(t + 1) % 2], c, m_next)
            else:
                m_next = score_chunk(kn_ref, qn_ref[0, 0], bufs[0], c, m_next)
            acc = value_chunk(bufs[t % 2], c, m_cur, acc)
            yield
        o_ref[0, o_rows, pl.ds(t * tq, tq)] = (acc[0:dv] / acc[dv:dv + 1]).astype(_BF16)
        m_cur = m_next
    m0_ref[slot] = m_cur


def _attn_kernel(base_ref, qa_ref, ka_ref, va_ref, qna_ref, kna_ref,
                 qb_ref, kb_ref, vb_ref, qnb_ref, knb_ref, o_ref,
                 sa0_ref, sa1_ref, sb0_ref, sb1_ref, m0_ref):
    base = base_ref[0]
    dv = va_ref.shape[1]
    first = _attn_stream(base, qa_ref, ka_ref, va_ref, qna_ref, kna_ref, o_ref, pl.ds(0, dv),
                         (sa0_ref, sa1_ref), m0_ref, 0)
    second = _attn_stream(base, qb_ref, kb_ref, vb_ref, qnb_ref, knb_ref, o_ref, pl.ds(dv, dv),
                          (sb0_ref, sb1_ref), m0_ref, 1)
    next(first), next(second)
    _trace(_stagger([first, second], lead=ATTN_STREAM_LEAD))


def _attention(q_all, k_all, v_all):
    bsz, _, _, s = q_all.shape
    total = k_all.shape[2]
    group = A_HEADS // A_KV_HEADS
    n_pairs = N_HEADS // 2
    kset = lambda h: jnp.where(h < A_HEADS, 0, h - A_HEADS + 1)
    vblk = lambda h: jnp.where(h < A_HEADS, h // group, h - A_HEADS + A_KV_HEADS)

    def nxt(b, g):
        n = jnp.minimum(b * n_pairs + g + 1, bsz * n_pairs - 1)
        return n // n_pairs, n % n_pairs

    def stream_specs(which):
        head = lambda g: 2 * g + which
        return [
            pl.BlockSpec((1, 1, HEAD_PAD, s), lambda b, g: (b, head(g), 0, 0)),
            pl.BlockSpec((1, 1, total, HEAD_PAD), lambda b, g: (b, kset(head(g)), 0, 0)),
            pl.BlockSpec((1, B_V_DIM, total), lambda b, g: (b, vblk(head(g)), 0)),
            pl.BlockSpec((1, 1, HEAD_PAD, ATTN_Q_TILE),
                         lambda b, g: (nxt(b, g)[0], head(nxt(b, g)[1]), 0, 0)),
            pl.BlockSpec((1, 1, total, HEAD_PAD),
                         lambda b, g: (nxt(b, g)[0], kset(head(nxt(b, g)[1])), 0, 0)),
        ]

    stream_args = (q_all, k_all, v_all, q_all, k_all)
    return pl.pallas_call(
        _attn_kernel,
        grid=(bsz, n_pairs),
        in_specs=[pl.BlockSpec(memory_space=pltpu.SMEM)] + stream_specs(0) + stream_specs(1),
        out_specs=pl.BlockSpec((1, 2 * B_V_DIM, s), lambda b, g: (b, g, 0)),
        out_shape=jax.ShapeDtypeStruct((bsz, N_HEADS * B_V_DIM, s), _BF16),
        scratch_shapes=[pltpu.VMEM((total, ATTN_Q_TILE), _F32)] * 4
        + [pltpu.VMEM((2, 1, ATTN_Q_TILE), _F32)],
        compiler_params=pltpu.CompilerParams(
            dimension_semantics=("arbitrary", "arbitrary"), vmem_limit_bytes=VMEM_LIMIT),
        name="attn",
    )(jnp.zeros((1,), jnp.int32), *stream_args, *stream_args)


def _merge_pieces(mrow, x, ot, win_ref, wpa_ref, wpb_ref, wo_ref, g_ref, b_ref, result):
    shift, scale, gate = mrow[:, 0:D_MODEL], mrow[:, D_MODEL:2 * D_MODEL], mrow[:, 2 * D_MODEL:3 * D_MODEL]
    h = (_layer_norm(x) * (1.0 + scale) + shift).astype(_BF16)
    yield
    na = A_HEADS * A_HEAD_DIM
    ys = []
    for r in range(D_MODEL // MERGE_ROW_CHUNK):
        ra = pl.ds(r * MERGE_ROW_CHUNK, MERGE_ROW_CHUNK)
        ga_rows = pl.ds(GATE_OFF + r * MERGE_ROW_CHUNK, MERGE_ROW_CHUNK)
        gb_rows = pl.ds(GATE_OFF + D_MODEL + r * MERGE_ROW_CHUNK, MERGE_ROW_CHUNK)
        ga = lax.dot_general(win_ref[ga_rows, :], h, _NT, preferred_element_type=_F32)
        gb = lax.dot_general(win_ref[gb_rows, :], h, _NT, preferred_element_type=_F32)
        ya = jnp.dot(wpa_ref[ra, :], ot[0:na], preferred_element_type=_F32)
        yb = jnp.dot(wpb_ref[ra, :], ot[na:], preferred_element_type=_F32)
        ys.append((_sigmoid(ga) * ya + _sigmoid(gb) * yb).astype(_BF16))
        yield
    zt = jnp.dot(wo_ref[...], jnp.concatenate(ys, axis=0), preferred_element_type=_F32)
    r = DEEPNORM_ALPHA * x + gate * zt.T
    result.append(_layer_norm(r) * g_ref[...] + b_ref[...])
    yield


def _ffn_pieces(mrow, x, wup_ref, wdn_ref, g_ref, b_ref, result):
    shift, scale, gate = mrow[:, 0:D_MODEL], mrow[:, D_MODEL:2 * D_MODEL], mrow[:, 2 * D_MODEL:3 * D_MODEL]
    h = (_layer_norm(x) * (1.0 + scale) + shift).astype(_BF16)
    yield
    fs = []
    for c in range(FFN_HIDDEN // FFN_COL_CHUNK):
        a = jnp.dot(h, wup_ref[:, pl.ds(c * FFN_COL_CHUNK, FFN_COL_CHUNK)], preferred_element_type=_F32)
        u = jnp.dot(h, wup_ref[:, pl.ds(FFN_HIDDEN + c * FFN_COL_CHUNK, FFN_COL_CHUNK)],
                    preferred_element_type=_F32)
        fs.append((a * _sigmoid(a) * u).astype(_BF16))
        yield
    dn = jnp.dot(jnp.concatenate(fs, axis=1), wdn_ref[...], preferred_element_type=_F32)
    r = DEEPNORM_ALPHA * x + gate * dn
    result.append(_layer_norm(r) * g_ref[...] + b_ref[...])
    yield


def _post_kernel(mod_ref, x_ref, ot_ref, win_ref, wpa_ref, wpb_ref, wo_ref, g1_ref, b1_ref,
                 wup_ref, wdn_ref, g2_ref, b2_ref, out_ref, x1_ref, *, tiles_per_batch):
    j = pl.program_id(0)
    n_tiles = pl.num_programs(0) - 1
    half_batch = n_tiles // tiles_per_batch

    @pl.when(j == 0)
    def _():
        x1_ref[...] = jnp.zeros_like(x1_ref)

    b_prev = jnp.maximum(j - 1, 0) // tiles_per_batch
    b_cur = jnp.minimum(j, n_tiles - 1) // tiles_per_batch
    ffn_out, merge_out = ([], []), ([], [])

    def stream(half):
        off = half * half_batch
        return _stagger(
            [_ffn_pieces(mod_ref[pl.ds(b_prev + off, 1), 3 * D_MODEL:6 * D_MODEL], x1_ref[half],
                         wup_ref, wdn_ref, g2_ref, b2_ref, ffn_out[half]),
             _merge_pieces(mod_ref[pl.ds(b_cur + off, 1), 0:3 * D_MODEL], x_ref[half, 0],
                           ot_ref[half, 0], win_ref, wpa_ref, wpb_ref, wo_ref, g1_ref, b1_ref,
                           merge_out[half])],
            rates=[POST_FFN_PIECES_PER_MERGE_PIECE, 1])

    _trace(_stagger([stream(0), stream(1)], lead=POST_STREAM_LEAD))
    for half in range(2):
        out_ref[half, 0] = ffn_out[half][0]
        x1_ref[half] = merge_out[half][0]


def _post(mod, x, ot, w_in_t, wpat, wpbt, wot, g1, b1, wup, wdn, g2, b2):
    bsz, s, d = x.shape
    t = TOKEN_TILE
    tpb = s // t
    hb = bsz // 2
    n_tiles = hb * tpb
    const = lambda a: pl.BlockSpec(a.shape, lambda j: (0,) * a.ndim, pipeline_mode=pl.Buffered(1))
    cur = lambda j: jnp.minimum(j, n_tiles - 1)
    prev = lambda j: jnp.maximum(j - 1, 0)
    out = pl.pallas_call(
        functools.partial(_post_kernel, tiles_per_batch=tpb),
        grid=(n_tiles + 1,),
        in_specs=[
            const(mod),
            pl.BlockSpec((2, 1, t, d), lambda j: (0, cur(j) // tpb, cur(j) % tpb, 0)),
            pl.BlockSpec((2, 1, ot.shape[1], t), lambda j: (0, cur(j) // tpb, 0, cur(j) % tpb)),
            const(w_in_t), const(wpat), const(wpbt), const(wot), const(g1), const(b1),
            const(wup), const(wdn), const(g2), const(b2),
        ],
        out_specs=pl.BlockSpec((2, 1, t, d), lambda j: (0, prev(j) // tpb, prev(j) % tpb, 0)),
        out_shape=jax.ShapeDtypeStruct((2, hb, s, d), _F32),
        scratch_shapes=[pltpu.VMEM((2, t, d), _F32)],
        compiler_params=pltpu.CompilerParams(
            dimension_semantics=("arbitrary",), vmem_limit_bytes=VMEM_LIMIT),
        name="post",
    )(mod, x.reshape(2, hb, s, d), ot.reshape(2, hb, ot.shape[1], s),
      w_in_t, wpat, wpbt, wot, g1, b1, wup, wdn, g2, b2)
    return out.reshape(bsz, s, d)


def _rope_tables(seq, half):
    pos = np.arange(seq)
    freqs = ROPE_THETA ** (-np.arange(half, dtype=np.float64) / half)
    tabs = []
    for p in (pos // GRID_W, pos % GRID_W):
        ang = p[None, :].astype(np.float64) * freqs[:, None]
        for fn, ident in ((np.cos, 1.0), (np.sin, 0.0)):
            tabs.append(np.concatenate([np.full((half, CTX_LEN), ident), fn(ang)], axis=1))
    return jnp.asarray(np.stack(tabs), dtype=_F32)


def kernel(x, c, ctx, c_ctx, w_mod, b_mod, w_in, q_norm_a, k_norm_a, cq_norm, ckv_norm,
           w_uq, w_ukv, w_proj_a, w_proj_b, w_out, ln1_g, ln1_b, w_up, w_down, ln2_g, ln2_b):
    bsz, s, d = x.shape
    assert w_mod.shape[0] == DEPTH == 1 and d == D_MODEL and ctx.shape[1] == CTX_LEN
    assert bsz == 8 and s % ATTN_Q_TILE == 0 and s % TOKEN_TILE == 0

    cc = jnp.concatenate([c, c_ctx[None, :], jnp.zeros((bsz - 1, d), _F32)], axis=0)
    mod = _modulation(cc, w_mod[0], b_mod[0][None, :])

    tbf = lambda w: w[0].astype(_BF16).T
    w_in_t = tbf(w_in)
    col = lambda v: v[0][:, None]
    tab_a = _rope_tables(s, A_HEAD_DIM // 4)
    tab_b = _rope_tables(s, B_ROPE_DIM // 4)

    q_all, k_all, v_all = _project(
        mod, ctx, x, w_in_t, tbf(w_uq), tbf(w_ukv),
        col(q_norm_a), col(k_norm_a), col(cq_norm), col(ckv_norm), tab_a, tab_b)
    ot = _attention(q_all, k_all, v_all)
    return _post(mod, x, ot, w_in_t, tbf(w_proj_a), tbf(w_proj_b), tbf(w_out), ln1_g, ln1_b,
                 w_up[0].astype(_BF16), w_down[0].astype(_BF16), ln2_g, ln2_b)
```

```python
import functools
import math

import numpy as np
import jax
import jax.numpy as jnp
from jax import lax
from jax.experimental import pallas as pl
from jax.experimental.pallas import tpu as pltpu

D_MODEL = 1024
GRID_W = 64
CTX_LEN = 256
ROPE_THETA = 10000.0
EPS = 1e-6

A_HEADS = 8
A_KV_HEADS = 2
A_HEAD_DIM = 64
B_HEADS = 8
B_Q_RANK = 384
B_KV_RANK = 256
B_NOPE_DIM = 64
B_ROPE_DIM = 32
B_V_DIM = 64
FFN_HIDDEN = 2816
DEPTH = 1

A_SCALE = A_HEAD_DIM ** -0.5
B_SCALE = (B_NOPE_DIM + B_ROPE_DIM) ** -0.5
DEEPNORM_ALPHA = (2.0 * DEPTH) ** 0.25
LOG2_E = math.log2(math.e)

QA_OFF = 0
KA_OFF = QA_OFF + A_HEADS * A_HEAD_DIM
VA_OFF = KA_OFF + A_KV_HEADS * A_HEAD_DIM
CQ_OFF = VA_OFF + A_KV_HEADS * A_HEAD_DIM
CKV_OFF = CQ_OFF + B_Q_RANK
KR_OFF = CKV_OFF + B_KV_RANK
GATE_OFF = KR_OFF + B_ROPE_DIM
QKV_COLS = GATE_OFF

N_HEADS = A_HEADS + B_HEADS
HEAD_PAD = 128
N_KSETS = 1 + B_HEADS
V_ROWS = A_KV_HEADS * A_HEAD_DIM + B_HEADS * B_V_DIM

MOD_COL_TILE = 1024
PROJ_TILE = 256
PROJ_ROW_CHUNK = 288
PROJ_STREAMS = 4
PROJ_STREAM_LEAD = 3
ATTN_Q_TILE = 512
ATTN_KEY_CHUNK = 256
ATTN_STREAM_LEAD = 9
ATTN_ONES_ROWS = 16
TOKEN_TILE = 256
MERGE_ROW_CHUNK = 256
FFN_COL_CHUNK = 256
POST_FFN_PIECES_PER_MERGE_PIECE = 2
POST_STREAM_LEAD = 15
VMEM_LIMIT = 56 * 1024 * 1024

_NT = (((1,), (1,)), ((), ()))
_F32 = jnp.float32
_BF16 = jnp.bfloat16


def _layer_norm(x):
    mu = jnp.mean(x, axis=-1, keepdims=True)
    xc = x - mu
    var = jnp.mean(xc * xc, axis=-1, keepdims=True)
    return xc * lax.rsqrt(var + EPS)


def _sigmoid(x):
    return jax.nn.sigmoid(x)


def _mod_kernel(c_ref, w_ref, b_ref, o_ref):
    c = c_ref[...]
    a = (c * _sigmoid(c)).astype(_BF16)
    o_ref[...] = jnp.dot(a, w_ref[...].astype(_BF16), preferred_element_type=_F32) + b_ref[...]


def _modulation(cc, w_mod, b_mod):
    rows, d = cc.shape
    n = w_mod.shape[1]
    return pl.pallas_call(
        _mod_kernel,
        grid=(n // MOD_COL_TILE,),
        in_specs=[
            pl.BlockSpec((rows, d), lambda i: (0, 0)),
            pl.BlockSpec((d, MOD_COL_TILE), lambda i: (0, i)),
            pl.BlockSpec((1, MOD_COL_TILE), lambda i: (0, i)),
        ],
        out_specs=pl.BlockSpec((rows, MOD_COL_TILE), lambda i: (0, i)),
        out_shape=jax.ShapeDtypeStruct((rows, n), _F32),
        compiler_params=pltpu.CompilerParams(
            dimension_semantics=("arbitrary",), vmem_limit_bytes=VMEM_LIMIT),
        name="mod",
    )(cc, w_mod, b_mod)


def _rms_rows(x, g):
    ms = jnp.mean(x * x, axis=0, keepdims=True)
    return x * lax.rsqrt(ms + EPS) * g


def _axial_rope_rows(x, half, tab):
    r1, r2 = x[0:half], x[half:2 * half]
    c1, c2 = x[2 * half:3 * half], x[3 * half:4 * half]
    cr, sr, cc, sc = tab[0], tab[1], tab[2], tab[3]
    return jnp.concatenate(
        [r1 * cr - r2 * sr, r1 * sr + r2 * cr, c1 * cc - c2 * sc, c1 * sc + c2 * cc], axis=0)


def _stagger(streams, lead=0, rates=None):
    rates = rates or [1] * len(streams)
    live = dict(enumerate(streams))
    rnd = 0
    while live:
        for i in sorted(live):
            if rnd < i * lead:
                continue
            for _ in range(rates[i]):
                if next(live[i], StopIteration) is StopIteration:
                    del live[i]
                    break
                yield
        rnd += 1


def _trace(pieces):
    for _ in pieces:
        pass


def _proj_pieces(mrow, xin, w1_ref, wuq_ref, wukv_ref, qn_ref, kn_ref, cqn_ref, ckvn_ref, ta, tb,
                 q_out, k_out, v_out):
    shift, scale = mrow[:, 0:D_MODEL], mrow[:, D_MODEL:2 * D_MODEL]
    h = (_layer_norm(xin) * (1.0 + scale) + shift).astype(_BF16)
    yield
    parts = []
    for r in range(QKV_COLS // PROJ_ROW_CHUNK):
        rows = pl.ds(r * PROJ_ROW_CHUNK, PROJ_ROW_CHUNK)
        parts.append(lax.dot_general(w1_ref[rows, :], h, _NT, preferred_element_type=_F32))
        yield
    pt = jnp.concatenate(parts, axis=0)
    t = pt.shape[1]
    z64 = jnp.zeros((A_HEAD_DIM, t), _F32)
    z32 = jnp.zeros((HEAD_PAD - B_NOPE_DIM - B_ROPE_DIM, t), _F32)

    qn = qn_ref[...] * (A_SCALE * LOG2_E)
    group = A_HEADS // A_KV_HEADS
    for hd in range(A_HEADS):
        xh = pt[QA_OFF + hd * A_HEAD_DIM:QA_OFF + (hd + 1) * A_HEAD_DIM]
        xh = _axial_rope_rows(_rms_rows(xh, qn), A_HEAD_DIM // 4, ta)
        full = jnp.concatenate([xh, z64] if hd // group == 0 else [z64, xh], axis=0)
        q_out[hd] = full.astype(_BF16)
        if hd % 4 == 3:
            yield

    kn = kn_ref[...]
    ka = [
        _axial_rope_rows(
            _rms_rows(pt[KA_OFF + g * A_HEAD_DIM:KA_OFF + (g + 1) * A_HEAD_DIM], kn),
            A_HEAD_DIM // 4, ta)
        for g in range(A_KV_HEADS)
    ]
    k_out[0] = jnp.concatenate(ka, axis=0).T.astype(_BF16)
    v_out[0:A_KV_HEADS * A_HEAD_DIM, :] = pt[VA_OFF:CQ_OFF].astype(_BF16)
    yield

    cq = _rms_rows(pt[CQ_OFF:CKV_OFF], cqn_ref[...]).astype(_BF16)
    qb = jnp.dot(wuq_ref[...], cq, preferred_element_type=_F32) * (B_SCALE * LOG2_E)
    yield
    qdim = B_NOPE_DIM + B_ROPE_DIM
    for hd in range(B_HEADS):
        blk = qb[hd * qdim:(hd + 1) * qdim]
        rp = _axial_rope_rows(blk[B_NOPE_DIM:qdim], B_ROPE_DIM // 4, tb)
        full = jnp.concatenate([blk[0:B_NOPE_DIM], rp, z32], axis=0)
        q_out[A_HEADS + hd] = full.astype(_BF16)
        if hd % 4 == 3:
            yield

    ckv = _rms_rows(pt[CKV_OFF:KR_OFF], ckvn_ref[...]).astype(_BF16)
    kv = jnp.dot(wukv_ref[...], ckv, preferred_element_type=_F32)
    kr = _axial_rope_rows(pt[KR_OFF:GATE_OFF], B_ROPE_DIM // 4, tb)
    yield
    kvdim = B_NOPE_DIM + B_V_DIM
    v_base = A_KV_HEADS * A_HEAD_DIM
    for hd in range(B_HEADS):
        kn_h = kv[hd * kvdim:hd * kvdim + B_NOPE_DIM]
        v_h = kv[hd * kvdim + B_NOPE_DIM:(hd + 1) * kvdim]
        v_out[v_base + hd * B_V_DIM:v_base + (hd + 1) * B_V_DIM, :] = v_h.astype(_BF16)
        k_out[1 + hd] = jnp.concatenate([kn_h, kr, z32], axis=0).T.astype(_BF16)
        if hd % 4 == 3:
            yield


def _proj_kernel(mod_ref, ctx_ref, x_ref, w1_ref, wuq_ref, wukv_ref,
                 qn_ref, kn_ref, cqn_ref, ckvn_ref, ta_ref, tb_ref,
                 q_out, k_out, v_out):
    b = pl.program_id(0)
    j = pl.program_id(1)
    batch_stride = pl.num_programs(0)
    is_ctx = j == 0
    ctx_row = mod_ref.shape[0] // 2
    ta = ta_ref[...]
    tb = tb_ref[...]

    def stream(i):
        xin = jnp.where(is_ctx, ctx_ref[i, 0], x_ref[i, 0])
        mrow = jnp.where(is_ctx, mod_ref[ctx_row:ctx_row + 1, 0:2 * D_MODEL],
                         mod_ref[pl.ds(b + i * batch_stride, 1), 0:2 * D_MODEL])
        return _proj_pieces(mrow, xin, w1_ref, wuq_ref, wukv_ref, qn_ref, kn_ref, cqn_ref, ckvn_ref,
                            ta, tb, q_out.at[i, 0], k_out.at[i, 0], v_out.at[i, 0])

    _trace(_stagger([stream(i) for i in range(PROJ_STREAMS)], lead=PROJ_STREAM_LEAD))


def _project(mod, ctx, x, w_in_t, wuqt, wukvt, qn, kn, cqn, ckvn, tab_a, tab_b):
    bsz, s, d = x.shape
    t = PROJ_TILE
    n_steps = (CTX_LEN + s) // t
    total = CTX_LEN + s
    ns = PROJ_STREAMS
    hb = bsz // ns
    lat = lambda j: jnp.maximum(j - 1, 0)
    full2 = lambda shape: pl.BlockSpec(shape, lambda b, j: (0, 0))
    q_all, k_all, v_all = pl.pallas_call(
        _proj_kernel,
        grid=(hb, n_steps),
        in_specs=[
            full2(mod.shape),
            pl.BlockSpec((ns, 1, CTX_LEN, d), lambda b, j: (0, b, 0, 0)),
            pl.BlockSpec((ns, 1, t, d), lambda b, j: (0, b, lat(j), 0)),
            full2((QKV_COLS, d)),
            full2(wuqt.shape), full2(wukvt.shape),
            full2(qn.shape), full2(kn.shape), full2(cqn.shape), full2(ckvn.shape),
            pl.BlockSpec((4, tab_a.shape[1], t), lambda b, j: (0, 0, j)),
            pl.BlockSpec((4, tab_b.shape[1], t), lambda b, j: (0, 0, j)),
        ],
        out_specs=[
            pl.BlockSpec((ns, 1, N_HEADS, HEAD_PAD, t), lambda b, j: (0, b, 0, 0, lat(j))),
            pl.BlockSpec((ns, 1, N_KSETS, t, HEAD_PAD), lambda b, j: (0, b, 0, j, 0)),
            pl.BlockSpec((ns, 1, V_ROWS, t), lambda b, j: (0, b, 0, j)),
        ],
        out_shape=[
            jax.ShapeDtypeStruct((ns, hb, N_HEADS, HEAD_PAD, s), _BF16),
            jax.ShapeDtypeStruct((ns, hb, N_KSETS, total, HEAD_PAD), _BF16),
            jax.ShapeDtypeStruct((ns, hb, V_ROWS, total), _BF16),
        ],
        compiler_params=pltpu.CompilerParams(
            dimension_semantics=("arbitrary", "arbitrary"), vmem_limit_bytes=VMEM_LIMIT),
        name="proj",
    )(mod, ctx.reshape(ns, hb, CTX_LEN, d), x.reshape(ns, hb, s, d),
      w_in_t, wuqt, wukvt, qn, kn, cqn, ckvn, tab_a, tab_b)
    return (q_all.reshape(bsz, N_HEADS, HEAD_PAD, s), k_all.reshape(bsz, N_KSETS, total, HEAD_PAD),
            v_all.reshape(bsz, V_ROWS, total))


def _attn_stream(base, q_ref, k_ref, v_ref, qn_ref, kn_ref, o_ref, o_rows, bufs, m0_ref, slot):
    tq, kc = ATTN_Q_TILE, ATTN_KEY_CHUNK
    n_tiles = q_ref.shape[3] // tq
    n_chunks = k_ref.shape[2] // kc
    assert n_tiles % 2 == 0
    assert n_tiles * tq == q_ref.shape[3] and n_chunks * kc == k_ref.shape[2]
    dv = v_ref.shape[1]
    ones = jnp.ones((ATTN_ONES_ROWS, kc), _BF16)

    def buf_rows(c):
        return pl.ds(pl.multiple_of(base + c * kc, kc), kc)

    def score_chunk(keys_ref, q_tile, buf, c, m):
        s = jnp.dot(keys_ref[0, 0, pl.ds(c * kc, kc), :], q_tile,
                    preferred_element_type=_F32)
        buf[buf_rows(c), :] = s
        cm = jnp.max(s, axis=0, keepdims=True)
        return cm if m is None else jnp.maximum(m, cm)

    def value_chunk(buf, c, m, acc):
        p = jnp.exp2(buf[buf_rows(c), :] - m)
        vext = jnp.concatenate([v_ref[0, :, pl.ds(c * kc, kc)], ones], axis=0)
        pv = jnp.dot(vext, p.astype(_BF16), preferred_element_type=_F32)
        return pv if acc is None else acc + pv

    @pl.when((pl.program_id(0) == 0) & (pl.program_id(1) == 0))
    def _():
        m = None
        for c in range(n_chunks):
            m = score_chunk(k_ref, q_ref[0, 0, :, 0:tq], bufs[0], c, m)
        m0_ref[slot] = m
    yield

    m_cur = m0_ref[slot]
    for t in range(n_tiles):
        m_next = acc = None
        for c in range(n_chunks):
            if t + 1 < n_tiles:
                m_next = score_chunk(k_ref, q_ref[0, 0, :, pl.ds((t + 1) * tq, tq)],
                                     bufs[(t + 1) % 2], c, m_next)
            else:
                m_next = score_chunk(kn_ref, qn_ref[0, 0], bufs[0], c, m_next)
            acc = value_chunk(bufs[t % 2], c, m_cur, acc)
            yield
        o_ref[0, o_rows, pl.ds(t * tq, tq)] = (acc[0:dv] / acc[dv:dv + 1]).astype(_BF16)
        m_cur = m_next
    m0_ref[slot] = m_cur


def _attn_kernel(base_ref, qa_ref, ka_ref, va_ref, qna_ref, kna_ref,
                 qb_ref, kb_ref, vb_ref, qnb_ref, knb_ref, o_ref,
                 sa0_ref, sa1_ref, sb0_ref, sb1_ref, m0_ref):
    base = base_ref[0]
    dv = va_ref.shape[1]
    first = _attn_stream(base, qa_ref, ka_ref, va_ref, qna_ref, kna_ref, o_ref, pl.ds(0, dv),
                         (sa0_ref, sa1_ref), m0_ref, 0)
    second = _attn_stream(base, qb_ref, kb_ref, vb_ref, qnb_ref, knb_ref, o_ref, pl.ds(dv, dv),
                          (sb0_ref, sb1_ref), m0_ref, 1)
    next(first), next(second)
    _trace(_stagger([first, second], lead=ATTN_STREAM_LEAD))


def _attention(q_all, k_all, v_all):
    bsz, _, _, s = q_all.shape
    total = k_all.shape[2]
    group = A_HEADS // A_KV_HEADS
    n_pairs = N_HEADS // 2
    kset = lambda h: jnp.where(h < A_HEADS, 0, h - A_HEADS + 1)
    vblk = lambda h: jnp.where(h < A_HEADS, h // group, h - A_HEADS + A_KV_HEADS)

    def nxt(b, g):
        n = jnp.minimum(b * n_pairs + g + 1, bsz * n_pairs - 1)
        return n // n_pairs, n % n_pairs

    def stream_specs(which):
        head = lambda g: 2 * g + which
        return [
            pl.BlockSpec((1, 1, HEAD_PAD, s), lambda b, g: (b, head(g), 0, 0)),
            pl.BlockSpec((1, 1, total, HEAD_PAD), lambda b, g: (b, kset(head(g)), 0, 0)),
            pl.BlockSpec((1, B_V_DIM, total), lambda b, g: (b, vblk(head(g)), 0)),
            pl.BlockSpec((1, 1, HEAD_PAD, ATTN_Q_TILE),
                         lambda b, g: (nxt(b, g)[0], head(nxt(b, g)[1]), 0, 0)),
            pl.BlockSpec((1, 1, total, HEAD_PAD),
                         lambda b, g: (nxt(b, g)[0], kset(head(nxt(b, g)[1])), 0, 0)),
        ]

    stream_args = (q_all, k_all, v_all, q_all, k_all)
    return pl.pallas_call(
        _attn_kernel,
        grid=(bsz, n_pairs),
        in_specs=[pl.BlockSpec(memory_space=pltpu.SMEM)] + stream_specs(0) + stream_specs(1),
        out_specs=pl.BlockSpec((1, 2 * B_V_DIM, s), lambda b, g: (b, g, 0)),
        out_shape=jax.ShapeDtypeStruct((bsz, N_HEADS * B_V_DIM, s), _BF16),
        scratch_shapes=[pltpu.VMEM((total, ATTN_Q_TILE), _F32)] * 4
        + [pltpu.VMEM((2, 1, ATTN_Q_TILE), _F32)],
        compiler_params=pltpu.CompilerParams(
            dimension_semantics=("arbitrary", "arbitrary"), vmem_limit_bytes=VMEM_LIMIT),
        name="attn",
    )(jnp.zeros((1,), jnp.int32), *stream_args, *stream_args)


def _merge_pieces(mrow, x, ot, win_ref, wpa_ref, wpb_ref, wo_ref, g_ref, b_ref, result):
    shift, scale, gate = mrow[:, 0:D_MODEL], mrow[:, D_MODEL:2 * D_MODEL], mrow[:, 2 * D_MODEL:3 * D_MODEL]
    h = (_layer_norm(x) * (1.0 + scale) + shift).astype(_BF16)
    yield
    na = A_HEADS * A_HEAD_DIM
    ys = []
    for r in range(D_MODEL // MERGE_ROW_CHUNK):
        ra = pl.ds(r * MERGE_ROW_CHUNK, MERGE_ROW_CHUNK)
        ga_rows = pl.ds(GATE_OFF + r * MERGE_ROW_CHUNK, MERGE_ROW_CHUNK)
        gb_rows = pl.ds(GATE_OFF + D_MODEL + r * MERGE_ROW_CHUNK, MERGE_ROW_CHUNK)
        ga = lax.dot_general(win_ref[ga_rows, :], h, _NT, preferred_element_type=_F32)
        gb = lax.dot_general(win_ref[gb_rows, :], h, _NT, preferred_element_type=_F32)
        ya = jnp.dot(wpa_ref[ra, :], ot[0:na], preferred_element_type=_F32)
        yb = jnp.dot(wpb_ref[ra, :], ot[na:], preferred_element_type=_F32)
        ys.append((_sigmoid(ga) * ya + _sigmoid(gb) * yb).astype(_BF16))
        yield
    zt = jnp.dot(wo_ref[...], jnp.concatenate(ys, axis=0), preferred_element_type=_F32)
    r = DEEPNORM_ALPHA * x + gate * zt.T
    result.append(_layer_norm(r) * g_ref[...] + b_ref[...])
    yield


def _ffn_pieces(mrow, x, wup_ref, wdn_ref, g_ref, b_ref, result):
    shift, scale, gate = mrow[:, 0:D_MODEL], mrow[:, D_MODEL:2 * D_MODEL], mrow[:, 2 * D_MODEL:3 * D_MODEL]
    h = (_layer_norm(x) * (1.0 + scale) + shift).astype(_BF16)
    yield
    fs = []
    for c in range(FFN_HIDDEN // FFN_COL_CHUNK):
        a = jnp.dot(h, wup_ref[:, pl.ds(c * FFN_COL_CHUNK, FFN_COL_CHUNK)], preferred_element_type=_F32)
        u = jnp.dot(h, wup_ref[:, pl.ds(FFN_HIDDEN + c * FFN_COL_CHUNK, FFN_COL_CHUNK)],
                    preferred_element_type=_F32)
        fs.append((a * _sigmoid(a) * u).astype(_BF16))
        yield
    dn = jnp.dot(jnp.concatenate(fs, axis=1), wdn_ref[...], preferred_element_type=_F32)
    r = DEEPNORM_ALPHA * x + gate * dn
    result.append(_layer_norm(r) * g_ref[...] + b_ref[...])
    yield


def _post_kernel(mod_ref, x_ref, ot_ref, win_ref, wpa_ref, wpb_ref, wo_ref, g1_ref, b1_ref,
                 wup_ref, wdn_ref, g2_ref, b2_ref, out_ref, x1_ref, *, tiles_per_batch):
    j = pl.program_id(0)
    n_tiles = pl.num_programs(0) - 1
    half_batch = n_tiles // tiles_per_batch

    @pl.when(j == 0)
    def _():
        x1_ref[...] = jnp.zeros_like(x1_ref)

    b_prev = jnp.maximum(j - 1, 0) // tiles_per_batch
    b_cur = jnp.minimum(j, n_tiles - 1) // tiles_per_batch
    ffn_out, merge_out = ([], []), ([], [])

    def stream(half):
        off = half * half_batch
        return _stagger(
            [_ffn_pieces(mod_ref[pl.ds(b_prev + off, 1), 3 * D_MODEL:6 * D_MODEL], x1_ref[half],
                         wup_ref, wdn_ref, g2_ref, b2_ref, ffn_out[half]),
             _merge_pieces(mod_ref[pl.ds(b_cur + off, 1), 0:3 * D_MODEL], x_ref[half, 0],
                           ot_ref[half, 0], win_ref, wpa_ref, wpb_ref, wo_ref, g1_ref, b1_ref,
                           merge_out[half])],
            rates=[POST_FFN_PIECES_PER_MERGE_PIECE, 1])

    _trace(_stagger([stream(0), stream(1)], lead=POST_STREAM_LEAD))
    for half in range(2):
        out_ref[half, 0] = ffn_out[half][0]
        x1_ref[half] = merge_out[half][0]


def _post(mod, x, ot, w_in_t, wpat, wpbt, wot, g1, b1, wup, wdn, g2, b2):
    bsz, s, d = x.shape
    t = TOKEN_TILE
    tpb = s // t
    hb = bsz // 2
    n_tiles = hb * tpb
    const = lambda a: pl.BlockSpec(a.shape, lambda j: (0,) * a.ndim, pipeline_mode=pl.Buffered(1))
    cur = lambda j: jnp.minimum(j, n_tiles - 1)
    prev = lambda j: jnp.maximum(j - 1, 0)
    out = pl.pallas_call(
        functools.partial(_post_kernel, tiles_per_batch=tpb),
        grid=(n_tiles + 1,),
        in_specs=[
            const(mod),
            pl.BlockSpec((2, 1, t, d), lambda j: (0, cur(j) // tpb, cur(j) % tpb, 0)),
            pl.BlockSpec((2, 1, ot.shape[1], t), lambda j: (0, cur(j) // tpb, 0, cur(j) % tpb)),
            const(w_in_t), const(wpat), const(wpbt), const(wot), const(g1), const(b1),
            const(wup), const(wdn), const(g2), const(b2),
        ],
        out_specs=pl.BlockSpec((2, 1, t, d), lambda j: (0, prev(j) // tpb, prev(j) % tpb, 0)),
        out_shape=jax.ShapeDtypeStruct((2, hb, s, d), _F32),
        scratch_shapes=[pltpu.VMEM((2, t, d), _F32)],
        compiler_params=pltpu.CompilerParams(
            dimension_semantics=("arbitrary",), vmem_limit_bytes=VMEM_LIMIT),
        name="post",
    )(mod, x.reshape(2, hb, s, d), ot.reshape(2, hb, ot.shape[1], s),
      w_in_t, wpat, wpbt, wot, g1, b1, wup, wdn, g2, b2)
    return out.reshape(bsz, s, d)


def _rope_tables(seq, half):
    pos = np.arange(seq)
    freqs = ROPE_THETA ** (-np.arange(half, dtype=np.float64) / half)
    tabs = []
    for p in (pos // GRID_W, pos % GRID_W):
        ang = p[None, :].astype(np.float64) * freqs[:, None]
        for fn, ident in ((np.cos, 1.0), (np.sin, 0.0)):
            tabs.append(np.concatenate([np.full((half, CTX_LEN), ident), fn(ang)], axis=1))
    return jnp.asarray(np.stack(tabs), dtype=_F32)


def kernel(x, c, ctx, c_ctx, w_mod, b_mod, w_in, q_norm_a, k_norm_a, cq_norm, ckv_norm,
           w_uq, w_ukv, w_proj_a, w_proj_b, w_out, ln1_g, ln1_b, w_up, w_down, ln2_g, ln2_b):
    bsz, s, d = x.shape
    assert w_mod.shape[0] == DEPTH == 1 and d == D_MODEL and ctx.shape[1] == CTX_LEN
    assert bsz == 8 and s % ATTN_Q_TILE == 0 and s % TOKEN_TILE == 0

    cc = jnp.concatenate([c, c_ctx[None, :], jnp.zeros((bsz - 1, d), _F32)], axis=0)
    mod = _modulation(cc, w_mod[0], b_mod[0][None, :])

    tbf = lambda w: w[0].astype(_BF16).T
    w_in_t = tbf(w_in)
    col = lambda v: v[0][:, None]
    tab_a = _rope_tables(s, A_HEAD_DIM // 4)
    tab_b = _rope_tables(s, B_ROPE_DIM // 4)

    q_all, k_all, v_all = _project(
        mod, ctx, x, w_in_t, tbf(w_uq), tbf(w_ukv),
        col(q_norm_a), col(k_norm_a), col(cq_norm), col(ckv_norm), tab_a, tab_b)
    ot = _attention(q_all, k_all, v_all)
    return _post(mod, x, ot, w_in_t, tbf(w_proj_a), tbf(w_proj_b), tbf(w_out), ln1_g, ln1_b,
                 w_up[0].astype(_BF16), w_down[0].astype(_BF16), ln2_g, ln2_b)
```

```python
import functools
import math

import numpy as np
import jax
import jax.numpy as jnp
from jax import lax
from jax.experimental import pallas as pl
from jax.experimental.pallas import tpu as pltpu

D_MODEL = 1024
GRID_W = 64
CTX_LEN = 256
ROPE_THETA = 10000.0
EPS = 1e-6

A_HEADS = 8
A_KV_HEADS = 2
A_HEAD_DIM = 64
B_HEADS = 8
B_Q_RANK = 384
B_KV_RANK = 256
B_NOPE_DIM = 64
B_ROPE_DIM = 32
B_V_DIM = 64
FFN_HIDDEN = 2816
DEPTH = 1

A_SCALE = A_HEAD_DIM ** -0.5
B_SCALE = (B_NOPE_DIM + B_ROPE_DIM) ** -0.5
DEEPNORM_ALPHA = (2.0 * DEPTH) ** 0.25
LOG2_E = math.log2(math.e)

QA_OFF = 0
KA_OFF = QA_OFF + A_HEADS * A_HEAD_DIM
VA_OFF = KA_OFF + A_KV_HEADS * A_HEAD_DIM
CQ_OFF = VA_OFF + A_KV_HEADS * A_HEAD_DIM
CKV_OFF = CQ_OFF + B_Q_RANK
KR_OFF = CKV_OFF + B_KV_RANK
GATE_OFF = KR_OFF + B_ROPE_DIM
QKV_COLS = GATE_OFF

N_HEADS = A_HEADS + B_HEADS
HEAD_PAD = 128
N_KSETS = 1 + B_HEADS
V_ROWS = A_KV_HEADS * A_HEAD_DIM + B_HEADS * B_V_DIM

MOD_COL_TILE = 1024
PROJ_TILE = 256
PROJ_ROW_CHUNK = 288
PROJ_STREAMS = 4
PROJ_STREAM_LEAD = 3
ATTN_Q_TILE = 512
ATTN_KEY_CHUNK = 256
ATTN_STREAM_LEAD = 9
ATTN_ONES_ROWS = 16
TOKEN_TILE = 256
MERGE_ROW_CHUNK = 256
FFN_COL_CHUNK = 256
POST_FFN_PIECES_PER_MERGE_PIECE = 2
POST_STREAM_LEAD = 10
VMEM_LIMIT = 56 * 1024 * 1024

_NT = (((1,), (1,)), ((), ()))
_F32 = jnp.float32
_BF16 = jnp.bfloat16


def _layer_norm(x):
    mu = jnp.mean(x, axis=-1, keepdims=True)
    xc = x - mu
    var = jnp.mean(xc * xc, axis=-1, keepdims=True)
    return xc * lax.rsqrt(var + EPS)


def _sigmoid(x):
    return jax.nn.sigmoid(x)


def _mod_kernel(c_ref, w_ref, b_ref, o_ref):
    c = c_ref[...]
    a = (c * _sigmoid(c)).astype(_BF16)
    o_ref[...] = jnp.dot(a, w_ref[...].astype(_BF16), preferred_element_type=_F32) + b_ref[...]


def _modulation(cc, w_mod, b_mod):
    rows, d = cc.shape
    n = w_mod.shape[1]
    return pl.pallas_call(
        _mod_kernel,
        grid=(n // MOD_COL_TILE,),
        in_specs=[
            pl.BlockSpec((rows, d), lambda i: (0, 0)),
            pl.BlockSpec((d, MOD_COL_TILE), lambda i: (0, i)),
            pl.BlockSpec((1, MOD_COL_TILE), lambda i: (0, i)),
        ],
        out_specs=pl.BlockSpec((rows, MOD_COL_TILE), lambda i: (0, i)),
        out_shape=jax.ShapeDtypeStruct((rows, n), _F32),
        compiler_params=pltpu.CompilerParams(
            dimension_semantics=("arbitrary",), vmem_limit_bytes=VMEM_LIMIT),
        name="mod",
    )(cc, w_mod, b_mod)


def _rms_rows(x, g):
    ms = jnp.mean(x * x, axis=0, keepdims=True)
    return x * lax.rsqrt(ms + EPS) * g


def _axial_rope_rows(x, half, tab):
    r1, r2 = x[0:half], x[half:2 * half]
    c1, c2 = x[2 * half:3 * half], x[3 * half:4 * half]
    cr, sr, cc, sc = tab[0], tab[1], tab[2], tab[3]
    return jnp.concatenate(
        [r1 * cr - r2 * sr, r1 * sr + r2 * cr, c1 * cc - c2 * sc, c1 * sc + c2 * cc], axis=0)


def _stagger(streams, lead=0, rates=None):
    rates = rates or [1] * len(streams)
    live = dict(enumerate(streams))
    rnd = 0
    while live:
        for i in sorted(live):
            if rnd < i * lead:
                continue
            for _ in range(rates[i]):
                if next(live[i], StopIteration) is StopIteration:
                    del live[i]
                    break
                yield
        rnd += 1


def _trace(pieces):
    for _ in pieces:
        pass


def _proj_pieces(mrow, xin, w1_ref, wuq_ref, wukv_ref, qn_ref, kn_ref, cqn_ref, ckvn_ref, ta, tb,
                 q_out, k_out, v_out):
    shift, scale = mrow[:, 0:D_MODEL], mrow[:, D_MODEL:2 * D_MODEL]
    h = (_layer_norm(xin) * (1.0 + scale) + shift).astype(_BF16)
    yield
    parts = []
    for r in range(QKV_COLS // PROJ_ROW_CHUNK):
        rows = pl.ds(r * PROJ_ROW_CHUNK, PROJ_ROW_CHUNK)
        parts.append(lax.dot_general(w1_ref[rows, :], h, _NT, preferred_element_type=_F32))
        yield
    pt = jnp.concatenate(parts, axis=0)
    t = pt.shape[1]
    z64 = jnp.zeros((A_HEAD_DIM, t), _F32)
    z32 = jnp.zeros((HEAD_PAD - B_NOPE_DIM - B_ROPE_DIM, t), _F32)

    qn = qn_ref[...] * (A_SCALE * LOG2_E)
    group = A_HEADS // A_KV_HEADS
    for hd in range(A_HEADS):
        xh = pt[QA_OFF + hd * A_HEAD_DIM:QA_OFF + (hd + 1) * A_HEAD_DIM]
        xh = _axial_rope_rows(_rms_rows(xh, qn), A_HEAD_DIM // 4, ta)
        full = jnp.concatenate([xh, z64] if hd // group == 0 else [z64, xh], axis=0)
        q_out[hd] = full.astype(_BF16)
        if hd % 4 == 3:
            yield

    kn = kn_ref[...]
    ka = [
        _axial_rope_rows(
            _rms_rows(pt[KA_OFF + g * A_HEAD_DIM:KA_OFF + (g + 1) * A_HEAD_DIM], kn),
            A_HEAD_DIM // 4, ta)
        for g in range(A_KV_HEADS)
    ]
    k_out[0] = jnp.concatenate(ka, axis=0).T.astype(_BF16)
    v_out[0:A_KV_HEADS * A_HEAD_DIM, :] = pt[VA_OFF:CQ_OFF].astype(_BF16)
    yield

    cq = _rms_rows(pt[CQ_OFF:CKV_OFF], cqn_ref[...]).astype(_BF16)
    qb = jnp.dot(wuq_ref[...], cq, preferred_element_type=_F32) * (B_SCALE * LOG2_E)
    yield
    qdim = B_NOPE_DIM + B_ROPE_DIM
    for hd in range(B_HEADS):
        blk = qb[hd * qdim:(hd + 1) * qdim]
        rp = _axial_rope_rows(blk[B_NOPE_DIM:qdim], B_ROPE_DIM // 4, tb)
        full = jnp.concatenate([blk[0:B_NOPE_DIM], rp, z32], axis=0)
        q_out[A_HEADS + hd] = full.astype(_BF16)
        if hd % 4 == 3:
            yield

    ckv = _rms_rows(pt[CKV_OFF:KR_OFF], ckvn_ref[...]).astype(_BF16)
    kv = jnp.dot(wukv_ref[...], ckv, preferred_element_type=_F32)
    kr = _axial_rope_rows(pt[KR_OFF:GATE_OFF], B_ROPE_DIM // 4, tb)
    yield
    kvdim = B_NOPE_DIM + B_V_DIM
    v_base = A_KV_HEADS * A_HEAD_DIM
    for hd in range(B_HEADS):
        kn_h = kv[hd * kvdim:hd * kvdim + B_NOPE_DIM]
        v_h = kv[hd * kvdim + B_NOPE_DIM:(hd + 1) * kvdim]
        v_out[v_base + hd * B_V_DIM:v_base + (hd + 1) * B_V_DIM, :] = v_h.astype(_BF16)
        k_out[1 + hd] = jnp.concatenate([kn_h, kr, z32], axis=0).T.astype(_BF16)
        if hd % 4 == 3:
            yield


def _proj_kernel(mod_ref, ctx_ref, x_ref, w1_ref, wuq_ref, wukv_ref,
                 qn_ref, kn_ref, cqn_ref, ckvn_ref, ta_ref, tb_ref,
                 q_out, k_out, v_out):
    b = pl.program_id(0)
    j = pl.program_id(1)
    batch_stride = pl.num_programs(0)
    is_ctx = j == 0
    ctx_row = mod_ref.shape[0] // 2
    ta = ta_ref[...]
    tb = tb_ref[...]

    def stream(i):
        xin = jnp.where(is_ctx, ctx_ref[i, 0], x_ref[i, 0])
        mrow = jnp.where(is_ctx, mod_ref[ctx_row:ctx_row + 1, 0:2 * D_MODEL],
                         mod_ref[pl.ds(b + i * batch_stride, 1), 0:2 * D_MODEL])
        return _proj_pieces(mrow, xin, w1_ref, wuq_ref, wukv_ref, qn_ref, kn_ref, cqn_ref, ckvn_ref,
                            ta, tb, q_out.at[i, 0], k_out.at[i, 0], v_out.at[i, 0])

    _trace(_stagger([stream(i) for i in range(PROJ_STREAMS)], lead=PROJ_STREAM_LEAD))


def _project(mod, ctx, x, w_in_t, wuqt, wukvt, qn, kn, cqn, ckvn, tab_a, tab_b):
    bsz, s, d = x.shape
    t = PROJ_TILE
    n_steps = (CTX_LEN + s) // t
    total = CTX_LEN + s
    ns = PROJ_STREAMS
    hb = bsz // ns
    lat = lambda j: jnp.maximum(j - 1, 0)
    full2 = lambda shape: pl.BlockSpec(shape, lambda b, j: (0, 0))
    q_all, k_all, v_all = pl.pallas_call(
        _proj_kernel,
        grid=(hb, n_steps),
        in_specs=[
            full2(mod.shape),
            pl.BlockSpec((ns, 1, CTX_LEN, d), lambda b, j: (0, b, 0, 0)),
            pl.BlockSpec((ns, 1, t, d), lambda b, j: (0, b, lat(j), 0)),
            full2((QKV_COLS, d)),
            full2(wuqt.shape), full2(wukvt.shape),
            full2(qn.shape), full2(kn.shape), full2(cqn.shape), full2(ckvn.shape),
            pl.BlockSpec((4, tab_a.shape[1], t), lambda b, j: (0, 0, j)),
            pl.BlockSpec((4, tab_b.shape[1], t), lambda b, j: (0, 0, j)),
        ],
        out_specs=[
            pl.BlockSpec((ns, 1, N_HEADS, HEAD_PAD, t), lambda b, j: (0, b, 0, 0, lat(j))),
            pl.BlockSpec((ns, 1, N_KSETS, t, HEAD_PAD), lambda b, j: (0, b, 0, j, 0)),
            pl.BlockSpec((ns, 1, V_ROWS, t), lambda b, j: (0, b, 0, j)),
        ],
        out_shape=[
            jax.ShapeDtypeStruct((ns, hb, N_HEADS, HEAD_PAD, s), _BF16),
            jax.ShapeDtypeStruct((ns, hb, N_KSETS, total, HEAD_PAD), _BF16),
            jax.ShapeDtypeStruct((ns, hb, V_ROWS, total), _BF16),
        ],
        compiler_params=pltpu.CompilerParams(
            dimension_semantics=("arbitrary", "arbitrary"), vmem_limit_bytes=VMEM_LIMIT),
        name="proj",
    )(mod, ctx.reshape(ns, hb, CTX_LEN, d), x.reshape(ns, hb, s, d),
      w_in_t, wuqt, wukvt, qn, kn, cqn, ckvn, tab_a, tab_b)
    return (q_all.reshape(bsz, N_HEADS, HEAD_PAD, s), k_all.reshape(bsz, N_KSETS, total, HEAD_PAD),
            v_all.reshape(bsz, V_ROWS, total))


def _attn_stream(base, q_ref, k_ref, v_ref, qn_ref, kn_ref, o_ref, o_rows, bufs, m0_ref, slot):
    tq, kc = ATTN_Q_TILE, ATTN_KEY_CHUNK
    n_tiles = q_ref.shape[3] // tq
    n_chunks = k_ref.shape[2] // kc
    assert n_tiles % 2 == 0
    assert n_tiles * tq == q_ref.shape[3] and n_chunks * kc == k_ref.shape[2]
    dv = v_ref.shape[1]
    ones = jnp.ones((ATTN_ONES_ROWS, kc), _BF16)

    def buf_rows(c):
        return pl.ds(pl.multiple_of(base + c * kc, kc), kc)

    def score_chunk(keys_ref, q_tile, buf, c, m):
        s = jnp.dot(keys_ref[0, 0, pl.ds(c * kc, kc), :], q_tile,
                    preferred_element_type=_F32)
        buf[buf_rows(c), :] = s
        cm = jnp.max(s.reshape(kc // 8, 8, s.shape[1]), axis=0)
        return cm if m is None else jnp.maximum(m, cm)

    def value_chunk(buf, c, m, acc):
        p = jnp.exp2(buf[buf_rows(c), :] - m)
        vext = jnp.concatenate([v_ref[0, :, pl.ds(c * kc, kc)], ones], axis=0)
        pv = jnp.dot(vext, p.astype(_BF16), preferred_element_type=_F32)
        return pv if acc is None else acc + pv

    @pl.when((pl.program_id(0) == 0) & (pl.program_id(1) == 0))
    def _():
        m = None
        for c in range(n_chunks):
            m = score_chunk(k_ref, q_ref[0, 0, :, 0:tq], bufs[0], c, m)
        m0_ref[slot] = jnp.max(m, axis=0, keepdims=True)
    yield

    m_cur = m0_ref[slot]
    for t in range(n_tiles):
        m_next = acc = None
        for c in range(n_chunks):
            if t + 1 < n_tiles:
                m_next = score_chunk(k_ref, q_ref[0, 0, :, pl.ds((t + 1) * tq, tq)],
                                     bufs[(t + 1) % 2], c, m_next)
            else:
                m_next = score_chunk(kn_ref, qn_ref[0, 0], bufs[0], c, m_next)
            acc = value_chunk(bufs[t % 2], c, m_cur, acc)
            yield
        o_ref[0, o_rows, pl.ds(t * tq, tq)] = (acc[0:dv] / acc[dv:dv + 1]).astype(_BF16)
        m_cur = jnp.max(m_next, axis=0, keepdims=True)
    m0_ref[slot] = m_cur


def _attn_kernel(base_ref, qa_ref, ka_ref, va_ref, qna_ref, kna_ref,
                 qb_ref, kb_ref, vb_ref, qnb_ref, knb_ref, o_ref,
                 sa0_ref, sa1_ref, sb0_ref, sb1_ref, m0_ref):
    base = base_ref[0]
    dv = va_ref.shape[1]
    first = _attn_stream(base, qa_ref, ka_ref, va_ref, qna_ref, kna_ref, o_ref, pl.ds(0, dv),
                         (sa0_ref, sa1_ref), m0_ref, 0)
    second = _attn_stream(base, qb_ref, kb_ref, vb_ref, qnb_ref, knb_ref, o_ref, pl.ds(dv, dv),
                          (sb0_ref, sb1_ref), m0_ref, 1)
    next(first), next(second)
    _trace(_stagger([first, second], lead=ATTN_STREAM_LEAD))


def _attention(q_all, k_all, v_all):
    bsz, _, _, s = q_all.shape
    total = k_all.shape[2]
    group = A_HEADS // A_KV_HEADS
    n_pairs = N_HEADS // 2
    kset = lambda h: jnp.where(h < A_HEADS, 0, h - A_HEADS + 1)
    vblk = lambda h: jnp.where(h < A_HEADS, h // group, h - A_HEADS + A_KV_HEADS)

    def nxt(b, g):
        n = jnp.minimum(b * n_pairs + g + 1, bsz * n_pairs - 1)
        return n // n_pairs, n % n_pairs

    def stream_specs(which):
        head = lambda g: 2 * g + which
        return [
            pl.BlockSpec((1, 1, HEAD_PAD, s), lambda b, g: (b, head(g), 0, 0)),
            pl.BlockSpec((1, 1, total, HEAD_PAD), lambda b, g: (b, kset(head(g)), 0, 0)),
            pl.BlockSpec((1, B_V_DIM, total), lambda b, g: (b, vblk(head(g)), 0)),
            pl.BlockSpec((1, 1, HEAD_PAD, ATTN_Q_TILE),
                         lambda b, g: (nxt(b, g)[0], head(nxt(b, g)[1]), 0, 0)),
            pl.BlockSpec((1, 1, total, HEAD_PAD),
                         lambda b, g: (nxt(b, g)[0], kset(head(nxt(b, g)[1])), 0, 0)),
        ]

    stream_args = (q_all, k_all, v_all, q_all, k_all)
    return pl.pallas_call(
        _attn_kernel,
        grid=(bsz, n_pairs),
        in_specs=[pl.BlockSpec(memory_space=pltpu.SMEM)] + stream_specs(0) + stream_specs(1),
        out_specs=pl.BlockSpec((1, 2 * B_V_DIM, s), lambda b, g: (b, g, 0)),
        out_shape=jax.ShapeDtypeStruct((bsz, N_HEADS * B_V_DIM, s), _BF16),
        scratch_shapes=[pltpu.VMEM((total, ATTN_Q_TILE), _F32)] * 4
        + [pltpu.VMEM((2, 1, ATTN_Q_TILE), _F32)],
        compiler_params=pltpu.CompilerParams(
            dimension_semantics=("arbitrary", "arbitrary"), vmem_limit_bytes=VMEM_LIMIT),
        name="attn",
    )(jnp.zeros((1,), jnp.int32), *stream_args, *stream_args)


def _merge_pieces(mrow, x, ot, win_ref, wpa_ref, wpb_ref, wo_ref, g_ref, b_ref, result):
    shift, scale, gate = mrow[:, 0:D_MODEL], mrow[:, D_MODEL:2 * D_MODEL], mrow[:, 2 * D_MODEL:3 * D_MODEL]
    h = (_layer_norm(x) * (1.0 + scale) + shift).astype(_BF16)
    yield
    na = A_HEADS * A_HEAD_DIM
    ys = []
    for r in range(D_MODEL // MERGE_ROW_CHUNK):
        ra = pl.ds(r * MERGE_ROW_CHUNK, MERGE_ROW_CHUNK)
        ga_rows = pl.ds(GATE_OFF + r * MERGE_ROW_CHUNK, MERGE_ROW_CHUNK)
        gb_rows = pl.ds(GATE_OFF + D_MODEL + r * MERGE_ROW_CHUNK, MERGE_ROW_CHUNK)
        ga = lax.dot_general(win_ref[ga_rows, :], h, _NT, preferred_element_type=_F32)
        gb = lax.dot_general(win_ref[gb_rows, :], h, _NT, preferred_element_type=_F32)
        ya = jnp.dot(wpa_ref[ra, :], ot[0:na], preferred_element_type=_F32)
        yb = jnp.dot(wpb_ref[ra, :], ot[na:], preferred_element_type=_F32)
        ys.append((_sigmoid(ga) * ya + _sigmoid(gb) * yb).astype(_BF16))
        yield
    zt = jnp.dot(wo_ref[...], jnp.concatenate(ys, axis=0), preferred_element_type=_F32)
    r = DEEPNORM_ALPHA * x + gate * zt.T
    result.append(_layer_norm(r) * g_ref[...] + b_ref[...])
    yield


def _ffn_pieces(mrow, x, wup_ref, wdn_ref, g_ref, b_ref, result):
    shift, scale, gate = mrow[:, 0:D_MODEL], mrow[:, D_MODEL:2 * D_MODEL], mrow[:, 2 * D_MODEL:3 * D_MODEL]
    h = (_layer_norm(x) * (1.0 + scale) + shift).astype(_BF16)
    yield
    fs = []
    for c in range(FFN_HIDDEN // FFN_COL_CHUNK):
        a = jnp.dot(h, wup_ref[:, pl.ds(c * FFN_COL_CHUNK, FFN_COL_CHUNK)], preferred_element_type=_F32)
        u = jnp.dot(h, wup_ref[:, pl.ds(FFN_HIDDEN + c * FFN_COL_CHUNK, FFN_COL_CHUNK)],
                    preferred_element_type=_F32)
        fs.append((a * _sigmoid(a) * u).astype(_BF16))
        yield
    dn = jnp.dot(jnp.concatenate(fs, axis=1), wdn_ref[...], preferred_element_type=_F32)
    r = DEEPNORM_ALPHA * x + gate * dn
    result.append(_layer_norm(r) * g_ref[...] + b_ref[...])
    yield


def _post_kernel(mod_ref, x_ref, ot_ref, win_ref, wpa_ref, wpb_ref, wo_ref, g1_ref, b1_ref,
                 wup_ref, wdn_ref, g2_ref, b2_ref, out_ref, x1_ref, *, tiles_per_batch):
    j = pl.program_id(0)
    n_tiles = pl.num_programs(0) - 1
    half_batch = n_tiles // tiles_per_batch

    @pl.when(j == 0)
    def _():
        x1_ref[...] = jnp.zeros_like(x1_ref)

    b_prev = jnp.maximum(j - 1, 0) // tiles_per_batch
    b_cur = jnp.minimum(j, n_tiles - 1) // tiles_per_batch
    ffn_out, merge_out = ([], []), ([], [])

    def stream(half):
        off = half * half_batch
        return _stagger(
            [_ffn_pieces(mod_ref[pl.ds(b_prev + off, 1), 3 * D_MODEL:6 * D_MODEL], x1_ref[half],
                         wup_ref, wdn_ref, g2_ref, b2_ref, ffn_out[half]),
             _merge_pieces(mod_ref[pl.ds(b_cur + off, 1), 0:3 * D_MODEL], x_ref[half, 0],
                           ot_ref[half, 0], win_ref, wpa_ref, wpb_ref, wo_ref, g1_ref, b1_ref,
                           merge_out[half])],
            rates=[POST_FFN_PIECES_PER_MERGE_PIECE, 1])

    _trace(_stagger([stream(0), stream(1)], lead=POST_STREAM_LEAD))
    for half in range(2):
        out_ref[half, 0] = ffn_out[half][0]
        x1_ref[half] = merge_out[half][0]


def _post(mod, x, ot, w_in_t, wpat, wpbt, wot, g1, b1, wup, wdn, g2, b2):
    bsz, s, d = x.shape
    t = TOKEN_TILE
    tpb = s // t
    hb = bsz // 2
    n_tiles = hb * tpb
    const = lambda a: pl.BlockSpec(a.shape, lambda j: (0,) * a.ndim, pipeline_mode=pl.Buffered(1))
    cur = lambda j: jnp.minimum(j, n_tiles - 1)
    prev = lambda j: jnp.maximum(j - 1, 0)
    out = pl.pallas_call(
        functools.partial(_post_kernel, tiles_per_batch=tpb),
        grid=(n_tiles + 1,),
        in_specs=[
            const(mod),
            pl.BlockSpec((2, 1, t, d), lambda j: (0, cur(j) // tpb, cur(j) % tpb, 0)),
            pl.BlockSpec((2, 1, ot.shape[1], t), lambda j: (0, cur(j) // tpb, 0, cur(j) % tpb)),
            const(w_in_t), const(wpat), const(wpbt), const(wot), const(g1), const(b1),
            const(wup), const(wdn), const(g2), const(b2),
        ],
        out_specs=pl.BlockSpec((2, 1, t, d), lambda j: (0, prev(j) // tpb, prev(j) % tpb, 0)),
        out_shape=jax.ShapeDtypeStruct((2, hb, s, d), _F32),
        scratch_shapes=[pltpu.VMEM((2, t, d), _F32)],
        compiler_params=pltpu.CompilerParams(
            dimension_semantics=("arbitrary",), vmem_limit_bytes=VMEM_LIMIT),
        name="post",
    )(mod, x.reshape(2, hb, s, d), ot.reshape(2, hb, ot.shape[1], s),
      w_in_t, wpat, wpbt, wot, g1, b1, wup, wdn, g2, b2)
    return out.reshape(bsz, s, d)


def _rope_tables(seq, half):
    pos = np.arange(seq)
    freqs = ROPE_THETA ** (-np.arange(half, dtype=np.float64) / half)
    tabs = []
    for p in (pos // GRID_W, pos % GRID_W):
        ang = p[None, :].astype(np.float64) * freqs[:, None]
        for fn, ident in ((np.cos, 1.0), (np.sin, 0.0)):
            tabs.append(np.concatenate([np.full((half, CTX_LEN), ident), fn(ang)], axis=1))
    return jnp.asarray(np.stack(tabs), dtype=_F32)


def kernel(x, c, ctx, c_ctx, w_mod, b_mod, w_in, q_norm_a, k_norm_a, cq_norm, ckv_norm,
           w_uq, w_ukv, w_proj_a, w_proj_b, w_out, ln1_g, ln1_b, w_up, w_down, ln2_g, ln2_b):
    bsz, s, d = x.shape
    assert w_mod.shape[0] == DEPTH == 1 and d == D_MODEL and ctx.shape[1] == CTX_LEN
    assert bsz == 8 and s % ATTN_Q_TILE == 0 and s % TOKEN_TILE == 0

    cc = jnp.concatenate([c, c_ctx[None, :], jnp.zeros((bsz - 1, d), _F32)], axis=0)
    mod = _modulation(cc, w_mod[0], b_mod[0][None, :])

    tbf = lambda w: w[0].astype(_BF16).T
    w_in_t = tbf(w_in)
    col = lambda v: v[0][:, None]
    tab_a = _rope_tables(s, A_HEAD_DIM // 4)
    tab_b = _rope_tables(s, B_ROPE_DIM // 4)

    q_all, k_all, v_all = _project(
        mod, ctx, x, w_in_t, tbf(w_uq), tbf(w_ukv),
        col(q_norm_a), col(k_norm_a), col(cq_norm), col(ckv_norm), tab_a, tab_b)
    ot = _attention(q_all, k_all, v_all)
    return _post(mod, x, ot, w_in_t, tbf(w_proj_a), tbf(w_proj_b), tbf(w_out), ln1_g, ln1_b,
                 w_up[0].astype(_BF16), w_down[0].astype(_BF16), ln2_g, ln2_b)
```

```python
import functools
import math

import numpy as np
import jax
import jax.numpy as jnp
from jax import lax
from jax.experimental import pallas as pl
from jax.experimental.pallas import tpu as pltpu

D_MODEL = 1024
GRID_W = 64
CTX_LEN = 256
ROPE_THETA = 10000.0
EPS = 1e-6

A_HEADS = 8
A_KV_HEADS = 2
A_HEAD_DIM = 64
B_HEADS = 8
B_Q_RANK = 384
B_KV_RANK = 256
B_NOPE_DIM = 64
B_ROPE_DIM = 32
B_V_DIM = 64
FFN_HIDDEN = 2816
DEPTH = 1

A_SCALE = A_HEAD_DIM ** -0.5
B_SCALE = (B_NOPE_DIM + B_ROPE_DIM) ** -0.5
DEEPNORM_ALPHA = (2.0 * DEPTH) ** 0.25
LOG2_E = math.log2(math.e)

QA_OFF = 0
KA_OFF = QA_OFF + A_HEADS * A_HEAD_DIM
VA_OFF = KA_OFF + A_KV_HEADS * A_HEAD_DIM
CQ_OFF = VA_OFF + A_KV_HEADS * A_HEAD_DIM
CKV_OFF = CQ_OFF + B_Q_RANK
KR_OFF = CKV_OFF + B_KV_RANK
GATE_OFF = KR_OFF + B_ROPE_DIM
QKV_COLS = GATE_OFF

N_HEADS = A_HEADS + B_HEADS
HEAD_PAD = 128
N_KSETS = 1 + B_HEADS
V_ROWS = A_KV_HEADS * A_HEAD_DIM + B_HEADS * B_V_DIM

MOD_COL_TILE = 1024
PROJ_TILE = 256
PROJ_ROW_CHUNK = 288
PROJ_STREAMS = 4
PROJ_STREAM_LEAD = 3
ATTN_Q_TILE = 512
ATTN_KEY_CHUNK = 256
ATTN_STREAM_LEAD = 9
ATTN_ONES_ROWS = 16
TOKEN_TILE = 256
MERGE_ROW_CHUNK = 256
FFN_COL_CHUNK = 256
POST_FFN_PIECES_PER_MERGE_PIECE = 2
POST_STREAM_LEAD = 10
VMEM_LIMIT = 56 * 1024 * 1024

_NT = (((1,), (1,)), ((), ()))
_F32 = jnp.float32
_BF16 = jnp.bfloat16


def _layer_norm(x):
    mu = jnp.mean(x, axis=-1, keepdims=True)
    xc = x - mu
    var = jnp.mean(xc * xc, axis=-1, keepdims=True)
    return xc * lax.rsqrt(var + EPS)


def _sigmoid(x):
    return jax.nn.sigmoid(x)


def _mod_kernel(c_ref, w_ref, b_ref, o_ref):
    c = c_ref[...]
    a = (c * _sigmoid(c)).astype(_BF16)
    o_ref[...] = jnp.dot(a, w_ref[...].astype(_BF16), preferred_element_type=_F32) + b_ref[...]


def _modulation(cc, w_mod, b_mod):
    rows, d = cc.shape
    n = w_mod.shape[1]
    return pl.pallas_call(
        _mod_kernel,
        grid=(n // MOD_COL_TILE,),
        in_specs=[
            pl.BlockSpec((rows, d), lambda i: (0, 0)),
            pl.BlockSpec((d, MOD_COL_TILE), lambda i: (0, i)),
            pl.BlockSpec((1, MOD_COL_TILE), lambda i: (0, i)),
        ],
        out_specs=pl.BlockSpec((rows, MOD_COL_TILE), lambda i: (0, i)),
        out_shape=jax.ShapeDtypeStruct((rows, n), _F32),
        compiler_params=pltpu.CompilerParams(
            dimension_semantics=("arbitrary",), vmem_limit_bytes=VMEM_LIMIT),
        name="mod",
    )(cc, w_mod, b_mod)


def _rms_rows(x, g):
    ms = jnp.mean(x * x, axis=0, keepdims=True)
    return x * lax.rsqrt(ms + EPS) * g


def _axial_rope_rows(x, half, tab):
    r1, r2 = x[0:half], x[half:2 * half]
    c1, c2 = x[2 * half:3 * half], x[3 * half:4 * half]
    cr, sr, cc, sc = tab[0], tab[1], tab[2], tab[3]
    return jnp.concatenate(
        [r1 * cr - r2 * sr, r1 * sr + r2 * cr, c1 * cc - c2 * sc, c1 * sc + c2 * cc], axis=0)


def _stagger(streams, lead=0, rates=None):
    rates = rates or [1] * len(streams)
    live = dict(enumerate(streams))
    rnd = 0
    while live:
        for i in sorted(live):
            if rnd < i * lead:
                continue
            for _ in range(rates[i]):
                if next(live[i], StopIteration) is StopIteration:
                    del live[i]
                    break
                yield
        rnd += 1


def _trace(pieces):
    for _ in pieces:
        pass


def _proj_pieces(mrow, xin, w1_ref, wuq_ref, wukv_ref, qn_ref, kn_ref, cqn_ref, ckvn_ref, ta, tb,
                 q_out, k_out, v_out):
    shift, scale = mrow[:, 0:D_MODEL], mrow[:, D_MODEL:2 * D_MODEL]
    h = (_layer_norm(xin) * (1.0 + scale) + shift).astype(_BF16)
    yield
    parts = []
    for r in range(QKV_COLS // PROJ_ROW_CHUNK):
        rows = pl.ds(r * PROJ_ROW_CHUNK, PROJ_ROW_CHUNK)
        parts.append(lax.dot_general(w1_ref[rows, :], h, _NT, preferred_element_type=_F32))
        yield
    pt = jnp.concatenate(parts, axis=0)
    t = pt.shape[1]
    z64 = jnp.zeros((A_HEAD_DIM, t), _F32)
    z32 = jnp.zeros((HEAD_PAD - B_NOPE_DIM - B_ROPE_DIM, t), _F32)

    qn = qn_ref[...] * (A_SCALE * LOG2_E)
    group = A_HEADS // A_KV_HEADS
    for hd in range(A_HEADS):
        xh = pt[QA_OFF + hd * A_HEAD_DIM:QA_OFF + (hd + 1) * A_HEAD_DIM]
        xh = _axial_rope_rows(_rms_rows(xh, qn), A_HEAD_DIM // 4, ta)
        full = jnp.concatenate([xh, z64] if hd // group == 0 else [z64, xh], axis=0)
        q_out[hd] = full.astype(_BF16)
        if hd % 4 == 3:
            yield

    kn = kn_ref[...]
    ka = [
        _axial_rope_rows(
            _rms_rows(pt[KA_OFF + g * A_HEAD_DIM:KA_OFF + (g + 1) * A_HEAD_DIM], kn),
            A_HEAD_DIM // 4, ta)
        for g in range(A_KV_HEADS)
    ]
    k_out[0] = jnp.concatenate(ka, axis=0).T.astype(_BF16)
    v_out[0:A_KV_HEADS * A_HEAD_DIM, :] = pt[VA_OFF:CQ_OFF].astype(_BF16)
    yield

    cq = _rms_rows(pt[CQ_OFF:CKV_OFF], cqn_ref[...]).astype(_BF16)
    qb = jnp.dot(wuq_ref[...], cq, preferred_element_type=_F32) * (B_SCALE * LOG2_E)
    yield
    qdim = B_NOPE_DIM + B_ROPE_DIM
    for hd in range(B_HEADS):
        blk = qb[hd * qdim:(hd + 1) * qdim]
        rp = _axial_rope_rows(blk[B_NOPE_DIM:qdim], B_ROPE_DIM // 4, tb)
        full = jnp.concatenate([blk[0:B_NOPE_DIM], rp, z32], axis=0)
        q_out[A_HEADS + hd] = full.astype(_BF16)
        if hd % 4 == 3:
            yield

    ckv = _rms_rows(pt[CKV_OFF:KR_OFF], ckvn_ref[...]).astype(_BF16)
    kv = jnp.dot(wukv_ref[...], ckv, preferred_element_type=_F32)
    kr = _axial_rope_rows(pt[KR_OFF:GATE_OFF], B_ROPE_DIM // 4, tb)
    yield
    kvdim = B_NOPE_DIM + B_V_DIM
    v_base = A_KV_HEADS * A_HEAD_DIM
    for hd in range(B_HEADS):
        kn_h = kv[hd * kvdim:hd * kvdim + B_NOPE_DIM]
        v_h = kv[hd * kvdim + B_NOPE_DIM:(hd + 1) * kvdim]
        v_out[v_base + hd * B_V_DIM:v_base + (hd + 1) * B_V_DIM, :] = v_h.astype(_BF16)
        k_out[1 + hd] = jnp.concatenate([kn_h, kr, z32], axis=0).T.astype(_BF16)
        if hd % 4 == 3:
            yield


def _proj_kernel(mod_ref, ctx_ref, x_ref, w1_ref, wuq_ref, wukv_ref,
                 qn_ref, kn_ref, cqn_ref, ckvn_ref, ta_ref, tb_ref,
                 q_out, k_out, v_out):
    b = pl.program_id(0)
    j = pl.program_id(1)
    batch_stride = pl.num_programs(0)
    is_ctx = j == 0
    ctx_row = mod_ref.shape[0] // 2
    ta = ta_ref[...]
    tb = tb_ref[...]

    def stream(i):
        xin = jnp.where(is_ctx, ctx_ref[i, 0], x_ref[i, 0])
        mrow = jnp.where(is_ctx, mod_ref[ctx_row:ctx_row + 1, 0:2 * D_MODEL],
                         mod_ref[pl.ds(b + i * batch_stride, 1), 0:2 * D_MODEL])
        return _proj_pieces(mrow, xin, w1_ref, wuq_ref, wukv_ref, qn_ref, kn_ref, cqn_ref, ckvn_ref,
                            ta, tb, q_out.at[i, 0], k_out.at[i, 0], v_out.at[i, 0])

    _trace(_stagger([stream(i) for i in range(PROJ_STREAMS)], lead=PROJ_STREAM_LEAD))


def _project(mod, ctx, x, w_in_t, wuqt, wukvt, qn, kn, cqn, ckvn, tab_a, tab_b):
    bsz, s, d = x.shape
    t = PROJ_TILE
    n_steps = (CTX_LEN + s) // t
    total = CTX_LEN + s
    ns = PROJ_STREAMS
    hb = bsz // ns
    lat = lambda j: jnp.maximum(j - 1, 0)
    full2 = lambda shape: pl.BlockSpec(shape, lambda b, j: (0, 0))
    q_all, k_all, v_all = pl.pallas_call(
        _proj_kernel,
        grid=(hb, n_steps),
        in_specs=[
            full2(mod.shape),
            pl.BlockSpec((ns, 1, CTX_LEN, d), lambda b, j: (0, b, 0, 0)),
            pl.BlockSpec((ns, 1, t, d), lambda b, j: (0, b, lat(j), 0)),
            full2((QKV_COLS, d)),
            full2(wuqt.shape), full2(wukvt.shape),
            full2(qn.shape), full2(kn.shape), full2(cqn.shape), full2(ckvn.shape),
            pl.BlockSpec((4, tab_a.shape[1], t), lambda b, j: (0, 0, j)),
            pl.BlockSpec((4, tab_b.shape[1], t), lambda b, j: (0, 0, j)),
        ],
        out_specs=[
            pl.BlockSpec((ns, 1, N_HEADS, HEAD_PAD, t), lambda b, j: (0, b, 0, 0, lat(j))),
            pl.BlockSpec((ns, 1, N_KSETS, t, HEAD_PAD), lambda b, j: (0, b, 0, j, 0)),
            pl.BlockSpec((ns, 1, V_ROWS, t), lambda b, j: (0, b, 0, j)),
        ],
        out_shape=[
            jax.ShapeDtypeStruct((ns, hb, N_HEADS, HEAD_PAD, s), _BF16),
            jax.ShapeDtypeStruct((ns, hb, N_KSETS, total, HEAD_PAD), _BF16),
            jax.ShapeDtypeStruct((ns, hb, V_ROWS, total), _BF16),
        ],
        compiler_params=pltpu.CompilerParams(
            dimension_semantics=("arbitrary", "arbitrary"), vmem_limit_bytes=VMEM_LIMIT),
        name="proj",
    )(mod, ctx.reshape(ns, hb, CTX_LEN, d), x.reshape(ns, hb, s, d),
      w_in_t, wuqt, wukvt, qn, kn, cqn, ckvn, tab_a, tab_b)
    return (q_all.reshape(bsz, N_HEADS, HEAD_PAD, s), k_all.reshape(bsz, N_KSETS, total, HEAD_PAD),
            v_all.reshape(bsz, V_ROWS, total))


def _attn_stream(base, q_ref, k_ref, v_ref, qn_ref, kn_ref, o_ref, o_rows, bufs, m0_ref, slot):
    tq, kc = ATTN_Q_TILE, ATTN_KEY_CHUNK
    n_tiles = q_ref.shape[3] // tq
    n_chunks = k_ref.shape[2] // kc
    assert n_tiles % 2 == 0
    assert n_tiles * tq == q_ref.shape[3] and n_chunks * kc == k_ref.shape[2]
    dv = v_ref.shape[1]
    ones = jnp.ones((ATTN_ONES_ROWS, kc), _BF16)

    def buf_rows(c):
        return pl.ds(pl.multiple_of(base + c * kc, kc), kc)

    def score_chunk(keys_ref, q_tile, buf, c, m):
        s = jnp.dot(keys_ref[0, 0, pl.ds(c * kc, kc), :], q_tile,
                    preferred_element_type=_F32)
        buf[buf_rows(c), :] = s
        cm = jnp.max(s, axis=0, keepdims=True)
        return cm if m is None else jnp.maximum(m, cm)

    def value_chunk(buf, c, m, acc):
        p = jnp.exp2(buf[buf_rows(c), :] - m)
        vext = jnp.concatenate([v_ref[0, :, pl.ds(c * kc, kc)], ones], axis=0)
        pv = jnp.dot(vext, p.astype(_BF16), preferred_element_type=_F32)
        return pv if acc is None else acc + pv

    @pl.when((pl.program_id(0) == 0) & (pl.program_id(1) == 0))
    def _():
        m = None
        for c in range(n_chunks):
            m = score_chunk(k_ref, q_ref[0, 0, :, 0:tq], bufs[0], c, m)
        m0_ref[slot] = m
    yield

    m_cur = m0_ref[slot]
    for t in range(n_tiles):
        m_next = acc = None
        for c in range(n_chunks):
            acc = value_chunk(bufs[t % 2], c, m_cur, acc)
            if t + 1 < n_tiles:
                m_next = score_chunk(k_ref, q_ref[0, 0, :, pl.ds((t + 1) * tq, tq)],
                                     bufs[(t + 1) % 2], c, m_next)
            else:
                m_next = score_chunk(kn_ref, qn_ref[0, 0], bufs[0], c, m_next)
            yield
        o_ref[0, o_rows, pl.ds(t * tq, tq)] = (acc[0:dv] / acc[dv:dv + 1]).astype(_BF16)
        m_cur = m_next
    m0_ref[slot] = m_cur


def _attn_kernel(base_ref, qa_ref, ka_ref, va_ref, qna_ref, kna_ref,
                 qb_ref, kb_ref, vb_ref, qnb_ref, knb_ref, o_ref,
                 sa0_ref, sa1_ref, sb0_ref, sb1_ref, m0_ref):
    base = base_ref[0]
    dv = va_ref.shape[1]
    first = _attn_stream(base, qa_ref, ka_ref, va_ref, qna_ref, kna_ref, o_ref, pl.ds(0, dv),
                         (sa0_ref, sa1_ref), m0_ref, 0)
    second = _attn_stream(base, qb_ref, kb_ref, vb_ref, qnb_ref, knb_ref, o_ref, pl.ds(dv, dv),
                          (sb0_ref, sb1_ref), m0_ref, 1)
    next(first), next(second)
    _trace(_stagger([first, second], lead=ATTN_STREAM_LEAD))


def _attention(q_all, k_all, v_all):
    bsz, _, _, s = q_all.shape
    total = k_all.shape[2]
    group = A_HEADS // A_KV_HEADS
    n_pairs = N_HEADS // 2
    kset = lambda h: jnp.where(h < A_HEADS, 0, h - A_HEADS + 1)
    vblk = lambda h: jnp.where(h < A_HEADS, h // group, h - A_HEADS + A_KV_HEADS)

    def nxt(b, g):
        n = jnp.minimum(b * n_pairs + g + 1, bsz * n_pairs - 1)
        return n // n_pairs, n % n_pairs

    def stream_specs(which):
        head = lambda g: 2 * g + which
        return [
            pl.BlockSpec((1, 1, HEAD_PAD, s), lambda b, g: (b, head(g), 0, 0)),
            pl.BlockSpec((1, 1, total, HEAD_PAD), lambda b, g: (b, kset(head(g)), 0, 0)),
            pl.BlockSpec((1, B_V_DIM, total), lambda b, g: (b, vblk(head(g)), 0)),
            pl.BlockSpec((1, 1, HEAD_PAD, ATTN_Q_TILE),
                         lambda b, g: (nxt(b, g)[0], head(nxt(b, g)[1]), 0, 0)),
            pl.BlockSpec((1, 1, total, HEAD_PAD),
                         lambda b, g: (nxt(b, g)[0], kset(head(nxt(b, g)[1])), 0, 0)),
        ]

    stream_args = (q_all, k_all, v_all, q_all, k_all)
    return pl.pallas_call(
        _attn_kernel,
        grid=(bsz, n_pairs),
        in_specs=[pl.BlockSpec(memory_space=pltpu.SMEM)] + stream_specs(0) + stream_specs(1),
        out_specs=pl.BlockSpec((1, 2 * B_V_DIM, s), lambda b, g: (b, g, 0)),
        out_shape=jax.ShapeDtypeStruct((bsz, N_HEADS * B_V_DIM, s), _BF16),
        scratch_shapes=[pltpu.VMEM((total, ATTN_Q_TILE), _F32)] * 4
        + [pltpu.VMEM((2, 1, ATTN_Q_TILE), _F32)],
        compiler_params=pltpu.CompilerParams(
            dimension_semantics=("arbitrary", "arbitrary"), vmem_limit_bytes=VMEM_LIMIT),
        name="attn",
    )(jnp.zeros((1,), jnp.int32), *stream_args, *stream_args)


def _merge_pieces(mrow, x, ot, win_ref, wpa_ref, wpb_ref, wo_ref, g_ref, b_ref, result):
    shift, scale, gate = mrow[:, 0:D_MODEL], mrow[:, D_MODEL:2 * D_MODEL], mrow[:, 2 * D_MODEL:3 * D_MODEL]
    h = (_layer_norm(x) * (1.0 + scale) + shift).astype(_BF16)
    yield
    na = A_HEADS * A_HEAD_DIM
    ys = []
    for r in range(D_MODEL // MERGE_ROW_CHUNK):
        ra = pl.ds(r * MERGE_ROW_CHUNK, MERGE_ROW_CHUNK)
        ga_rows = pl.ds(GATE_OFF + r * MERGE_ROW_CHUNK, MERGE_ROW_CHUNK)
        gb_rows = pl.ds(GATE_OFF + D_MODEL + r * MERGE_ROW_CHUNK, MERGE_ROW_CHUNK)
        ga = lax.dot_general(win_ref[ga_rows, :], h, _NT, preferred_element_type=_F32)
        gb = lax.dot_general(win_ref[gb_rows, :], h, _NT, preferred_element_type=_F32)
        ya = jnp.dot(wpa_ref[ra, :], ot[0:na], preferred_element_type=_F32)
        yb = jnp.dot(wpb_ref[ra, :], ot[na:], preferred_element_type=_F32)
        ys.append((_sigmoid(ga) * ya + _sigmoid(gb) * yb).astype(_BF16))
        yield
    zt = jnp.dot(wo_ref[...], jnp.concatenate(ys, axis=0), preferred_element_type=_F32)
    r = DEEPNORM_ALPHA * x + gate * zt.T
    result.append(_layer_norm(r) * g_ref[...] + b_ref[...])
    yield


def _ffn_pieces(mrow, x, wup_ref, wdn_ref, g_ref, b_ref, result):
    shift, scale, gate = mrow[:, 0:D_MODEL], mrow[:, D_MODEL:2 * D_MODEL], mrow[:, 2 * D_MODEL:3 * D_MODEL]
    h = (_layer_norm(x) * (1.0 + scale) + shift).astype(_BF16)
    yield
    fs = []
    for c in range(FFN_HIDDEN // FFN_COL_CHUNK):
        a = jnp.dot(h, wup_ref[:, pl.ds(c * FFN_COL_CHUNK, FFN_COL_CHUNK)], preferred_element_type=_F32)
        u = jnp.dot(h, wup_ref[:, pl.ds(FFN_HIDDEN + c * FFN_COL_CHUNK, FFN_COL_CHUNK)],
                    preferred_element_type=_F32)
        fs.append((a * _sigmoid(a) * u).astype(_BF16))
        yield
    dn = jnp.dot(jnp.concatenate(fs, axis=1), wdn_ref[...], preferred_element_type=_F32)
    r = DEEPNORM_ALPHA * x + gate * dn
    result.append(_layer_norm(r) * g_ref[...] + b_ref[...])
    yield


def _post_kernel(mod_ref, x_ref, ot_ref, win_ref, wpa_ref, wpb_ref, wo_ref, g1_ref, b1_ref,
                 wup_ref, wdn_ref, g2_ref, b2_ref, out_ref, x1_ref, *, tiles_per_batch):
    j = pl.program_id(0)
    n_tiles = pl.num_programs(0) - 1
    half_batch = n_tiles // tiles_per_batch

    @pl.when(j == 0)
    def _():
        x1_ref[...] = jnp.zeros_like(x1_ref)

    b_prev = jnp.maximum(j - 1, 0) // tiles_per_batch
    b_cur = jnp.minimum(j, n_tiles - 1) // tiles_per_batch
    ffn_out, merge_out = ([], []), ([], [])

    def stream(half):
        off = half * half_batch
        return _stagger(
            [_ffn_pieces(mod_ref[pl.ds(b_prev + off, 1), 3 * D_MODEL:6 * D_MODEL], x1_ref[half],
                         wup_ref, wdn_ref, g2_ref, b2_ref, ffn_out[half]),
             _merge_pieces(mod_ref[pl.ds(b_cur + off, 1), 0:3 * D_MODEL], x_ref[half, 0],
                           ot_ref[half, 0], win_ref, wpa_ref, wpb_ref, wo_ref, g1_ref, b1_ref,
                           merge_out[half])],
            rates=[POST_FFN_PIECES_PER_MERGE_PIECE, 1])

    _trace(_stagger([stream(0), stream(1)], lead=POST_STREAM_LEAD))
    for half in range(2):
        out_ref[half, 0] = ffn_out[half][0]
        x1_ref[half] = merge_out[half][0]


def _post(mod, x, ot, w_in_t, wpat, wpbt, wot, g1, b1, wup, wdn, g2, b2):
    bsz, s, d = x.shape
    t = TOKEN_TILE
    tpb = s // t
    hb = bsz // 2
    n_tiles = hb * tpb
    const = lambda a: pl.BlockSpec(a.shape, lambda j: (0,) * a.ndim, pipeline_mode=pl.Buffered(1))
    cur = lambda j: jnp.minimum(j, n_tiles - 1)
    prev = lambda j: jnp.maximum(j - 1, 0)
    out = pl.pallas_call(
        functools.partial(_post_kernel, tiles_per_batch=tpb),
        grid=(n_tiles + 1,),
        in_specs=[
            const(mod),
            pl.BlockSpec((2, 1, t, d), lambda j: (0, cur(j) // tpb, cur(j) % tpb, 0)),
            pl.BlockSpec((2, 1, ot.shape[1], t), lambda j: (0, cur(j) // tpb, 0, cur(j) % tpb)),
            const(w_in_t), const(wpat), const(wpbt), const(wot), const(g1), const(b1),
            const(wup), const(wdn), const(g2), const(b2),
        ],
        out_specs=pl.BlockSpec((2, 1, t, d), lambda j: (0, prev(j) // tpb, prev(j) % tpb, 0)),
        out_shape=jax.ShapeDtypeStruct((2, hb, s, d), _F32),
        scratch_shapes=[pltpu.VMEM((2, t, d), _F32)],
        compiler_params=pltpu.CompilerParams(
            dimension_semantics=("arbitrary",), vmem_limit_bytes=VMEM_LIMIT),
        name="post",
    )(mod, x.reshape(2, hb, s, d), ot.reshape(2, hb, ot.shape[1], s),
      w_in_t, wpat, wpbt, wot, g1, b1, wup, wdn, g2, b2)
    return out.reshape(bsz, s, d)


def _rope_tables(seq, half):
    pos = np.arange(seq)
    freqs = ROPE_THETA ** (-np.arange(half, dtype=np.float64) / half)
    tabs = []
    for p in (pos // GRID_W, pos % GRID_W):
        ang = p[None, :].astype(np.float64) * freqs[:, None]
        for fn, ident in ((np.cos, 1.0), (np.sin, 0.0)):
            tabs.append(np.concatenate([np.full((half, CTX_LEN), ident), fn(ang)], axis=1))
    return jnp.asarray(np.stack(tabs), dtype=_F32)


def kernel(x, c, ctx, c_ctx, w_mod, b_mod, w_in, q_norm_a, k_norm_a, cq_norm, ckv_norm,
           w_uq, w_ukv, w_proj_a, w_proj_b, w_out, ln1_g, ln1_b, w_up, w_down, ln2_g, ln2_b):
    bsz, s, d = x.shape
    assert w_mod.shape[0] == DEPTH == 1 and d == D_MODEL and ctx.shape[1] == CTX_LEN
    assert bsz == 8 and s % ATTN_Q_TILE == 0 and s % TOKEN_TILE == 0

    cc = jnp.concatenate([c, c_ctx[None, :], jnp.zeros((bsz - 1, d), _F32)], axis=0)
    mod = _modulation(cc, w_mod[0], b_mod[0][None, :])

    tbf = lambda w: w[0].astype(_BF16).T
    w_in_t = tbf(w_in)
    col = lambda v: v[0][:, None]
    tab_a = _rope_tables(s, A_HEAD_DIM // 4)
    tab_b = _rope_tables(s, B_ROPE_DIM // 4)

    q_all, k_all, v_all = _project(
        mod, ctx, x, w_in_t, tbf(w_uq), tbf(w_ukv),
        col(q_norm_a), col(k_norm_a), col(cq_norm), col(ckv_norm), tab_a, tab_b)
    ot = _attention(q_all, k_all, v_all)
    return _post(mod, x, ot, w_in_t, tbf(w_proj_a), tbf(w_proj_b), tbf(w_out), ln1_g, ln1_b,
                 w_up[0].astype(_BF16), w_down[0].astype(_BF16), ln2_g, ln2_b)
```

```python
import functools
import math

import numpy as np
import jax
import jax.numpy as jnp
from jax import lax
from jax.experimental import pallas as pl
from jax.experimental.pallas import tpu as pltpu

D_MODEL = 1024
GRID_W = 64
CTX_LEN = 256
ROPE_THETA = 10000.0
EPS = 1e-6

A_HEADS = 8
A_KV_HEADS = 2
A_HEAD_DIM = 64
B_HEADS = 8
B_Q_RANK = 384
B_KV_RANK = 256
B_NOPE_DIM = 64
B_ROPE_DIM = 32
B_V_DIM = 64
FFN_HIDDEN = 2816
DEPTH = 1

A_SCALE = A_HEAD_DIM ** -0.5
B_SCALE = (B_NOPE_DIM + B_ROPE_DIM) ** -0.5
DEEPNORM_ALPHA = (2.0 * DEPTH) ** 0.25
LOG2_E = math.log2(math.e)

QA_OFF = 0
KA_OFF = QA_OFF + A_HEADS * A_HEAD_DIM
VA_OFF = KA_OFF + A_KV_HEADS * A_HEAD_DIM
CQ_OFF = VA_OFF + A_KV_HEADS * A_HEAD_DIM
CKV_OFF = CQ_OFF + B_Q_RANK
KR_OFF = CKV_OFF + B_KV_RANK
GATE_OFF = KR_OFF + B_ROPE_DIM
QKV_COLS = GATE_OFF

N_HEADS = A_HEADS + B_HEADS
HEAD_PAD = 128
N_KSETS = 1 + B_HEADS
V_ROWS = A_KV_HEADS * A_HEAD_DIM + B_HEADS * B_V_DIM

MOD_COL_TILE = 1024
PROJ_TILE = 256
PROJ_ROW_CHUNK = 288
PROJ_STREAMS = 4
PROJ_STREAM_LEAD = 3
ATTN_Q_TILE = 512
ATTN_KEY_CHUNK = 256
ATTN_STREAM_LEAD = 18
ATTN_ONES_ROWS = 16
TOKEN_TILE = 256
MERGE_ROW_CHUNK = 256
FFN_COL_CHUNK = 256
POST_FFN_PIECES_PER_MERGE_PIECE = 2
POST_STREAM_LEAD = 10
VMEM_LIMIT = 56 * 1024 * 1024

_NT = (((1,), (1,)), ((), ()))
_F32 = jnp.float32
_BF16 = jnp.bfloat16


def _layer_norm(x):
    mu = jnp.mean(x, axis=-1, keepdims=True)
    xc = x - mu
    var = jnp.mean(xc * xc, axis=-1, keepdims=True)
    return xc * lax.rsqrt(var + EPS)


def _sigmoid(x):
    return jax.nn.sigmoid(x)


def _mod_kernel(c_ref, w_ref, b_ref, o_ref):
    c = c_ref[...]
    a = (c * _sigmoid(c)).astype(_BF16)
    o_ref[...] = jnp.dot(a, w_ref[...].astype(_BF16), preferred_element_type=_F32) + b_ref[...]


def _modulation(cc, w_mod, b_mod):
    rows, d = cc.shape
    n = w_mod.shape[1]
    return pl.pallas_call(
        _mod_kernel,
        grid=(n // MOD_COL_TILE,),
        in_specs=[
            pl.BlockSpec((rows, d), lambda i: (0, 0)),
            pl.BlockSpec((d, MOD_COL_TILE), lambda i: (0, i)),
            pl.BlockSpec((1, MOD_COL_TILE), lambda i: (0, i)),
        ],
        out_specs=pl.BlockSpec((rows, MOD_COL_TILE), lambda i: (0, i)),
        out_shape=jax.ShapeDtypeStruct((rows, n), _F32),
        compiler_params=pltpu.CompilerParams(
            dimension_semantics=("arbitrary",), vmem_limit_bytes=VMEM_LIMIT),
        name="mod",
    )(cc, w_mod, b_mod)


def _rms_rows(x, g):
    ms = jnp.mean(x * x, axis=0, keepdims=True)
    return x * lax.rsqrt(ms + EPS) * g


def _axial_rope_rows(x, half, tab):
    r1, r2 = x[0:half], x[half:2 * half]
    c1, c2 = x[2 * half:3 * half], x[3 * half:4 * half]
    cr, sr, cc, sc = tab[0], tab[1], tab[2], tab[3]
    return jnp.concatenate(
        [r1 * cr - r2 * sr, r1 * sr + r2 * cr, c1 * cc - c2 * sc, c1 * sc + c2 * cc], axis=0)


def _stagger(streams, lead=0, rates=None):
    rates = rates or [1] * len(streams)
    live = dict(enumerate(streams))
    rnd = 0
    while live:
        for i in sorted(live):
            if rnd < i * lead:
                continue
            for _ in range(rates[i]):
                if next(live[i], StopIteration) is StopIteration:
                    del live[i]
                    break
                yield
        rnd += 1


def _trace(pieces):
    for _ in pieces:
        pass


def _proj_pieces(mrow, xin, w1_ref, wuq_ref, wukv_ref, qn_ref, kn_ref, cqn_ref, ckvn_ref, ta, tb,
                 q_out, k_out, v_out):
    shift, scale = mrow[:, 0:D_MODEL], mrow[:, D_MODEL:2 * D_MODEL]
    h = (_layer_norm(xin) * (1.0 + scale) + shift).astype(_BF16)
    yield
    parts = []
    for r in range(QKV_COLS // PROJ_ROW_CHUNK):
        rows = pl.ds(r * PROJ_ROW_CHUNK, PROJ_ROW_CHUNK)
        parts.append(lax.dot_general(w1_ref[rows, :], h, _NT, preferred_element_type=_F32))
        yield
    pt = jnp.concatenate(parts, axis=0)
    t = pt.shape[1]
    z64 = jnp.zeros((A_HEAD_DIM, t), _F32)
    z32 = jnp.zeros((HEAD_PAD - B_NOPE_DIM - B_ROPE_DIM, t), _F32)

    qn = qn_ref[...] * (A_SCALE * LOG2_E)
    group = A_HEADS // A_KV_HEADS
    for hd in range(A_HEADS):
        xh = pt[QA_OFF + hd * A_HEAD_DIM:QA_OFF + (hd + 1) * A_HEAD_DIM]
        xh = _axial_rope_rows(_rms_rows(xh, qn), A_HEAD_DIM // 4, ta)
        full = jnp.concatenate([xh, z64] if hd // group == 0 else [z64, xh], axis=0)
        q_out[hd] = full.astype(_BF16)
        if hd % 4 == 3:
            yield

    kn = kn_ref[...]
    ka = [
        _axial_rope_rows(
            _rms_rows(pt[KA_OFF + g * A_HEAD_DIM:KA_OFF + (g + 1) * A_HEAD_DIM], kn),
            A_HEAD_DIM // 4, ta)
        for g in range(A_KV_HEADS)
    ]
    k_out[0] = jnp.concatenate(ka, axis=0).T.astype(_BF16)
    v_out[0:A_KV_HEADS * A_HEAD_DIM, :] = pt[VA_OFF:CQ_OFF].astype(_BF16)
    yield

    cq = _rms_rows(pt[CQ_OFF:CKV_OFF], cqn_ref[...]).astype(_BF16)
    qb = jnp.dot(wuq_ref[...], cq, preferred_element_type=_F32) * (B_SCALE * LOG2_E)
    yield
    qdim = B_NOPE_DIM + B_ROPE_DIM
    for hd in range(B_HEADS):
        blk = qb[hd * qdim:(hd + 1) * qdim]
        rp = _axial_rope_rows(blk[B_NOPE_DIM:qdim], B_ROPE_DIM // 4, tb)
        full = jnp.concatenate([blk[0:B_NOPE_DIM], rp, z32], axis=0)
        q_out[A_HEADS + hd] = full.astype(_BF16)
        if hd % 4 == 3:
            yield

    ckv = _rms_rows(pt[CKV_OFF:KR_OFF], ckvn_ref[...]).astype(_BF16)
    kv = jnp.dot(wukv_ref[...], ckv, preferred_element_type=_F32)
    kr = _axial_rope_rows(pt[KR_OFF:GATE_OFF], B_ROPE_DIM // 4, tb)
    yield
    kvdim = B_NOPE_DIM + B_V_DIM
    v_base = A_KV_HEADS * A_HEAD_DIM
    for hd in range(B_HEADS):
        kn_h = kv[hd * kvdim:hd * kvdim + B_NOPE_DIM]
        v_h = kv[hd * kvdim + B_NOPE_DIM:(hd + 1) * kvdim]
        v_out[v_base + hd * B_V_DIM:v_base + (hd + 1) * B_V_DIM, :] = v_h.astype(_BF16)
        k_out[1 + hd] = jnp.concatenate([kn_h, kr, z32], axis=0).T.astype(_BF16)
        if hd % 4 == 3:
            yield


def _proj_kernel(mod_ref, ctx_ref, x_ref, w1_ref, wuq_ref, wukv_ref,
                 qn_ref, kn_ref, cqn_ref, ckvn_ref, ta_ref, tb_ref,
                 q_out, k_out, v_out):
    b = pl.program_id(0)
    j = pl.program_id(1)
    batch_stride = pl.num_programs(0)
    is_ctx = j == 0
    ctx_row = mod_ref.shape[0] // 2
    ta = ta_ref[...]
    tb = tb_ref[...]

    def stream(i):
        xin = jnp.where(is_ctx, ctx_ref[i, 0], x_ref[i, 0])
        mrow = jnp.where(is_ctx, mod_ref[ctx_row:ctx_row + 1, 0:2 * D_MODEL],
                         mod_ref[pl.ds(b + i * batch_stride, 1), 0:2 * D_MODEL])
        return _proj_pieces(mrow, xin, w1_ref, wuq_ref, wukv_ref, qn_ref, kn_ref, cqn_ref, ckvn_ref,
                            ta, tb, q_out.at[i, 0], k_out.at[i, 0], v_out.at[i, 0])

    _trace(_stagger([stream(i) for i in range(PROJ_STREAMS)], lead=PROJ_STREAM_LEAD))


def _project(mod, ctx, x, w_in_t, wuqt, wukvt, qn, kn, cqn, ckvn, tab_a, tab_b):
    bsz, s, d = x.shape
    t = PROJ_TILE
    n_steps = (CTX_LEN + s) // t
    total = CTX_LEN + s
    ns = PROJ_STREAMS
    hb = bsz // ns
    lat = lambda j: jnp.maximum(j - 1, 0)
    full2 = lambda shape: pl.BlockSpec(shape, lambda b, j: (0, 0))
    q_all, k_all, v_all = pl.pallas_call(
        _proj_kernel,
        grid=(hb, n_steps),
        in_specs=[
            full2(mod.shape),
            pl.BlockSpec((ns, 1, CTX_LEN, d), lambda b, j: (0, b, 0, 0)),
            pl.BlockSpec((ns, 1, t, d), lambda b, j: (0, b, lat(j), 0)),
            full2((QKV_COLS, d)),
            full2(wuqt.shape), full2(wukvt.shape),
            full2(qn.shape), full2(kn.shape), full2(cqn.shape), full2(ckvn.shape),
            pl.BlockSpec((4, tab_a.shape[1], t), lambda b, j: (0, 0, j)),
            pl.BlockSpec((4, tab_b.shape[1], t), lambda b, j: (0, 0, j)),
        ],
        out_specs=[
            pl.BlockSpec((ns, 1, N_HEADS, HEAD_PAD, t), lambda b, j: (0, b, 0, 0, lat(j))),
            pl.BlockSpec((ns, 1, N_KSETS, t, HEAD_PAD), lambda b, j: (0, b, 0, j, 0)),
            pl.BlockSpec((ns, 1, V_ROWS, t), lambda b, j: (0, b, 0, j)),
        ],
        out_shape=[
            jax.ShapeDtypeStruct((ns, hb, N_HEADS, HEAD_PAD, s), _BF16),
            jax.ShapeDtypeStruct((ns, hb, N_KSETS, total, HEAD_PAD), _BF16),
            jax.ShapeDtypeStruct((ns, hb, V_ROWS, total), _BF16),
        ],
        compiler_params=pltpu.CompilerParams(
            dimension_semantics=("arbitrary", "arbitrary"), vmem_limit_bytes=VMEM_LIMIT),
        name="proj",
    )(mod, ctx.reshape(ns, hb, CTX_LEN, d), x.reshape(ns, hb, s, d),
      w_in_t, wuqt, wukvt, qn, kn, cqn, ckvn, tab_a, tab_b)
    return (q_all.reshape(bsz, N_HEADS, HEAD_PAD, s), k_all.reshape(bsz, N_KSETS, total, HEAD_PAD),
            v_all.reshape(bsz, V_ROWS, total))


def _attn_stream(base, q_ref, k_ref, v_ref, qn_ref, kn_ref, o_ref, o_rows, bufs, m0_ref, slot):
    tq, kc = ATTN_Q_TILE, ATTN_KEY_CHUNK
    n_tiles = q_ref.shape[3] // tq
    n_chunks = k_ref.shape[2] // kc
    assert n_tiles % 2 == 0
    assert n_tiles * tq == q_ref.shape[3] and n_chunks * kc == k_ref.shape[2]
    dv = v_ref.shape[1]
    ones = jnp.ones((ATTN_ONES_ROWS, kc), _BF16)

    def buf_rows(c):
        return pl.ds(pl.multiple_of(base + c * kc, kc), kc)

    def score_chunk(keys_ref, q_tile, buf, c, m):
        s = jnp.dot(keys_ref[0, 0, pl.ds(c * kc, kc), :], q_tile,
                    preferred_element_type=_F32)
        buf[buf_rows(c), :] = s
        cm = jnp.max(s, axis=0, keepdims=True)
        return cm if m is None else jnp.maximum(m, cm)

    def value_chunk(buf, c, m, acc):
        p = jnp.exp2(buf[buf_rows(c), :] - m)
        vext = jnp.concatenate([v_ref[0, :, pl.ds(c * kc, kc)], ones], axis=0)
        pv = jnp.dot(vext, p.astype(_BF16), preferred_element_type=_F32)
        return pv if acc is None else acc + pv

    @pl.when((pl.program_id(0) == 0) & (pl.program_id(1) == 0))
    def _():
        m = None
        for c in range(n_chunks):
            m = score_chunk(k_ref, q_ref[0, 0, :, 0:tq], bufs[0], c, m)
        m0_ref[slot] = m
    yield

    m_cur = m0_ref[slot]
    for t in range(n_tiles):
        m_next = acc = None
        for c in range(n_chunks):
            if t + 1 < n_tiles:
                m_next = score_chunk(k_ref, q_ref[0, 0, :, pl.ds((t + 1) * tq, tq)],
                                     bufs[(t + 1) % 2], c, m_next)
            else:
                m_next = score_chunk(kn_ref, qn_ref[0, 0], bufs[0], c, m_next)
            yield
            acc = value_chunk(bufs[t % 2], c, m_cur, acc)
            yield
        o_ref[0, o_rows, pl.ds(t * tq, tq)] = (acc[0:dv] / acc[dv:dv + 1]).astype(_BF16)
        m_cur = m_next
    m0_ref[slot] = m_cur


def _attn_kernel(base_ref, qa_ref, ka_ref, va_ref, qna_ref, kna_ref,
                 qb_ref, kb_ref, vb_ref, qnb_ref, knb_ref, o_ref,
                 sa0_ref, sa1_ref, sb0_ref, sb1_ref, m0_ref):
    base = base_ref[0]
    dv = va_ref.shape[1]
    first = _attn_stream(base, qa_ref, ka_ref, va_ref, qna_ref, kna_ref, o_ref, pl.ds(0, dv),
                         (sa0_ref, sa1_ref), m0_ref, 0)
    second = _attn_stream(base, qb_ref, kb_ref, vb_ref, qnb_ref, knb_ref, o_ref, pl.ds(dv, dv),
                          (sb0_ref, sb1_ref), m0_ref, 1)
    next(first), next(second)
    _trace(_stagger([first, second], lead=ATTN_STREAM_LEAD))


def _attention(q_all, k_all, v_all):
    bsz, _, _, s = q_all.shape
    total = k_all.shape[2]
    group = A_HEADS // A_KV_HEADS
    n_pairs = N_HEADS // 2
    kset = lambda h: jnp.where(h < A_HEADS, 0, h - A_HEADS + 1)
    vblk = lambda h: jnp.where(h < A_HEADS, h // group, h - A_HEADS + A_KV_HEADS)

    def nxt(b, g):
        n = jnp.minimum(b * n_pairs + g + 1, bsz * n_pairs - 1)
        return n // n_pairs, n % n_pairs

    def stream_specs(which):
        head = lambda g: 2 * g + which
        return [
            pl.BlockSpec((1, 1, HEAD_PAD, s), lambda b, g: (b, head(g), 0, 0)),
            pl.BlockSpec((1, 1, total, HEAD_PAD), lambda b, g: (b, kset(head(g)), 0, 0)),
            pl.BlockSpec((1, B_V_DIM, total), lambda b, g: (b, vblk(head(g)), 0)),
            pl.BlockSpec((1, 1, HEAD_PAD, ATTN_Q_TILE),
                         lambda b, g: (nxt(b, g)[0], head(nxt(b, g)[1]), 0, 0)),
            pl.BlockSpec((1, 1, total, HEAD_PAD),
                         lambda b, g: (nxt(b, g)[0], kset(head(nxt(b, g)[1])), 0, 0)),
        ]

    stream_args = (q_all, k_all, v_all, q_all, k_all)
    return pl.pallas_call(
        _attn_kernel,
        grid=(bsz, n_pairs),
        in_specs=[pl.BlockSpec(memory_space=pltpu.SMEM)] + stream_specs(0) + stream_specs(1),
        out_specs=pl.BlockSpec((1, 2 * B_V_DIM, s), lambda b, g: (b, g, 0)),
        out_shape=jax.ShapeDtypeStruct((bsz, N_HEADS * B_V_DIM, s), _BF16),
        scratch_shapes=[pltpu.VMEM((total, ATTN_Q_TILE), _F32)] * 4
        + [pltpu.VMEM((2, 1, ATTN_Q_TILE), _F32)],
        compiler_params=pltpu.CompilerParams(
            dimension_semantics=("arbitrary", "arbitrary"), vmem_limit_bytes=VMEM_LIMIT),
        name="attn",
    )(jnp.zeros((1,), jnp.int32), *stream_args, *stream_args)


def _merge_pieces(mrow, x, ot, win_ref, wpa_ref, wpb_ref, wo_ref, g_ref, b_ref, result):
    shift, scale, gate = mrow[:, 0:D_MODEL], mrow[:, D_MODEL:2 * D_MODEL], mrow[:, 2 * D_MODEL:3 * D_MODEL]
    h = (_layer_norm(x) * (1.0 + scale) + shift).astype(_BF16)
    yield
    na = A_HEADS * A_HEAD_DIM
    ys = []
    for r in range(D_MODEL // MERGE_ROW_CHUNK):
        ra = pl.ds(r * MERGE_ROW_CHUNK, MERGE_ROW_CHUNK)
        ga_rows = pl.ds(GATE_OFF + r * MERGE_ROW_CHUNK, MERGE_ROW_CHUNK)
        gb_rows = pl.ds(GATE_OFF + D_MODEL + r * MERGE_ROW_CHUNK, MERGE_ROW_CHUNK)
        ga = lax.dot_general(win_ref[ga_rows, :], h, _NT, preferred_element_type=_F32)
        gb = lax.dot_general(win_ref[gb_rows, :], h, _NT, preferred_element_type=_F32)
        ya = jnp.dot(wpa_ref[ra, :], ot[0:na], preferred_element_type=_F32)
        yb = jnp.dot(wpb_ref[ra, :], ot[na:], preferred_element_type=_F32)
        ys.append((_sigmoid(ga) * ya + _sigmoid(gb) * yb).astype(_BF16))
        yield
    zt = jnp.dot(wo_ref[...], jnp.concatenate(ys, axis=0), preferred_element_type=_F32)
    r = DEEPNORM_ALPHA * x + gate * zt.T
    result.append(_layer_norm(r) * g_ref[...] + b_ref[...])
    yield


def _ffn_pieces(mrow, x, wup_ref, wdn_ref, g_ref, b_ref, result):
    shift, scale, gate = mrow[:, 0:D_MODEL], mrow[:, D_MODEL:2 * D_MODEL], mrow[:, 2 * D_MODEL:3 * D_MODEL]
    h = (_layer_norm(x) * (1.0 + scale) + shift).astype(_BF16)
    yield
    fs = []
    for c in range(FFN_HIDDEN // FFN_COL_CHUNK):
        a = jnp.dot(h, wup_ref[:, pl.ds(c * FFN_COL_CHUNK, FFN_COL_CHUNK)], preferred_element_type=_F32)
        u = jnp.dot(h, wup_ref[:, pl.ds(FFN_HIDDEN + c * FFN_COL_CHUNK, FFN_COL_CHUNK)],
                    preferred_element_type=_F32)
        fs.append((a * _sigmoid(a) * u).astype(_BF16))
        yield
    dn = jnp.dot(jnp.concatenate(fs, axis=1), wdn_ref[...], preferred_element_type=_F32)
    r = DEEPNORM_ALPHA * x + gate * dn
    result.append(_layer_norm(r) * g_ref[...] + b_ref[...])
    yield


def _post_kernel(mod_ref, x_ref, ot_ref, win_ref, wpa_ref, wpb_ref, wo_ref, g1_ref, b1_ref,
                 wup_ref, wdn_ref, g2_ref, b2_ref, out_ref, x1_ref, *, tiles_per_batch):
    j = pl.program_id(0)
    n_tiles = pl.num_programs(0) - 1
    half_batch = n_tiles // tiles_per_batch

    @pl.when(j == 0)
    def _():
        x1_ref[...] = jnp.zeros_like(x1_ref)

    b_prev = jnp.maximum(j - 1, 0) // tiles_per_batch
    b_cur = jnp.minimum(j, n_tiles - 1) // tiles_per_batch
    ffn_out, merge_out = ([], []), ([], [])

    def stream(half):
        off = half * half_batch
        return _stagger(
            [_ffn_pieces(mod_ref[pl.ds(b_prev + off, 1), 3 * D_MODEL:6 * D_MODEL], x1_ref[half],
                         wup_ref, wdn_ref, g2_ref, b2_ref, ffn_out[half]),
             _merge_pieces(mod_ref[pl.ds(b_cur + off, 1), 0:3 * D_MODEL], x_ref[half, 0],
                           ot_ref[half, 0], win_ref, wpa_ref, wpb_ref, wo_ref, g1_ref, b1_ref,
                           merge_out[half])],
            rates=[POST_FFN_PIECES_PER_MERGE_PIECE, 1])

    _trace(_stagger([stream(0), stream(1)], lead=POST_STREAM_LEAD))
    for half in range(2):
        out_ref[half, 0] = ffn_out[half][0]
        x1_ref[half] = merge_out[half][0]


def _post(mod, x, ot, w_in_t, wpat, wpbt, wot, g1, b1, wup, wdn, g2, b2):
    bsz, s, d = x.shape
    t = TOKEN_TILE
    tpb = s // t
    hb = bsz // 2
    n_tiles = hb * tpb
    const = lambda a: pl.BlockSpec(a.shape, lambda j: (0,) * a.ndim, pipeline_mode=pl.Buffered(1))
    cur = lambda j: jnp.minimum(j, n_tiles - 1)
    prev = lambda j: jnp.maximum(j - 1, 0)
    out = pl.pallas_call(
        functools.partial(_post_kernel, tiles_per_batch=tpb),
        grid=(n_tiles + 1,),
        in_specs=[
            const(mod),
            pl.BlockSpec((2, 1, t, d), lambda j: (0, cur(j) // tpb, cur(j) % tpb, 0)),
            pl.BlockSpec((2, 1, ot.shape[1], t), lambda j: (0, cur(j) // tpb, 0, cur(j) % tpb)),
            const(w_in_t), const(wpat), const(wpbt), const(wot), const(g1), const(b1),
            const(wup), const(wdn), const(g2), const(b2),
        ],
        out_specs=pl.BlockSpec((2, 1, t, d), lambda j: (0, prev(j) // tpb, prev(j) % tpb, 0)),
        out_shape=jax.ShapeDtypeStruct((2, hb, s, d), _F32),
        scratch_shapes=[pltpu.VMEM((2, t, d), _F32)],
        compiler_params=pltpu.CompilerParams(
            dimension_semantics=("arbitrary",), vmem_limit_bytes=VMEM_LIMIT),
        name="post",
    )(mod, x.reshape(2, hb, s, d), ot.reshape(2, hb, ot.shape[1], s),
      w_in_t, wpat, wpbt, wot, g1, b1, wup, wdn, g2, b2)
    return out.reshape(bsz, s, d)


def _rope_tables(seq, half):
    pos = np.arange(seq)
    freqs = ROPE_THETA ** (-np.arange(half, dtype=np.float64) / half)
    tabs = []
    for p in (pos // GRID_W, pos % GRID_W):
        ang = p[None, :].astype(np.float64) * freqs[:, None]
        for fn, ident in ((np.cos, 1.0), (np.sin, 0.0)):
            tabs.append(np.concatenate([np.full((half, CTX_LEN), ident), fn(ang)], axis=1))
    return jnp.asarray(np.stack(tabs), dtype=_F32)


def kernel(x, c, ctx, c_ctx, w_mod, b_mod, w_in, q_norm_a, k_norm_a, cq_norm, ckv_norm,
           w_uq, w_ukv, w_proj_a, w_proj_b, w_out, ln1_g, ln1_b, w_up, w_down, ln2_g, ln2_b):
    bsz, s, d = x.shape
    assert w_mod.shape[0] == DEPTH == 1 and d == D_MODEL and ctx.shape[1] == CTX_LEN
    assert bsz == 8 and s % ATTN_Q_TILE == 0 and s % TOKEN_TILE == 0

    cc = jnp.concatenate([c, c_ctx[None, :], jnp.zeros((bsz - 1, d), _F32)], axis=0)
    mod = _modulation(cc, w_mod[0], b_mod[0][None, :])

    tbf = lambda w: w[0].astype(_BF16).T
    w_in_t = tbf(w_in)
    col = lambda v: v[0][:, None]
    tab_a = _rope_tables(s, A_HEAD_DIM // 4)
    tab_b = _rope_tables(s, B_ROPE_DIM // 4)

    q_all, k_all, v_all = _project(
        mod, ctx, x, w_in_t, tbf(w_uq), tbf(w_ukv),
        col(q_norm_a), col(k_norm_a), col(cq_norm), col(ckv_norm), tab_a, tab_b)
    ot = _attention(q_all, k_all, v_all)
    return _post(mod, x, ot, w_in_t, tbf(w_proj_a), tbf(w_proj_b), tbf(w_out), ln1_g, ln1_b,
                 w_up[0].astype(_BF16), w_down[0].astype(_BF16), ln2_g, ln2_b)
```

```python
import functools
import math

import numpy as np
import jax
import jax.numpy as jnp
from jax import lax
from jax.experimental import pallas as pl
from jax.experimental.pallas import tpu as pltpu

D_MODEL = 1024
GRID_W = 64
CTX_LEN = 256
ROPE_THETA = 10000.0
EPS = 1e-6

A_HEADS = 8
A_KV_HEADS = 2
A_HEAD_DIM = 64
B_HEADS = 8
B_Q_RANK = 384
B_KV_RANK = 256
B_NOPE_DIM = 64
B_ROPE_DIM = 32
B_V_DIM = 64
FFN_HIDDEN = 2816
DEPTH = 1

A_SCALE = A_HEAD_DIM ** -0.5
B_SCALE = (B_NOPE_DIM + B_ROPE_DIM) ** -0.5
DEEPNORM_ALPHA = (2.0 * DEPTH) ** 0.25
LOG2_E = math.log2(math.e)

QA_OFF = 0
KA_OFF = QA_OFF + A_HEADS * A_HEAD_DIM
VA_OFF = KA_OFF + A_KV_HEADS * A_HEAD_DIM
CQ_OFF = VA_OFF + A_KV_HEADS * A_HEAD_DIM
CKV_OFF = CQ_OFF + B_Q_RANK
KR_OFF = CKV_OFF + B_KV_RANK
GATE_OFF = KR_OFF + B_ROPE_DIM
QKV_COLS = GATE_OFF

N_HEADS = A_HEADS + B_HEADS
HEAD_PAD = 128
N_KSETS = 1 + B_HEADS
V_ROWS = A_KV_HEADS * A_HEAD_DIM + B_HEADS * B_V_DIM

MOD_COL_TILE = 1024
PROJ_TILE = 256
PROJ_ROW_CHUNK = 288
PROJ_STREAMS = 4
PROJ_STREAM_LEAD = 3
ATTN_Q_TILE = 512
ATTN_KEY_CHUNK = 256
ATTN_STREAM_LEAD = 10
ATTN_ONES_ROWS = 16
TOKEN_TILE = 256
MERGE_ROW_CHUNK = 256
FFN_COL_CHUNK = 256
POST_FFN_PIECES_PER_MERGE_PIECE = 2
POST_STREAM_LEAD = 10
VMEM_LIMIT = 56 * 1024 * 1024

_NT = (((1,), (1,)), ((), ()))
_F32 = jnp.float32
_BF16 = jnp.bfloat16


def _layer_norm(x):
    mu = jnp.mean(x, axis=-1, keepdims=True)
    xc = x - mu
    var = jnp.mean(xc * xc, axis=-1, keepdims=True)
    return xc * lax.rsqrt(var + EPS)


def _sigmoid(x):
    return jax.nn.sigmoid(x)


def _mod_kernel(c_ref, w_ref, b_ref, o_ref):
    c = c_ref[...]
    a = (c * _sigmoid(c)).astype(_BF16)
    o_ref[...] = jnp.dot(a, w_ref[...].astype(_BF16), preferred_element_type=_F32) + b_ref[...]


def _modulation(cc, w_mod, b_mod):
    rows, d = cc.shape
    n = w_mod.shape[1]
    return pl.pallas_call(
        _mod_kernel,
        grid=(n // MOD_COL_TILE,),
        in_specs=[
            pl.BlockSpec((rows, d), lambda i: (0, 0)),
            pl.BlockSpec((d, MOD_COL_TILE), lambda i: (0, i)),
            pl.BlockSpec((1, MOD_COL_TILE), lambda i: (0, i)),
        ],
        out_specs=pl.BlockSpec((rows, MOD_COL_TILE), lambda i: (0, i)),
        out_shape=jax.ShapeDtypeStruct((rows, n), _F32),
        compiler_params=pltpu.CompilerParams(
            dimension_semantics=("arbitrary",), vmem_limit_bytes=VMEM_LIMIT),
        name="mod",
    )(cc, w_mod, b_mod)


def _rms_rows(x, g):
    ms = jnp.mean(x * x, axis=0, keepdims=True)
    return x * lax.rsqrt(ms + EPS) * g


def _axial_rope_rows(x, half, tab):
    r1, r2 = x[0:half], x[half:2 * half]
    c1, c2 = x[2 * half:3 * half], x[3 * half:4 * half]
    cr, sr, cc, sc = tab[0], tab[1], tab[2], tab[3]
    return jnp.concatenate(
        [r1 * cr - r2 * sr, r1 * sr + r2 * cr, c1 * cc - c2 * sc, c1 * sc + c2 * cc], axis=0)


def _stagger(streams, lead=0, rates=None):
    rates = rates or [1] * len(streams)
    live = dict(enumerate(streams))
    rnd = 0
    while live:
        for i in sorted(live):
            if rnd < i * lead:
                continue
            for _ in range(rates[i]):
                if next(live[i], StopIteration) is StopIteration:
                    del live[i]
                    break
                yield
        rnd += 1


def _trace(pieces):
    for _ in pieces:
        pass


def _proj_pieces(mrow, xin, w1_ref, wuq_ref, wukv_ref, qn_ref, kn_ref, cqn_ref, ckvn_ref, ta, tb,
                 q_out, k_out, v_out):
    shift, scale = mrow[:, 0:D_MODEL], mrow[:, D_MODEL:2 * D_MODEL]
    h = (_layer_norm(xin) * (1.0 + scale) + shift).astype(_BF16)
    yield
    parts = []
    for r in range(QKV_COLS // PROJ_ROW_CHUNK):
        rows = pl.ds(r * PROJ_ROW_CHUNK, PROJ_ROW_CHUNK)
        parts.append(lax.dot_general(w1_ref[rows, :], h, _NT, preferred_element_type=_F32))
        yield
    pt = jnp.concatenate(parts, axis=0)
    t = pt.shape[1]
    z64 = jnp.zeros((A_HEAD_DIM, t), _F32)
    z32 = jnp.zeros((HEAD_PAD - B_NOPE_DIM - B_ROPE_DIM, t), _F32)

    qn = qn_ref[...] * (A_SCALE * LOG2_E)
    group = A_HEADS // A_KV_HEADS
    for hd in range(A_HEADS):
        xh = pt[QA_OFF + hd * A_HEAD_DIM:QA_OFF + (hd + 1) * A_HEAD_DIM]
        xh = _axial_rope_rows(_rms_rows(xh, qn), A_HEAD_DIM // 4, ta)
        full = jnp.concatenate([xh, z64] if hd // group == 0 else [z64, xh], axis=0)
        q_out[hd] = full.astype(_BF16)
        if hd % 4 == 3:
            yield

    kn = kn_ref[...]
    ka = [
        _axial_rope_rows(
            _rms_rows(pt[KA_OFF + g * A_HEAD_DIM:KA_OFF + (g + 1) * A_HEAD_DIM], kn),
            A_HEAD_DIM // 4, ta)
        for g in range(A_KV_HEADS)
    ]
    k_out[0] = jnp.concatenate(ka, axis=0).T.astype(_BF16)
    v_out[0:A_KV_HEADS * A_HEAD_DIM, :] = pt[VA_OFF:CQ_OFF].astype(_BF16)
    yield

    cq = _rms_rows(pt[CQ_OFF:CKV_OFF], cqn_ref[...]).astype(_BF16)
    qb = jnp.dot(wuq_ref[...], cq, preferred_element_type=_F32) * (B_SCALE * LOG2_E)
    yield
    qdim = B_NOPE_DIM + B_ROPE_DIM
    for hd in range(B_HEADS):
        blk = qb[hd * qdim:(hd + 1) * qdim]
        rp = _axial_rope_rows(blk[B_NOPE_DIM:qdim], B_ROPE_DIM // 4, tb)
        full = jnp.concatenate([blk[0:B_NOPE_DIM], rp, z32], axis=0)
        q_out[A_HEADS + hd] = full.astype(_BF16)
        if hd % 4 == 3:
            yield

    ckv = _rms_rows(pt[CKV_OFF:KR_OFF], ckvn_ref[...]).astype(_BF16)
    kv = jnp.dot(wukv_ref[...], ckv, preferred_element_type=_F32)
    kr = _axial_rope_rows(pt[KR_OFF:GATE_OFF], B_ROPE_DIM // 4, tb)
    yield
    kvdim = B_NOPE_DIM + B_V_DIM
    v_base = A_KV_HEADS * A_HEAD_DIM
    for hd in range(B_HEADS):
        kn_h = kv[hd * kvdim:hd * kvdim + B_NOPE_DIM]
        v_h = kv[hd * kvdim + B_NOPE_DIM:(hd + 1) * kvdim]
        v_out[v_base + hd * B_V_DIM:v_base + (hd + 1) * B_V_DIM, :] = v_h.astype(_BF16)
        k_out[1 + hd] = jnp.concatenate([kn_h, kr, z32], axis=0).T.astype(_BF16)
        if hd % 4 == 3:
            yield


def _proj_kernel(mod_ref, ctx_ref, x_ref, w1_ref, wuq_ref, wukv_ref,
                 qn_ref, kn_ref, cqn_ref, ckvn_ref, ta_ref, tb_ref,
                 q_out, k_out, v_out):
    b = pl.program_id(0)
    j = pl.program_id(1)
    batch_stride = pl.num_programs(0)
    is_ctx = j == 0
    ctx_row = mod_ref.shape[0] // 2
    ta = ta_ref[...]
    tb = tb_ref[...]

    def stream(i):
        xin = jnp.where(is_ctx, ctx_ref[i, 0], x_ref[i, 0])
        mrow = jnp.where(is_ctx, mod_ref[ctx_row:ctx_row + 1, 0:2 * D_MODEL],
                         mod_ref[pl.ds(b + i * batch_stride, 1), 0:2 * D_MODEL])
        return _proj_pieces(mrow, xin, w1_ref, wuq_ref, wukv_ref, qn_ref, kn_ref, cqn_ref, ckvn_ref,
                            ta, tb, q_out.at[i, 0], k_out.at[i, 0], v_out.at[i, 0])

    _trace(_stagger([stream(i) for i in range(PROJ_STREAMS)], lead=PROJ_STREAM_LEAD))


def _project(mod, ctx, x, w_in_t, wuqt, wukvt, qn, kn, cqn, ckvn, tab_a, tab_b):
    bsz, s, d = x.shape
    t = PROJ_TILE
    n_steps = (CTX_LEN + s) // t
    total = CTX_LEN + s
    ns = PROJ_STREAMS
    hb = bsz // ns
    lat = lambda j: jnp.maximum(j - 1, 0)
    full2 = lambda shape: pl.BlockSpec(shape, lambda b, j: (0, 0))
    q_all, k_all, v_all = pl.pallas_call(
        _proj_kernel,
        grid=(hb, n_steps),
        in_specs=[
            full2(mod.shape),
            pl.BlockSpec((ns, 1, CTX_LEN, d), lambda b, j: (0, b, 0, 0)),
            pl.BlockSpec((ns, 1, t, d), lambda b, j: (0, b, lat(j), 0)),
            full2((QKV_COLS, d)),
            full2(wuqt.shape), full2(wukvt.shape),
            full2(qn.shape), full2(kn.shape), full2(cqn.shape), full2(ckvn.shape),
            pl.BlockSpec((4, tab_a.shape[1], t), lambda b, j: (0, 0, j)),
            pl.BlockSpec((4, tab_b.shape[1], t), lambda b, j: (0, 0, j)),
        ],
        out_specs=[
            pl.BlockSpec((ns, 1, N_HEADS, HEAD_PAD, t), lambda b, j: (0, b, 0, 0, lat(j))),
            pl.BlockSpec((ns, 1, N_KSETS, t, HEAD_PAD), lambda b, j: (0, b, 0, j, 0)),
            pl.BlockSpec((ns, 1, V_ROWS, t), lambda b, j: (0, b, 0, j)),
        ],
        out_shape=[
            jax.ShapeDtypeStruct((ns, hb, N_HEADS, HEAD_PAD, s), _BF16),
            jax.ShapeDtypeStruct((ns, hb, N_KSETS, total, HEAD_PAD), _BF16),
            jax.ShapeDtypeStruct((ns, hb, V_ROWS, total), _BF16),
        ],
        compiler_params=pltpu.CompilerParams(
            dimension_semantics=("arbitrary", "arbitrary"), vmem_limit_bytes=VMEM_LIMIT),
        name="proj",
    )(mod, ctx.reshape(ns, hb, CTX_LEN, d), x.reshape(ns, hb, s, d),
      w_in_t, wuqt, wukvt, qn, kn, cqn, ckvn, tab_a, tab_b)
    return (q_all.reshape(bsz, N_HEADS, HEAD_PAD, s), k_all.reshape(bsz, N_KSETS, total, HEAD_PAD),
            v_all.reshape(bsz, V_ROWS, total))


def _attn_stream(base, q_ref, k_ref, v_ref, qn_ref, kn_ref, o_ref, o_rows, bufs, m0_ref, slot):
    tq, kc = ATTN_Q_TILE, ATTN_KEY_CHUNK
    n_tiles = q_ref.shape[3] // tq
    n_chunks = k_ref.shape[2] // kc
    assert n_tiles % 2 == 0
    assert n_tiles * tq == q_ref.shape[3] and n_chunks * kc == k_ref.shape[2]
    dv = v_ref.shape[1]
    ones = jnp.ones((ATTN_ONES_ROWS, kc), _BF16)

    def buf_rows(c):
        return pl.ds(pl.multiple_of(base + c * kc, kc), kc)

    def score_chunk(keys_ref, q_tile, buf, c, m):
        s = jnp.dot(keys_ref[0, 0, pl.ds(c * kc, kc), :], q_tile,
                    preferred_element_type=_F32)
        buf[buf_rows(c), :] = s
        cm = jnp.max(s, axis=0, keepdims=True)
        return cm if m is None else jnp.maximum(m, cm)

    def value_chunk(buf, c, m, acc):
        p = jnp.exp2(buf[buf_rows(c), :] - m)
        vext = jnp.concatenate([v_ref[0, :, pl.ds(c * kc, kc)], ones], axis=0)
        pv = jnp.dot(vext, p.astype(_BF16), preferred_element_type=_F32)
        return pv if acc is None else acc + pv

    @pl.when((pl.program_id(0) == 0) & (pl.program_id(1) == 0))
    def _():
        m = None
        for c in range(n_chunks):
            m = score_chunk(k_ref, q_ref[0, 0, :, 0:tq], bufs[0], c, m)
        m0_ref[slot] = m
    yield

    m_cur = m0_ref[slot]
    for t in range(n_tiles):
        m_next = acc = None
        for c in range(n_chunks):
            if t + 1 < n_tiles:
                m_next = score_chunk(k_ref, q_ref[0, 0, :, pl.ds((t + 1) * tq, tq)],
                                     bufs[(t + 1) % 2], c, m_next)
            else:
                m_next = score_chunk(kn_ref, qn_ref[0, 0], bufs[0], c, m_next)
            acc = value_chunk(bufs[t % 2], c, m_cur, acc)
            yield
        o_ref[0, o_rows, pl.ds(t * tq, tq)] = (acc[0:dv] / acc[dv:dv + 1]).astype(_BF16)
        m_cur = m_next
    m0_ref[slot] = m_cur


def _attn_kernel(base_ref, qa_ref, ka_ref, va_ref, qna_ref, kna_ref,
                 qb_ref, kb_ref, vb_ref, qnb_ref, knb_ref, o_ref,
                 sa0_ref, sa1_ref, sb0_ref, sb1_ref, m0_ref):
    base = base_ref[0]
    dv = va_ref.shape[1]
    first = _attn_stream(base, qa_ref, ka_ref, va_ref, qna_ref, kna_ref, o_ref, pl.ds(0, dv),
                         (sa0_ref, sa1_ref), m0_ref, 0)
    second = _attn_stream(base, qb_ref, kb_ref, vb_ref, qnb_ref, knb_ref, o_ref, pl.ds(dv, dv),
                          (sb0_ref, sb1_ref), m0_ref, 1)
    next(first), next(second)
    _trace(_stagger([first, second], lead=ATTN_STREAM_LEAD))


def _attention(q_all, k_all, v_all):
    bsz, _, _, s = q_all.shape
    total = k_all.shape[2]
    group = A_HEADS // A_KV_HEADS
    n_pairs = N_HEADS // 2
    kset = lambda h: jnp.where(h < A_HEADS, 0, h - A_HEADS + 1)
    vblk = lambda h: jnp.where(h < A_HEADS, h // group, h - A_HEADS + A_KV_HEADS)

    def nxt(b, g):
        n = jnp.minimum(b * n_pairs + g + 1, bsz * n_pairs - 1)
        return n // n_pairs, n % n_pairs

    def stream_specs(which):
        head = lambda g: 2 * g + which
        return [
            pl.BlockSpec((1, 1, HEAD_PAD, s), lambda b, g: (b, head(g), 0, 0)),
            pl.BlockSpec((1, 1, total, HEAD_PAD), lambda b, g: (b, kset(head(g)), 0, 0)),
            pl.BlockSpec((1, B_V_DIM, total), lambda b, g: (b, vblk(head(g)), 0)),
            pl.BlockSpec((1, 1, HEAD_PAD, ATTN_Q_TILE),
                         lambda b, g: (nxt(b, g)[0], head(nxt(b, g)[1]), 0, 0)),
            pl.BlockSpec((1, 1, total, HEAD_PAD),
                         lambda b, g: (nxt(b, g)[0], kset(head(nxt(b, g)[1])), 0, 0)),
        ]

    stream_args = (q_all, k_all, v_all, q_all, k_all)
    return pl.pallas_call(
        _attn_kernel,
        grid=(bsz, n_pairs),
        in_specs=[pl.BlockSpec(memory_space=pltpu.SMEM)] + stream_specs(0) + stream_specs(1),
        out_specs=pl.BlockSpec((1, 2 * B_V_DIM, s), lambda b, g: (b, g, 0)),
        out_shape=jax.ShapeDtypeStruct((bsz, N_HEADS * B_V_DIM, s), _BF16),
        scratch_shapes=[pltpu.VMEM((total, ATTN_Q_TILE), _F32)] * 4
        + [pltpu.VMEM((2, 1, ATTN_Q_TILE), _F32)],
        compiler_params=pltpu.CompilerParams(
            dimension_semantics=("arbitrary", "arbitrary"), vmem_limit_bytes=VMEM_LIMIT),
        name="attn",
    )(jnp.zeros((1,), jnp.int32), *stream_args, *stream_args)


def _merge_pieces(mrow, x, ot, win_ref, wpa_ref, wpb_ref, wo_ref, g_ref, b_ref, result):
    shift, scale, gate = mrow[:, 0:D_MODEL], mrow[:, D_MODEL:2 * D_MODEL], mrow[:, 2 * D_MODEL:3 * D_MODEL]
    h = (_layer_norm(x) * (1.0 + scale) + shift).astype(_BF16)
    yield
    na = A_HEADS * A_HEAD_DIM
    ys = []
    for r in range(D_MODEL // MERGE_ROW_CHUNK):
        ra = pl.ds(r * MERGE_ROW_CHUNK, MERGE_ROW_CHUNK)
        ga_rows = pl.ds(GATE_OFF + r * MERGE_ROW_CHUNK, MERGE_ROW_CHUNK)
        gb_rows = pl.ds(GATE_OFF + D_MODEL + r * MERGE_ROW_CHUNK, MERGE_ROW_CHUNK)
        ga = lax.dot_general(win_ref[ga_rows, :], h, _NT, preferred_element_type=_F32)
        gb = lax.dot_general(win_ref[gb_rows, :], h, _NT, preferred_element_type=_F32)
        ya = jnp.dot(wpa_ref[ra, :], ot[0:na], preferred_element_type=_F32)
        yb = jnp.dot(wpb_ref[ra, :], ot[na:], preferred_element_type=_F32)
        ys.append((_sigmoid(ga) * ya + _sigmoid(gb) * yb).astype(_BF16))
        yield
    zt = jnp.dot(wo_ref[...], jnp.concatenate(ys, axis=0), preferred_element_type=_F32)
    r = DEEPNORM_ALPHA * x + gate * zt.T
    result.append(_layer_norm(r) * g_ref[...] + b_ref[...])
    yield


def _ffn_pieces(mrow, x, wup_ref, wdn_ref, g_ref, b_ref, result):
    shift, scale, gate = mrow[:, 0:D_MODEL], mrow[:, D_MODEL:2 * D_MODEL], mrow[:, 2 * D_MODEL:3 * D_MODEL]
    h = (_layer_norm(x) * (1.0 + scale) + shift).astype(_BF16)
    yield
    fs = []
    for c in range(FFN_HIDDEN // FFN_COL_CHUNK):
        a = jnp.dot(h, wup_ref[:, pl.ds(c * FFN_COL_CHUNK, FFN_COL_CHUNK)], preferred_element_type=_F32)
        u = jnp.dot(h, wup_ref[:, pl.ds(FFN_HIDDEN + c * FFN_COL_CHUNK, FFN_COL_CHUNK)],
                    preferred_element_type=_F32)
        fs.append((a * _sigmoid(a) * u).astype(_BF16))
        yield
    dn = jnp.dot(jnp.concatenate(fs, axis=1), wdn_ref[...], preferred_element_type=_F32)
    r = DEEPNORM_ALPHA * x + gate * dn
    result.append(_layer_norm(r) * g_ref[...] + b_ref[...])
    yield


def _post_kernel(mod_ref, x_ref, ot_ref, win_ref, wpa_ref, wpb_ref, wo_ref, g1_ref, b1_ref,
                 wup_ref, wdn_ref, g2_ref, b2_ref, out_ref, x1_ref, *, tiles_per_batch):
    j = pl.program_id(0)
    n_tiles = pl.num_programs(0) - 1
    half_batch = n_tiles // tiles_per_batch

    @pl.when(j == 0)
    def _():
        x1_ref[...] = jnp.zeros_like(x1_ref)

    b_prev = jnp.maximum(j - 1, 0) // tiles_per_batch
    b_cur = jnp.minimum(j, n_tiles - 1) // tiles_per_batch
    ffn_out, merge_out = ([], []), ([], [])

    def stream(half):
        off = half * half_batch
        return _stagger(
            [_ffn_pieces(mod_ref[pl.ds(b_prev + off, 1), 3 * D_MODEL:6 * D_MODEL], x1_ref[half],
                         wup_ref, wdn_ref, g2_ref, b2_ref, ffn_out[half]),
             _merge_pieces(mod_ref[pl.ds(b_cur + off, 1), 0:3 * D_MODEL], x_ref[half, 0],
                           ot_ref[half, 0], win_ref, wpa_ref, wpb_ref, wo_ref, g1_ref, b1_ref,
                           merge_out[half])],
            rates=[POST_FFN_PIECES_PER_MERGE_PIECE, 1])

    _trace(_stagger([stream(0), stream(1)], lead=POST_STREAM_LEAD))
    for half in range(2):
        out_ref[half, 0] = ffn_out[half][0]
        x1_ref[half] = merge_out[half][0]


def _post(mod, x, ot, w_in_t, wpat, wpbt, wot, g1, b1, wup, wdn, g2, b2):
    bsz, s, d = x.shape
    t = TOKEN_TILE
    tpb = s // t
    hb = bsz // 2
    n_tiles = hb * tpb
    const = lambda a: pl.BlockSpec(a.shape, lambda j: (0,) * a.ndim, pipeline_mode=pl.Buffered(1))
    cur = lambda j: jnp.minimum(j, n_tiles - 1)
    prev = lambda j: jnp.maximum(j - 1, 0)
    out = pl.pallas_call(
        functools.partial(_post_kernel, tiles_per_batch=tpb),
        grid=(n_tiles + 1,),
        in_specs=[
            const(mod),
            pl.BlockSpec((2, 1, t, d), lambda j: (0, cur(j) // tpb, cur(j) % tpb, 0)),
            pl.BlockSpec((2, 1, ot.shape[1], t), lambda j: (0, cur(j) // tpb, 0, cur(j) % tpb)),
            const(w_in_t), const(wpat), const(wpbt), const(wot), const(g1), const(b1),
            const(wup), const(wdn), const(g2), const(b2),
        ],
        out_specs=pl.BlockSpec((2, 1, t, d), lambda j: (0, prev(j) // tpb, prev(j) % tpb, 0)),
        out_shape=jax.ShapeDtypeStruct((2, hb, s, d), _F32),
        scratch_shapes=[pltpu.VMEM((2, t, d), _F32)],
        compiler_params=pltpu.CompilerParams(
            dimension_semantics=("arbitrary",), vmem_limit_bytes=VMEM_LIMIT),
        name="post",
    )(mod, x.reshape(2, hb, s, d), ot.reshape(2, hb, ot.shape[1], s),
      w_in_t, wpat, wpbt, wot, g1, b1, wup, wdn, g2, b2)
    return out.reshape(bsz, s, d)


def _rope_tables(seq, half):
    pos = np.arange(seq)
    freqs = ROPE_THETA ** (-np.arange(half, dtype=np.float64) / half)
    tabs = []
    for p in (pos // GRID_W, pos % GRID_W):
        ang = p[None, :].astype(np.float64) * freqs[:, None]
        for fn, ident in ((np.cos, 1.0), (np.sin, 0.0)):
            tabs.append(np.concatenate([np.full((half, CTX_LEN), ident), fn(ang)], axis=1))
    return jnp.asarray(np.stack(tabs), dtype=_F32)


def kernel(x, c, ctx, c_ctx, w_mod, b_mod, w_in, q_norm_a, k_norm_a, cq_norm, ckv_norm,
           w_uq, w_ukv, w_proj_a, w_proj_b, w_out, ln1_g, ln1_b, w_up, w_down, ln2_g, ln2_b):
    bsz, s, d = x.shape
    assert w_mod.shape[0] == DEPTH == 1 and d == D_MODEL and ctx.shape[1] == CTX_LEN
    assert bsz == 8 and s % ATTN_Q_TILE == 0 and s % TOKEN_TILE == 0

    cc = jnp.concatenate([c, c_ctx[None, :], jnp.zeros((bsz - 1, d), _F32)], axis=0)
    mod = _modulation(cc, w_mod[0], b_mod[0][None, :])

    tbf = lambda w: w[0].astype(_BF16).T
    w_in_t = tbf(w_in)
    col = lambda v: v[0][:, None]
    tab_a = _rope_tables(s, A_HEAD_DIM // 4)
    tab_b = _rope_tables(s, B_ROPE_DIM // 4)

    q_all, k_all, v_all = _project(
        mod, ctx, x, w_in_t, tbf(w_uq), tbf(w_ukv),
        col(q_norm_a), col(k_norm_a), col(cq_norm), col(ckv_norm), tab_a, tab_b)
    ot = _attention(q_all, k_all, v_all)
    return _post(mod, x, ot, w_in_t, tbf(w_proj_a), tbf(w_proj_b), tbf(w_out), ln1_g, ln1_b,
                 w_up[0].astype(_BF16), w_down[0].astype(_BF16), ln2_g, ln2_b)
```

```python
import functools
import math

import numpy as np
import jax
import jax.numpy as jnp
from jax import lax
from jax.experimental import pallas as pl
from jax.experimental.pallas import tpu as pltpu

D_MODEL = 1024
GRID_W = 64
CTX_LEN = 256
ROPE_THETA = 10000.0
EPS = 1e-6

A_HEADS = 8
A_KV_HEADS = 2
A_HEAD_DIM = 64
B_HEADS = 8
B_Q_RANK = 384
B_KV_RANK = 256
B_NOPE_DIM = 64
B_ROPE_DIM = 32
B_V_DIM = 64
FFN_HIDDEN = 2816
DEPTH = 1

A_SCALE = A_HEAD_DIM ** -0.5
B_SCALE = (B_NOPE_DIM + B_ROPE_DIM) ** -0.5
DEEPNORM_ALPHA = (2.0 * DEPTH) ** 0.25
LOG2_E = math.log2(math.e)

QA_OFF = 0
KA_OFF = QA_OFF + A_HEADS * A_HEAD_DIM
VA_OFF = KA_OFF + A_KV_HEADS * A_HEAD_DIM
CQ_OFF = VA_OFF + A_KV_HEADS * A_HEAD_DIM
CKV_OFF = CQ_OFF + B_Q_RANK
KR_OFF = CKV_OFF + B_KV_RANK
GATE_OFF = KR_OFF + B_ROPE_DIM
QKV_COLS = GATE_OFF

N_HEADS = A_HEADS + B_HEADS
HEAD_PAD = 128
N_KSETS = 1 + B_HEADS
V_ROWS = A_KV_HEADS * A_HEAD_DIM + B_HEADS * B_V_DIM

MOD_COL_TILE = 1024
PROJ_TILE = 256
PROJ_ROW_CHUNK = 288
PROJ_STREAMS = 4
PROJ_STREAM_LEAD = 3
ATTN_Q_TILE = 512
ATTN_KEY_CHUNK = 256
ATTN_STREAM_LEAD = 0
ATTN_ONES_ROWS = 16
TOKEN_TILE = 256
MERGE_ROW_CHUNK = 256
FFN_COL_CHUNK = 256
POST_FFN_PIECES_PER_MERGE_PIECE = 2
POST_STREAM_LEAD = 10
VMEM_LIMIT = 56 * 1024 * 1024

_NT = (((1,), (1,)), ((), ()))
_F32 = jnp.float32
_BF16 = jnp.bfloat16


def _layer_norm(x):
    mu = jnp.mean(x, axis=-1, keepdims=True)
    xc = x - mu
    var = jnp.mean(xc * xc, axis=-1, keepdims=True)
    return xc * lax.rsqrt(var + EPS)


def _sigmoid(x):
    return jax.nn.sigmoid(x)


def _mod_kernel(c_ref, w_ref, b_ref, o_ref):
    c = c_ref[...]
    a = (c * _sigmoid(c)).astype(_BF16)
    o_ref[...] = jnp.dot(a, w_ref[...].astype(_BF16), preferred_element_type=_F32) + b_ref[...]


def _modulation(cc, w_mod, b_mod):
    rows, d = cc.shape
    n = w_mod.shape[1]
    return pl.pallas_call(
        _mod_kernel,
        grid=(n // MOD_COL_TILE,),
        in_specs=[
            pl.BlockSpec((rows, d), lambda i: (0, 0)),
            pl.BlockSpec((d, MOD_COL_TILE), lambda i: (0, i)),
            pl.BlockSpec((1, MOD_COL_TILE), lambda i: (0, i)),
        ],
        out_specs=pl.BlockSpec((rows, MOD_COL_TILE), lambda i: (0, i)),
        out_shape=jax.ShapeDtypeStruct((rows, n), _F32),
        compiler_params=pltpu.CompilerParams(
            dimension_semantics=("arbitrary",), vmem_limit_bytes=VMEM_LIMIT),
        name="mod",
    )(cc, w_mod, b_mod)


def _rms_rows(x, g):
    ms = jnp.mean(x * x, axis=0, keepdims=True)
    return x * lax.rsqrt(ms + EPS) * g


def _axial_rope_rows(x, half, tab):
    r1, r2 = x[0:half], x[half:2 * half]
    c1, c2 = x[2 * half:3 * half], x[3 * half:4 * half]
    cr, sr, cc, sc = tab[0], tab[1], tab[2], tab[3]
    return jnp.concatenate(
        [r1 * cr - r2 * sr, r1 * sr + r2 * cr, c1 * cc - c2 * sc, c1 * sc + c2 * cc], axis=0)


def _stagger(streams, lead=0, rates=None):
    rates = rates or [1] * len(streams)
    live = dict(enumerate(streams))
    rnd = 0
    while live:
        for i in sorted(live):
            if rnd < i * lead:
                continue
            for _ in range(rates[i]):
                if next(live[i], StopIteration) is StopIteration:
                    del live[i]
                    break
                yield
        rnd += 1


def _trace(pieces):
    for _ in pieces:
        pass


def _proj_pieces(mrow, xin, w1_ref, wuq_ref, wukv_ref, qn_ref, kn_ref, cqn_ref, ckvn_ref, ta, tb,
                 q_out, k_out, v_out):
    shift, scale = mrow[:, 0:D_MODEL], mrow[:, D_MODEL:2 * D_MODEL]
    h = (_layer_norm(xin) * (1.0 + scale) + shift).astype(_BF16)
    yield
    parts = []
    for r in range(QKV_COLS // PROJ_ROW_CHUNK):
        rows = pl.ds(r * PROJ_ROW_CHUNK, PROJ_ROW_CHUNK)
        parts.append(lax.dot_general(w1_ref[rows, :], h, _NT, preferred_element_type=_F32))
        yield
    pt = jnp.concatenate(parts, axis=0)
    t = pt.shape[1]
    z64 = jnp.zeros((A_HEAD_DIM, t), _F32)
    z32 = jnp.zeros((HEAD_PAD - B_NOPE_DIM - B_ROPE_DIM, t), _F32)

    qn = qn_ref[...] * (A_SCALE * LOG2_E)
    group = A_HEADS // A_KV_HEADS
    for hd in range(A_HEADS):
        xh = pt[QA_OFF + hd * A_HEAD_DIM:QA_OFF + (hd + 1) * A_HEAD_DIM]
        xh = _axial_rope_rows(_rms_rows(xh, qn), A_HEAD_DIM // 4, ta)
        full = jnp.concatenate([xh, z64] if hd // group == 0 else [z64, xh], axis=0)
        q_out[hd] = full.astype(_BF16)
        if hd % 4 == 3:
            yield

    kn = kn_ref[...]
    ka = [
        _axial_rope_rows(
            _rms_rows(pt[KA_OFF + g * A_HEAD_DIM:KA_OFF + (g + 1) * A_HEAD_DIM], kn),
            A_HEAD_DIM // 4, ta)
        for g in range(A_KV_HEADS)
    ]
    k_out[0] = jnp.concatenate(ka, axis=0).T.astype(_BF16)
    v_out[0:A_KV_HEADS * A_HEAD_DIM, :] = pt[VA_OFF:CQ_OFF].astype(_BF16)
    yield

    cq = _rms_rows(pt[CQ_OFF:CKV_OFF], cqn_ref[...]).astype(_BF16)
    qb = jnp.dot(wuq_ref[...], cq, preferred_element_type=_F32) * (B_SCALE * LOG2_E)
    yield
    qdim = B_NOPE_DIM + B_ROPE_DIM
    for hd in range(B_HEADS):
        blk = qb[hd * qdim:(hd + 1) * qdim]
        rp = _axial_rope_rows(blk[B_NOPE_DIM:qdim], B_ROPE_DIM // 4, tb)
        full = jnp.concatenate([blk[0:B_NOPE_DIM], rp, z32], axis=0)
        q_out[A_HEADS + hd] = full.astype(_BF16)
        if hd % 4 == 3:
            yield

    ckv = _rms_rows(pt[CKV_OFF:KR_OFF], ckvn_ref[...]).astype(_BF16)
    kv = jnp.dot(wukv_ref[...], ckv, preferred_element_type=_F32)
    kr = _axial_rope_rows(pt[KR_OFF:GATE_OFF], B_ROPE_DIM // 4, tb)
    yield
    kvdim = B_NOPE_DIM + B_V_DIM
    v_base = A_KV_HEADS * A_HEAD_DIM
    for hd in range(B_HEADS):
        kn_h = kv[hd * kvdim:hd * kvdim + B_NOPE_DIM]
        v_h = kv[hd * kvdim + B_NOPE_DIM:(hd + 1) * kvdim]
        v_out[v_base + hd * B_V_DIM:v_base + (hd + 1) * B_V_DIM, :] = v_h.astype(_BF16)
        k_out[1 + hd] = jnp.concatenate([kn_h, kr, z32], axis=0).T.astype(_BF16)
        if hd % 4 == 3:
            yield


def _proj_kernel(mod_ref, ctx_ref, x_ref, w1_ref, wuq_ref, wukv_ref,
                 qn_ref, kn_ref, cqn_ref, ckvn_ref, ta_ref, tb_ref,
                 q_out, k_out, v_out):
    b = pl.program_id(0)
    j = pl.program_id(1)
    batch_stride = pl.num_programs(0)
    is_ctx = j == 0
    ctx_row = mod_ref.shape[0] // 2
    ta = ta_ref[...]
    tb = tb_ref[...]

    def stream(i):
        xin = jnp.where(is_ctx, ctx_ref[i, 0], x_ref[i, 0])
        mrow = jnp.where(is_ctx, mod_ref[ctx_row:ctx_row + 1, 0:2 * D_MODEL],
                         mod_ref[pl.ds(b + i * batch_stride, 1), 0:2 * D_MODEL])
        return _proj_pieces(mrow, xin, w1_ref, wuq_ref, wukv_ref, qn_ref, kn_ref, cqn_ref, ckvn_ref,
                            ta, tb, q_out.at[i, 0], k_out.at[i, 0], v_out.at[i, 0])

    _trace(_stagger([stream(i) for i in range(PROJ_STREAMS)], lead=PROJ_STREAM_LEAD))


def _project(mod, ctx, x, w_in_t, wuqt, wukvt, qn, kn, cqn, ckvn, tab_a, tab_b):
    bsz, s, d = x.shape
    t = PROJ_TILE
    n_steps = (CTX_LEN + s) // t
    total = CTX_LEN + s
    ns = PROJ_STREAMS
    hb = bsz // ns
    lat = lambda j: jnp.maximum(j - 1, 0)
    full2 = lambda shape: pl.BlockSpec(shape, lambda b, j: (0, 0))
    q_all, k_all, v_all = pl.pallas_call(
        _proj_kernel,
        grid=(hb, n_steps),
        in_specs=[
            full2(mod.shape),
            pl.BlockSpec((ns, 1, CTX_LEN, d), lambda b, j: (0, b, 0, 0)),
            pl.BlockSpec((ns, 1, t, d), lambda b, j: (0, b, lat(j), 0)),
            full2((QKV_COLS, d)),
            full2(wuqt.shape), full2(wukvt.shape),
            full2(qn.shape), full2(kn.shape), full2(cqn.shape), full2(ckvn.shape),
            pl.BlockSpec((4, tab_a.shape[1], t), lambda b, j: (0, 0, j)),
            pl.BlockSpec((4, tab_b.shape[1], t), lambda b, j: (0, 0, j)),
        ],
        out_specs=[
            pl.BlockSpec((ns, 1, N_HEADS, HEAD_PAD, t), lambda b, j: (0, b, 0, 0, lat(j))),
            pl.BlockSpec((ns, 1, N_KSETS, t, HEAD_PAD), lambda b, j: (0, b, 0, j, 0)),
            pl.BlockSpec((ns, 1, V_ROWS, t), lambda b, j: (0, b, 0, j)),
        ],
        out_shape=[
            jax.ShapeDtypeStruct((ns, hb, N_HEADS, HEAD_PAD, s), _BF16),
            jax.ShapeDtypeStruct((ns, hb, N_KSETS, total, HEAD_PAD), _BF16),
            jax.ShapeDtypeStruct((ns, hb, V_ROWS, total), _BF16),
        ],
        compiler_params=pltpu.CompilerParams(
            dimension_semantics=("arbitrary", "arbitrary"), vmem_limit_bytes=VMEM_LIMIT),
        name="proj",
    )(mod, ctx.reshape(ns, hb, CTX_LEN, d), x.reshape(ns, hb, s, d),
      w_in_t, wuqt, wukvt, qn, kn, cqn, ckvn, tab_a, tab_b)
    return (q_all.reshape(bsz, N_HEADS, HEAD_PAD, s), k_all.reshape(bsz, N_KSETS, total, HEAD_PAD),
            v_all.reshape(bsz, V_ROWS, total))


def _attn_stream(base, q_ref, k_ref, v_ref, qn_ref, kn_ref, o_ref, o_rows, bufs, m0_ref, slot):
    tq, kc = ATTN_Q_TILE, ATTN_KEY_CHUNK
    n_tiles = q_ref.shape[3] // tq
    n_chunks = k_ref.shape[2] // kc
    assert n_tiles % 2 == 0
    assert n_tiles * tq == q_ref.shape[3] and n_chunks * kc == k_ref.shape[2]
    dv = v_ref.shape[1]
    ones = jnp.ones((ATTN_ONES_ROWS, kc), _BF16)

    def buf_rows(c):
        return pl.ds(pl.multiple_of(base + c * kc, kc), kc)

    def score_chunk(keys_ref, q_tile, buf, c, m):
        s = jnp.dot(keys_ref[0, 0, pl.ds(c * kc, kc), :], q_tile,
                    preferred_element_type=_F32)
        buf[buf_rows(c), :] = s
        cm = jnp.max(s, axis=0, keepdims=True)
        return cm if m is None else jnp.maximum(m, cm)

    def value_chunk(buf, c, m, acc):
        p = jnp.exp2(buf[buf_rows(c), :] - m)
        vext = jnp.concatenate([v_ref[0, :, pl.ds(c * kc, kc)], ones], axis=0)
        pv = jnp.dot(vext, p.astype(_BF16), preferred_element_type=_F32)
        return pv if acc is None else acc + pv

    @pl.when((pl.program_id(0) == 0) & (pl.program_id(1) == 0))
    def _():
        m = None
        for c in range(n_chunks):
            m = score_chunk(k_ref, q_ref[0, 0, :, 0:tq], bufs[0], c, m)
        m0_ref[slot] = m
    yield

    m_cur = m0_ref[slot]
    for t in range(n_tiles):
        m_next = acc = None
        for c in range(n_chunks):
            if t + 1 < n_tiles:
                m_next = score_chunk(k_ref, q_ref[0, 0, :, pl.ds((t + 1) * tq, tq)],
                                     bufs[(t + 1) % 2], c, m_next)
            else:
                m_next = score_chunk(kn_ref, qn_ref[0, 0], bufs[0], c, m_next)
            acc = value_chunk(bufs[t % 2], c, m_cur, acc)
            yield
        o_ref[0, o_rows, pl.ds(t * tq, tq)] = (acc[0:dv] / acc[dv:dv + 1]).astype(_BF16)
        m_cur = m_next
    m0_ref[slot] = m_cur


def _attn_kernel(base_ref, qa_ref, ka_ref, va_ref, qna_ref, kna_ref,
                 qb_ref, kb_ref, vb_ref, qnb_ref, knb_ref, o_ref,
                 sa0_ref, sa1_ref, sb0_ref, sb1_ref, m0_ref):
    base = base_ref[0]
    dv = va_ref.shape[1]
    first = _attn_stream(base, qa_ref, ka_ref, va_ref, qna_ref, kna_ref, o_ref, pl.ds(0, dv),
                         (sa0_ref, sa1_ref), m0_ref, 0)
    second = _attn_stream(base, qb_ref, kb_ref, vb_ref, qnb_ref, knb_ref, o_ref, pl.ds(dv, dv),
                          (sb0_ref, sb1_ref), m0_ref, 1)
    next(first), next(second)
    _trace(_stagger([first, second], lead=ATTN_STREAM_LEAD))


def _attention(q_all, k_all, v_all):
    bsz, _, _, s = q_all.shape
    total = k_all.shape[2]
    group = A_HEADS // A_KV_HEADS
    n_pairs = N_HEADS // 2
    kset = lambda h: jnp.where(h < A_HEADS, 0, h - A_HEADS + 1)
    vblk = lambda h: jnp.where(h < A_HEADS, h // group, h - A_HEADS + A_KV_HEADS)

    def nxt(b, g):
        n = jnp.minimum(b * n_pairs + g + 1, bsz * n_pairs - 1)
        return n // n_pairs, n % n_pairs

    def stream_specs(which):
        head = lambda g: 2 * g + which
        return [
            pl.BlockSpec((1, 1, HEAD_PAD, s), lambda b, g: (b, head(g), 0, 0)),
            pl.BlockSpec((1, 1, total, HEAD_PAD), lambda b, g: (b, kset(head(g)), 0, 0)),
            pl.BlockSpec((1, B_V_DIM, total), lambda b, g: (b, vblk(head(g)), 0)),
            pl.BlockSpec((1, 1, HEAD_PAD, ATTN_Q_TILE),
                         lambda b, g: (nxt(b, g)[0], head(nxt(b, g)[1]), 0, 0)),
            pl.BlockSpec((1, 1, total, HEAD_PAD),
                         lambda b, g: (nxt(b, g)[0], kset(head(nxt(b, g)[1])), 0, 0)),
        ]

    stream_args = (q_all, k_all, v_all, q_all, k_all)
    return pl.pallas_call(
        _attn_kernel,
        grid=(bsz, n_pairs),
        in_specs=[pl.BlockSpec(memory_space=pltpu.SMEM)] + stream_specs(0) + stream_specs(1),
        out_specs=pl.BlockSpec((1, 2 * B_V_DIM, s), lambda b, g: (b, g, 0)),
        out_shape=jax.ShapeDtypeStruct((bsz, N_HEADS * B_V_DIM, s), _BF16),
        scratch_shapes=[pltpu.VMEM((total, ATTN_Q_TILE), _F32)] * 4
        + [pltpu.VMEM((2, 1, ATTN_Q_TILE), _F32)],
        compiler_params=pltpu.CompilerParams(
            dimension_semantics=("arbitrary", "arbitrary"), vmem_limit_bytes=VMEM_LIMIT),
        name="attn",
    )(jnp.zeros((1,), jnp.int32), *stream_args, *stream_args)


def _merge_pieces(mrow, x, ot, win_ref, wpa_ref, wpb_ref, wo_ref, g_ref, b_ref, result):
    shift, scale, gate = mrow[:, 0:D_MODEL], mrow[:, D_MODEL:2 * D_MODEL], mrow[:, 2 * D_MODEL:3 * D_MODEL]
    h = (_layer_norm(x) * (1.0 + scale) + shift).astype(_BF16)
    yield
    na = A_HEADS * A_HEAD_DIM
    ys = []
    for r in range(D_MODEL // MERGE_ROW_CHUNK):
        ra = pl.ds(r * MERGE_ROW_CHUNK, MERGE_ROW_CHUNK)
        ga_rows = pl.ds(GATE_OFF + r * MERGE_ROW_CHUNK, MERGE_ROW_CHUNK)
        gb_rows = pl.ds(GATE_OFF + D_MODEL + r * MERGE_ROW_CHUNK, MERGE_ROW_CHUNK)
        ga = lax.dot_general(win_ref[ga_rows, :], h, _NT, preferred_element_type=_F32)
        gb = lax.dot_general(win_ref[gb_rows, :], h, _NT, preferred_element_type=_F32)
        ya = jnp.dot(wpa_ref[ra, :], ot[0:na], preferred_element_type=_F32)
        yb = jnp.dot(wpb_ref[ra, :], ot[na:], preferred_element_type=_F32)
        ys.append((_sigmoid(ga) * ya + _sigmoid(gb) * yb).astype(_BF16))
        yield
    zt = jnp.dot(wo_ref[...], jnp.concatenate(ys, axis=0), preferred_element_type=_F32)
    r = DEEPNORM_ALPHA * x + gate * zt.T
    result.append(_layer_norm(r) * g_ref[...] + b_ref[...])
    yield


def _ffn_pieces(mrow, x, wup_ref, wdn_ref, g_ref, b_ref, result):
    shift, scale, gate = mrow[:, 0:D_MODEL], mrow[:, D_MODEL:2 * D_MODEL], mrow[:, 2 * D_MODEL:3 * D_MODEL]
    h = (_layer_norm(x) * (1.0 + scale) + shift).astype(_BF16)
    yield
    fs = []
    for c in range(FFN_HIDDEN // FFN_COL_CHUNK):
        a = jnp.dot(h, wup_ref[:, pl.ds(c * FFN_COL_CHUNK, FFN_COL_CHUNK)], preferred_element_type=_F32)
        u = jnp.dot(h, wup_ref[:, pl.ds(FFN_HIDDEN + c * FFN_COL_CHUNK, FFN_COL_CHUNK)],
                    preferred_element_type=_F32)
        fs.append((a * _sigmoid(a) * u).astype(_BF16))
        yield
    dn = jnp.dot(jnp.concatenate(fs, axis=1), wdn_ref[...], preferred_element_type=_F32)
    r = DEEPNORM_ALPHA * x + gate * dn
    result.append(_layer_norm(r) * g_ref[...] + b_ref[...])
    yield


def _post_kernel(mod_ref, x_ref, ot_ref, win_ref, wpa_ref, wpb_ref, wo_ref, g1_ref, b1_ref,
                 wup_ref, wdn_ref, g2_ref, b2_ref, out_ref, x1_ref, *, tiles_per_batch):
    j = pl.program_id(0)
    n_tiles = pl.num_programs(0) - 1
    half_batch = n_tiles // tiles_per_batch

    @pl.when(j == 0)
    def _():
        x1_ref[...] = jnp.zeros_like(x1_ref)

    b_prev = jnp.maximum(j - 1, 0) // tiles_per_batch
    b_cur = jnp.minimum(j, n_tiles - 1) // tiles_per_batch
    ffn_out, merge_out = ([], []), ([], [])

    def stream(half):
        off = half * half_batch
        return _stagger(
            [_ffn_pieces(mod_ref[pl.ds(b_prev + off, 1), 3 * D_MODEL:6 * D_MODEL], x1_ref[half],
                         wup_ref, wdn_ref, g2_ref, b2_ref, ffn_out[half]),
             _merge_pieces(mod_ref[pl.ds(b_cur + off, 1), 0:3 * D_MODEL], x_ref[half, 0],
                           ot_ref[half, 0], win_ref, wpa_ref, wpb_ref, wo_ref, g1_ref, b1_ref,
                           merge_out[half])],
            rates=[POST_FFN_PIECES_PER_MERGE_PIECE, 1])

    _trace(_stagger([stream(0), stream(1)], lead=POST_STREAM_LEAD))
    for half in range(2):
        out_ref[half, 0] = ffn_out[half][0]
        x1_ref[half] = merge_out[half][0]


def _post(mod, x, ot, w_in_t, wpat, wpbt, wot, g1, b1, wup, wdn, g2, b2):
    bsz, s, d = x.shape
    t = TOKEN_TILE
    tpb = s // t
    hb = bsz // 2
    n_tiles = hb * tpb
    const = lambda a: pl.BlockSpec(a.shape, lambda j: (0,) * a.ndim, pipeline_mode=pl.Buffered(1))
    cur = lambda j: jnp.minimum(j, n_tiles - 1)
    prev = lambda j: jnp.maximum(j - 1, 0)
    out = pl.pallas_call(
        functools.partial(_post_kernel, tiles_per_batch=tpb),
        grid=(n_tiles + 1,),
        in_specs=[
            const(mod),
            pl.BlockSpec((2, 1, t, d), lambda j: (0, cur(j) // tpb, cur(j) % tpb, 0)),
            pl.BlockSpec((2, 1, ot.shape[1], t), lambda j: (0, cur(j) // tpb, 0, cur(j) % tpb)),
            const(w_in_t), const(wpat), const(wpbt), const(wot), const(g1), const(b1),
            const(wup), const(wdn), const(g2), const(b2),
        ],
        out_specs=pl.BlockSpec((2, 1, t, d), lambda j: (0, prev(j) // tpb, prev(j) % tpb, 0)),
        out_shape=jax.ShapeDtypeStruct((2, hb, s, d), _F32),
        scratch_shapes=[pltpu.VMEM((2, t, d), _F32)],
        compiler_params=pltpu.CompilerParams(
            dimension_semantics=("arbitrary",), vmem_limit_bytes=VMEM_LIMIT),
        name="post",
    )(mod, x.reshape(2, hb, s, d), ot.reshape(2, hb, ot.shape[1], s),
      w_in_t, wpat, wpbt, wot, g1, b1, wup, wdn, g2, b2)
    return out.reshape(bsz, s, d)


def _rope_tables(seq, half):
    pos = np.arange(seq)
    freqs = ROPE_THETA ** (-np.arange(half, dtype=np.float64) / half)
    tabs = []
    for p in (pos // GRID_W, pos % GRID_W):
        ang = p[None, :].astype(np.float64) * freqs[:, None]
        for fn, ident in ((np.cos, 1.0), (np.sin, 0.0)):
            tabs.append(np.concatenate([np.full((half, CTX_LEN), ident), fn(ang)], axis=1))
    return jnp.asarray(np.stack(tabs), dtype=_F32)


def kernel(x, c, ctx, c_ctx, w_mod, b_mod, w_in, q_norm_a, k_norm_a, cq_norm, ckv_norm,
           w_uq, w_ukv, w_proj_a, w_proj_b, w_out, ln1_g, ln1_b, w_up, w_down, ln2_g, ln2_b):
    bsz, s, d = x.shape
    assert w_mod.shape[0] == DEPTH == 1 and d == D_MODEL and ctx.shape[1] == CTX_LEN
    assert bsz == 8 and s % ATTN_Q_TILE == 0 and s % TOKEN_TILE == 0

    cc = jnp.concatenate([c, c_ctx[None, :], jnp.zeros((bsz - 1, d), _F32)], axis=0)
    mod = _modulation(cc, w_mod[0], b_mod[0][None, :])

    tbf = lambda w: w[0].astype(_BF16).T
    w_in_t = tbf(w_in)
    col = lambda v: v[0][:, None]
    tab_a = _rope_tables(s, A_HEAD_DIM // 4)
    tab_b = _rope_tables(s, B_ROPE_DIM // 4)

    q_all, k_all, v_all = _project(
        mod, ctx, x, w_in_t, tbf(w_uq), tbf(w_ukv),
        col(q_norm_a), col(k_norm_a), col(cq_norm), col(ckv_norm), tab_a, tab_b)
    ot = _attention(q_all, k_all, v_all)
    return _post(mod, x, ot, w_in_t, tbf(w_proj_a), tbf(w_proj_b), tbf(w_out), ln1_g, ln1_b,
                 w_up[0].astype(_BF16), w_down[0].astype(_BF16), ln2_g, ln2_b)
```

```python
import functools
import math

import numpy as np
import jax
import jax.numpy as jnp
from jax import lax
from jax.experimental import pallas as pl
from jax.experimental.pallas import tpu as pltpu

D_MODEL = 1024
GRID_W = 64
CTX_LEN = 256
ROPE_THETA = 10000.0
EPS = 1e-6

A_HEADS = 8
A_KV_HEADS = 2
A_HEAD_DIM = 64
B_HEADS = 8
B_Q_RANK = 384
B_KV_RANK = 256
B_NOPE_DIM = 64
B_ROPE_DIM = 32
B_V_DIM = 64
FFN_HIDDEN = 2816
DEPTH = 1

A_SCALE = A_HEAD_DIM ** -0.5
B_SCALE = (B_NOPE_DIM + B_ROPE_DIM) ** -0.5
DEEPNORM_ALPHA = (2.0 * DEPTH) ** 0.25
LOG2_E = math.log2(math.e)

QA_OFF = 0
KA_OFF = QA_OFF + A_HEADS * A_HEAD_DIM
VA_OFF = KA_OFF + A_KV_HEADS * A_HEAD_DIM
CQ_OFF = VA_OFF + A_KV_HEADS * A_HEAD_DIM
CKV_OFF = CQ_OFF + B_Q_RANK
KR_OFF = CKV_OFF + B_KV_RANK
GATE_OFF = KR_OFF + B_ROPE_DIM
QKV_COLS = GATE_OFF

N_HEADS = A_HEADS + B_HEADS
HEAD_PAD = 128
N_KSETS = 1 + B_HEADS
V_ROWS = A_KV_HEADS * A_HEAD_DIM + B_HEADS * B_V_DIM

MOD_COL_TILE = 1024
PROJ_TILE = 256
PROJ_ROW_CHUNK = 288
PROJ_STREAMS = 4
PROJ_STREAM_LEAD = 3
ATTN_Q_TILE = 512
ATTN_KEY_CHUNK = 256
ATTN_STREAM_LEAD = 9
ATTN_ONES_ROWS = 16
ATTN_FINISH_DELAY = 1
TOKEN_TILE = 256
MERGE_ROW_CHUNK = 256
FFN_COL_CHUNK = 256
POST_FFN_PIECES_PER_MERGE_PIECE = 2
POST_STREAM_LEAD = 10
VMEM_LIMIT = 56 * 1024 * 1024

_NT = (((1,), (1,)), ((), ()))
_F32 = jnp.float32
_BF16 = jnp.bfloat16


def _layer_norm(x):
    mu = jnp.mean(x, axis=-1, keepdims=True)
    xc = x - mu
    var = jnp.mean(xc * xc, axis=-1, keepdims=True)
    return xc * lax.rsqrt(var + EPS)


def _sigmoid(x):
    return jax.nn.sigmoid(x)


def _mod_kernel(c_ref, w_ref, b_ref, o_ref):
    c = c_ref[...]
    a = (c * _sigmoid(c)).astype(_BF16)
    o_ref[...] = jnp.dot(a, w_ref[...].astype(_BF16), preferred_element_type=_F32) + b_ref[...]


def _modulation(cc, w_mod, b_mod):
    rows, d = cc.shape
    n = w_mod.shape[1]
    return pl.pallas_call(
        _mod_kernel,
        grid=(n // MOD_COL_TILE,),
        in_specs=[
            pl.BlockSpec((rows, d), lambda i: (0, 0)),
            pl.BlockSpec((d, MOD_COL_TILE), lambda i: (0, i)),
            pl.BlockSpec((1, MOD_COL_TILE), lambda i: (0, i)),
        ],
        out_specs=pl.BlockSpec((rows, MOD_COL_TILE), lambda i: (0, i)),
        out_shape=jax.ShapeDtypeStruct((rows, n), _F32),
        compiler_params=pltpu.CompilerParams(
            dimension_semantics=("arbitrary",), vmem_limit_bytes=VMEM_LIMIT),
        name="mod",
    )(cc, w_mod, b_mod)


def _rms_rows(x, g):
    ms = jnp.mean(x * x, axis=0, keepdims=True)
    return x * lax.rsqrt(ms + EPS) * g


def _axial_rope_rows(x, half, tab):
    r1, r2 = x[0:half], x[half:2 * half]
    c1, c2 = x[2 * half:3 * half], x[3 * half:4 * half]
    cr, sr, cc, sc = tab[0], tab[1], tab[2], tab[3]
    return jnp.concatenate(
        [r1 * cr - r2 * sr, r1 * sr + r2 * cr, c1 * cc - c2 * sc, c1 * sc + c2 * cc], axis=0)


def _stagger(streams, lead=0, rates=None):
    rates = rates or [1] * len(streams)
    live = dict(enumerate(streams))
    rnd = 0
    while live:
        for i in sorted(live):
            if rnd < i * lead:
                continue
            for _ in range(rates[i]):
                if next(live[i], StopIteration) is StopIteration:
                    del live[i]
                    break
                yield
        rnd += 1


def _trace(pieces):
    for _ in pieces:
        pass


def _proj_pieces(mrow, xin, w1_ref, wuq_ref, wukv_ref, qn_ref, kn_ref, cqn_ref, ckvn_ref, ta, tb,
                 q_out, k_out, v_out):
    shift, scale = mrow[:, 0:D_MODEL], mrow[:, D_MODEL:2 * D_MODEL]
    h = (_layer_norm(xin) * (1.0 + scale) + shift).astype(_BF16)
    yield
    parts = []
    for r in range(QKV_COLS // PROJ_ROW_CHUNK):
        rows = pl.ds(r * PROJ_ROW_CHUNK, PROJ_ROW_CHUNK)
        parts.append(lax.dot_general(w1_ref[rows, :], h, _NT, preferred_element_type=_F32))
        yield
    pt = jnp.concatenate(parts, axis=0)
    t = pt.shape[1]
    z64 = jnp.zeros((A_HEAD_DIM, t), _F32)
    z32 = jnp.zeros((HEAD_PAD - B_NOPE_DIM - B_ROPE_DIM, t), _F32)

    qn = qn_ref[...] * (A_SCALE * LOG2_E)
    group = A_HEADS // A_KV_HEADS
    for hd in range(A_HEADS):
        xh = pt[QA_OFF + hd * A_HEAD_DIM:QA_OFF + (hd + 1) * A_HEAD_DIM]
        xh = _axial_rope_rows(_rms_rows(xh, qn), A_HEAD_DIM // 4, ta)
        full = jnp.concatenate([xh, z64] if hd // group == 0 else [z64, xh], axis=0)
        q_out[hd] = full.astype(_BF16)
        if hd % 4 == 3:
            yield

    kn = kn_ref[...]
    ka = [
        _axial_rope_rows(
            _rms_rows(pt[KA_OFF + g * A_HEAD_DIM:KA_OFF + (g + 1) * A_HEAD_DIM], kn),
            A_HEAD_DIM // 4, ta)
        for g in range(A_KV_HEADS)
    ]
    k_out[0] = jnp.concatenate(ka, axis=0).T.astype(_BF16)
    v_out[0:A_KV_HEADS * A_HEAD_DIM, :] = pt[VA_OFF:CQ_OFF].astype(_BF16)
    yield

    cq = _rms_rows(pt[CQ_OFF:CKV_OFF], cqn_ref[...]).astype(_BF16)
    qb = jnp.dot(wuq_ref[...], cq, preferred_element_type=_F32) * (B_SCALE * LOG2_E)
    yield
    qdim = B_NOPE_DIM + B_ROPE_DIM
    for hd in range(B_HEADS):
        blk = qb[hd * qdim:(hd + 1) * qdim]
        rp = _axial_rope_rows(blk[B_NOPE_DIM:qdim], B_ROPE_DIM // 4, tb)
        full = jnp.concatenate([blk[0:B_NOPE_DIM], rp, z32], axis=0)
        q_out[A_HEADS + hd] = full.astype(_BF16)
        if hd % 4 == 3:
            yield

    ckv = _rms_rows(pt[CKV_OFF:KR_OFF], ckvn_ref[...]).astype(_BF16)
    kv = jnp.dot(wukv_ref[...], ckv, preferred_element_type=_F32)
    kr = _axial_rope_rows(pt[KR_OFF:GATE_OFF], B_ROPE_DIM // 4, tb)
    yield
    kvdim = B_NOPE_DIM + B_V_DIM
    v_base = A_KV_HEADS * A_HEAD_DIM
    for hd in range(B_HEADS):
        kn_h = kv[hd * kvdim:hd * kvdim + B_NOPE_DIM]
        v_h = kv[hd * kvdim + B_NOPE_DIM:(hd + 1) * kvdim]
        v_out[v_base + hd * B_V_DIM:v_base + (hd + 1) * B_V_DIM, :] = v_h.astype(_BF16)
        k_out[1 + hd] = jnp.concatenate([kn_h, kr, z32], axis=0).T.astype(_BF16)
        if hd % 4 == 3:
            yield


def _proj_kernel(mod_ref, ctx_ref, x_ref, w1_ref, wuq_ref, wukv_ref,
                 qn_ref, kn_ref, cqn_ref, ckvn_ref, ta_ref, tb_ref,
                 q_out, k_out, v_out):
    b = pl.program_id(0)
    j = pl.program_id(1)
    batch_stride = pl.num_programs(0)
    is_ctx = j == 0
    ctx_row = mod_ref.shape[0] // 2
    ta = ta_ref[...]
    tb = tb_ref[...]

    def stream(i):
        xin = jnp.where(is_ctx, ctx_ref[i, 0], x_ref[i, 0])
        mrow = jnp.where(is_ctx, mod_ref[ctx_row:ctx_row + 1, 0:2 * D_MODEL],
                         mod_ref[pl.ds(b + i * batch_stride, 1), 0:2 * D_MODEL])
        return _proj_pieces(mrow, xin, w1_ref, wuq_ref, wukv_ref, qn_ref, kn_ref, cqn_ref, ckvn_ref,
                            ta, tb, q_out.at[i, 0], k_out.at[i, 0], v_out.at[i, 0])

    _trace(_stagger([stream(i) for i in range(PROJ_STREAMS)], lead=PROJ_STREAM_LEAD))


def _project(mod, ctx, x, w_in_t, wuqt, wukvt, qn, kn, cqn, ckvn, tab_a, tab_b):
    bsz, s, d = x.shape
    t = PROJ_TILE
    n_steps = (CTX_LEN + s) // t
    total = CTX_LEN + s
    ns = PROJ_STREAMS
    hb = bsz // ns
    lat = lambda j: jnp.maximum(j - 1, 0)
    full2 = lambda shape: pl.BlockSpec(shape, lambda b, j: (0, 0))
    q_all, k_all, v_all = pl.pallas_call(
        _proj_kernel,
        grid=(hb, n_steps),
        in_specs=[
            full2(mod.shape),
            pl.BlockSpec((ns, 1, CTX_LEN, d), lambda b, j: (0, b, 0, 0)),
            pl.BlockSpec((ns, 1, t, d), lambda b, j: (0, b, lat(j), 0)),
            full2((QKV_COLS, d)),
            full2(wuqt.shape), full2(wukvt.shape),
            full2(qn.shape), full2(kn.shape), full2(cqn.shape), full2(ckvn.shape),
            pl.BlockSpec((4, tab_a.shape[1], t), lambda b, j: (0, 0, j)),
            pl.BlockSpec((4, tab_b.shape[1], t), lambda b, j: (0, 0, j)),
        ],
        out_specs=[
            pl.BlockSpec((ns, 1, N_HEADS, HEAD_PAD, t), lambda b, j: (0, b, 0, 0, lat(j))),
            pl.BlockSpec((ns, 1, N_KSETS, t, HEAD_PAD), lambda b, j: (0, b, 0, j, 0)),
            pl.BlockSpec((ns, 1, V_ROWS, t), lambda b, j: (0, b, 0, j)),
        ],
        out_shape=[
            jax.ShapeDtypeStruct((ns, hb, N_HEADS, HEAD_PAD, s), _BF16),
            jax.ShapeDtypeStruct((ns, hb, N_KSETS, total, HEAD_PAD), _BF16),
            jax.ShapeDtypeStruct((ns, hb, V_ROWS, total), _BF16),
        ],
        compiler_params=pltpu.CompilerParams(
            dimension_semantics=("arbitrary", "arbitrary"), vmem_limit_bytes=VMEM_LIMIT),
        name="proj",
    )(mod, ctx.reshape(ns, hb, CTX_LEN, d), x.reshape(ns, hb, s, d),
      w_in_t, wuqt, wukvt, qn, kn, cqn, ckvn, tab_a, tab_b)
    return (q_all.reshape(bsz, N_HEADS, HEAD_PAD, s), k_all.reshape(bsz, N_KSETS, total, HEAD_PAD),
            v_all.reshape(bsz, V_ROWS, total))


def _attn_stream(base, q_ref, k_ref, v_ref, qn_ref, kn_ref, o_ref, o_rows, bufs, m0_ref, slot):
    tq, kc = ATTN_Q_TILE, ATTN_KEY_CHUNK
    n_tiles = q_ref.shape[3] // tq
    n_chunks = k_ref.shape[2] // kc
    assert n_tiles % 2 == 0
    assert n_tiles * tq == q_ref.shape[3] and n_chunks * kc == k_ref.shape[2]
    dv = v_ref.shape[1]
    ones = jnp.ones((ATTN_ONES_ROWS, kc), _BF16)

    def buf_rows(c):
        return pl.ds(pl.multiple_of(base + c * kc, kc), kc)

    def score_chunk(keys_ref, q_tile, buf, c, m):
        s = jnp.dot(keys_ref[0, 0, pl.ds(c * kc, kc), :], q_tile,
                    preferred_element_type=_F32)
        buf[buf_rows(c), :] = s
        cm = jnp.max(s, axis=0, keepdims=True)
        return cm if m is None else jnp.maximum(m, cm)

    def value_chunk(buf, c, m, acc):
        p = jnp.exp2(buf[buf_rows(c), :] - m)
        vext = jnp.concatenate([v_ref[0, :, pl.ds(c * kc, kc)], ones], axis=0)
        pv = jnp.dot(vext, p.astype(_BF16), preferred_element_type=_F32)
        return pv if acc is None else acc + pv

    @pl.when((pl.program_id(0) == 0) & (pl.program_id(1) == 0))
    def _():
        m = None
        for c in range(n_chunks):
            m = score_chunk(k_ref, q_ref[0, 0, :, 0:tq], bufs[0], c, m)
        m0_ref[slot] = m
    yield

    def finish(t, acc):
        o_ref[0, o_rows, pl.ds(t * tq, tq)] = (acc[0:dv] / acc[dv:dv + 1]).astype(_BF16)

    m_cur = m0_ref[slot]
    pending = None
    for t in range(n_tiles):
        m_next = acc = None
        for c in range(n_chunks):
            if t + 1 < n_tiles:
                m_next = score_chunk(k_ref, q_ref[0, 0, :, pl.ds((t + 1) * tq, tq)],
                                     bufs[(t + 1) % 2], c, m_next)
            else:
                m_next = score_chunk(kn_ref, qn_ref[0, 0], bufs[0], c, m_next)
            acc = value_chunk(bufs[t % 2], c, m_cur, acc)
            if c == ATTN_FINISH_DELAY and pending is not None:
                finish(*pending)
                pending = None
            yield
        pending = (t, acc)
        m_cur = m_next
    finish(*pending)
    m0_ref[slot] = m_cur


def _attn_kernel(base_ref, qa_ref, ka_ref, va_ref, qna_ref, kna_ref,
                 qb_ref, kb_ref, vb_ref, qnb_ref, knb_ref, o_ref,
                 sa0_ref, sa1_ref, sb0_ref, sb1_ref, m0_ref):
    base = base_ref[0]
    dv = va_ref.shape[1]
    first = _attn_stream(base, qa_ref, ka_ref, va_ref, qna_ref, kna_ref, o_ref, pl.ds(0, dv),
                         (sa0_ref, sa1_ref), m0_ref, 0)
    second = _attn_stream(base, qb_ref, kb_ref, vb_ref, qnb_ref, knb_ref, o_ref, pl.ds(dv, dv),
                          (sb0_ref, sb1_ref), m0_ref, 1)
    next(first), next(second)
    _trace(_stagger([first, second], lead=ATTN_STREAM_LEAD))


def _attention(q_all, k_all, v_all):
    bsz, _, _, s = q_all.shape
    total = k_all.shape[2]
    group = A_HEADS // A_KV_HEADS
    n_pairs = N_HEADS // 2
    kset = lambda h: jnp.where(h < A_HEADS, 0, h - A_HEADS + 1)
    vblk = lambda h: jnp.where(h < A_HEADS, h // group, h - A_HEADS + A_KV_HEADS)

    def nxt(b, g):
        n = jnp.minimum(b * n_pairs + g + 1, bsz * n_pairs - 1)
        return n // n_pairs, n % n_pairs

    def stream_specs(which):
        head = lambda g: 2 * g + which
        return [
            pl.BlockSpec((1, 1, HEAD_PAD, s), lambda b, g: (b, head(g), 0, 0)),
            pl.BlockSpec((1, 1, total, HEAD_PAD), lambda b, g: (b, kset(head(g)), 0, 0)),
            pl.BlockSpec((1, B_V_DIM, total), lambda b, g: (b, vblk(head(g)), 0)),
            pl.BlockSpec((1, 1, HEAD_PAD, ATTN_Q_TILE),
                         lambda b, g: (nxt(b, g)[0], head(nxt(b, g)[1]), 0, 0)),
            pl.BlockSpec((1, 1, total, HEAD_PAD),
                         lambda b, g: (nxt(b, g)[0], kset(head(nxt(b, g)[1])), 0, 0)),
        ]

    stream_args = (q_all, k_all, v_all, q_all, k_all)
    return pl.pallas_call(
        _attn_kernel,
        grid=(bsz, n_pairs),
        in_specs=[pl.BlockSpec(memory_space=pltpu.SMEM)] + stream_specs(0) + stream_specs(1),
        out_specs=pl.BlockSpec((1, 2 * B_V_DIM, s), lambda b, g: (b, g, 0)),
        out_shape=jax.ShapeDtypeStruct((bsz, N_HEADS * B_V_DIM, s), _BF16),
        scratch_shapes=[pltpu.VMEM((total, ATTN_Q_TILE), _F32)] * 4
        + [pltpu.VMEM((2, 1, ATTN_Q_TILE), _F32)],
        compiler_params=pltpu.CompilerParams(
            dimension_semantics=("arbitrary", "arbitrary"), vmem_limit_bytes=VMEM_LIMIT),
        name="attn",
    )(jnp.zeros((1,), jnp.int32), *stream_args, *stream_args)


def _merge_pieces(mrow, x, ot, win_ref, wpa_ref, wpb_ref, wo_ref, g_ref, b_ref, result):
    shift, scale, gate = mrow[:, 0:D_MODEL], mrow[:, D_MODEL:2 * D_MODEL], mrow[:, 2 * D_MODEL:3 * D_MODEL]
    h = (_layer_norm(x) * (1.0 + scale) + shift).astype(_BF16)
    yield
    na = A_HEADS * A_HEAD_DIM
    ys = []
    for r in range(D_MODEL // MERGE_ROW_CHUNK):
        ra = pl.ds(r * MERGE_ROW_CHUNK, MERGE_ROW_CHUNK)
        ga_rows = pl.ds(GATE_OFF + r * MERGE_ROW_CHUNK, MERGE_ROW_CHUNK)
        gb_rows = pl.ds(GATE_OFF + D_MODEL + r * MERGE_ROW_CHUNK, MERGE_ROW_CHUNK)
        ga = lax.dot_general(win_ref[ga_rows, :], h, _NT, preferred_element_type=_F32)
        gb = lax.dot_general(win_ref[gb_rows, :], h, _NT, preferred_element_type=_F32)
        ya = jnp.dot(wpa_ref[ra, :], ot[0:na], preferred_element_type=_F32)
        yb = jnp.dot(wpb_ref[ra, :], ot[na:], preferred_element_type=_F32)
        ys.append((_sigmoid(ga) * ya + _sigmoid(gb) * yb).astype(_BF16))
        yield
    zt = jnp.dot(wo_ref[...], jnp.concatenate(ys, axis=0), preferred_element_type=_F32)
    r = DEEPNORM_ALPHA * x + gate * zt.T
    result.append(_layer_norm(r) * g_ref[...] + b_ref[...])
    yield


def _ffn_pieces(mrow, x, wup_ref, wdn_ref, g_ref, b_ref, result):
    shift, scale, gate = mrow[:, 0:D_MODEL], mrow[:, D_MODEL:2 * D_MODEL], mrow[:, 2 * D_MODEL:3 * D_MODEL]
    h = (_layer_norm(x) * (1.0 + scale) + shift).astype(_BF16)
    yield
    fs = []
    for c in range(FFN_HIDDEN // FFN_COL_CHUNK):
        a = jnp.dot(h, wup_ref[:, pl.ds(c * FFN_COL_CHUNK, FFN_COL_CHUNK)], preferred_element_type=_F32)
        u = jnp.dot(h, wup_ref[:, pl.ds(FFN_HIDDEN + c * FFN_COL_CHUNK, FFN_COL_CHUNK)],
                    preferred_element_type=_F32)
        fs.append((a * _sigmoid(a) * u).astype(_BF16))
        yield
    dn = jnp.dot(jnp.concatenate(fs, axis=1), wdn_ref[...], preferred_element_type=_F32)
    r = DEEPNORM_ALPHA * x + gate * dn
    result.append(_layer_norm(r) * g_ref[...] + b_ref[...])
    yield


def _post_kernel(mod_ref, x_ref, ot_ref, win_ref, wpa_ref, wpb_ref, wo_ref, g1_ref, b1_ref,
                 wup_ref, wdn_ref, g2_ref, b2_ref, out_ref, x1_ref, *, tiles_per_batch):
    j = pl.program_id(0)
    n_tiles = pl.num_programs(0) - 1
    half_batch = n_tiles // tiles_per_batch

    @pl.when(j == 0)
    def _():
        x1_ref[...] = jnp.zeros_like(x1_ref)

    b_prev = jnp.maximum(j - 1, 0) // tiles_per_batch
    b_cur = jnp.minimum(j, n_tiles - 1) // tiles_per_batch
    ffn_out, merge_out = ([], []), ([], [])

    def stream(half):
        off = half * half_batch
        return _stagger(
            [_ffn_pieces(mod_ref[pl.ds(b_prev + off, 1), 3 * D_MODEL:6 * D_MODEL], x1_ref[half],
                         wup_ref, wdn_ref, g2_ref, b2_ref, ffn_out[half]),
             _merge_pieces(mod_ref[pl.ds(b_cur + off, 1), 0:3 * D_MODEL], x_ref[half, 0],
                           ot_ref[half, 0], win_ref, wpa_ref, wpb_ref, wo_ref, g1_ref, b1_ref,
                           merge_out[half])],
            rates=[POST_FFN_PIECES_PER_MERGE_PIECE, 1])

    _trace(_stagger([stream(0), stream(1)], lead=POST_STREAM_LEAD))
    for half in range(2):
        out_ref[half, 0] = ffn_out[half][0]
        x1_ref[half] = merge_out[half][0]


def _post(mod, x, ot, w_in_t, wpat, wpbt, wot, g1, b1, wup, wdn, g2, b2):
    bsz, s, d = x.shape
    t = TOKEN_TILE
    tpb = s // t
    hb = bsz // 2
    n_tiles = hb * tpb
    const = lambda a: pl.BlockSpec(a.shape, lambda j: (0,) * a.ndim, pipeline_mode=pl.Buffered(1))
    cur = lambda j: jnp.minimum(j, n_tiles - 1)
    prev = lambda j: jnp.maximum(j - 1, 0)
    out = pl.pallas_call(
        functools.partial(_post_kernel, tiles_per_batch=tpb),
        grid=(n_tiles + 1,),
        in_specs=[
            const(mod),
            pl.BlockSpec((2, 1, t, d), lambda j: (0, cur(j) // tpb, cur(j) % tpb, 0)),
            pl.BlockSpec((2, 1, ot.shape[1], t), lambda j: (0, cur(j) // tpb, 0, cur(j) % tpb)),
            const(w_in_t), const(wpat), const(wpbt), const(wot), const(g1), const(b1),
            const(wup), const(wdn), const(g2), const(b2),
        ],
        out_specs=pl.BlockSpec((2, 1, t, d), lambda j: (0, prev(j) // tpb, prev(j) % tpb, 0)),
        out_shape=jax.ShapeDtypeStruct((2, hb, s, d), _F32),
        scratch_shapes=[pltpu.VMEM((2, t, d), _F32)],
        compiler_params=pltpu.CompilerParams(
            dimension_semantics=("arbitrary",), vmem_limit_bytes=VMEM_LIMIT),
        name="post",
    )(mod, x.reshape(2, hb, s, d), ot.reshape(2, hb, ot.shape[1], s),
      w_in_t, wpat, wpbt, wot, g1, b1, wup, wdn, g2, b2)
    return out.reshape(bsz, s, d)


def _rope_tables(seq, half):
    pos = np.arange(seq)
    freqs = ROPE_THETA ** (-np.arange(half, dtype=np.float64) / half)
    tabs = []
    for p in (pos // GRID_W, pos % GRID_W):
        ang = p[None, :].astype(np.float64) * freqs[:, None]
        for fn, ident in ((np.cos, 1.0), (np.sin, 0.0)):
            tabs.append(np.concatenate([np.full((half, CTX_LEN), ident), fn(ang)], axis=1))
    return jnp.asarray(np.stack(tabs), dtype=_F32)


def kernel(x, c, ctx, c_ctx, w_mod, b_mod, w_in, q_norm_a, k_norm_a, cq_norm, ckv_norm,
           w_uq, w_ukv, w_proj_a, w_proj_b, w_out, ln1_g, ln1_b, w_up, w_down, ln2_g, ln2_b):
    bsz, s, d = x.shape
    assert w_mod.shape[0] == DEPTH == 1 and d == D_MODEL and ctx.shape[1] == CTX_LEN
    assert bsz == 8 and s % ATTN_Q_TILE == 0 and s % TOKEN_TILE == 0

    cc = jnp.concatenate([c, c_ctx[None, :], jnp.zeros((bsz - 1, d), _F32)], axis=0)
    mod = _modulation(cc, w_mod[0], b_mod[0][None, :])

    tbf = lambda w: w[0].astype(_BF16).T
    w_in_t = tbf(w_in)
    col = lambda v: v[0][:, None]
    tab_a = _rope_tables(s, A_HEAD_DIM // 4)
    tab_b = _rope_tables(s, B_ROPE_DIM // 4)

    q_all, k_all, v_all = _project(
        mod, ctx, x, w_in_t, tbf(w_uq), tbf(w_ukv),
        col(q_norm_a), col(k_norm_a), col(cq_norm), col(ckv_norm), tab_a, tab_b)
    ot = _attention(q_all, k_all, v_all)
    return _post(mod, x, ot, w_in_t, tbf(w_proj_a), tbf(w_proj_b), tbf(w_out), ln1_g, ln1_b,
                 w_up[0].astype(_BF16), w_down[0].astype(_BF16), ln2_g, ln2_b)
```

```python
import functools
import math

import numpy as np
import jax
import jax.numpy as jnp
from jax import lax
from jax.experimental import pallas as pl
from jax.experimental.pallas import tpu as pltpu

D_MODEL = 1024
GRID_W = 64
CTX_LEN = 256
ROPE_THETA = 10000.0
EPS = 1e-6

A_HEADS = 8
A_KV_HEADS = 2
A_HEAD_DIM = 64
B_HEADS = 8
B_Q_RANK = 384
B_KV_RANK = 256
B_NOPE_DIM = 64
B_ROPE_DIM = 32
B_V_DIM = 64
FFN_HIDDEN = 2816
DEPTH = 1

A_SCALE = A_HEAD_DIM ** -0.5
B_SCALE = (B_NOPE_DIM + B_ROPE_DIM) ** -0.5
DEEPNORM_ALPHA = (2.0 * DEPTH) ** 0.25
LOG2_E = math.log2(math.e)

QA_OFF = 0
KA_OFF = QA_OFF + A_HEADS * A_HEAD_DIM
VA_OFF = KA_OFF + A_KV_HEADS * A_HEAD_DIM
CQ_OFF = VA_OFF + A_KV_HEADS * A_HEAD_DIM
CKV_OFF = CQ_OFF + B_Q_RANK
KR_OFF = CKV_OFF + B_KV_RANK
GATE_OFF = KR_OFF + B_ROPE_DIM
QKV_COLS = GATE_OFF

N_HEADS = A_HEADS + B_HEADS
HEAD_PAD = 128
N_KSETS = 1 + B_HEADS
V_ROWS = A_KV_HEADS * A_HEAD_DIM + B_HEADS * B_V_DIM

MOD_COL_TILE = 1024
PROJ_TILE = 256
PROJ_ROW_CHUNK = 288
PROJ_STREAMS = 4
PROJ_STREAM_LEAD = 3
ATTN_Q_TILE = 512
ATTN_KEY_CHUNK = 256
ATTN_LOOKAHEAD = 2
ATTN_STREAM_LEAD = 9
ATTN_ONES_ROWS = 16
TOKEN_TILE = 256
MERGE_ROW_CHUNK = 256
FFN_COL_CHUNK = 256
POST_FFN_PIECES_PER_MERGE_PIECE = 2
POST_STREAM_LEAD = 10
VMEM_LIMIT = 56 * 1024 * 1024

_NT = (((1,), (1,)), ((), ()))
_F32 = jnp.float32
_BF16 = jnp.bfloat16


def _layer_norm(x):
    mu = jnp.mean(x, axis=-1, keepdims=True)
    xc = x - mu
    var = jnp.mean(xc * xc, axis=-1, keepdims=True)
    return xc * lax.rsqrt(var + EPS)


def _sigmoid(x):
    return jax.nn.sigmoid(x)


def _mod_kernel(c_ref, w_ref, b_ref, o_ref):
    c = c_ref[...]
    a = (c * _sigmoid(c)).astype(_BF16)
    o_ref[...] = jnp.dot(a, w_ref[...].astype(_BF16), preferred_element_type=_F32) + b_ref[...]


def _modulation(cc, w_mod, b_mod):
    rows, d = cc.shape
    n = w_mod.shape[1]
    return pl.pallas_call(
        _mod_kernel,
        grid=(n // MOD_COL_TILE,),
        in_specs=[
            pl.BlockSpec((rows, d), lambda i: (0, 0)),
            pl.BlockSpec((d, MOD_COL_TILE), lambda i: (0, i)),
            pl.BlockSpec((1, MOD_COL_TILE), lambda i: (0, i)),
        ],
        out_specs=pl.BlockSpec((rows, MOD_COL_TILE), lambda i: (0, i)),
        out_shape=jax.ShapeDtypeStruct((rows, n), _F32),
        compiler_params=pltpu.CompilerParams(
            dimension_semantics=("arbitrary",), vmem_limit_bytes=VMEM_LIMIT),
        name="mod",
    )(cc, w_mod, b_mod)


def _rms_rows(x, g):
    ms = jnp.mean(x * x, axis=0, keepdims=True)
    return x * lax.rsqrt(ms + EPS) * g


def _axial_rope_rows(x, half, tab):
    r1, r2 = x[0:half], x[half:2 * half]
    c1, c2 = x[2 * half:3 * half], x[3 * half:4 * half]
    cr, sr, cc, sc = tab[0], tab[1], tab[2], tab[3]
    return jnp.concatenate(
        [r1 * cr - r2 * sr, r1 * sr + r2 * cr, c1 * cc - c2 * sc, c1 * sc + c2 * cc], axis=0)


def _stagger(streams, lead=0, rates=None):
    rates = rates or [1] * len(streams)
    live = dict(enumerate(streams))
    rnd = 0
    while live:
        for i in sorted(live):
            if rnd < i * lead:
                continue
            for _ in range(rates[i]):
                if next(live[i], StopIteration) is StopIteration:
                    del live[i]
                    break
                yield
        rnd += 1


def _trace(pieces):
    for _ in pieces:
        pass


def _proj_pieces(mrow, xin, w1_ref, wuq_ref, wukv_ref, qn_ref, kn_ref, cqn_ref, ckvn_ref, ta, tb,
                 q_out, k_out, v_out):
    shift, scale = mrow[:, 0:D_MODEL], mrow[:, D_MODEL:2 * D_MODEL]
    h = (_layer_norm(xin) * (1.0 + scale) + shift).astype(_BF16)
    yield
    parts = []
    for r in range(QKV_COLS // PROJ_ROW_CHUNK):
        rows = pl.ds(r * PROJ_ROW_CHUNK, PROJ_ROW_CHUNK)
        parts.append(lax.dot_general(w1_ref[rows, :], h, _NT, preferred_element_type=_F32))
        yield
    pt = jnp.concatenate(parts, axis=0)
    t = pt.shape[1]
    z64 = jnp.zeros((A_HEAD_DIM, t), _F32)
    z32 = jnp.zeros((HEAD_PAD - B_NOPE_DIM - B_ROPE_DIM, t), _F32)

    qn = qn_ref[...] * (A_SCALE * LOG2_E)
    group = A_HEADS // A_KV_HEADS
    for hd in range(A_HEADS):
        xh = pt[QA_OFF + hd * A_HEAD_DIM:QA_OFF + (hd + 1) * A_HEAD_DIM]
        xh = _axial_rope_rows(_rms_rows(xh, qn), A_HEAD_DIM // 4, ta)
        full = jnp.concatenate([xh, z64] if hd // group == 0 else [z64, xh], axis=0)
        q_out[hd] = full.astype(_BF16)
        if hd % 4 == 3:
            yield

    kn = kn_ref[...]
    ka = [
        _axial_rope_rows(
            _rms_rows(pt[KA_OFF + g * A_HEAD_DIM:KA_OFF + (g + 1) * A_HEAD_DIM], kn),
            A_HEAD_DIM // 4, ta)
        for g in range(A_KV_HEADS)
    ]
    k_out[0] = jnp.concatenate(ka, axis=0).T.astype(_BF16)
    v_out[0:A_KV_HEADS * A_HEAD_DIM, :] = pt[VA_OFF:CQ_OFF].astype(_BF16)
    yield

    cq = _rms_rows(pt[CQ_OFF:CKV_OFF], cqn_ref[...]).astype(_BF16)
    qb = jnp.dot(wuq_ref[...], cq, preferred_element_type=_F32) * (B_SCALE * LOG2_E)
    yield
    qdim = B_NOPE_DIM + B_ROPE_DIM
    for hd in range(B_HEADS):
        blk = qb[hd * qdim:(hd + 1) * qdim]
        rp = _axial_rope_rows(blk[B_NOPE_DIM:qdim], B_ROPE_DIM // 4, tb)
        full = jnp.concatenate([blk[0:B_NOPE_DIM], rp, z32], axis=0)
        q_out[A_HEADS + hd] = full.astype(_BF16)
        if hd % 4 == 3:
            yield

    ckv = _rms_rows(pt[CKV_OFF:KR_OFF], ckvn_ref[...]).astype(_BF16)
    kv = jnp.dot(wukv_ref[...], ckv, preferred_element_type=_F32)
    kr = _axial_rope_rows(pt[KR_OFF:GATE_OFF], B_ROPE_DIM // 4, tb)
    yield
    kvdim = B_NOPE_DIM + B_V_DIM
    v_base = A_KV_HEADS * A_HEAD_DIM
    for hd in range(B_HEADS):
        kn_h = kv[hd * kvdim:hd * kvdim + B_NOPE_DIM]
        v_h = kv[hd * kvdim + B_NOPE_DIM:(hd + 1) * kvdim]
        v_out[v_base + hd * B_V_DIM:v_base + (hd + 1) * B_V_DIM, :] = v_h.astype(_BF16)
        k_out[1 + hd] = jnp.concatenate([kn_h, kr, z32], axis=0).T.astype(_BF16)
        if hd % 4 == 3:
            yield


def _proj_kernel(mod_ref, ctx_ref, x_ref, w1_ref, wuq_ref, wukv_ref,
                 qn_ref, kn_ref, cqn_ref, ckvn_ref, ta_ref, tb_ref,
                 q_out, k_out, v_out):
    b = pl.program_id(0)
    j = pl.program_id(1)
    batch_stride = pl.num_programs(0)
    is_ctx = j == 0
    ctx_row = mod_ref.shape[0] // 2
    ta = ta_ref[...]
    tb = tb_ref[...]

    def stream(i):
        xin = jnp.where(is_ctx, ctx_ref[i, 0], x_ref[i, 0])
        mrow = jnp.where(is_ctx, mod_ref[ctx_row:ctx_row + 1, 0:2 * D_MODEL],
                         mod_ref[pl.ds(b + i * batch_stride, 1), 0:2 * D_MODEL])
        return _proj_pieces(mrow, xin, w1_ref, wuq_ref, wukv_ref, qn_ref, kn_ref, cqn_ref, ckvn_ref,
                            ta, tb, q_out.at[i, 0], k_out.at[i, 0], v_out.at[i, 0])

    _trace(_stagger([stream(i) for i in range(PROJ_STREAMS)], lead=PROJ_STREAM_LEAD))


def _project(mod, ctx, x, w_in_t, wuqt, wukvt, qn, kn, cqn, ckvn, tab_a, tab_b):
    bsz, s, d = x.shape
    t = PROJ_TILE
    n_steps = (CTX_LEN + s) // t
    total = CTX_LEN + s
    ns = PROJ_STREAMS
    hb = bsz // ns
    lat = lambda j: jnp.maximum(j - 1, 0)
    full2 = lambda shape: pl.BlockSpec(shape, lambda b, j: (0, 0))
    q_all, k_all, v_all = pl.pallas_call(
        _proj_kernel,
        grid=(hb, n_steps),
        in_specs=[
            full2(mod.shape),
            pl.BlockSpec((ns, 1, CTX_LEN, d), lambda b, j: (0, b, 0, 0)),
            pl.BlockSpec((ns, 1, t, d), lambda b, j: (0, b, lat(j), 0)),
            full2((QKV_COLS, d)),
            full2(wuqt.shape), full2(wukvt.shape),
            full2(qn.shape), full2(kn.shape), full2(cqn.shape), full2(ckvn.shape),
            pl.BlockSpec((4, tab_a.shape[1], t), lambda b, j: (0, 0, j)),
            pl.BlockSpec((4, tab_b.shape[1], t), lambda b, j: (0, 0, j)),
        ],
        out_specs=[
            pl.BlockSpec((ns, 1, N_HEADS, HEAD_PAD, t), lambda b, j: (0, b, 0, 0, lat(j))),
            pl.BlockSpec((ns, 1, N_KSETS, t, HEAD_PAD), lambda b, j: (0, b, 0, j, 0)),
            pl.BlockSpec((ns, 1, V_ROWS, t), lambda b, j: (0, b, 0, j)),
        ],
        out_shape=[
            jax.ShapeDtypeStruct((ns, hb, N_HEADS, HEAD_PAD, s), _BF16),
            jax.ShapeDtypeStruct((ns, hb, N_KSETS, total, HEAD_PAD), _BF16),
            jax.ShapeDtypeStruct((ns, hb, V_ROWS, total), _BF16),
        ],
        compiler_params=pltpu.CompilerParams(
            dimension_semantics=("arbitrary", "arbitrary"), vmem_limit_bytes=VMEM_LIMIT),
        name="proj",
    )(mod, ctx.reshape(ns, hb, CTX_LEN, d), x.reshape(ns, hb, s, d),
      w_in_t, wuqt, wukvt, qn, kn, cqn, ckvn, tab_a, tab_b)
    return (q_all.reshape(bsz, N_HEADS, HEAD_PAD, s), k_all.reshape(bsz, N_KSETS, total, HEAD_PAD),
            v_all.reshape(bsz, V_ROWS, total))


def _attn_stream(base, q_ref, k_ref, v_ref, qn_ref, kn_ref, o_ref, o_rows, bufs, m0_ref, slot):
    tq, kc = ATTN_Q_TILE, ATTN_KEY_CHUNK
    n_tiles = q_ref.shape[3] // tq
    n_chunks = k_ref.shape[2] // kc
    look = ATTN_LOOKAHEAD
    assert n_tiles == len(bufs) and look <= n_tiles
    assert n_tiles * tq == q_ref.shape[3] and n_chunks * kc == k_ref.shape[2]
    dv = v_ref.shape[1]
    ones = jnp.ones((ATTN_ONES_ROWS, kc), _BF16)

    def buf_rows(c):
        return pl.ds(pl.multiple_of(base + c * kc, kc), kc)

    def score_chunk(keys_ref, q_tile, buf, c, m):
        s = jnp.dot(keys_ref[0, 0, pl.ds(c * kc, kc), :], q_tile,
                    preferred_element_type=_F32)
        buf[buf_rows(c), :] = s
        cm = jnp.max(s, axis=0, keepdims=True)
        return cm if m is None else jnp.maximum(m, cm)

    def value_chunk(buf, c, m, acc):
        p = jnp.exp2(buf[buf_rows(c), :] - m)
        vext = jnp.concatenate([v_ref[0, :, pl.ds(c * kc, kc)], ones], axis=0)
        pv = jnp.dot(vext, p.astype(_BF16), preferred_element_type=_F32)
        return pv if acc is None else acc + pv

    @pl.when((pl.program_id(0) == 0) & (pl.program_id(1) == 0))
    def _():
        for t in range(look):
            m = None
            for c in range(n_chunks):
                m = score_chunk(k_ref, q_ref[0, 0, :, pl.ds(t * tq, tq)], bufs[t], c, m)
            m0_ref[slot, t] = m
    yield

    ms = [m0_ref[slot, t] for t in range(look)] + [None] * n_tiles
    for t in range(n_tiles):
        ahead = t + look
        m_new = acc = None
        for c in range(n_chunks):
            if ahead < n_tiles:
                m_new = score_chunk(k_ref, q_ref[0, 0, :, pl.ds(ahead * tq, tq)], bufs[ahead], c, m_new)
            else:
                m_new = score_chunk(kn_ref, qn_ref[0, 0, :, pl.ds((ahead - n_tiles) * tq, tq)],
                                    bufs[ahead - n_tiles], c, m_new)
            acc = value_chunk(bufs[t], c, ms[t], acc)
            yield
        ms[ahead] = m_new
        o_ref[0, o_rows, pl.ds(t * tq, tq)] = (acc[0:dv] / acc[dv:dv + 1]).astype(_BF16)
    for t in range(look):
        m0_ref[slot, t] = ms[n_tiles + t]


def _attn_kernel(base_ref, qa_ref, ka_ref, va_ref, qna_ref, kna_ref,
                 qb_ref, kb_ref, vb_ref, qnb_ref, knb_ref, o_ref,
                 sa0_ref, sa1_ref, sa2_ref, sa3_ref, sb0_ref, sb1_ref, sb2_ref, sb3_ref, m0_ref):
    base = base_ref[0]
    dv = va_ref.shape[1]
    first = _attn_stream(base, qa_ref, ka_ref, va_ref, qna_ref, kna_ref, o_ref, pl.ds(0, dv),
                         (sa0_ref, sa1_ref, sa2_ref, sa3_ref), m0_ref, 0)
    second = _attn_stream(base, qb_ref, kb_ref, vb_ref, qnb_ref, knb_ref, o_ref, pl.ds(dv, dv),
                          (sb0_ref, sb1_ref, sb2_ref, sb3_ref), m0_ref, 1)
    next(first), next(second)
    _trace(_stagger([first, second], lead=ATTN_STREAM_LEAD))


def _attention(q_all, k_all, v_all):
    bsz, _, _, s = q_all.shape
    total = k_all.shape[2]
    group = A_HEADS // A_KV_HEADS
    n_pairs = N_HEADS // 2
    kset = lambda h: jnp.where(h < A_HEADS, 0, h - A_HEADS + 1)
    vblk = lambda h: jnp.where(h < A_HEADS, h // group, h - A_HEADS + A_KV_HEADS)

    def nxt(b, g):
        n = jnp.minimum(b * n_pairs + g + 1, bsz * n_pairs - 1)
        return n // n_pairs, n % n_pairs

    def stream_specs(which):
        head = lambda g: 2 * g + which
        return [
            pl.BlockSpec((1, 1, HEAD_PAD, s), lambda b, g: (b, head(g), 0, 0)),
            pl.BlockSpec((1, 1, total, HEAD_PAD), lambda b, g: (b, kset(head(g)), 0, 0)),
            pl.BlockSpec((1, B_V_DIM, total), lambda b, g: (b, vblk(head(g)), 0)),
            pl.BlockSpec((1, 1, HEAD_PAD, ATTN_LOOKAHEAD * ATTN_Q_TILE),
                         lambda b, g: (nxt(b, g)[0], head(nxt(b, g)[1]), 0, 0)),
            pl.BlockSpec((1, 1, total, HEAD_PAD),
                         lambda b, g: (nxt(b, g)[0], kset(head(nxt(b, g)[1])), 0, 0)),
        ]

    stream_args = (q_all, k_all, v_all, q_all, k_all)
    return pl.pallas_call(
        _attn_kernel,
        grid=(bsz, n_pairs),
        in_specs=[pl.BlockSpec(memory_space=pltpu.SMEM)] + stream_specs(0) + stream_specs(1),
        out_specs=pl.BlockSpec((1, 2 * B_V_DIM, s), lambda b, g: (b, g, 0)),
        out_shape=jax.ShapeDtypeStruct((bsz, N_HEADS * B_V_DIM, s), _BF16),
        scratch_shapes=[pltpu.VMEM((total, ATTN_Q_TILE), _F32)] * 8
        + [pltpu.VMEM((2, ATTN_LOOKAHEAD, 1, ATTN_Q_TILE), _F32)],
        compiler_params=pltpu.CompilerParams(
            dimension_semantics=("arbitrary", "arbitrary"), vmem_limit_bytes=VMEM_LIMIT),
        name="attn",
    )(jnp.zeros((1,), jnp.int32), *stream_args, *stream_args)


def _merge_pieces(mrow, x, ot, win_ref, wpa_ref, wpb_ref, wo_ref, g_ref, b_ref, result):
    shift, scale, gate = mrow[:, 0:D_MODEL], mrow[:, D_MODEL:2 * D_MODEL], mrow[:, 2 * D_MODEL:3 * D_MODEL]
    h = (_layer_norm(x) * (1.0 + scale) + shift).astype(_BF16)
    yield
    na = A_HEADS * A_HEAD_DIM
    ys = []
    for r in range(D_MODEL // MERGE_ROW_CHUNK):
        ra = pl.ds(r * MERGE_ROW_CHUNK, MERGE_ROW_CHUNK)
        ga_rows = pl.ds(GATE_OFF + r * MERGE_ROW_CHUNK, MERGE_ROW_CHUNK)
        gb_rows = pl.ds(GATE_OFF + D_MODEL + r * MERGE_ROW_CHUNK, MERGE_ROW_CHUNK)
        ga = lax.dot_general(win_ref[ga_rows, :], h, _NT, preferred_element_type=_F32)
        gb = lax.dot_general(win_ref[gb_rows, :], h, _NT, preferred_element_type=_F32)
        ya = jnp.dot(wpa_ref[ra, :], ot[0:na], preferred_element_type=_F32)
        yb = jnp.dot(wpb_ref[ra, :], ot[na:], preferred_element_type=_F32)
        ys.append((_sigmoid(ga) * ya + _sigmoid(gb) * yb).astype(_BF16))
        yield
    zt = jnp.dot(wo_ref[...], jnp.concatenate(ys, axis=0), preferred_element_type=_F32)
    r = DEEPNORM_ALPHA * x + gate * zt.T
    result.append(_layer_norm(r) * g_ref[...] + b_ref[...])
    yield


def _ffn_pieces(mrow, x, wup_ref, wdn_ref, g_ref, b_ref, result):
    shift, scale, gate = mrow[:, 0:D_MODEL], mrow[:, D_MODEL:2 * D_MODEL], mrow[:, 2 * D_MODEL:3 * D_MODEL]
    h = (_layer_norm(x) * (1.0 + scale) + shift).astype(_BF16)
    yield
    fs = []
    for c in range(FFN_HIDDEN // FFN_COL_CHUNK):
        a = jnp.dot(h, wup_ref[:, pl.ds(c * FFN_COL_CHUNK, FFN_COL_CHUNK)], preferred_element_type=_F32)
        u = jnp.dot(h, wup_ref[:, pl.ds(FFN_HIDDEN + c * FFN_COL_CHUNK, FFN_COL_CHUNK)],
                    preferred_element_type=_F32)
        fs.append((a * _sigmoid(a) * u).astype(_BF16))
        yield
    dn = jnp.dot(jnp.concatenate(fs, axis=1), wdn_ref[...], preferred_element_type=_F32)
    r = DEEPNORM_ALPHA * x + gate * dn
    result.append(_layer_norm(r) * g_ref[...] + b_ref[...])
    yield


def _post_kernel(mod_ref, x_ref, ot_ref, win_ref, wpa_ref, wpb_ref, wo_ref, g1_ref, b1_ref,
                 wup_ref, wdn_ref, g2_ref, b2_ref, out_ref, x1_ref, *, tiles_per_batch):
    j = pl.program_id(0)
    n_tiles = pl.num_programs(0) - 1
    half_batch = n_tiles // tiles_per_batch

    @pl.when(j == 0)
    def _():
        x1_ref[...] = jnp.zeros_like(x1_ref)

    b_prev = jnp.maximum(j - 1, 0) // tiles_per_batch
    b_cur = jnp.minimum(j, n_tiles - 1) // tiles_per_batch
    ffn_out, merge_out = ([], []), ([], [])

    def stream(half):
        off = half * half_batch
        return _stagger(
            [_ffn_pieces(mod_ref[pl.ds(b_prev + off, 1), 3 * D_MODEL:6 * D_MODEL], x1_ref[half],
                         wup_ref, wdn_ref, g2_ref, b2_ref, ffn_out[half]),
             _merge_pieces(mod_ref[pl.ds(b_cur + off, 1), 0:3 * D_MODEL], x_ref[half, 0],
                           ot_ref[half, 0], win_ref, wpa_ref, wpb_ref, wo_ref, g1_ref, b1_ref,
                           merge_out[half])],
            rates=[POST_FFN_PIECES_PER_MERGE_PIECE, 1])

    _trace(_stagger([stream(0), stream(1)], lead=POST_STREAM_LEAD))
    for half in range(2):
        out_ref[half, 0] = ffn_out[half][0]
        x1_ref[half] = merge_out[half][0]


def _post(mod, x, ot, w_in_t, wpat, wpbt, wot, g1, b1, wup, wdn, g2, b2):
    bsz, s, d = x.shape
    t = TOKEN_TILE
    tpb = s // t
    hb = bsz // 2
    n_tiles = hb * tpb
    const = lambda a: pl.BlockSpec(a.shape, lambda j: (0,) * a.ndim, pipeline_mode=pl.Buffered(1))
    cur = lambda j: jnp.minimum(j, n_tiles - 1)
    prev = lambda j: jnp.maximum(j - 1, 0)
    out = pl.pallas_call(
        functools.partial(_post_kernel, tiles_per_batch=tpb),
        grid=(n_tiles + 1,),
        in_specs=[
            const(mod),
            pl.BlockSpec((2, 1, t, d), lambda j: (0, cur(j) // tpb, cur(j) % tpb, 0)),
            pl.BlockSpec((2, 1, ot.shape[1], t), lambda j: (0, cur(j) // tpb, 0, cur(j) % tpb)),
            const(w_in_t), const(wpat), const(wpbt), const(wot), const(g1), const(b1),
            const(wup), const(wdn), const(g2), const(b2),
        ],
        out_specs=pl.BlockSpec((2, 1, t, d), lambda j: (0, prev(j) // tpb, prev(j) % tpb, 0)),
        out_shape=jax.ShapeDtypeStruct((2, hb, s, d), _F32),
        scratch_shapes=[pltpu.VMEM((2, t, d), _F32)],
        compiler_params=pltpu.CompilerParams(
            dimension_semantics=("arbitrary",), vmem_limit_bytes=VMEM_LIMIT),
        name="post",
    )(mod, x.reshape(2, hb, s, d), ot.reshape(2, hb, ot.shape[1], s),
      w_in_t, wpat, wpbt, wot, g1, b1, wup, wdn, g2, b2)
    return out.reshape(bsz, s, d)


def _rope_tables(seq, half):
    pos = np.arange(seq)
    freqs = ROPE_THETA ** (-np.arange(half, dtype=np.float64) / half)
    tabs = []
    for p in (pos // GRID_W, pos % GRID_W):
        ang = p[None, :].astype(np.float64) * freqs[:, None]
        for fn, ident in ((np.cos, 1.0), (np.sin, 0.0)):
            tabs.append(np.concatenate([np.full((half, CTX_LEN), ident), fn(ang)], axis=1))
    return jnp.asarray(np.stack(tabs), dtype=_F32)


def kernel(x, c, ctx, c_ctx, w_mod, b_mod, w_in, q_norm_a, k_norm_a, cq_norm, ckv_norm,
           w_uq, w_ukv, w_proj_a, w_proj_b, w_out, ln1_g, ln1_b, w_up, w_down, ln2_g, ln2_b):
    bsz, s, d = x.shape
    assert w_mod.shape[0] == DEPTH == 1 and d == D_MODEL and ctx.shape[1] == CTX_LEN
    assert bsz == 8 and s % ATTN_Q_TILE == 0 and s % TOKEN_TILE == 0

    cc = jnp.concatenate([c, c_ctx[None, :], jnp.zeros((bsz - 1, d), _F32)], axis=0)
    mod = _modulation(cc, w_mod[0], b_mod[0][None, :])

    tbf = lambda w: w[0].astype(_BF16).T
    w_in_t = tbf(w_in)
    col = lambda v: v[0][:, None]
    tab_a = _rope_tables(s, A_HEAD_DIM // 4)
    tab_b = _rope_tables(s, B_ROPE_DIM // 4)

    q_all, k_all, v_all = _project(
        mod, ctx, x, w_in_t, tbf(w_uq), tbf(w_ukv),
        col(q_norm_a), col(k_norm_a), col(cq_norm), col(ckv_norm), tab_a, tab_b)
    ot = _attention(q_all, k_all, v_all)
    return _post(mod, x, ot, w_in_t, tbf(w_proj_a), tbf(w_proj_b), tbf(w_out), ln1_g, ln1_b,
                 w_up[0].astype(_BF16), w_down[0].astype(_BF16), ln2_g, ln2_b)
```

```python
import functools
import math

import numpy as np
import jax
import jax.numpy as jnp
from jax import lax
from jax.experimental import pallas as pl
from jax.experimental.pallas import tpu as pltpu

D_MODEL = 1024
GRID_W = 64
CTX_LEN = 256
ROPE_THETA = 10000.0
EPS = 1e-6

A_HEADS = 8
A_KV_HEADS = 2
A_HEAD_DIM = 64
B_HEADS = 8
B_Q_RANK = 384
B_KV_RANK = 256
B_NOPE_DIM = 64
B_ROPE_DIM = 32
B_V_DIM = 64
FFN_HIDDEN = 2816
DEPTH = 1

A_SCALE = A_HEAD_DIM ** -0.5
B_SCALE = (B_NOPE_DIM + B_ROPE_DIM) ** -0.5
DEEPNORM_ALPHA = (2.0 * DEPTH) ** 0.25
LOG2_E = math.log2(math.e)

QA_OFF = 0
KA_OFF = QA_OFF + A_HEADS * A_HEAD_DIM
VA_OFF = KA_OFF + A_KV_HEADS * A_HEAD_DIM
CQ_OFF = VA_OFF + A_KV_HEADS * A_HEAD_DIM
CKV_OFF = CQ_OFF + B_Q_RANK
KR_OFF = CKV_OFF + B_KV_RANK
GATE_OFF = KR_OFF + B_ROPE_DIM
QKV_COLS = GATE_OFF

N_HEADS = A_HEADS + B_HEADS
HEAD_PAD = 128
N_KSETS = 1 + B_HEADS
V_ROWS = A_KV_HEADS * A_HEAD_DIM + B_HEADS * B_V_DIM

MOD_COL_TILE = 1024
PROJ_TILE = 256
PROJ_ROW_CHUNK = 288
PROJ_STREAMS = 4
PROJ_STREAM_LEAD = 3
PROJ_X_BUFFERS = 3
ATTN_Q_TILE = 512
ATTN_KEY_CHUNK = 256
ATTN_STREAM_LEAD = 9
ATTN_ONES_ROWS = 16
TOKEN_TILE = 256
MERGE_ROW_CHUNK = 256
FFN_COL_CHUNK = 256
POST_FFN_PIECES_PER_MERGE_PIECE = 2
POST_STREAM_LEAD = 10
VMEM_LIMIT = 56 * 1024 * 1024

_NT = (((1,), (1,)), ((), ()))
_F32 = jnp.float32
_BF16 = jnp.bfloat16


def _layer_norm(x):
    mu = jnp.mean(x, axis=-1, keepdims=True)
    xc = x - mu
    var = jnp.mean(xc * xc, axis=-1, keepdims=True)
    return xc * lax.rsqrt(var + EPS)


def _sigmoid(x):
    return jax.nn.sigmoid(x)


def _mod_kernel(c_ref, w_ref, b_ref, o_ref):
    c = c_ref[...]
    a = (c * _sigmoid(c)).astype(_BF16)
    o_ref[...] = jnp.dot(a, w_ref[...].astype(_BF16), preferred_element_type=_F32) + b_ref[...]


def _modulation(cc, w_mod, b_mod):
    rows, d = cc.shape
    n = w_mod.shape[1]
    return pl.pallas_call(
        _mod_kernel,
        grid=(n // MOD_COL_TILE,),
        in_specs=[
            pl.BlockSpec((rows, d), lambda i: (0, 0)),
            pl.BlockSpec((d, MOD_COL_TILE), lambda i: (0, i)),
            pl.BlockSpec((1, MOD_COL_TILE), lambda i: (0, i)),
        ],
        out_specs=pl.BlockSpec((rows, MOD_COL_TILE), lambda i: (0, i)),
        out_shape=jax.ShapeDtypeStruct((rows, n), _F32),
        compiler_params=pltpu.CompilerParams(
            dimension_semantics=("arbitrary",), vmem_limit_bytes=VMEM_LIMIT),
        name="mod",
    )(cc, w_mod, b_mod)


def _rms_rows(x, g):
    ms = jnp.mean(x * x, axis=0, keepdims=True)
    return x * lax.rsqrt(ms + EPS) * g


def _axial_rope_rows(x, half, tab):
    r1, r2 = x[0:half], x[half:2 * half]
    c1, c2 = x[2 * half:3 * half], x[3 * half:4 * half]
    cr, sr, cc, sc = tab[0], tab[1], tab[2], tab[3]
    return jnp.concatenate(
        [r1 * cr - r2 * sr, r1 * sr + r2 * cr, c1 * cc - c2 * sc, c1 * sc + c2 * cc], axis=0)


def _stagger(streams, lead=0, rates=None):
    rates = rates or [1] * len(streams)
    live = dict(enumerate(streams))
    rnd = 0
    while live:
        for i in sorted(live):
            if rnd < i * lead:
                continue
            for _ in range(rates[i]):
                if next(live[i], StopIteration) is StopIteration:
                    del live[i]
                    break
                yield
        rnd += 1


def _trace(pieces):
    for _ in pieces:
        pass


def _proj_pieces(mrow, xin, w1_ref, wuq_ref, wukv_ref, qn_ref, kn_ref, cqn_ref, ckvn_ref, ta, tb,
                 q_out, k_out, v_out):
    shift, scale = mrow[:, 0:D_MODEL], mrow[:, D_MODEL:2 * D_MODEL]
    h = (_layer_norm(xin) * (1.0 + scale) + shift).astype(_BF16)
    yield
    parts = []
    for r in range(QKV_COLS // PROJ_ROW_CHUNK):
        rows = pl.ds(r * PROJ_ROW_CHUNK, PROJ_ROW_CHUNK)
        parts.append(lax.dot_general(w1_ref[rows, :], h, _NT, preferred_element_type=_F32))
        yield
    pt = jnp.concatenate(parts, axis=0)
    t = pt.shape[1]
    z64 = jnp.zeros((A_HEAD_DIM, t), _F32)
    z32 = jnp.zeros((HEAD_PAD - B_NOPE_DIM - B_ROPE_DIM, t), _F32)

    qn = qn_ref[...] * (A_SCALE * LOG2_E)
    group = A_HEADS // A_KV_HEADS
    for hd in range(A_HEADS):
        xh = pt[QA_OFF + hd * A_HEAD_DIM:QA_OFF + (hd + 1) * A_HEAD_DIM]
        xh = _axial_rope_rows(_rms_rows(xh, qn), A_HEAD_DIM // 4, ta)
        full = jnp.concatenate([xh, z64] if hd // group == 0 else [z64, xh], axis=0)
        q_out[hd] = full.astype(_BF16)
        if hd % 4 == 3:
            yield

    kn = kn_ref[...]
    ka = [
        _axial_rope_rows(
            _rms_rows(pt[KA_OFF + g * A_HEAD_DIM:KA_OFF + (g + 1) * A_HEAD_DIM], kn),
            A_HEAD_DIM // 4, ta)
        for g in range(A_KV_HEADS)
    ]
    k_out[0] = jnp.concatenate(ka, axis=0).T.astype(_BF16)
    v_out[0:A_KV_HEADS * A_HEAD_DIM, :] = pt[VA_OFF:CQ_OFF].astype(_BF16)
    yield

    cq = _rms_rows(pt[CQ_OFF:CKV_OFF], cqn_ref[...]).astype(_BF16)
    qb = jnp.dot(wuq_ref[...], cq, preferred_element_type=_F32) * (B_SCALE * LOG2_E)
    yield
    qdim = B_NOPE_DIM + B_ROPE_DIM
    for hd in range(B_HEADS):
        blk = qb[hd * qdim:(hd + 1) * qdim]
        rp = _axial_rope_rows(blk[B_NOPE_DIM:qdim], B_ROPE_DIM // 4, tb)
        full = jnp.concatenate([blk[0:B_NOPE_DIM], rp, z32], axis=0)
        q_out[A_HEADS + hd] = full.astype(_BF16)
        if hd % 4 == 3:
            yield

    ckv = _rms_rows(pt[CKV_OFF:KR_OFF], ckvn_ref[...]).astype(_BF16)
    kv = jnp.dot(wukv_ref[...], ckv, preferred_element_type=_F32)
    kr = _axial_rope_rows(pt[KR_OFF:GATE_OFF], B_ROPE_DIM // 4, tb)
    yield
    kvdim = B_NOPE_DIM + B_V_DIM
    v_base = A_KV_HEADS * A_HEAD_DIM
    for hd in range(B_HEADS):
        kn_h = kv[hd * kvdim:hd * kvdim + B_NOPE_DIM]
        v_h = kv[hd * kvdim + B_NOPE_DIM:(hd + 1) * kvdim]
        v_out[v_base + hd * B_V_DIM:v_base + (hd + 1) * B_V_DIM, :] = v_h.astype(_BF16)
        k_out[1 + hd] = jnp.concatenate([kn_h, kr, z32], axis=0).T.astype(_BF16)
        if hd % 4 == 3:
            yield


def _proj_kernel(mod_ref, ctx_ref, x_ref, w1_ref, wuq_ref, wukv_ref,
                 qn_ref, kn_ref, cqn_ref, ckvn_ref, ta_ref, tb_ref,
                 q_out, k_out, v_out, xbuf_ref, xsem_ref):
    b = pl.program_id(0)
    j = pl.program_id(1)
    batch_stride = pl.num_programs(0)
    n_j = pl.num_programs(1)
    n_steps = batch_stride * n_j
    step = b * n_j + j
    t = xbuf_ref.shape[2]
    depth = PROJ_X_BUFFERS - 1

    def x_copy(s):
        sb, sj = s // n_j, s % n_j
        rows = pl.ds(pl.multiple_of(jnp.maximum(sj - 1, 0) * t, t), t)
        slot = s % PROJ_X_BUFFERS
        return pltpu.make_async_copy(x_ref.at[:, sb, rows, :], xbuf_ref.at[slot], xsem_ref.at[slot])

    @pl.when(step == 0)
    def _():
        for s in range(depth):
            x_copy(s).start()

    @pl.when(step + depth < n_steps)
    def _():
        x_copy(step + depth).start()

    x_copy(step).wait()
    x_tiles = xbuf_ref.at[step % PROJ_X_BUFFERS]

    is_ctx = j == 0
    ctx_row = mod_ref.shape[0] // 2
    ta = ta_ref[...]
    tb = tb_ref[...]

    def stream(i):
        xin = jnp.where(is_ctx, ctx_ref[i, 0], x_tiles[i])
        mrow = jnp.where(is_ctx, mod_ref[ctx_row:ctx_row + 1, 0:2 * D_MODEL],
                         mod_ref[pl.ds(b + i * batch_stride, 1), 0:2 * D_MODEL])
        return _proj_pieces(mrow, xin, w1_ref, wuq_ref, wukv_ref, qn_ref, kn_ref, cqn_ref, ckvn_ref,
                            ta, tb, q_out.at[i, 0], k_out.at[i, 0], v_out.at[i, 0])

    _trace(_stagger([stream(i) for i in range(PROJ_STREAMS)], lead=PROJ_STREAM_LEAD))


def _project(mod, ctx, x, w_in_t, wuqt, wukvt, qn, kn, cqn, ckvn, tab_a, tab_b):
    bsz, s, d = x.shape
    t = PROJ_TILE
    n_steps = (CTX_LEN + s) // t
    total = CTX_LEN + s
    ns = PROJ_STREAMS
    hb = bsz // ns
    lat = lambda j: jnp.maximum(j - 1, 0)
    full2 = lambda shape: pl.BlockSpec(shape, lambda b, j: (0, 0))
    q_all, k_all, v_all = pl.pallas_call(
        _proj_kernel,
        grid=(hb, n_steps),
        in_specs=[
            full2(mod.shape),
            pl.BlockSpec((ns, 1, CTX_LEN, d), lambda b, j: (0, b, 0, 0)),
            pl.BlockSpec(memory_space=pl.ANY),
            full2((QKV_COLS, d)),
            full2(wuqt.shape), full2(wukvt.shape),
            full2(qn.shape), full2(kn.shape), full2(cqn.shape), full2(ckvn.shape),
            pl.BlockSpec((4, tab_a.shape[1], t), lambda b, j: (0, 0, j)),
            pl.BlockSpec((4, tab_b.shape[1], t), lambda b, j: (0, 0, j)),
        ],
        out_specs=[
            pl.BlockSpec((ns, 1, N_HEADS, HEAD_PAD, t), lambda b, j: (0, b, 0, 0, lat(j))),
            pl.BlockSpec((ns, 1, N_KSETS, t, HEAD_PAD), lambda b, j: (0, b, 0, j, 0)),
            pl.BlockSpec((ns, 1, V_ROWS, t), lambda b, j: (0, b, 0, j)),
        ],
        out_shape=[
            jax.ShapeDtypeStruct((ns, hb, N_HEADS, HEAD_PAD, s), _BF16),
            jax.ShapeDtypeStruct((ns, hb, N_KSETS, total, HEAD_PAD), _BF16),
            jax.ShapeDtypeStruct((ns, hb, V_ROWS, total), _BF16),
        ],
        scratch_shapes=[pltpu.VMEM((PROJ_X_BUFFERS, ns, t, d), _F32),
                        pltpu.SemaphoreType.DMA((PROJ_X_BUFFERS,))],
        compiler_params=pltpu.CompilerParams(
            dimension_semantics=("arbitrary", "arbitrary"), vmem_limit_bytes=VMEM_LIMIT),
        name="proj",
    )(mod, ctx.reshape(ns, hb, CTX_LEN, d), x.reshape(ns, hb, s, d),
      w_in_t, wuqt, wukvt, qn, kn, cqn, ckvn, tab_a, tab_b)
    return (q_all.reshape(bsz, N_HEADS, HEAD_PAD, s), k_all.reshape(bsz, N_KSETS, total, HEAD_PAD),
            v_all.reshape(bsz, V_ROWS, total))


def _attn_stream(base, q_ref, k_ref, v_ref, qn_ref, kn_ref, o_ref, o_rows, bufs, m0_ref, slot):
    tq, kc = ATTN_Q_TILE, ATTN_KEY_CHUNK
    n_tiles = q_ref.shape[3] // tq
    n_chunks = k_ref.shape[2] // kc
    assert n_tiles % 2 == 0
    assert n_tiles * tq == q_ref.shape[3] and n_chunks * kc == k_ref.shape[2]
    dv = v_ref.shape[1]
    ones = jnp.ones((ATTN_ONES_ROWS, kc), _BF16)

    def buf_rows(c):
        return pl.ds(pl.multiple_of(base + c * kc, kc), kc)

    def score_chunk(keys_ref, q_tile, buf, c, m):
        s = jnp.dot(keys_ref[0, 0, pl.ds(c * kc, kc), :], q_tile,
                    preferred_element_type=_F32)
        buf[buf_rows(c), :] = s
        cm = jnp.max(s, axis=0, keepdims=True)
        return cm if m is None else jnp.maximum(m, cm)

    def value_chunk(buf, c, m, acc):
        p = jnp.exp2(buf[buf_rows(c), :] - m)
        vext = jnp.concatenate([v_ref[0, :, pl.ds(c * kc, kc)], ones], axis=0)
        pv = jnp.dot(vext, p.astype(_BF16), preferred_element_type=_F32)
        return pv if acc is None else acc + pv

    @pl.when((pl.program_id(0) == 0) & (pl.program_id(1) == 0))
    def _():
        m = None
        for c in range(n_chunks):
            m = score_chunk(k_ref, q_ref[0, 0, :, 0:tq], bufs[0], c, m)
        m0_ref[slot] = m
    yield

    m_cur = m0_ref[slot]
    for t in range(n_tiles):
        m_next = acc = None
        for c in range(n_chunks):
            if t + 1 < n_tiles:
                m_next = score_chunk(k_ref, q_ref[0, 0, :, pl.ds((t + 1) * tq, tq)],
                                     bufs[(t + 1) % 2], c, m_next)
            else:
                m_next = score_chunk(kn_ref, qn_ref[0, 0], bufs[0], c, m_next)
            acc = value_chunk(bufs[t % 2], c, m_cur, acc)
            yield
        o_ref[0, o_rows, pl.ds(t * tq, tq)] = (acc[0:dv] / acc[dv:dv + 1]).astype(_BF16)
        m_cur = m_next
    m0_ref[slot] = m_cur


def _attn_kernel(base_ref, qa_ref, ka_ref, va_ref, qna_ref, kna_ref,
                 qb_ref, kb_ref, vb_ref, qnb_ref, knb_ref, o_ref,
                 sa0_ref, sa1_ref, sb0_ref, sb1_ref, m0_ref):
    base = base_ref[0]
    dv = va_ref.shape[1]
    first = _attn_stream(base, qa_ref, ka_ref, va_ref, qna_ref, kna_ref, o_ref, pl.ds(0, dv),
                         (sa0_ref, sa1_ref), m0_ref, 0)
    second = _attn_stream(base, qb_ref, kb_ref, vb_ref, qnb_ref, knb_ref, o_ref, pl.ds(dv, dv),
                          (sb0_ref, sb1_ref), m0_ref, 1)
    next(first), next(second)
    _trace(_stagger([first, second], lead=ATTN_STREAM_LEAD))


def _attention(q_all, k_all, v_all):
    bsz, _, _, s = q_all.shape
    total = k_all.shape[2]
    group = A_HEADS // A_KV_HEADS
    n_pairs = N_HEADS // 2
    kset = lambda h: jnp.where(h < A_HEADS, 0, h - A_HEADS + 1)
    vblk = lambda h: jnp.where(h < A_HEADS, h // group, h - A_HEADS + A_KV_HEADS)

    def nxt(b, g):
        n = jnp.minimum(b * n_pairs + g + 1, bsz * n_pairs - 1)
        return n // n_pairs, n % n_pairs

    def stream_specs(which):
        head = lambda g: 2 * g + which
        return [
            pl.BlockSpec((1, 1, HEAD_PAD, s), lambda b, g: (b, head(g), 0, 0)),
            pl.BlockSpec((1, 1, total, HEAD_PAD), lambda b, g: (b, kset(head(g)), 0, 0)),
            pl.BlockSpec((1, B_V_DIM, total), lambda b, g: (b, vblk(head(g)), 0)),
            pl.BlockSpec((1, 1, HEAD_PAD, ATTN_Q_TILE),
                         lambda b, g: (nxt(b, g)[0], head(nxt(b, g)[1]), 0, 0)),
            pl.BlockSpec((1, 1, total, HEAD_PAD),
                         lambda b, g: (nxt(b, g)[0], kset(head(nxt(b, g)[1])), 0, 0)),
        ]

    stream_args = (q_all, k_all, v_all, q_all, k_all)
    return pl.pallas_call(
        _attn_kernel,
        grid=(bsz, n_pairs),
        in_specs=[pl.BlockSpec(memory_space=pltpu.SMEM)] + stream_specs(0) + stream_specs(1),
        out_specs=pl.BlockSpec((1, 2 * B_V_DIM, s), lambda b, g: (b, g, 0)),
        out_shape=jax.ShapeDtypeStruct((bsz, N_HEADS * B_V_DIM, s), _BF16),
        scratch_shapes=[pltpu.VMEM((total, ATTN_Q_TILE), _F32)] * 4
        + [pltpu.VMEM((2, 1, ATTN_Q_TILE), _F32)],
        compiler_params=pltpu.CompilerParams(
            dimension_semantics=("arbitrary", "arbitrary"), vmem_limit_bytes=VMEM_LIMIT),
        name="attn",
    )(jnp.zeros((1,), jnp.int32), *stream_args, *stream_args)


def _merge_pieces(mrow, x, ot, win_ref, wpa_ref, wpb_ref, wo_ref, g_ref, b_ref, result):
    shift, scale, gate = mrow[:, 0:D_MODEL], mrow[:, D_MODEL:2 * D_MODEL], mrow[:, 2 * D_MODEL:3 * D_MODEL]
    h = (_layer_norm(x) * (1.0 + scale) + shift).astype(_BF16)
    yield
    na = A_HEADS * A_HEAD_DIM
    ys = []
    for r in range(D_MODEL // MERGE_ROW_CHUNK):
        ra = pl.ds(r * MERGE_ROW_CHUNK, MERGE_ROW_CHUNK)
        ga_rows = pl.ds(GATE_OFF + r * MERGE_ROW_CHUNK, MERGE_ROW_CHUNK)
        gb_rows = pl.ds(GATE_OFF + D_MODEL + r * MERGE_ROW_CHUNK, MERGE_ROW_CHUNK)
        ga = lax.dot_general(win_ref[ga_rows, :], h, _NT, preferred_element_type=_F32)
        gb = lax.dot_general(win_ref[gb_rows, :], h, _NT, preferred_element_type=_F32)
        ya = jnp.dot(wpa_ref[ra, :], ot[0:na], preferred_element_type=_F32)
        yb = jnp.dot(wpb_ref[ra, :], ot[na:], preferred_element_type=_F32)
        ys.append((_sigmoid(ga) * ya + _sigmoid(gb) * yb).astype(_BF16))
        yield
    zt = jnp.dot(wo_ref[...], jnp.concatenate(ys, axis=0), preferred_element_type=_F32)
    r = DEEPNORM_ALPHA * x + gate * zt.T
    result.append(_layer_norm(r) * g_ref[...] + b_ref[...])
    yield


def _ffn_pieces(mrow, x, wup_ref, wdn_ref, g_ref, b_ref, result):
    shift, scale, gate = mrow[:, 0:D_MODEL], mrow[:, D_MODEL:2 * D_MODEL], mrow[:, 2 * D_MODEL:3 * D_MODEL]
    h = (_layer_norm(x) * (1.0 + scale) + shift).astype(_BF16)
    yield
    fs = []
    for c in range(FFN_HIDDEN // FFN_COL_CHUNK):
        a = jnp.dot(h, wup_ref[:, pl.ds(c * FFN_COL_CHUNK, FFN_COL_CHUNK)], preferred_element_type=_F32)
        u = jnp.dot(h, wup_ref[:, pl.ds(FFN_HIDDEN + c * FFN_COL_CHUNK, FFN_COL_CHUNK)],
                    preferred_element_type=_F32)
        fs.append((a * _sigmoid(a) * u).astype(_BF16))
        yield
    dn = jnp.dot(jnp.concatenate(fs, axis=1), wdn_ref[...], preferred_element_type=_F32)
    r = DEEPNORM_ALPHA * x + gate * dn
    result.append(_layer_norm(r) * g_ref[...] + b_ref[...])
    yield


def _post_kernel(mod_ref, x_ref, ot_ref, win_ref, wpa_ref, wpb_ref, wo_ref, g1_ref, b1_ref,
                 wup_ref, wdn_ref, g2_ref, b2_ref, out_ref, x1_ref, *, tiles_per_batch):
    j = pl.program_id(0)
    n_tiles = pl.num_programs(0) - 1
    half_batch = n_tiles // tiles_per_batch

    @pl.when(j == 0)
    def _():
        x1_ref[...] = jnp.zeros_like(x1_ref)

    b_prev = jnp.maximum(j - 1, 0) // tiles_per_batch
    b_cur = jnp.minimum(j, n_tiles - 1) // tiles_per_batch
    ffn_out, merge_out = ([], []), ([], [])

    def stream(half):
        off = half * half_batch
        return _stagger(
            [_ffn_pieces(mod_ref[pl.ds(b_prev + off, 1), 3 * D_MODEL:6 * D_MODEL], x1_ref[half],
                         wup_ref, wdn_ref, g2_ref, b2_ref, ffn_out[half]),
             _merge_pieces(mod_ref[pl.ds(b_cur + off, 1), 0:3 * D_MODEL], x_ref[half, 0],
                           ot_ref[half, 0], win_ref, wpa_ref, wpb_ref, wo_ref, g1_ref, b1_ref,
                           merge_out[half])],
            rates=[POST_FFN_PIECES_PER_MERGE_PIECE, 1])

    _trace(_stagger([stream(0), stream(1)], lead=POST_STREAM_LEAD))
    for half in range(2):
        out_ref[half, 0] = ffn_out[half][0]
        x1_ref[half] = merge_out[half][0]


def _post(mod, x, ot, w_in_t, wpat, wpbt, wot, g1, b1, wup, wdn, g2, b2):
    bsz, s, d = x.shape
    t = TOKEN_TILE
    tpb = s // t
    hb = bsz // 2
    n_tiles = hb * tpb
    const = lambda a: pl.BlockSpec(a.shape, lambda j: (0,) * a.ndim, pipeline_mode=pl.Buffered(1))
    cur = lambda j: jnp.minimum(j, n_tiles - 1)
    prev = lambda j: jnp.maximum(j - 1, 0)
    out = pl.pallas_call(
        functools.partial(_post_kernel, tiles_per_batch=tpb),
        grid=(n_tiles + 1,),
        in_specs=[
            const(mod),
            pl.BlockSpec((2, 1, t, d), lambda j: (0, cur(j) // tpb, cur(j) % tpb, 0)),
            pl.BlockSpec((2, 1, ot.shape[1], t), lambda j: (0, cur(j) // tpb, 0, cur(j) % tpb)),
            const(w_in_t), const(wpat), const(wpbt), const(wot), const(g1), const(b1),
            const(wup), const(wdn), const(g2), const(b2),
        ],
        out_specs=pl.BlockSpec((2, 1, t, d), lambda j: (0, prev(j) // tpb, prev(j) % tpb, 0)),
        out_shape=jax.ShapeDtypeStruct((2, hb, s, d), _F32),
        scratch_shapes=[pltpu.VMEM((2, t, d), _F32)],
        compiler_params=pltpu.CompilerParams(
            dimension_semantics=("arbitrary",), vmem_limit_bytes=VMEM_LIMIT),
        name="post",
    )(mod, x.reshape(2, hb, s, d), ot.reshape(2, hb, ot.shape[1], s),
      w_in_t, wpat, wpbt, wot, g1, b1, wup, wdn, g2, b2)
    return out.reshape(bsz, s, d)


def _rope_tables(seq, half):
    pos = np.arange(seq)
    freqs = ROPE_THETA ** (-np.arange(half, dtype=np.float64) / half)
    tabs = []
    for p in (pos // GRID_W, pos % GRID_W):
        ang = p[None, :].astype(np.float64) * freqs[:, None]
        for fn, ident in ((np.cos, 1.0), (np.sin, 0.0)):
            tabs.append(np.concatenate([np.full((half, CTX_LEN), ident), fn(ang)], axis=1))
    return jnp.asarray(np.stack(tabs), dtype=_F32)


def kernel(x, c, ctx, c_ctx, w_mod, b_mod, w_in, q_norm_a, k_norm_a, cq_norm, ckv_norm,
           w_uq, w_ukv, w_proj_a, w_proj_b, w_out, ln1_g, ln1_b, w_up, w_down, ln2_g, ln2_b):
    bsz, s, d = x.shape
    assert w_mod.shape[0] == DEPTH == 1 and d == D_MODEL and ctx.shape[1] == CTX_LEN
    assert bsz == 8 and s % ATTN_Q_TILE == 0 and s % TOKEN_TILE == 0

    cc = jnp.concatenate([c, c_ctx[None, :], jnp.zeros((bsz - 1, d), _F32)], axis=0)
    mod = _modulation(cc, w_mod[0], b_mod[0][None, :])

    tbf = lambda w: w[0].astype(_BF16).T
    w_in_t = tbf(w_in)
    col = lambda v: v[0][:, None]
    tab_a = _rope_tables(s, A_HEAD_DIM // 4)
    tab_b = _rope_tables(s, B_ROPE_DIM // 4)

    q_all, k_all, v_all = _project(
        mod, ctx, x, w_in_t, tbf(w_uq), tbf(w_ukv),
        col(q_norm_a), col(k_norm_a), col(cq_norm), col(ckv_norm), tab_a, tab_b)
    ot = _attention(q_all, k_all, v_all)
    return _post(mod, x, ot, w_in_t, tbf(w_proj_a), tbf(w_proj_b), tbf(w_out), ln1_g, ln1_b,
                 w_up[0].astype(_BF16), w_down[0].astype(_BF16), ln2_g, ln2_b)
```

```python
import functools
import math

import numpy as np
import jax
import jax.numpy as jnp
from jax import lax
from jax.experimental import pallas as pl
from jax.experimental.pallas import tpu as pltpu

D_MODEL = 1024
GRID_W = 64
CTX_LEN = 256
ROPE_THETA = 10000.0
EPS = 1e-6

A_HEADS = 8
A_KV_HEADS = 2
A_HEAD_DIM = 64
B_HEADS = 8
B_Q_RANK = 384
B_KV_RANK = 256
B_NOPE_DIM = 64
B_ROPE_DIM = 32
B_V_DIM = 64
FFN_HIDDEN = 2816
DEPTH = 1

A_SCALE = A_HEAD_DIM ** -0.5
B_SCALE = (B_NOPE_DIM + B_ROPE_DIM) ** -0.5
DEEPNORM_ALPHA = (2.0 * DEPTH) ** 0.25
LOG2_E = math.log2(math.e)

QA_OFF = 0
KA_OFF = QA_OFF + A_HEADS * A_HEAD_DIM
VA_OFF = KA_OFF + A_KV_HEADS * A_HEAD_DIM
CQ_OFF = VA_OFF + A_KV_HEADS * A_HEAD_DIM
CKV_OFF = CQ_OFF + B_Q_RANK
KR_OFF = CKV_OFF + B_KV_RANK
GATE_OFF = KR_OFF + B_ROPE_DIM
QKV_COLS = GATE_OFF

N_HEADS = A_HEADS + B_HEADS
HEAD_PAD = 128
N_KSETS = 1 + B_HEADS
V_ROWS = A_KV_HEADS * A_HEAD_DIM + B_HEADS * B_V_DIM

MOD_COL_TILE = 1024
PROJ_TILE = 256
PROJ_ROW_CHUNK = 288
PROJ_STREAMS = 4
PROJ_STREAM_LEAD = 3
ATTN_Q_TILE = 512
ATTN_KEY_CHUNK = 256
ATTN_STREAM_LEAD = 9
ATTN_ONES_ROWS = 16
TOKEN_TILE = 256
MERGE_ROW_CHUNK = 256
FFN_COL_CHUNK = 256
POST_FFN_PIECES_PER_MERGE_PIECE = 2
POST_STREAM_LEAD = 10
VMEM_LIMIT = 56 * 1024 * 1024
ATTN_VMEM_LIMIT = 40 * 1024 * 1024

_NT = (((1,), (1,)), ((), ()))
_F32 = jnp.float32
_BF16 = jnp.bfloat16


def _layer_norm(x):
    mu = jnp.mean(x, axis=-1, keepdims=True)
    xc = x - mu
    var = jnp.mean(xc * xc, axis=-1, keepdims=True)
    return xc * lax.rsqrt(var + EPS)


def _sigmoid(x):
    return jax.nn.sigmoid(x)


def _mod_kernel(c_ref, w_ref, b_ref, o_ref):
    c = c_ref[...]
    a = (c * _sigmoid(c)).astype(_BF16)
    o_ref[...] = jnp.dot(a, w_ref[...].astype(_BF16), preferred_element_type=_F32) + b_ref[...]


def _modulation(cc, w_mod, b_mod):
    rows, d = cc.shape
    n = w_mod.shape[1]
    return pl.pallas_call(
        _mod_kernel,
        grid=(n // MOD_COL_TILE,),
        in_specs=[
            pl.BlockSpec((rows, d), lambda i: (0, 0)),
            pl.BlockSpec((d, MOD_COL_TILE), lambda i: (0, i)),
            pl.BlockSpec((1, MOD_COL_TILE), lambda i: (0, i)),
        ],
        out_specs=pl.BlockSpec((rows, MOD_COL_TILE), lambda i: (0, i)),
        out_shape=jax.ShapeDtypeStruct((rows, n), _F32),
        compiler_params=pltpu.CompilerParams(
            dimension_semantics=("arbitrary",), vmem_limit_bytes=VMEM_LIMIT),
        name="mod",
    )(cc, w_mod, b_mod)


def _rms_rows(x, g):
    ms = jnp.mean(x * x, axis=0, keepdims=True)
    return x * lax.rsqrt(ms + EPS) * g


def _axial_rope_rows(x, half, tab):
    r1, r2 = x[0:half], x[half:2 * half]
    c1, c2 = x[2 * half:3 * half], x[3 * half:4 * half]
    cr, sr, cc, sc = tab[0], tab[1], tab[2], tab[3]
    return jnp.concatenate(
        [r1 * cr - r2 * sr, r1 * sr + r2 * cr, c1 * cc - c2 * sc, c1 * sc + c2 * cc], axis=0)


def _stagger(streams, lead=0, rates=None):
    rates = rates or [1] * len(streams)
    live = dict(enumerate(streams))
    rnd = 0
    while live:
        for i in sorted(live):
            if rnd < i * lead:
                continue
            for _ in range(rates[i]):
                if next(live[i], StopIteration) is StopIteration:
                    del live[i]
                    break
                yield
        rnd += 1


def _trace(pieces):
    for _ in pieces:
        pass


def _proj_pieces(mrow, xin, w1_ref, wuq_ref, wukv_ref, qn_ref, kn_ref, cqn_ref, ckvn_ref, ta, tb,
                 q_out, k_out, v_out):
    shift, scale = mrow[:, 0:D_MODEL], mrow[:, D_MODEL:2 * D_MODEL]
    h = (_layer_norm(xin) * (1.0 + scale) + shift).astype(_BF16)
    yield
    parts = []
    for r in range(QKV_COLS // PROJ_ROW_CHUNK):
        rows = pl.ds(r * PROJ_ROW_CHUNK, PROJ_ROW_CHUNK)
        parts.append(lax.dot_general(w1_ref[rows, :], h, _NT, preferred_element_type=_F32))
        yield
    pt = jnp.concatenate(parts, axis=0)
    t = pt.shape[1]
    z64 = jnp.zeros((A_HEAD_DIM, t), _F32)
    z32 = jnp.zeros((HEAD_PAD - B_NOPE_DIM - B_ROPE_DIM, t), _F32)

    qn = qn_ref[...] * (A_SCALE * LOG2_E)
    group = A_HEADS // A_KV_HEADS
    for hd in range(A_HEADS):
        xh = pt[QA_OFF + hd * A_HEAD_DIM:QA_OFF + (hd + 1) * A_HEAD_DIM]
        xh = _axial_rope_rows(_rms_rows(xh, qn), A_HEAD_DIM // 4, ta)
        full = jnp.concatenate([xh, z64] if hd // group == 0 else [z64, xh], axis=0)
        q_out[hd] = full.astype(_BF16)
        if hd % 4 == 3:
            yield

    kn = kn_ref[...]
    ka = [
        _axial_rope_rows(
            _rms_rows(pt[KA_OFF + g * A_HEAD_DIM:KA_OFF + (g + 1) * A_HEAD_DIM], kn),
            A_HEAD_DIM // 4, ta)
        for g in range(A_KV_HEADS)
    ]
    k_out[0] = jnp.concatenate(ka, axis=0).T.astype(_BF16)
    v_out[0:A_KV_HEADS * A_HEAD_DIM, :] = pt[VA_OFF:CQ_OFF].astype(_BF16)
    yield

    cq = _rms_rows(pt[CQ_OFF:CKV_OFF], cqn_ref[...]).astype(_BF16)
    qb = jnp.dot(wuq_ref[...], cq, preferred_element_type=_F32) * (B_SCALE * LOG2_E)
    yield
    qdim = B_NOPE_DIM + B_ROPE_DIM
    for hd in range(B_HEADS):
        blk = qb[hd * qdim:(hd + 1) * qdim]
        rp = _axial_rope_rows(blk[B_NOPE_DIM:qdim], B_ROPE_DIM // 4, tb)
        full = jnp.concatenate([blk[0:B_NOPE_DIM], rp, z32], axis=0)
        q_out[A_HEADS + hd] = full.astype(_BF16)
        if hd % 4 == 3:
            yield

    ckv = _rms_rows(pt[CKV_OFF:KR_OFF], ckvn_ref[...]).astype(_BF16)
    kv = jnp.dot(wukv_ref[...], ckv, preferred_element_type=_F32)
    kr = _axial_rope_rows(pt[KR_OFF:GATE_OFF], B_ROPE_DIM // 4, tb)
    yield
    kvdim = B_NOPE_DIM + B_V_DIM
    v_base = A_KV_HEADS * A_HEAD_DIM
    for hd in range(B_HEADS):
        kn_h = kv[hd * kvdim:hd * kvdim + B_NOPE_DIM]
        v_h = kv[hd * kvdim + B_NOPE_DIM:(hd + 1) * kvdim]
        v_out[v_base + hd * B_V_DIM:v_base + (hd + 1) * B_V_DIM, :] = v_h.astype(_BF16)
        k_out[1 + hd] = jnp.concatenate([kn_h, kr, z32], axis=0).T.astype(_BF16)
        if hd % 4 == 3:
            yield


def _proj_kernel(mod_ref, ctx_ref, x_ref, w1_ref, wuq_ref, wukv_ref,
                 qn_ref, kn_ref, cqn_ref, ckvn_ref, ta_ref, tb_ref,
                 q_out, k_out, v_out):
    b = pl.program_id(0)
    j = pl.program_id(1)
    batch_stride = pl.num_programs(0)
    is_ctx = j == 0
    ctx_row = mod_ref.shape[0] // 2
    ta = ta_ref[...]
    tb = tb_ref[...]

    def stream(i):
        xin = jnp.where(is_ctx, ctx_ref[i, 0], x_ref[i, 0])
        mrow = jnp.where(is_ctx, mod_ref[ctx_row:ctx_row + 1, 0:2 * D_MODEL],
                         mod_ref[pl.ds(b + i * batch_stride, 1), 0:2 * D_MODEL])
        return _proj_pieces(mrow, xin, w1_ref, wuq_ref, wukv_ref, qn_ref, kn_ref, cqn_ref, ckvn_ref,
                            ta, tb, q_out.at[i, 0], k_out.at[i, 0], v_out.at[i, 0])

    _trace(_stagger([stream(i) for i in range(PROJ_STREAMS)], lead=PROJ_STREAM_LEAD))


def _project(mod, ctx, x, w_in_t, wuqt, wukvt, qn, kn, cqn, ckvn, tab_a, tab_b):
    bsz, s, d = x.shape
    t = PROJ_TILE
    n_steps = (CTX_LEN + s) // t
    total = CTX_LEN + s
    ns = PROJ_STREAMS
    hb = bsz // ns
    lat = lambda j: jnp.maximum(j - 1, 0)
    full2 = lambda shape: pl.BlockSpec(shape, lambda b, j: (0, 0))
    q_all, k_all, v_all = pl.pallas_call(
        _proj_kernel,
        grid=(hb, n_steps),
        in_specs=[
            full2(mod.shape),
            pl.BlockSpec((ns, 1, CTX_LEN, d), lambda b, j: (0, b, 0, 0)),
            pl.BlockSpec((ns, 1, t, d), lambda b, j: (0, b, lat(j), 0)),
            full2((QKV_COLS, d)),
            full2(wuqt.shape), full2(wukvt.shape),
            full2(qn.shape), full2(kn.shape), full2(cqn.shape), full2(ckvn.shape),
            pl.BlockSpec((4, tab_a.shape[1], t), lambda b, j: (0, 0, j)),
            pl.BlockSpec((4, tab_b.shape[1], t), lambda b, j: (0, 0, j)),
        ],
        out_specs=[
            pl.BlockSpec((ns, 1, N_HEADS, HEAD_PAD, t), lambda b, j: (0, b, 0, 0, lat(j))),
            pl.BlockSpec((ns, 1, N_KSETS, t, HEAD_PAD), lambda b, j: (0, b, 0, j, 0)),
            pl.BlockSpec((ns, 1, V_ROWS, t), lambda b, j: (0, b, 0, j)),
        ],
        out_shape=[
            jax.ShapeDtypeStruct((ns, hb, N_HEADS, HEAD_PAD, s), _BF16),
            jax.ShapeDtypeStruct((ns, hb, N_KSETS, total, HEAD_PAD), _BF16),
            jax.ShapeDtypeStruct((ns, hb, V_ROWS, total), _BF16),
        ],
        compiler_params=pltpu.CompilerParams(
            dimension_semantics=("arbitrary", "arbitrary"), vmem_limit_bytes=VMEM_LIMIT),
        name="proj",
    )(mod, ctx.reshape(ns, hb, CTX_LEN, d), x.reshape(ns, hb, s, d),
      w_in_t, wuqt, wukvt, qn, kn, cqn, ckvn, tab_a, tab_b)
    return (q_all.reshape(bsz, N_HEADS, HEAD_PAD, s), k_all.reshape(bsz, N_KSETS, total, HEAD_PAD),
            v_all.reshape(bsz, V_ROWS, total))


def _attn_stream(base, q_ref, k_ref, v_ref, qn_ref, kn_ref, o_ref, o_rows, bufs, m0_ref, slot):
    tq, kc = ATTN_Q_TILE, ATTN_KEY_CHUNK
    n_tiles = q_ref.shape[3] // tq
    n_chunks = k_ref.shape[2] // kc
    assert n_tiles % 2 == 0
    assert n_tiles * tq == q_ref.shape[3] and n_chunks * kc == k_ref.shape[2]
    dv = v_ref.shape[1]
    ones = jnp.ones((ATTN_ONES_ROWS, kc), _BF16)

    def buf_rows(c):
        return pl.ds(pl.multiple_of(base + c * kc, kc), kc)

    def score_chunk(keys_ref, q_tile, buf, c, m):
        s = jnp.dot(keys_ref[0, 0, pl.ds(c * kc, kc), :], q_tile,
                    preferred_element_type=_F32)
        buf[buf_rows(c), :] = s
        cm = jnp.max(s, axis=0, keepdims=True)
        return cm if m is None else jnp.maximum(m, cm)

    def value_chunk(buf, c, m, acc):
        p = jnp.exp2(buf[buf_rows(c), :] - m)
        vext = jnp.concatenate([v_ref[0, :, pl.ds(c * kc, kc)], ones], axis=0)
        pv = jnp.dot(vext, p.astype(_BF16), preferred_element_type=_F32)
        return pv if acc is None else acc + pv

    @pl.when((pl.program_id(0) == 0) & (pl.program_id(1) == 0))
    def _():
        m = None
        for c in range(n_chunks):
            m = score_chunk(k_ref, q_ref[0, 0, :, 0:tq], bufs[0], c, m)
        m0_ref[slot] = m
    yield

    m_cur = m0_ref[slot]
    for t in range(n_tiles):
        m_next = acc = None
        for c in range(n_chunks):
            if t + 1 < n_tiles:
                m_next = score_chunk(k_ref, q_ref[0, 0, :, pl.ds((t + 1) * tq, tq)],
                                     bufs[(t + 1) % 2], c, m_next)
            else:
                m_next = score_chunk(kn_ref, qn_ref[0, 0], bufs[0], c, m_next)
            acc = value_chunk(bufs[t % 2], c, m_cur, acc)
            yield
        o_ref[0, o_rows, pl.ds(t * tq, tq)] = (acc[0:dv] / acc[dv:dv + 1]).astype(_BF16)
        m_cur = m_next
    m0_ref[slot] = m_cur


def _attn_kernel(base_ref, qa_ref, ka_ref, va_ref, qna_ref, kna_ref,
                 qb_ref, kb_ref, vb_ref, qnb_ref, knb_ref, o_ref,
                 sa0_ref, sa1_ref, sb0_ref, sb1_ref, m0_ref):
    base = base_ref[0]
    dv = va_ref.shape[1]
    first = _attn_stream(base, qa_ref, ka_ref, va_ref, qna_ref, kna_ref, o_ref, pl.ds(0, dv),
                         (sa0_ref, sa1_ref), m0_ref, 0)
    second = _attn_stream(base, qb_ref, kb_ref, vb_ref, qnb_ref, knb_ref, o_ref, pl.ds(dv, dv),
                          (sb0_ref, sb1_ref), m0_ref, 1)
    next(first), next(second)
    _trace(_stagger([first, second], lead=ATTN_STREAM_LEAD))


def _attention(q_all, k_all, v_all):
    bsz, _, _, s = q_all.shape
    total = k_all.shape[2]
    group = A_HEADS // A_KV_HEADS
    n_pairs = N_HEADS // 2
    kset = lambda h: jnp.where(h < A_HEADS, 0, h - A_HEADS + 1)
    vblk = lambda h: jnp.where(h < A_HEADS, h // group, h - A_HEADS + A_KV_HEADS)

    def nxt(b, g):
        n = jnp.minimum(b * n_pairs + g + 1, bsz * n_pairs - 1)
        return n // n_pairs, n % n_pairs

    def stream_specs(which):
        head = lambda g: 2 * g + which
        return [
            pl.BlockSpec((1, 1, HEAD_PAD, s), lambda b, g: (b, head(g), 0, 0)),
            pl.BlockSpec((1, 1, total, HEAD_PAD), lambda b, g: (b, kset(head(g)), 0, 0)),
            pl.BlockSpec((1, B_V_DIM, total), lambda b, g: (b, vblk(head(g)), 0)),
            pl.BlockSpec((1, 1, HEAD_PAD, ATTN_Q_TILE),
                         lambda b, g: (nxt(b, g)[0], head(nxt(b, g)[1]), 0, 0)),
            pl.BlockSpec((1, 1, total, HEAD_PAD),
                         lambda b, g: (nxt(b, g)[0], kset(head(nxt(b, g)[1])), 0, 0)),
        ]

    stream_args = (q_all, k_all, v_all, q_all, k_all)
    return pl.pallas_call(
        _attn_kernel,
        grid=(bsz, n_pairs),
        in_specs=[pl.BlockSpec(memory_space=pltpu.SMEM)] + stream_specs(0) + stream_specs(1),
        out_specs=pl.BlockSpec((1, 2 * B_V_DIM, s), lambda b, g: (b, g, 0)),
        out_shape=jax.ShapeDtypeStruct((bsz, N_HEADS * B_V_DIM, s), _BF16),
        scratch_shapes=[pltpu.VMEM((total, ATTN_Q_TILE), _F32)] * 4
        + [pltpu.VMEM((2, 1, ATTN_Q_TILE), _F32)],
        compiler_params=pltpu.CompilerParams(
            dimension_semantics=("arbitrary", "arbitrary"), vmem_limit_bytes=ATTN_VMEM_LIMIT),
        name="attn",
    )(jnp.zeros((1,), jnp.int32), *stream_args, *stream_args)


def _merge_pieces(mrow, x, ot, win_ref, wpa_ref, wpb_ref, wo_ref, g_ref, b_ref, result):
    shift, scale, gate = mrow[:, 0:D_MODEL], mrow[:, D_MODEL:2 * D_MODEL], mrow[:, 2 * D_MODEL:3 * D_MODEL]
    h = (_layer_norm(x) * (1.0 + scale) + shift).astype(_BF16)
    yield
    na = A_HEADS * A_HEAD_DIM
    ys = []
    for r in range(D_MODEL // MERGE_ROW_CHUNK):
        ra = pl.ds(r * MERGE_ROW_CHUNK, MERGE_ROW_CHUNK)
        ga_rows = pl.ds(GATE_OFF + r * MERGE_ROW_CHUNK, MERGE_ROW_CHUNK)
        gb_rows = pl.ds(GATE_OFF + D_MODEL + r * MERGE_ROW_CHUNK, MERGE_ROW_CHUNK)
        ga = lax.dot_general(win_ref[ga_rows, :], h, _NT, preferred_element_type=_F32)
        gb = lax.dot_general(win_ref[gb_rows, :], h, _NT, preferred_element_type=_F32)
        ya = jnp.dot(wpa_ref[ra, :], ot[0:na], preferred_element_type=_F32)
        yb = jnp.dot(wpb_ref[ra, :], ot[na:], preferred_element_type=_F32)
        ys.append((_sigmoid(ga) * ya + _sigmoid(gb) * yb).astype(_BF16))
        yield
    zt = jnp.dot(wo_ref[...], jnp.concatenate(ys, axis=0), preferred_element_type=_F32)
    r = DEEPNORM_ALPHA * x + gate * zt.T
    result.append(_layer_norm(r) * g_ref[...] + b_ref[...])
    yield


def _ffn_pieces(mrow, x, wup_ref, wdn_ref, g_ref, b_ref, result):
    shift, scale, gate = mrow[:, 0:D_MODEL], mrow[:, D_MODEL:2 * D_MODEL], mrow[:, 2 * D_MODEL:3 * D_MODEL]
    h = (_layer_norm(x) * (1.0 + scale) + shift).astype(_BF16)
    yield
    fs = []
    for c in range(FFN_HIDDEN // FFN_COL_CHUNK):
        a = jnp.dot(h, wup_ref[:, pl.ds(c * FFN_COL_CHUNK, FFN_COL_CHUNK)], preferred_element_type=_F32)
        u = jnp.dot(h, wup_ref[:, pl.ds(FFN_HIDDEN + c * FFN_COL_CHUNK, FFN_COL_CHUNK)],
                    preferred_element_type=_F32)
        fs.append((a * _sigmoid(a) * u).astype(_BF16))
        yield
    dn = jnp.dot(jnp.concatenate(fs, axis=1), wdn_ref[...], preferred_element_type=_F32)
    r = DEEPNORM_ALPHA * x + gate * dn
    result.append(_layer_norm(r) * g_ref[...] + b_ref[...])
    yield


def _post_kernel(mod_ref, x_ref, ot_ref, win_ref, wpa_ref, wpb_ref, wo_ref, g1_ref, b1_ref,
                 wup_ref, wdn_ref, g2_ref, b2_ref, out_ref, x1_ref, *, tiles_per_batch):
    j = pl.program_id(0)
    n_tiles = pl.num_programs(0) - 1
    half_batch = n_tiles // tiles_per_batch

    @pl.when(j == 0)
    def _():
        x1_ref[...] = jnp.zeros_like(x1_ref)

    b_prev = jnp.maximum(j - 1, 0) // tiles_per_batch
    b_cur = jnp.minimum(j, n_tiles - 1) // tiles_per_batch
    ffn_out, merge_out = ([], []), ([], [])

    def stream(half):
        off = half * half_batch
        return _stagger(
            [_ffn_pieces(mod_ref[pl.ds(b_prev + off, 1), 3 * D_MODEL:6 * D_MODEL], x1_ref[half],
                         wup_ref, wdn_ref, g2_ref, b2_ref, ffn_out[half]),
             _merge_pieces(mod_ref[pl.ds(b_cur + off, 1), 0:3 * D_MODEL], x_ref[half, 0],
                           ot_ref[half, 0], win_ref, wpa_ref, wpb_ref, wo_ref, g1_ref, b1_ref,
                           merge_out[half])],
            rates=[POST_FFN_PIECES_PER_MERGE_PIECE, 1])

    _trace(_stagger([stream(0), stream(1)], lead=POST_STREAM_LEAD))
    for half in range(2):
        out_ref[half, 0] = ffn_out[half][0]
        x1_ref[half] = merge_out[half][0]


def _post(mod, x, ot, w_in_t, wpat, wpbt, wot, g1, b1, wup, wdn, g2, b2):
    bsz, s, d = x.shape
    t = TOKEN_TILE
    tpb = s // t
    hb = bsz // 2
    n_tiles = hb * tpb
    const = lambda a: pl.BlockSpec(a.shape, lambda j: (0,) * a.ndim, pipeline_mode=pl.Buffered(1))
    cur = lambda j: jnp.minimum(j, n_tiles - 1)
    prev = lambda j: jnp.maximum(j - 1, 0)
    out = pl.pallas_call(
        functools.partial(_post_kernel, tiles_per_batch=tpb),
        grid=(n_tiles + 1,),
        in_specs=[
            const(mod),
            pl.BlockSpec((2, 1, t, d), lambda j: (0, cur(j) // tpb, cur(j) % tpb, 0)),
            pl.BlockSpec((2, 1, ot.shape[1], t), lambda j: (0, cur(j) // tpb, 0, cur(j) % tpb)),
            const(w_in_t), const(wpat), const(wpbt), const(wot), const(g1), const(b1),
            const(wup), const(wdn), const(g2), const(b2),
        ],
        out_specs=pl.BlockSpec((2, 1, t, d), lambda j: (0, prev(j) // tpb, prev(j) % tpb, 0)),
        out_shape=jax.ShapeDtypeStruct((2, hb, s, d), _F32),
        scratch_shapes=[pltpu.VMEM((2, t, d), _F32)],
        compiler_params=pltpu.CompilerParams(
            dimension_semantics=("arbitrary",), vmem_limit_bytes=VMEM_LIMIT),
        name="post",
    )(mod, x.reshape(2, hb, s, d), ot.reshape(2, hb, ot.shape[1], s),
      w_in_t, wpat, wpbt, wot, g1, b1, wup, wdn, g2, b2)
    return out.reshape(bsz, s, d)


def _rope_tables(seq, half):
    pos = np.arange(seq)
    freqs = ROPE_THETA ** (-np.arange(half, dtype=np.float64) / half)
    tabs = []
    for p in (pos // GRID_W, pos % GRID_W):
        ang = p[None, :].astype(np.float64) * freqs[:, None]
        for fn, ident in ((np.cos, 1.0), (np.sin, 0.0)):
            tabs.append(np.concatenate([np.full((half, CTX_LEN), ident), fn(ang)], axis=1))
    return jnp.asarray(np.stack(tabs), dtype=_F32)


def kernel(x, c, ctx, c_ctx, w_mod, b_mod, w_in, q_norm_a, k_norm_a, cq_norm, ckv_norm,
           w_uq, w_ukv, w_proj_a, w_proj_b, w_out, ln1_g, ln1_b, w_up, w_down, ln2_g, ln2_b):
    bsz, s, d = x.shape
    assert w_mod.shape[0] == DEPTH == 1 and d == D_MODEL and ctx.shape[1] == CTX_LEN
    assert bsz == 8 and s % ATTN_Q_TILE == 0 and s % TOKEN_TILE == 0

    cc = jnp.concatenate([c, c_ctx[None, :], jnp.zeros((bsz - 1, d), _F32)], axis=0)
    mod = _modulation(cc, w_mod[0], b_mod[0][None, :])

    tbf = lambda w: w[0].astype(_BF16).T
    w_in_t = tbf(w_in)
    col = lambda v: v[0][:, None]
    tab_a = _rope_tables(s, A_HEAD_DIM // 4)
    tab_b = _rope_tables(s, B_ROPE_DIM // 4)

    q_all, k_all, v_all = _project(
        mod, ctx, x, w_in_t, tbf(w_uq), tbf(w_ukv),
        col(q_norm_a), col(k_norm_a), col(cq_norm), col(ckv_norm), tab_a, tab_b)
    ot = _attention(q_all, k_all, v_all)
    return _post(mod, x, ot, w_in_t, tbf(w_proj_a), tbf(w_proj_b), tbf(w_out), ln1_g, ln1_b,
                 w_up[0].astype(_BF16), w_down[0].astype(_BF16), ln2_g, ln2_b)
```

```python
import functools
import math

import numpy as np
import jax
import jax.numpy as jnp
from jax import lax
from jax.experimental import pallas as pl
from jax.experimental.pallas import tpu as pltpu

D_MODEL = 1024
GRID_W = 64
CTX_LEN = 256
ROPE_THETA = 10000.0
EPS = 1e-6

A_HEADS = 8
A_KV_HEADS = 2
A_HEAD_DIM = 64
B_HEADS = 8
B_Q_RANK = 384
B_KV_RANK = 256
B_NOPE_DIM = 64
B_ROPE_DIM = 32
B_V_DIM = 64
FFN_HIDDEN = 2816
DEPTH = 1

A_SCALE = A_HEAD_DIM ** -0.5
B_SCALE = (B_NOPE_DIM + B_ROPE_DIM) ** -0.5
DEEPNORM_ALPHA = (2.0 * DEPTH) ** 0.25
LOG2_E = math.log2(math.e)

QA_OFF = 0
KA_OFF = QA_OFF + A_HEADS * A_HEAD_DIM
VA_OFF = KA_OFF + A_KV_HEADS * A_HEAD_DIM
CQ_OFF = VA_OFF + A_KV_HEADS * A_HEAD_DIM
CKV_OFF = CQ_OFF + B_Q_RANK
KR_OFF = CKV_OFF + B_KV_RANK
GATE_OFF = KR_OFF + B_ROPE_DIM
QKV_COLS = GATE_OFF

N_HEADS = A_HEADS + B_HEADS
HEAD_PAD = 128
N_KSETS = 1 + B_HEADS
V_ROWS = A_KV_HEADS * A_HEAD_DIM + B_HEADS * B_V_DIM

MOD_COL_TILE = 1024
PROJ_TILE = 256
PROJ_ROW_CHUNK = 288
PROJ_STREAMS = 4
PROJ_STREAM_LEAD = 3
ATTN_Q_TILE = 512
ATTN_KEY_CHUNK = 256
ATTN_STREAM_LEAD = 9
ATTN_ONES_ROWS = 16
TOKEN_TILE = 256
MERGE_ROW_CHUNK = 256
FFN_COL_CHUNK = 256
POST_FFN_PIECES_PER_MERGE_PIECE = 2
POST_STREAM_LEAD = 10
VMEM_LIMIT = 56 * 1024 * 1024
ATTN_VMEM_LIMIT = 40 * 1024 * 1024
PROJ_VMEM_LIMIT = 48 * 1024 * 1024

_NT = (((1,), (1,)), ((), ()))
_F32 = jnp.float32
_BF16 = jnp.bfloat16


def _layer_norm(x):
    mu = jnp.mean(x, axis=-1, keepdims=True)
    xc = x - mu
    var = jnp.mean(xc * xc, axis=-1, keepdims=True)
    return xc * lax.rsqrt(var + EPS)


def _sigmoid(x):
    return jax.nn.sigmoid(x)


def _mod_kernel(c_ref, w_ref, b_ref, o_ref):
    c = c_ref[...]
    a = (c * _sigmoid(c)).astype(_BF16)
    o_ref[...] = jnp.dot(a, w_ref[...].astype(_BF16), preferred_element_type=_F32) + b_ref[...]


def _modulation(cc, w_mod, b_mod):
    rows, d = cc.shape
    n = w_mod.shape[1]
    return pl.pallas_call(
        _mod_kernel,
        grid=(n // MOD_COL_TILE,),
        in_specs=[
            pl.BlockSpec((rows, d), lambda i: (0, 0)),
            pl.BlockSpec((d, MOD_COL_TILE), lambda i: (0, i)),
            pl.BlockSpec((1, MOD_COL_TILE), lambda i: (0, i)),
        ],
        out_specs=pl.BlockSpec((rows, MOD_COL_TILE), lambda i: (0, i)),
        out_shape=jax.ShapeDtypeStruct((rows, n), _F32),
        compiler_params=pltpu.CompilerParams(
            dimension_semantics=("arbitrary",), vmem_limit_bytes=VMEM_LIMIT),
        name="mod",
    )(cc, w_mod, b_mod)


def _rms_rows(x, g):
    ms = jnp.mean(x * x, axis=0, keepdims=True)
    return x * lax.rsqrt(ms + EPS) * g


def _axial_rope_rows(x, half, tab):
    r1, r2 = x[0:half], x[half:2 * half]
    c1, c2 = x[2 * half:3 * half], x[3 * half:4 * half]
    cr, sr, cc, sc = tab[0], tab[1], tab[2], tab[3]
    return jnp.concatenate(
        [r1 * cr - r2 * sr, r1 * sr + r2 * cr, c1 * cc - c2 * sc, c1 * sc + c2 * cc], axis=0)


def _stagger(streams, lead=0, rates=None):
    rates = rates or [1] * len(streams)
    live = dict(enumerate(streams))
    rnd = 0
    while live:
        for i in sorted(live):
            if rnd < i * lead:
                continue
            for _ in range(rates[i]):
                if next(live[i], StopIteration) is StopIteration:
                    del live[i]
                    break
                yield
        rnd += 1


def _trace(pieces):
    for _ in pieces:
        pass


def _proj_pieces(mrow, xin, w1_ref, wuq_ref, wukv_ref, qn_ref, kn_ref, cqn_ref, ckvn_ref, ta, tb,
                 q_out, k_out, v_out):
    shift, scale = mrow[:, 0:D_MODEL], mrow[:, D_MODEL:2 * D_MODEL]
    h = (_layer_norm(xin) * (1.0 + scale) + shift).astype(_BF16)
    yield
    parts = []
    for r in range(QKV_COLS // PROJ_ROW_CHUNK):
        rows = pl.ds(r * PROJ_ROW_CHUNK, PROJ_ROW_CHUNK)
        parts.append(lax.dot_general(w1_ref[rows, :], h, _NT, preferred_element_type=_F32))
        yield
    pt = jnp.concatenate(parts, axis=0)
    t = pt.shape[1]
    z64 = jnp.zeros((A_HEAD_DIM, t), _F32)
    z32 = jnp.zeros((HEAD_PAD - B_NOPE_DIM - B_ROPE_DIM, t), _F32)

    qn = qn_ref[...] * (A_SCALE * LOG2_E)
    group = A_HEADS // A_KV_HEADS
    for hd in range(A_HEADS):
        xh = pt[QA_OFF + hd * A_HEAD_DIM:QA_OFF + (hd + 1) * A_HEAD_DIM]
        xh = _axial_rope_rows(_rms_rows(xh, qn), A_HEAD_DIM // 4, ta)
        full = jnp.concatenate([xh, z64] if hd // group == 0 else [z64, xh], axis=0)
        q_out[hd] = full.astype(_BF16)
        if hd % 4 == 3:
            yield

    kn = kn_ref[...]
    ka = [
        _axial_rope_rows(
            _rms_rows(pt[KA_OFF + g * A_HEAD_DIM:KA_OFF + (g + 1) * A_HEAD_DIM], kn),
            A_HEAD_DIM // 4, ta)
        for g in range(A_KV_HEADS)
    ]
    k_out[0] = jnp.concatenate(ka, axis=0).T.astype(_BF16)
    v_out[0:A_KV_HEADS * A_HEAD_DIM, :] = pt[VA_OFF:CQ_OFF].astype(_BF16)
    yield

    cq = _rms_rows(pt[CQ_OFF:CKV_OFF], cqn_ref[...]).astype(_BF16)
    qb = jnp.dot(wuq_ref[...], cq, preferred_element_type=_F32) * (B_SCALE * LOG2_E)
    yield
    qdim = B_NOPE_DIM + B_ROPE_DIM
    for hd in range(B_HEADS):
        blk = qb[hd * qdim:(hd + 1) * qdim]
        rp = _axial_rope_rows(blk[B_NOPE_DIM:qdim], B_ROPE_DIM // 4, tb)
        full = jnp.concatenate([blk[0:B_NOPE_DIM], rp, z32], axis=0)
        q_out[A_HEADS + hd] = full.astype(_BF16)
        if hd % 4 == 3:
            yield

    ckv = _rms_rows(pt[CKV_OFF:KR_OFF], ckvn_ref[...]).astype(_BF16)
    kv = jnp.dot(wukv_ref[...], ckv, preferred_element_type=_F32)
    kr = _axial_rope_rows(pt[KR_OFF:GATE_OFF], B_ROPE_DIM // 4, tb)
    yield
    kvdim = B_NOPE_DIM + B_V_DIM
    v_base = A_KV_HEADS * A_HEAD_DIM
    for hd in range(B_HEADS):
        kn_h = kv[hd * kvdim:hd * kvdim + B_NOPE_DIM]
        v_h = kv[hd * kvdim + B_NOPE_DIM:(hd + 1) * kvdim]
        v_out[v_base + hd * B_V_DIM:v_base + (hd + 1) * B_V_DIM, :] = v_h.astype(_BF16)
        k_out[1 + hd] = jnp.concatenate([kn_h, kr, z32], axis=0).T.astype(_BF16)
        if hd % 4 == 3:
            yield


def _proj_kernel(mod_ref, ctx_ref, x_ref, w1_ref, wuq_ref, wukv_ref,
                 qn_ref, kn_ref, cqn_ref, ckvn_ref, ta_ref, tb_ref,
                 q_out, k_out, v_out):
    b = pl.program_id(0)
    j = pl.program_id(1)
    batch_stride = pl.num_programs(0)
    is_ctx = j == 0
    ctx_row = mod_ref.shape[0] // 2
    ta = ta_ref[...]
    tb = tb_ref[...]

    def stream(i):
        xin = jnp.where(is_ctx, ctx_ref[i, 0], x_ref[i, 0])
        mrow = jnp.where(is_ctx, mod_ref[ctx_row:ctx_row + 1, 0:2 * D_MODEL],
                         mod_ref[pl.ds(b + i * batch_stride, 1), 0:2 * D_MODEL])
        return _proj_pieces(mrow, xin, w1_ref, wuq_ref, wukv_ref, qn_ref, kn_ref, cqn_ref, ckvn_ref,
                            ta, tb, q_out.at[i, 0], k_out.at[i, 0], v_out.at[i, 0])

    _trace(_stagger([stream(i) for i in range(PROJ_STREAMS)], lead=PROJ_STREAM_LEAD))


def _project(mod, ctx, x, w_in_t, wuqt, wukvt, qn, kn, cqn, ckvn, tab_a, tab_b):
    bsz, s, d = x.shape
    t = PROJ_TILE
    n_steps = (CTX_LEN + s) // t
    total = CTX_LEN + s
    ns = PROJ_STREAMS
    hb = bsz // ns
    lat = lambda j: jnp.maximum(j - 1, 0)
    full2 = lambda shape: pl.BlockSpec(shape, lambda b, j: (0, 0))
    q_all, k_all, v_all = pl.pallas_call(
        _proj_kernel,
        grid=(hb, n_steps),
        in_specs=[
            full2(mod.shape),
            pl.BlockSpec((ns, 1, CTX_LEN, d), lambda b, j: (0, b, 0, 0)),
            pl.BlockSpec((ns, 1, t, d), lambda b, j: (0, b, lat(j), 0)),
            full2((QKV_COLS, d)),
            full2(wuqt.shape), full2(wukvt.shape),
            full2(qn.shape), full2(kn.shape), full2(cqn.shape), full2(ckvn.shape),
            pl.BlockSpec((4, tab_a.shape[1], t), lambda b, j: (0, 0, j)),
            pl.BlockSpec((4, tab_b.shape[1], t), lambda b, j: (0, 0, j)),
        ],
        out_specs=[
            pl.BlockSpec((ns, 1, N_HEADS, HEAD_PAD, t), lambda b, j: (0, b, 0, 0, lat(j))),
            pl.BlockSpec((ns, 1, N_KSETS, t, HEAD_PAD), lambda b, j: (0, b, 0, j, 0)),
            pl.BlockSpec((ns, 1, V_ROWS, t), lambda b, j: (0, b, 0, j)),
        ],
        out_shape=[
            jax.ShapeDtypeStruct((ns, hb, N_HEADS, HEAD_PAD, s), _BF16),
            jax.ShapeDtypeStruct((ns, hb, N_KSETS, total, HEAD_PAD), _BF16),
            jax.ShapeDtypeStruct((ns, hb, V_ROWS, total), _BF16),
        ],
        compiler_params=pltpu.CompilerParams(
            dimension_semantics=("arbitrary", "arbitrary"), vmem_limit_bytes=PROJ_VMEM_LIMIT),
        name="proj",
    )(mod, ctx.reshape(ns, hb, CTX_LEN, d), x.reshape(ns, hb, s, d),
      w_in_t, wuqt, wukvt, qn, kn, cqn, ckvn, tab_a, tab_b)
    return (q_all.reshape(bsz, N_HEADS, HEAD_PAD, s), k_all.reshape(bsz, N_KSETS, total, HEAD_PAD),
            v_all.reshape(bsz, V_ROWS, total))


def _attn_stream(base, q_ref, k_ref, v_ref, qn_ref, kn_ref, o_ref, o_rows, bufs, m0_ref, slot):
    tq, kc = ATTN_Q_TILE, ATTN_KEY_CHUNK
    n_tiles = q_ref.shape[3] // tq
    n_chunks = k_ref.shape[2] // kc
    assert n_tiles % 2 == 0
    assert n_tiles * tq == q_ref.shape[3] and n_chunks * kc == k_ref.shape[2]
    dv = v_ref.shape[1]
    ones = jnp.ones((ATTN_ONES_ROWS, kc), _BF16)

    def buf_rows(c):
        return pl.ds(pl.multiple_of(base + c * kc, kc), kc)

    def score_chunk(keys_ref, q_tile, buf, c, m):
        s = jnp.dot(keys_ref[0, 0, pl.ds(c * kc, kc), :], q_tile,
                    preferred_element_type=_F32)
        buf[buf_rows(c), :] = s
        cm = jnp.max(s, axis=0, keepdims=True)
        return cm if m is None else jnp.maximum(m, cm)

    def value_chunk(buf, c, m, acc):
        p = jnp.exp2(buf[buf_rows(c), :] - m)
        vext = jnp.concatenate([v_ref[0, :, pl.ds(c * kc, kc)], ones], axis=0)
        pv = jnp.dot(vext, p.astype(_BF16), preferred_element_type=_F32)
        return pv if acc is None else acc + pv

    @pl.when((pl.program_id(0) == 0) & (pl.program_id(1) == 0))
    def _():
        m = None
        for c in range(n_chunks):
            m = score_chunk(k_ref, q_ref[0, 0, :, 0:tq], bufs[0], c, m)
        m0_ref[slot] = m
    yield

    m_cur = m0_ref[slot]
    for t in range(n_tiles):
        m_next = acc = None
        for c in range(n_chunks):
            if t + 1 < n_tiles:
                m_next = score_chunk(k_ref, q_ref[0, 0, :, pl.ds((t + 1) * tq, tq)],
                                     bufs[(t + 1) % 2], c, m_next)
            else:
                m_next = score_chunk(kn_ref, qn_ref[0, 0], bufs[0], c, m_next)
            acc = value_chunk(bufs[t % 2], c, m_cur, acc)
            yield
        o_ref[0, o_rows, pl.ds(t * tq, tq)] = (acc[0:dv] / acc[dv:dv + 1]).astype(_BF16)
        m_cur = m_next
    m0_ref[slot] = m_cur


def _attn_kernel(base_ref, qa_ref, ka_ref, va_ref, qna_ref, kna_ref,
                 qb_ref, kb_ref, vb_ref, qnb_ref, knb_ref, o_ref,
                 sa0_ref, sa1_ref, sb0_ref, sb1_ref, m0_ref):
    base = base_ref[0]
    dv = va_ref.shape[1]
    first = _attn_stream(base, qa_ref, ka_ref, va_ref, qna_ref, kna_ref, o_ref, pl.ds(0, dv),
                         (sa0_ref, sa1_ref), m0_ref, 0)
    second = _attn_stream(base, qb_ref, kb_ref, vb_ref, qnb_ref, knb_ref, o_ref, pl.ds(dv, dv),
                          (sb0_ref, sb1_ref), m0_ref, 1)
    next(first), next(second)
    _trace(_stagger([first, second], lead=ATTN_STREAM_LEAD))


def _attention(q_all, k_all, v_all):
    bsz, _, _, s = q_all.shape
    total = k_all.shape[2]
    group = A_HEADS // A_KV_HEADS
    n_pairs = N_HEADS // 2
    kset = lambda h: jnp.where(h < A_HEADS, 0, h - A_HEADS + 1)
    vblk = lambda h: jnp.where(h < A_HEADS, h // group, h - A_HEADS + A_KV_HEADS)

    def nxt(b, g):
        n = jnp.minimum(b * n_pairs + g + 1, bsz * n_pairs - 1)
        return n // n_pairs, n % n_pairs

    def stream_specs(which):
        head = lambda g: 2 * g + which
        return [
            pl.BlockSpec((1, 1, HEAD_PAD, s), lambda b, g: (b, head(g), 0, 0)),
            pl.BlockSpec((1, 1, total, HEAD_PAD), lambda b, g: (b, kset(head(g)), 0, 0)),
            pl.BlockSpec((1, B_V_DIM, total), lambda b, g: (b, vblk(head(g)), 0)),
            pl.BlockSpec((1, 1, HEAD_PAD, ATTN_Q_TILE),
                         lambda b, g: (nxt(b, g)[0], head(nxt(b, g)[1]), 0, 0)),
            pl.BlockSpec((1, 1, total, HEAD_PAD),
                         lambda b, g: (nxt(b, g)[0], kset(head(nxt(b, g)[1])), 0, 0)),
        ]

    stream_args = (q_all, k_all, v_all, q_all, k_all)
    return pl.pallas_call(
        _attn_kernel,
        grid=(bsz, n_pairs),
        in_specs=[pl.BlockSpec(memory_space=pltpu.SMEM)] + stream_specs(0) + stream_specs(1),
        out_specs=pl.BlockSpec((1, 2 * B_V_DIM, s), lambda b, g: (b, g, 0)),
        out_shape=jax.ShapeDtypeStruct((bsz, N_HEADS * B_V_DIM, s), _BF16),
        scratch_shapes=[pltpu.VMEM((total, ATTN_Q_TILE), _F32)] * 4
        + [pltpu.VMEM((2, 1, ATTN_Q_TILE), _F32)],
        compiler_params=pltpu.CompilerParams(
            dimension_semantics=("arbitrary", "arbitrary"), vmem_limit_bytes=ATTN_VMEM_LIMIT),
        name="attn",
    )(jnp.zeros((1,), jnp.int32), *stream_args, *stream_args)


def _merge_pieces(mrow, x, ot, win_ref, wpa_ref, wpb_ref, wo_ref, g_ref, b_ref, result):
    shift, scale, gate = mrow[:, 0:D_MODEL], mrow[:, D_MODEL:2 * D_MODEL], mrow[:, 2 * D_MODEL:3 * D_MODEL]
    h = (_layer_norm(x) * (1.0 + scale) + shift).astype(_BF16)
    yield
    na = A_HEADS * A_HEAD_DIM
    ys = []
    for r in range(D_MODEL // MERGE_ROW_CHUNK):
        ra = pl.ds(r * MERGE_ROW_CHUNK, MERGE_ROW_CHUNK)
        ga_rows = pl.ds(GATE_OFF + r * MERGE_ROW_CHUNK, MERGE_ROW_CHUNK)
        gb_rows = pl.ds(GATE_OFF + D_MODEL + r * MERGE_ROW_CHUNK, MERGE_ROW_CHUNK)
        ga = lax.dot_general(win_ref[ga_rows, :], h, _NT, preferred_element_type=_F32)
        gb = lax.dot_general(win_ref[gb_rows, :], h, _NT, preferred_element_type=_F32)
        ya = jnp.dot(wpa_ref[ra, :], ot[0:na], preferred_element_type=_F32)
        yb = jnp.dot(wpb_ref[ra, :], ot[na:], preferred_element_type=_F32)
        ys.append((_sigmoid(ga) * ya + _sigmoid(gb) * yb).astype(_BF16))
        yield
    zt = jnp.dot(wo_ref[...], jnp.concatenate(ys, axis=0), preferred_element_type=_F32)
    r = DEEPNORM_ALPHA * x + gate * zt.T
    result.append(_layer_norm(r) * g_ref[...] + b_ref[...])
    yield


def _ffn_pieces(mrow, x, wup_ref, wdn_ref, g_ref, b_ref, result):
    shift, scale, gate = mrow[:, 0:D_MODEL], mrow[:, D_MODEL:2 * D_MODEL], mrow[:, 2 * D_MODEL:3 * D_MODEL]
    h = (_layer_norm(x) * (1.0 + scale) + shift).astype(_BF16)
    yield
    fs = []
    for c in range(FFN_HIDDEN // FFN_COL_CHUNK):
        a = jnp.dot(h, wup_ref[:, pl.ds(c * FFN_COL_CHUNK, FFN_COL_CHUNK)], preferred_element_type=_F32)
        u = jnp.dot(h, wup_ref[:, pl.ds(FFN_HIDDEN + c * FFN_COL_CHUNK, FFN_COL_CHUNK)],
                    preferred_element_type=_F32)
        fs.append((a * _sigmoid(a) * u).astype(_BF16))
        yield
    dn = jnp.dot(jnp.concatenate(fs, axis=1), wdn_ref[...], preferred_element_type=_F32)
    r = DEEPNORM_ALPHA * x + gate * dn
    result.append(_layer_norm(r) * g_ref[...] + b_ref[...])
    yield


def _post_kernel(mod_ref, x_ref, ot_ref, win_ref, wpa_ref, wpb_ref, wo_ref, g1_ref, b1_ref,
                 wup_ref, wdn_ref, g2_ref, b2_ref, out_ref, x1_ref, *, tiles_per_batch):
    j = pl.program_id(0)
    n_tiles = pl.num_programs(0) - 1
    half_batch = n_tiles // tiles_per_batch

    @pl.when(j == 0)
    def _():
        x1_ref[...] = jnp.zeros_like(x1_ref)

    b_prev = jnp.maximum(j - 1, 0) // tiles_per_batch
    b_cur = jnp.minimum(j, n_tiles - 1) // tiles_per_batch
    ffn_out, merge_out = ([], []), ([], [])

    def stream(half):
        off = half * half_batch
        return _stagger(
            [_ffn_pieces(mod_ref[pl.ds(b_prev + off, 1), 3 * D_MODEL:6 * D_MODEL], x1_ref[half],
                         wup_ref, wdn_ref, g2_ref, b2_ref, ffn_out[half]),
             _merge_pieces(mod_ref[pl.ds(b_cur + off, 1), 0:3 * D_MODEL], x_ref[half, 0],
                           ot_ref[half, 0], win_ref, wpa_ref, wpb_ref, wo_ref, g1_ref, b1_ref,
                           merge_out[half])],
            rates=[POST_FFN_PIECES_PER_MERGE_PIECE, 1])

    _trace(_stagger([stream(0), stream(1)], lead=POST_STREAM_LEAD))
    for half in range(2):
        out_ref[half, 0] = ffn_out[half][0]
        x1_ref[half] = merge_out[half][0]


def _post(mod, x, ot, w_in_t, wpat, wpbt, wot, g1, b1, wup, wdn, g2, b2):
    bsz, s, d = x.shape
    t = TOKEN_TILE
    tpb = s // t
    hb = bsz // 2
    n_tiles = hb * tpb
    const = lambda a: pl.BlockSpec(a.shape, lambda j: (0,) * a.ndim, pipeline_mode=pl.Buffered(1))
    cur = lambda j: jnp.minimum(j, n_tiles - 1)
    prev = lambda j: jnp.maximum(j - 1, 0)
    out = pl.pallas_call(
        functools.partial(_post_kernel, tiles_per_batch=tpb),
        grid=(n_tiles + 1,),
        in_specs=[
            const(mod),
            pl.BlockSpec((2, 1, t, d), lambda j: (0, cur(j) // tpb, cur(j) % tpb, 0)),
            pl.BlockSpec((2, 1, ot.shape[1], t), lambda j: (0, cur(j) // tpb, 0, cur(j) % tpb)),
            const(w_in_t), const(wpat), const(wpbt), const(wot), const(g1), const(b1),
            const(wup), const(wdn), const(g2), const(b2),
        ],
        out_specs=pl.BlockSpec((2, 1, t, d), lambda j: (0, prev(j) // tpb, prev(j) % tpb, 0)),
        out_shape=jax.ShapeDtypeStruct((2, hb, s, d), _F32),
        scratch_shapes=[pltpu.VMEM((2, t, d), _F32)],
        compiler_params=pltpu.CompilerParams(
            dimension_semantics=("arbitrary",), vmem_limit_bytes=VMEM_LIMIT),
        name="post",
    )(mod, x.reshape(2, hb, s, d), ot.reshape(2, hb, ot.shape[1], s),
      w_in_t, wpat, wpbt, wot, g1, b1, wup, wdn, g2, b2)
    return out.reshape(bsz, s, d)


def _rope_tables(seq, half):
    pos = np.arange(seq)
    freqs = ROPE_THETA ** (-np.arange(half, dtype=np.float64) / half)
    tabs = []
    for p in (pos // GRID_W, pos % GRID_W):
        ang = p[None, :].astype(np.float64) * freqs[:, None]
        for fn, ident in ((np.cos, 1.0), (np.sin, 0.0)):
            tabs.append(np.concatenate([np.full((half, CTX_LEN), ident), fn(ang)], axis=1))
    return jnp.asarray(np.stack(tabs), dtype=_F32)


def kernel(x, c, ctx, c_ctx, w_mod, b_mod, w_in, q_norm_a, k_norm_a, cq_norm, ckv_norm,
           w_uq, w_ukv, w_proj_a, w_proj_b, w_out, ln1_g, ln1_b, w_up, w_down, ln2_g, ln2_b):
    bsz, s, d = x.shape
    assert w_mod.shape[0] == DEPTH == 1 and d == D_MODEL and ctx.shape[1] == CTX_LEN
    assert bsz == 8 and s % ATTN_Q_TILE == 0 and s % TOKEN_TILE == 0

    cc = jnp.concatenate([c, c_ctx[None, :], jnp.zeros((bsz - 1, d), _F32)], axis=0)
    mod = _modulation(cc, w_mod[0], b_mod[0][None, :])

    tbf = lambda w: w[0].astype(_BF16).T
    w_in_t = tbf(w_in)
    col = lambda v: v[0][:, None]
    tab_a = _rope_tables(s, A_HEAD_DIM // 4)
    tab_b = _rope_tables(s, B_ROPE_DIM // 4)

    q_all, k_all, v_all = _project(
        mod, ctx, x, w_in_t, tbf(w_uq), tbf(w_ukv),
        col(q_norm_a), col(k_norm_a), col(cq_norm), col(ckv_norm), tab_a, tab_b)
    ot = _attention(q_all, k_all, v_all)
    return _post(mod, x, ot, w_in_t, tbf(w_proj_a), tbf(w_proj_b), tbf(w_out), ln1_g, ln1_b,
                 w_up[0].astype(_BF16), w_down[0].astype(_BF16), ln2_g, ln2_b)
```
